```python
import jax
import jax.numpy as jnp
from jax import lax
import numpy as np

D_MODEL = 1024
BATCH = 16
SEQ = 4096
DEPTH = 4

GRID_W = 64
CTX_LEN = 256
RMS_EPS = 1e-6
N_MOD = 6

RWKV_HEADS = 8
RWKV_HEAD_DIM = 64
RWKV_DIM = RWKV_HEADS * RWKV_HEAD_DIM
RWKV_DECAY_LORA = 64
RWKV_ICLR_LORA = 64
RWKV_GATE_LORA = 160
RWKV_LNX_EPS = 64e-5

GDN_HEADS = 4
GDN_HEAD_DIM = 128
GDN_DIM = GDN_HEADS * GDN_HEAD_DIM
GDN_CONV_W = 5
GDN_CHUNK = 64

ATTN_Q_HEADS = 8
ATTN_KV_HEADS = 2
ATTN_GROUP = ATTN_Q_HEADS // ATTN_KV_HEADS
ATTN_HEAD_DIM = 64
ATTN_Q_DIM = ATTN_Q_HEADS * ATTN_HEAD_DIM
ATTN_KV_DIM = ATTN_KV_HEADS * ATTN_HEAD_DIM
ATTN_BLOCK = 128
ROPE_THETA = 10000.0

N_BRANCH = 3
FFN_HIDDEN = -(-8 * D_MODEL // (3 * 256)) * 256

IN_SIZES = (3 * RWKV_DIM, 3 * GDN_DIM, GDN_DIM, ATTN_Q_DIM, 2 * ATTN_KV_DIM)
IN_DIM = 3 * RWKV_DIM + 4 * GDN_DIM + ATTN_Q_DIM + 2 * ATTN_KV_DIM
IN_SPLIT_POINTS = (3 * RWKV_DIM, 3 * RWKV_DIM + 3 * GDN_DIM, 3 * RWKV_DIM + 4 * GDN_DIM, 3 * RWKV_DIM + 4 * GDN_DIM + ATTN_Q_DIM)

kernel_name = 'hybrid_rwkv7_gdn_gqa_prefix_dit'


def rmsnorm(x, g):
    xf = x.astype(jnp.float32)
    y = xf * lax.rsqrt(jnp.mean(xf * xf, axis=-1, keepdims=True) + RMS_EPS)
    return (y * g.astype(jnp.float32)).astype(x.dtype)


def l2normalize(t):
    t = t.astype(jnp.float32)
    return t * lax.rsqrt(jnp.sum(t * t, axis=-1, keepdims=True) + RMS_EPS)


def modulate(h, shift, scale):
    return h * (1.0 + scale) + shift


def swiglu(h, w1, w3, w2):
    return (jax.nn.silu(h @ w1) * (h @ w3)) @ w2


def token_shift_mix(u, mu):
    p = jnp.pad(u, ((0, 0), (1, 1), (0, 0)))
    return u + (0.5 * (p[:, :-2] + p[:, 2:]) - u) * mu


def short_conv(u, w):
    n_ch, width = w.shape
    out = lax.conv_general_dilated(
        u, jnp.transpose(w)[:, None, :].astype(u.dtype), window_strides=(1,),
        padding=[(width // 2, width // 2)], dimension_numbers=('NWC', 'WIO', 'NWC'),
        feature_group_count=n_ch)
    return jax.nn.silu(out)


def rwkv7_scan(r, w, k, v, a, b, s0, reverse):
    def step(s, inp):
        r_t, w_t, k_t, v_t, a_t, b_t = inp
        sa = jnp.einsum('bhvk,bhk->bhv', s, a_t)
        s = s * w_t[:, :, None, :] + sa[..., None] * b_t[:, :, None, :] + v_t[..., None] * k_t[:, :, None, :]
        return s, jnp.einsum('bhvk,bhk->bhv', s, r_t)
    xs = tuple(jnp.swapaxes(t, 0, 1) for t in (r, w, k, v, a, b))
    s_final, ys = lax.scan(step, s0, xs, reverse=reverse)
    return jnp.swapaxes(ys, 0, 1), s_final


def rwkv_branch(h, rkv, lp, init):
    B, T, _ = h.shape
    H, N = RWKV_HEADS, RWKV_HEAD_DIM
    heads = lambda t: t.reshape(B, T, H, N).astype(jnp.float32)
    mu_x, mu_rkv = lp['rwkv_mu_x'], lp['rwkv_mu_rkv']
    x_w = token_shift_mix(h, mu_x[0])
    x_a = token_shift_mix(h, mu_x[1])
    x_g = token_shift_mix(h, mu_x[2])
    r0, k0, v0 = jnp.split(rkv, 3, axis=-1)
    r = heads(token_shift_mix(r0, mu_rkv[0]))
    k = heads(token_shift_mix(k0, mu_rkv[1]))
    v = heads(token_shift_mix(v0, mu_rkv[2]))
    g = jax.nn.sigmoid(x_g @ lp['rwkv_g1']) @ lp['rwkv_g2']
    kk = l2normalize(k * lp['rwkv_k_k'].reshape(H, N))
    k_a = lp['rwkv_k_a'].reshape(H, N)
    outs, finals, bonuses = [], [], []
    for d in range(2):
        w_log = -jax.nn.softplus(-(lp['rwkv_w0'][d] + jnp.tanh(x_w @ lp['rwkv_w1'][d]) @ lp['rwkv_w2'][d])) - 0.5
        decay = jnp.exp(-jnp.exp(heads(w_log)))
        a = heads(jax.nn.sigmoid(lp['rwkv_a0'][d] + (x_a @ lp['rwkv_a1'][d]) @ lp['rwkv_a2'][d]))
        k_d = k * (1.0 + (a - 1.0) * k_a)
        o, s = rwkv7_scan(r, decay, k_d, v, -kk, kk * a, init[d], reverse=(d == 1))
        outs.append(o)
        finals.append(s)
        bonuses.append(jnp.sum(r * k_d * lp['rwkv_r_k'], axis=-1, keepdims=True) * v)
    o = outs[0] + outs[1]
    mean = jnp.mean(o, axis=-1, keepdims=True)
    var = jnp.mean(jnp.square(o - mean), axis=-1, keepdims=True)
    o = (o - mean) * lax.rsqrt(var + RWKV_LNX_EPS) * lp['rwkv_lnx_w'].reshape(H, N) + lp['rwkv_lnx_b'].reshape(H, N)
    y = (o + bonuses[0] + bonuses[1]).reshape(B, T, RWKV_DIM).astype(h.dtype) * g
    return y, (finals[0], finals[1])


def chunk_gated_delta(q, k, v, g, beta, s0):
    B, T, H, K = q.shape
    V = v.shape[-1]
    C = GDN_CHUNK
    n = T // C
    q = q * (K ** -0.5)
    chunks = lambda t: t.reshape(B, n, C, H, t.shape[-1]).transpose(1, 0, 3, 2, 4)
    qc, kc, vc = chunks(q), chunks(k), chunks(v)
    gc = jnp.cumsum(g.reshape(B, n, C, H).transpose(1, 0, 3, 2), axis=-1)
    bc = beta.reshape(B, n, C, H).transpose(1, 0, 3, 2)
    tril = jnp.tril(jnp.ones((C, C), dtype=bool))
    strict = jnp.tril(jnp.ones((C, C), dtype=bool), -1)
    diff = gc[..., :, None] - gc[..., None, :]
    decay = jnp.where(tril, jnp.exp(jnp.where(tril, diff, 0.0)), 0.0)
    kb = kc * bc[..., None]
    lower = jnp.where(strict, jnp.einsum('nbhik,nbhjk->nbhij', kb, kc) * decay, 0.0)
    a_mat = lower + jnp.eye(C, dtype=jnp.float32)
    rhs = jnp.concatenate([vc * bc[..., None], kb * jnp.exp(gc)[..., None]], axis=-1)
    sol = lax.linalg.triangular_solve(a_mat, rhs, left_side=True, lower=True, unit_diagonal=True)
    u, w = sol[..., :V], sol[..., V:]
    attn_intra = jnp.einsum('nbhik,nbhjk->nbhij', qc, kc) * decay

    def step(s, inp):
        q_i, k_i, u_i, w_i, g_i, a_i = inp
        v_new = u_i - jnp.einsum('bhck,bhkv->bhcv', w_i, s)
        o = jnp.einsum('bhck,bhkv->bhcv', q_i * jnp.exp(g_i)[..., None], s) + jnp.einsum('bhij,bhjv->bhiv', a_i, v_new)
        g_last = g_i[..., -1]
        s = s * jnp.exp(g_last)[..., None, None] + jnp.einsum(
            'bhck,bhcv->bhkv', k_i * jnp.exp(g_last[..., None] - g_i)[..., None], v_new)
        return s, o

    s_final, o = lax.scan(step, s0, (qc, kc, u, w, gc, attn_intra))
    return o.transpose(1, 0, 3, 2, 4).reshape(B, T, H, V), s_final


def gdn_branch(h, qkv, z, lp, init):
    B, T, _ = h.shape
    heads = lambda t: t.reshape(B, T, GDN_HEADS, GDN_HEAD_DIM)
    q, k, v = jnp.split(short_conv(qkv, lp['gdn_conv']), 3, axis=-1)
    q, k = l2normalize(heads(q)), l2normalize(heads(k))
    v = heads(v).astype(jnp.float32)
    outs, finals = [], []
    for d in range(2):
        beta = jax.nn.sigmoid((h @ lp['gdn_w_beta'][d]).astype(jnp.float32))
        g = -jnp.exp(lp['gdn_A_log'][d].astype(jnp.float32)) * jax.nn.softplus(
            (h @ lp['gdn_w_alpha'][d] + lp['gdn_dt_bias'][d]).astype(jnp.float32))
        if d == 0:
            o, s = chunk_gated_delta(q, k, v, g, beta, init[0])
        else:
            o, s = chunk_gated_delta(*(jnp.flip(t, 1) for t in (q, k, v, g, beta)), init[1])
            o = jnp.flip(o, 1)
        outs.append(o)
        finals.append(s)
    o = rmsnorm(outs[0] + outs[1], lp['gdn_norm']) * jax.nn.silu(heads(z).astype(jnp.float32))
    return o.reshape(B, T, GDN_DIM).astype(h.dtype), (finals[0], finals[1])


def axial_rope_tables(T):
    rows = T // GRID_W
    row = jnp.repeat(jnp.arange(rows), GRID_W).astype(jnp.float32)
    col = jnp.tile(jnp.arange(GRID_W), rows).astype(jnp.float32)
    half = ATTN_HEAD_DIM // 2
    inv = ROPE_THETA ** (-jnp.arange(0, half, 2, dtype=jnp.float32) / half)
    ang = jnp.concatenate([row[:, None] * inv, col[:, None] * inv], axis=-1)
    return jnp.cos(ang), jnp.sin(ang)


def apply_rope(x, cos, sin):
    x2 = x.reshape(*x.shape[:-1], -1, 2).astype(jnp.float32)
    x0, x1 = x2[..., 0], x2[..., 1]
    c, s = cos[None, :, None, :], sin[None, :, None, :]
    out = jnp.stack([x0 * c - x1 * s, x0 * s + x1 * c], axis=-1)
    return out.reshape(x.shape).astype(x.dtype)


def gqa(q, k, v):
    s = jnp.einsum('bkgqd,bksd->bkgqs', q, k).astype(jnp.float32) * (ATTN_HEAD_DIM ** -0.5)
    p = jax.nn.softmax(s, axis=-1).astype(v.dtype)
    return jnp.einsum('bkgqs,bksd->bkgqd', p, v)


def attn_kv(kv, lp):
    B, T, _ = kv.shape
    k, v = jnp.split(kv.reshape(B, T, 2 * ATTN_KV_HEADS, ATTN_HEAD_DIM), 2, axis=2)
    return rmsnorm(k, lp['attn_k_norm']), v


def attn_latent(q_lin, kv_lin, kc, vc, lp, cos, sin):
    B, T, _ = q_lin.shape
    q = apply_rope(rmsnorm(q_lin.reshape(B, T, ATTN_Q_HEADS, ATTN_HEAD_DIM), lp['attn_q_norm']), cos, sin)
    k, v = attn_kv(kv_lin, lp)
    k = apply_rope(k, cos, sin)
    k_all = jnp.concatenate([kc, k], axis=1).transpose(0, 2, 1, 3)
    v_all = jnp.concatenate([vc, v], axis=1).transpose(0, 2, 1, 3)
    nblk = T // ATTN_BLOCK
    qb = q.reshape(B, nblk, ATTN_BLOCK, ATTN_KV_HEADS, ATTN_GROUP, ATTN_HEAD_DIM).transpose(1, 0, 3, 4, 2, 5)
    o = lax.map(lambda blk: gqa(blk, k_all, v_all), qb)
    return o.transpose(1, 0, 4, 2, 3, 5).reshape(B, T, ATTN_Q_DIM)


def attn_context(q_lin, kc, vc, lp):
    B, Tc, _ = q_lin.shape
    q = rmsnorm(q_lin.reshape(B, Tc, ATTN_KV_HEADS, ATTN_GROUP, ATTN_HEAD_DIM), lp['attn_q_norm']).transpose(0, 2, 3, 1, 4)
    o = gqa(q, kc.transpose(0, 2, 1, 3), vc.transpose(0, 2, 1, 3))
    return o.transpose(0, 3, 1, 2, 4).reshape(B, Tc, ATTN_Q_DIM)


def merge_branches(h, ya, yb, yc, lp):
    gates = jax.nn.sigmoid(h @ lp['w_gate'] + lp['b_gate'])
    ga, gb, gc = jnp.split(gates, N_BRANCH, axis=-1)
    m = ga * (ya @ lp['w_up_a']) + gb * (yb @ lp['w_up_b']) + gc * (yc @ lp['w_up_c'])
    return m @ lp['w_out']


def token_mixer(hx, hc, lp, cos, sin, need_ctx):
    B = hx.shape[0]
    px = jnp.split(hx @ lp['w_in'], IN_SPLIT_POINTS, axis=-1)
    pc = jnp.split(hc @ lp['w_in'], IN_SPLIT_POINTS, axis=-1)
    zr = jnp.zeros((B, RWKV_HEADS, RWKV_HEAD_DIM, RWKV_HEAD_DIM), jnp.float32)
    zg = jnp.zeros((B, GDN_HEADS, GDN_HEAD_DIM, GDN_HEAD_DIM), jnp.float32)
    ya_c, st_a = rwkv_branch(hc, pc[0], lp, (zr, zr))
    ya_x, _ = rwkv_branch(hx, px[0], lp, st_a)
    yb_c, st_b = gdn_branch(hc, pc[1], pc[2], lp, (zg, zg))
    yb_x, _ = gdn_branch(hx, px[1], px[2], lp, st_b)
    kc, vc = attn_kv(pc[4], lp)
    yc_x = attn_latent(px[3], px[4], kc, vc, lp, cos, sin)
    out_x = merge_branches(hx, ya_x, yb_x, yc_x, lp)
    if not need_ctx:
        return out_x, None
    yc_c = attn_context(pc[3], kc, vc, lp)
    return out_x, merge_branches(hc, ya_c, yb_c, yc_c, lp)


def setup_inputs(seed: int = 0) -> dict:
    key = jax.random.key(seed)
    ks = iter(jax.random.split(key, 64))
    L, D = DEPTH, D_MODEL
    nrm = lambda shape, scale: jax.random.normal(next(ks), shape, jnp.float32) * scale
    uni = lambda shape, lo, hi: jax.random.uniform(next(ks), shape, jnp.float32, lo, hi)
    return {
        'x': nrm((BATCH, SEQ, D), 1.0),
        'c': nrm((BATCH, D), 1.0),
        'ctx': nrm((BATCH, CTX_LEN, D), 1.0),
        'c_ctx': nrm((D,), 1.0),
        'ada_w': nrm((L, D, N_MOD * D), D ** -0.5),
        'ada_b': nrm((L, N_MOD * D), 0.02),
        'norm1': 1.0 + nrm((L, D), 0.05),
        'norm2': 1.0 + nrm((L, D), 0.05),
        'w_in': nrm((L, D, IN_DIM), D ** -0.5),
        'rwkv_mu_x': uni((L, 3, D), 0.0, 1.0),
        'rwkv_mu_rkv': uni((L, 3, RWKV_DIM), 0.0, 1.0),
        'rwkv_w0': uni((L, 2, RWKV_DIM), -6.0, -1.0),
        'rwkv_w1': nrm((L, 2, D, RWKV_DECAY_LORA), D ** -0.5),
        'rwkv_w2': nrm((L, 2, RWKV_DECAY_LORA, RWKV_DIM), 0.1 * RWKV_DECAY_LORA ** -0.5),
        'rwkv_a0': nrm((L, 2, RWKV_DIM), 0.1),
        'rwkv_a1': nrm((L, 2, D, RWKV_ICLR_LORA), D ** -0.5),
        'rwkv_a2': nrm((L, 2, RWKV_ICLR_LORA, RWKV_DIM), 0.1 * RWKV_ICLR_LORA ** -0.5),
        'rwkv_g1': nrm((L, D, RWKV_GATE_LORA), D ** -0.5),
        'rwkv_g2': nrm((L, RWKV_GATE_LORA, RWKV_DIM), RWKV_GATE_LORA ** -0.5),
        'rwkv_k_k': 0.85 + nrm((L, RWKV_DIM), 0.05),
        'rwkv_k_a': 1.0 + nrm((L, RWKV_DIM), 0.05),
        'rwkv_r_k': nrm((L, RWKV_HEADS, RWKV_HEAD_DIM), 0.1),
        'rwkv_lnx_w': 1.0 + nrm((L, RWKV_DIM), 0.05),
        'rwkv_lnx_b': nrm((L, RWKV_DIM), 0.02),
        'gdn_conv': nrm((L, 3 * GDN_DIM, GDN_CONV_W), GDN_CONV_W ** -0.5),
        'gdn_w_alpha': nrm((L, 2, D, GDN_HEADS), D ** -0.5),
        'gdn_dt_bias': uni((L, 2, GDN_HEADS), -6.9, -2.3),
        'gdn_A_log': jnp.log(uni((L, 2, GDN_HEADS), 1.0, 16.0)),
        'gdn_w_beta': nrm((L, 2, D, GDN_HEADS), D ** -0.5),
        'gdn_norm': 1.0 + nrm((L, GDN_HEAD_DIM), 0.05),
        'attn_q_norm': 1.0 + nrm((L, ATTN_HEAD_DIM), 0.05),
        'attn_k_norm': 1.0 + nrm((L, ATTN_HEAD_DIM), 0.05),
        'w_up_a': nrm((L, RWKV_DIM, D), RWKV_DIM ** -0.5),
        'w_up_b': nrm((L, GDN_DIM, D), GDN_DIM ** -0.5),
        'w_up_c': nrm((L, ATTN_Q_DIM, D), ATTN_Q_DIM ** -0.5),
        'w_gate': nrm((L, D, N_BRANCH * D), D ** -0.5),
        'b_gate': nrm((L, N_BRANCH * D), 0.02),
        'w_out': nrm((L, D, D), D ** -0.5),
        'ffn_w1': nrm((L, D, FFN_HIDDEN), D ** -0.5),
        'ffn_w3': nrm((L, D, FFN_HIDDEN), D ** -0.5),
        'ffn_w2': nrm((L, FFN_HIDDEN, D), FFN_HIDDEN ** -0.5),
        'final_norm': 1.0 + nrm((D,), 0.05),
    }


def reference(x, c, ctx, c_ctx, ada_w, ada_b, norm1, norm2, w_in, rwkv_mu_x, rwkv_mu_rkv,
              rwkv_w0, rwkv_w1, rwkv_w2, rwkv_a0, rwkv_a1, rwkv_a2, rwkv_g1, rwkv_g2,
              rwkv_k_k, rwkv_k_a, rwkv_r_k, rwkv_lnx_w, rwkv_lnx_b, gdn_conv, gdn_w_alpha,
              gdn_dt_bias, gdn_A_log, gdn_w_beta, gdn_norm, attn_q_norm, attn_k_norm,
              w_up_a, w_up_b, w_up_c, w_gate, b_gate, w_out, ffn_w1, ffn_w3, ffn_w2, final_norm):
    T = x.shape[1]
    cos, sin = axial_rope_tables(T)
    silu_c = jax.nn.silu(c)
    silu_cc = jax.nn.silu(c_ctx)[None, :]
    cs = ctx
    for l in range(DEPTH):
        last = l == DEPTH - 1
        mod_x = (silu_c @ ada_w[l] + ada_b[l])[:, None, :]
        mod_c = (silu_cc @ ada_w[l] + ada_b[l])[:, None, :]
        sh1x, sc1x, g1x, sh2x, sc2x, g2x = jnp.split(mod_x, N_MOD, axis=-1)
        sh1c, sc1c, g1c, sh2c, sc2c, g2c = jnp.split(mod_c, N_MOD, axis=-1)
        lp = {
            'w_in': w_in[l], 'rwkv_mu_x': rwkv_mu_x[l], 'rwkv_mu_rkv': rwkv_mu_rkv[l],
            'rwkv_w0': rwkv_w0[l], 'rwkv_w1': rwkv_w1[l], 'rwkv_w2': rwkv_w2[l],
            'rwkv_a0': rwkv_a0[l], 'rwkv_a1': rwkv_a1[l], 'rwkv_a2': rwkv_a2[l],
            'rwkv_g1': rwkv_g1[l], 'rwkv_g2': rwkv_g2[l], 'rwkv_k_k': rwkv_k_k[l],
            'rwkv_k_a': rwkv_k_a[l], 'rwkv_r_k': rwkv_r_k[l], 'rwkv_lnx_w': rwkv_lnx_w[l],
            'rwkv_lnx_b': rwkv_lnx_b[l], 'gdn_conv': gdn_conv[l], 'gdn_w_alpha': gdn_w_alpha[l],
            'gdn_dt_bias': gdn_dt_bias[l], 'gdn_A_log': gdn_A_log[l], 'gdn_w_beta': gdn_w_beta[l],
            'gdn_norm': gdn_norm[l], 'attn_q_norm': attn_q_norm[l], 'attn_k_norm': attn_k_norm[l],
            'w_up_a': w_up_a[l], 'w_up_b': w_up_b[l], 'w_up_c': w_up_c[l],
            'w_gate': w_gate[l], 'b_gate': b_gate[l], 'w_out': w_out[l],
        }
        hx = modulate(rmsnorm(x, norm1[l]), sh1x, sc1x)
        hc = modulate(rmsnorm(cs, norm1[l]), sh1c, sc1c)
        yx, yc = token_mixer(hx, hc, lp, cos, sin, need_ctx=not last)
        x = x + g1x * yx
        x = x + g2x * swiglu(modulate(rmsnorm(x, norm2[l]), sh2x, sc2x), ffn_w1[l], ffn_w3[l], ffn_w2[l])
        if not last:
            cs = cs + g1c * yc
            cs = cs + g2c * swiglu(modulate(rmsnorm(cs, norm2[l]), sh2c, sc2c), ffn_w1[l], ffn_w3[l], ffn_w2[l])
    return rmsnorm(x, final_norm)
```

```python
import functools

import jax
import jax.numpy as jnp
from jax import lax
from jax.experimental import pallas as pl
from jax.experimental.pallas import tpu as pltpu

F32 = jnp.float32
BF16 = jnp.bfloat16

RMS_EPS = 1e-6
RWKV_LNX_EPS = 64e-5
ROPE_THETA = 10000.0
GRID_W = 64

SUBLANES = 8
LANES = 128
CHUNK = 64
MAX_TOKEN_BLOCK = 256
MAX_MM_ROWS = 512
KEY_BLOCK = 256
RWKV_PASSES = 1
GDN_PASSES = 1
VMEM_LIMIT = 56 * 1024 * 1024


def _sigmoid(x):
    return 1.0 / (1.0 + jnp.exp(-x))


def _softplus(x):
    return jnp.maximum(x, 0.0) + jnp.log(1.0 + jnp.exp(-jnp.abs(x)))


def _split2(x):
    hi = x.astype(BF16)
    lo = (x - hi.astype(F32)).astype(BF16)
    return hi, lo


def _mm(a, b, dims, passes):
    d = functools.partial(lax.dot_general, dimension_numbers=(dims, ((), ())), preferred_element_type=F32)
    if passes == 1:
        return d(a.astype(BF16), b.astype(BF16))
    ah, al = _split2(a)
    bh, bl = _split2(b)
    return d(ah, bh) + (d(ah, bl) + d(al, bh))


def _dot(a, b, passes=1):
    return _mm(a, b, ((1,), (0,)), passes)


def _dot_nt(a, b, passes=1):
    return _mm(a, b, ((1,), (1,)), passes)


def _dot_tn(a, b, passes=1):
    return _mm(a, b, ((0,), (0,)), passes)


def _split3(x):
    hi = x.astype(BF16)
    r1 = x - hi.astype(F32)
    mid = r1.astype(BF16)
    lo = (r1 - mid.astype(F32)).astype(BF16)
    return hi, mid, lo


def _dot_exact_lhs(m01, x):
    d = functools.partial(jnp.dot, preferred_element_type=F32)
    hi, mid, lo = _split3(x)
    return d(m01, hi) + (d(m01, mid) + d(m01, lo))


def _segsum(x, bd_ref):
    d = functools.partial(jnp.dot, preferred_element_type=F32)
    hi, lo = _split2(x)
    bd = bd_ref[...]
    return d(hi, bd) + d(lo, bd)


def _tri_inverse(l, eye):
    n = l.shape[0]
    row = lax.broadcasted_iota(jnp.int32, (n, n), 0)
    col = lax.broadcasted_iota(jnp.int32, (n, n), 1)
    t = eye - jnp.where((row >> 1) == (col >> 1), l, 0.0)
    for k in range(2, n.bit_length()):
        off = ((row >> k) == (col >> k)) & ((row >> (k - 1)) != (col >> (k - 1)))
        t = t - _dot(_dot(t, jnp.where(off, l, 0.0), 3), t, 3)
    return t


def _row_iota(shape):
    return lax.broadcasted_iota(jnp.int32, shape, 0)


def _shift_rows(x, prev_ref, next_ref, k, first, last):
    n = x.shape[0]
    row = _row_iota(x.shape)
    y = pltpu.roll(x, (-k) % n, axis=0)
    if k < 0:
        for i in range(-k):
            edge = prev_ref[SUBLANES + k + i:SUBLANES + k + i + 1, :]
            edge = jnp.where(first, 0.0, edge)
            y = jnp.where(row == i, edge, y)
    else:
        for i in range(k):
            edge = next_ref[i:i + 1, :]
            edge = jnp.where(last, 0.0, edge)
            y = jnp.where(row == n - k + i, edge, y)
    return y


def _largest_divisor(n, candidates):
    for c in candidates:
        if n % c == 0:
            return c
    raise ValueError(f"no block size among {candidates} divides {n}")


class _Seq:
    def __init__(self, batch, ctx_len, seq_len):
        self.B = batch
        self.ctx = ctx_len
        self.S = ctx_len + seq_len
        self.TB = _largest_divisor(ctx_len, (MAX_TOKEN_BLOCK, 128, 64))
        assert seq_len % self.TB == 0 and self.TB % CHUNK == 0
        self.ncb = ctx_len // self.TB
        self.nblk = self.S // self.TB
        self.ncc = ctx_len // CHUNK
        self.nchunk = self.S // CHUNK
        self.grid = (batch, self.nblk)

    def rows(self, width, col_block=0):
        return pl.BlockSpec((None, self.TB, width), lambda b, j: (b, j, col_block))

    def rows2(self, width):
        return pl.BlockSpec((2, None, self.TB, width), lambda b, j: (0, b, j, 0))

    def heads(self, n_heads, width):
        return pl.BlockSpec((None, n_heads, self.TB, width), lambda b, j: (b, 0, j, 0))

    def prev_rows(self, width, col_block=0):
        per = self.TB // SUBLANES
        return pl.BlockSpec((None, SUBLANES, width),
                            lambda b, j: (b, jnp.maximum(j * per - 1, 0), col_block))

    def next_rows(self, width, col_block=0):
        per = self.TB // SUBLANES
        top = self.S // SUBLANES - 1
        return pl.BlockSpec((None, SUBLANES, width),
                            lambda b, j: (b, jnp.minimum((j + 1) * per, top), col_block))

    def const(self, shape):
        zeros = (0,) * len(shape)
        return pl.BlockSpec(shape, lambda b, j: zeros)

    def mod(self, d_model):
        ncb = self.ncb
        return pl.BlockSpec((None, None, 6, d_model),
                            lambda b, j: (b, (j >= ncb).astype(jnp.int32), 0, 0))

    def edges(self):
        j = pl.program_id(1)
        first = (j == 0) | (j == self.ncb)
        last = (j == self.ncb - 1) | (j == self.nblk - 1)
        return first, last


def _params(n_axes):
    return pltpu.CompilerParams(dimension_semantics=("arbitrary",) * n_axes,
                                vmem_limit_bytes=VMEM_LIMIT)


def _mm_body(*refs, act, pre_act, n_chunk, has_bias):
    if has_bias:
        x_ref, w_ref, b_ref, o_ref = refs
    else:
        x_ref, w_ref, o_ref = refs
        b_ref = None
    x = x_ref[...]
    if pre_act == "silu":
        x = x * _sigmoid(x)
    xb = x.astype(BF16)
    n = o_ref.shape[-1]
    for n0 in range(0, n, n_chunk):
        y = jnp.dot(xb, w_ref[:, n0:n0 + n_chunk], preferred_element_type=F32)
        if b_ref is not None:
            y = y + b_ref[:, n0:n0 + n_chunk]
        if act == "sigmoid":
            y = _sigmoid(y)
        o_ref[:, n0:n0 + n_chunk] = y.astype(o_ref.dtype)


def _matmul(x, w, bias=None, act=None, pre_act=None, out_dtype=F32):
    m, k = x.shape
    n = w.shape[1]
    tm = m if m <= MAX_MM_ROWS else _largest_divisor(m, (MAX_MM_ROWS, 256, 128, 64, 32, 16, 8))
    n_chunk = _largest_divisor(n, (512, 256, 128))
    in_specs = [pl.BlockSpec((tm, k), lambda i: (i, 0)),
                pl.BlockSpec((k, n), lambda i: (0, 0))]
    args = [x, w]
    if bias is not None:
        in_specs.append(pl.BlockSpec((1, n), lambda i: (0, 0)))
        args.append(bias.reshape(1, n))
    return pl.pallas_call(
        functools.partial(_mm_body, act=act, pre_act=pre_act, n_chunk=n_chunk,
                          has_bias=bias is not None),
        grid=(m // tm,),
        in_specs=in_specs,
        out_specs=pl.BlockSpec((tm, n), lambda i: (i, 0)),
        out_shape=jax.ShapeDtypeStruct((m, n), out_dtype),
        compiler_params=_params(1),
        name="matmul",
    )(*args)


def _normmod_body(x_ref, g_ref, mod_ref, o_ref, *, shift_row, scale_row):
    x = x_ref[...]
    y = x * lax.rsqrt(jnp.mean(x * x, axis=-1, keepdims=True) + RMS_EPS)
    y = y * g_ref[...]
    o_ref[...] = y * (1.0 + mod_ref[scale_row:scale_row + 1, :]) + mod_ref[shift_row:shift_row + 1, :]


def _normmod(sq, xs, g, modtab, shift_row, scale_row):
    d = xs.shape[-1]
    return pl.pallas_call(
        functools.partial(_normmod_body, shift_row=shift_row, scale_row=scale_row),
        grid=sq.grid,
        in_specs=[sq.rows(d), sq.const((1, d)), sq.mod(d)],
        out_specs=sq.rows(d),
        out_shape=jax.ShapeDtypeStruct(xs.shape, F32),
        compiler_params=_params(2),
        name="normmod",
    )(xs, g.reshape(1, d), modtab)


def _final_norm_body(x_ref, g_ref, o_ref):
    x = x_ref[...]
    y = x * lax.rsqrt(jnp.mean(x * x, axis=-1, keepdims=True) + RMS_EPS)
    o_ref[...] = y * g_ref[...]


def _final_norm(sq, xs, g):
    d = xs.shape[-1]
    ncb = sq.ncb
    return pl.pallas_call(
        _final_norm_body,
        grid=(sq.B, sq.nblk - ncb),
        in_specs=[pl.BlockSpec((None, sq.TB, d), lambda b, j: (b, j + ncb, 0)),
                  pl.BlockSpec((1, d), lambda b, j: (0, 0))],
        out_specs=pl.BlockSpec((None, sq.TB, d), lambda b, j: (b, j, 0)),
        out_shape=jax.ShapeDtypeStruct((sq.B, sq.S - sq.ctx, d), F32),
        compiler_params=_params(2),
        name="final_norm",
    )(xs, g.reshape(1, d))


def _lora_body(h_ref, hp_ref, hn_ref, mu_ref, w1_ref, w2_ref, w0_ref, a1_ref, a2_ref, a0_ref,
               g1_ref, g2_ref, wab_ref, abb_ref, alog_ref,
               lw_ref, a_ref, g_ref, ab_ref, *, sq, rdim, n_beta):
    first, last = sq.edges()
    h = h_ref[...]
    nb = 0.5 * (_shift_rows(h, hp_ref, hn_ref, -1, first, last)
                + _shift_rows(h, hp_ref, hn_ref, 1, first, last))
    dlt = nb - h
    xw = h + dlt * mu_ref[0:1, :]
    xa = h + dlt * mu_ref[1:2, :]
    xg = h + dlt * mu_ref[2:3, :]
    wl = w0_ref[...] + _dot(jnp.tanh(_dot(xw, w1_ref[...])), w2_ref[...])
    w_log = -_softplus(-wl) - 0.5
    lw = -jnp.exp(w_log)
    lw_ref[0] = lw[:, :rdim]
    lw_ref[1] = lw[:, rdim:]
    a = _sigmoid(a0_ref[...] + _dot(_dot(xa, a1_ref[...]), a2_ref[...]))
    a_ref[0] = a[:, :rdim]
    a_ref[1] = a[:, rdim:]
    g_ref[...] = _dot(_sigmoid(_dot(xg, g1_ref[...])), g2_ref[...])
    z = _dot(h, wab_ref[...])
    col = lax.broadcasted_iota(jnp.int32, z.shape, 1)
    gl = -jnp.exp(alog_ref[...]) * _softplus(z + abb_ref[...])
    ab_ref[...] = jnp.where(col < n_beta, _sigmoid(z), gl)


def _lora(sq, h, lw):
    d = h.shape[-1]
    rdim = lw["w0"].shape[-1] // 2
    consts = [lw["mu"], lw["w1"], lw["w2"], lw["w0"], lw["a1"], lw["a2"], lw["a0"],
              lw["g1"], lw["g2"], lw["wab"], lw["abb"], lw["alog"]]
    bsd = (sq.B, sq.S)
    return pl.pallas_call(
        functools.partial(_lora_body, sq=sq, rdim=rdim, n_beta=lw["n_beta"]),
        grid=sq.grid,
        in_specs=[sq.rows(d), sq.prev_rows(d), sq.next_rows(d)] + [sq.const(c.shape) for c in consts],
        out_specs=[sq.rows2(rdim), sq.rows2(rdim), sq.rows(rdim), sq.rows(LANES)],
        out_shape=[jax.ShapeDtypeStruct((2,) + bsd + (rdim,), F32),
                   jax.ShapeDtypeStruct((2,) + bsd + (rdim,), F32),
                   jax.ShapeDtypeStruct(bsd + (rdim,), F32),
                   jax.ShapeDtypeStruct(bsd + (LANES,), F32)],
        compiler_params=_params(2),
        name="lora",
    )(h, h, h, *consts)


def _rwkv_prep_body(p_ref, pp_ref, pn_ref, a_ref, mu_ref, kk_ref, ka_ref, bd_ref,
                    r_ref, v_ref, aa_ref, kd_ref, bb_ref, *, sq, rdim):
    first, last = sq.edges()
    p = p_ref[...]
    nb = 0.5 * (_shift_rows(p, pp_ref, pn_ref, -1, first, last)
                + _shift_rows(p, pp_ref, pn_ref, 1, first, last))
    dlt = nb - p
    r = p[:, :rdim] + dlt[:, :rdim] * mu_ref[0:1, :]
    k = p[:, rdim:2 * rdim] + dlt[:, rdim:2 * rdim] * mu_ref[1:2, :]
    v = p[:, 2 * rdim:] + dlt[:, 2 * rdim:] * mu_ref[2:3, :]
    t = k * kk_ref[...]
    kk = t * lax.rsqrt(_segsum(t * t, bd_ref) + RMS_EPS)
    r_ref[...] = r
    v_ref[...] = v
    aa_ref[...] = -kk
    for d in range(2):
        a = a_ref[d]
        kd_ref[d] = k * (1.0 + (a - 1.0) * ka_ref[...])
        bb_ref[d] = kk * a


def _rwkv_prep(sq, proj, a, lw):
    rdim = a.shape[-1]
    consts = [lw["mu_rkv"], lw["k_k"], lw["k_a"], lw["bd_r"]]
    one = jax.ShapeDtypeStruct((sq.B, sq.S, rdim), F32)
    two = jax.ShapeDtypeStruct((2, sq.B, sq.S, rdim), F32)
    return pl.pallas_call(
        functools.partial(_rwkv_prep_body, sq=sq, rdim=rdim),
        grid=sq.grid,
        in_specs=[sq.rows(3 * rdim), sq.prev_rows(3 * rdim), sq.next_rows(3 * rdim), sq.rows2(rdim)]
        + [sq.const(c.shape) for c in consts],
        out_specs=[sq.rows(rdim), sq.rows(rdim), sq.rows(rdim), sq.rows2(rdim), sq.rows2(rdim)],
        out_shape=[one, one, one, two, two],
        compiler_params=_params(2),
        name="rwkv_prep",
    )(proj, proj, proj, a, *consts)


def _chunk_index(d, i, ncc, nchunk):
    back = jnp.where(i < ncc, ncc - 1 - i, nchunk + ncc - 1 - i)
    return jnp.where(d == 0, i, back)


def _causal_masks(d, c, reps=1):
    row = lax.broadcasted_iota(jnp.int32, (c, reps * c), 0)
    col = lax.broadcasted_iota(jnp.int32, (c, reps * c), 1) & (c - 1)
    delta = jnp.where(d == 0, row - col, col - row)
    eye = jnp.where(row == col, 1.0, 0.0).astype(F32)
    return delta >= 0, delta > 0, eye


def _rwkv_scan_body(r_ref, v_ref, aa_ref, kd_ref, lw_ref, bb_ref, o_ref, s_ref, *, heads, hdim):
    d = pl.program_id(1)

    @pl.when(pl.program_id(2) == 0)
    def _():
        s_ref[...] = jnp.zeros_like(s_ref)

    c = r_ref.shape[0]
    incl, strict, eye = _causal_masks(d, c)
    incl2, _, _ = _causal_masks(d, c, reps=2)
    lw = lw_ref[...]
    cum = _dot_exact_lhs(jnp.where(incl, 1.0, 0.0).astype(BF16), lw)
    tot = jnp.sum(lw, axis=0, keepdims=True)
    e_cum = jnp.exp(cum)
    e_neg = jnp.exp(-cum)
    r, v, aa, kd, bb = r_ref[...], v_ref[...], aa_ref[...], kd_ref[...], bb_ref[...]
    a_t = aa * jnp.exp(cum - lw)
    r_t = r * e_cum
    b_t = bb * e_neg
    k_t = kd * e_neg
    e_rem = jnp.exp(tot - cum)
    b_h = bb * e_rem
    k_h = kd * e_rem
    e_tot = jnp.exp(tot)
    for h in range(heads):
        sl = slice(h * hdim, (h + 1) * hdim)
        ar = jnp.concatenate([a_t[:, sl], r_t[:, sl]], axis=0)
        bk = jnp.concatenate([b_t[:, sl], k_t[:, sl]], axis=0)
        m1 = _dot_nt(ar, bk, RWKV_PASSES)
        a_ab = jnp.where(strict, m1[:c, :c], 0.0)
        a_ak = jnp.where(strict, m1[:c, c:], 0.0)
        a_r = jnp.where(incl2, m1[c:, :], 0.0)
        t_inv = _tri_inverse(-a_ab, eye)
        s = s_ref[h]
        m2 = _dot_nt(ar, s, RWKV_PASSES)
        vh = v[:, sl]
        u = _dot(t_inv, m2[:c] + _dot(a_ak, vh, RWKV_PASSES), RWKV_PASSES)
        uv = jnp.concatenate([u, vh], axis=0)
        o_ref[:, sl] = m2[c:] + _dot(a_r, uv, RWKV_PASSES)
        bkh = jnp.concatenate([b_h[:, sl], k_h[:, sl]], axis=0)
        s_ref[h] = s * e_tot[:, sl] + _dot_tn(uv, bkh, RWKV_PASSES)


def _scan_specs(sq, width):
    ncc, nchunk = sq.ncc, sq.nchunk
    shared = pl.BlockSpec((None, CHUNK, width),
                          lambda b, d, i: (b, _chunk_index(d, i, ncc, nchunk), 0))
    per_dir = pl.BlockSpec((None, None, CHUNK, width),
                           lambda b, d, i: (d, b, _chunk_index(d, i, ncc, nchunk), 0))
    return shared, per_dir


def _rwkv_scan(sq, r, v, aa, kd, lw, bb, heads):
    rdim = r.shape[-1]
    hdim = rdim // heads
    shared, per_dir = _scan_specs(sq, rdim)
    return pl.pallas_call(
        functools.partial(_rwkv_scan_body, heads=heads, hdim=hdim),
        grid=(sq.B, 2, sq.nchunk),
        in_specs=[shared, shared, shared, per_dir, per_dir, per_dir],
        out_specs=per_dir,
        out_shape=jax.ShapeDtypeStruct((2, sq.B, sq.S, rdim), F32),
        scratch_shapes=[pltpu.VMEM((heads, hdim, hdim), F32)],
        compiler_params=_params(3),
        name="rwkv_scan",
    )(r, v, aa, kd, lw, bb)


def _rwkv_post_body(o_ref, r_ref, v_ref, kd_ref, g_ref, rk_ref, lnw_ref, lnb_ref, bd_ref, y_ref, *, hdim):
    o = o_ref[0] + o_ref[1]
    inv_n = 1.0 / hdim
    mean = _segsum(o, bd_ref) * inv_n
    cen = o - mean
    var = _segsum(cen * cen, bd_ref) * inv_n
    o = cen * lax.rsqrt(var + RWKV_LNX_EPS) * lnw_ref[...] + lnb_ref[...]
    r = r_ref[...]
    v = v_ref[...]
    rk = rk_ref[...]
    for d in range(2):
        o = o + _segsum(r * kd_ref[d] * rk, bd_ref) * v
    y_ref[...] = o * g_ref[...]


def _rwkv_post(sq, o, r, v, kd, g, lw, heads):
    rdim = r.shape[-1]
    consts = [lw["r_k"], lw["lnx_w"], lw["lnx_b"], lw["bd_r"]]
    return pl.pallas_call(
        functools.partial(_rwkv_post_body, hdim=rdim // heads),
        grid=sq.grid,
        in_specs=[sq.rows2(rdim), sq.rows(rdim), sq.rows(rdim), sq.rows2(rdim), sq.rows(rdim)]
        + [sq.const(c.shape) for c in consts],
        out_specs=sq.rows(rdim),
        out_shape=jax.ShapeDtypeStruct((sq.B, sq.S, rdim), F32),
        compiler_params=_params(2),
        name="rwkv_post",
    )(o, r, v, kd, g, *consts)


def _gdn_prep_body(p_ref, pp_ref, pn_ref, cw_ref, bd_ref, q_ref, k_ref, v_ref, *, sq, gdim):
    first, last = sq.edges()
    p = p_ref[...]
    width = cw_ref.shape[0]
    half = width // 2
    acc = p * cw_ref[half:half + 1, :]
    for j in range(width):
        if j != half:
            acc = acc + _shift_rows(p, pp_ref, pn_ref, j - half, first, last) * cw_ref[j:j + 1, :]
    y = acc * _sigmoid(acc)
    q = y[:, :gdim]
    k = y[:, gdim:2 * gdim]
    q_ref[...] = q * lax.rsqrt(_segsum(q * q, bd_ref) + RMS_EPS)
    k_ref[...] = k * lax.rsqrt(_segsum(k * k, bd_ref) + RMS_EPS)
    v_ref[...] = y[:, 2 * gdim:]


def _gdn_prep(sq, proj, lw, gdim, col_block):
    consts = [lw["conv"], lw["bd_g"]]
    one = jax.ShapeDtypeStruct((sq.B, sq.S, gdim), F32)
    w = 3 * gdim
    return pl.pallas_call(
        functools.partial(_gdn_prep_body, sq=sq, gdim=gdim),
        grid=sq.grid,
        in_specs=[sq.rows(w, col_block), sq.prev_rows(w, col_block), sq.next_rows(w, col_block)]
        + [sq.const(c.shape) for c in consts],
        out_specs=[sq.rows(gdim)] * 3,
        out_shape=[one, one, one],
        compiler_params=_params(2),
        name="gdn_prep",
    )(proj, proj, proj, *consts)


def _gdn_scan_body(q_ref, k_ref, v_ref, ab_ref, o_ref, s_ref, *, heads, hdim):
    d = pl.program_id(1)

    @pl.when(pl.program_id(2) == 0)
    def _():
        s_ref[...] = jnp.zeros_like(s_ref)

    c = q_ref.shape[0]
    incl, strict, eye = _causal_masks(d, c)
    ab = ab_ref[...]
    gc_all = _dot_exact_lhs(jnp.where(incl, 1.0, 0.0).astype(BF16), ab)
    gc_all_t = jnp.transpose(gc_all)
    tot_all = jnp.sum(ab, axis=0, keepdims=True)
    scale = hdim ** -0.5
    for h in range(heads):
        sl = slice(h * hdim, (h + 1) * hdim)
        lane = lax.broadcasted_iota(jnp.int32, ab.shape, 1)
        pick_b = lane == d * heads + h
        pick_g = lane == 2 * heads + d * heads + h
        beta = jnp.sum(jnp.where(pick_b, ab, 0.0), axis=1, keepdims=True)
        gc = jnp.sum(jnp.where(pick_g, gc_all, 0.0), axis=1, keepdims=True)
        g_last = jnp.sum(jnp.where(pick_g[:1], tot_all, 0.0), axis=1, keepdims=True)
        sub = lax.broadcasted_iota(jnp.int32, gc_all_t.shape, 0)
        gc_row = jnp.sum(jnp.where(sub == 2 * heads + d * heads + h, gc_all_t, 0.0),
                         axis=0, keepdims=True)
        diff = gc - gc_row
        decay = jnp.where(incl, jnp.exp(jnp.where(incl, diff, 0.0)), 0.0)
        q = q_ref[:, sl] * scale
        k = k_ref[:, sl]
        v = v_ref[:, sl]
        kb = k * beta
        lower = jnp.where(strict, _dot_nt(kb, k, GDN_PASSES) * decay, 0.0)
        t_inv = _tri_inverse(lower, eye)
        e_gc = jnp.exp(gc)
        sol = _dot(t_inv, jnp.concatenate([v * beta, kb * e_gc], axis=1), GDN_PASSES)
        u = sol[:, :hdim]
        w = sol[:, hdim:]
        s = s_ref[h]
        v_new = u - _dot(w, s, GDN_PASSES)
        attn = _dot_nt(q, k, GDN_PASSES) * decay
        o_ref[:, sl] = _dot(q * e_gc, s, GDN_PASSES) + _dot(attn, v_new, GDN_PASSES)
        s_ref[h] = s * jnp.exp(g_last) + _dot_tn(k * jnp.exp(g_last - gc), v_new, GDN_PASSES)


def _gdn_scan(sq, q, k, v, ab, heads):
    gdim = q.shape[-1]
    hdim = gdim // heads
    ncc, nchunk = sq.ncc, sq.nchunk
    shared, _ = _scan_specs(sq, gdim)
    small, _ = _scan_specs(sq, LANES)
    out = pl.BlockSpec((None, None, CHUNK, gdim),
                       lambda b, d, i: (d, b, _chunk_index(d, i, ncc, nchunk), 0))
    return pl.pallas_call(
        functools.partial(_gdn_scan_body, heads=heads, hdim=hdim),
        grid=(sq.B, 2, sq.nchunk),
        in_specs=[shared, shared, shared, small],
        out_specs=out,
        out_shape=jax.ShapeDtypeStruct((2, sq.B, sq.S, gdim), F32),
        scratch_shapes=[pltpu.VMEM((heads, hdim, hdim), F32)],
        compiler_params=_params(3),
        name="gdn_scan",
    )(q, k, v, ab)


def _gdn_post_body(o_ref, z_ref, gn_ref, bd_ref, y_ref, *, hdim):
    o = o_ref[0] + o_ref[1]
    ms = _segsum(o * o, bd_ref) * (1.0 / hdim)
    z = z_ref[...]
    y_ref[...] = o * lax.rsqrt(ms + RMS_EPS) * gn_ref[...] * (z * _sigmoid(z))


def _gdn_post(sq, o, proj, lw, heads, z_col_block):
    gdim = o.shape[-1]
    consts = [lw["gdn_norm"], lw["bd_g"]]
    return pl.pallas_call(
        functools.partial(_gdn_post_body, hdim=gdim // heads),
        grid=sq.grid,
        in_specs=[sq.rows2(gdim), sq.rows(gdim, z_col_block)] + [sq.const(c.shape) for c in consts],
        out_specs=sq.rows(gdim),
        out_shape=jax.ShapeDtypeStruct((sq.B, sq.S, gdim), F32),
        compiler_params=_params(2),
        name="gdn_post",
    )(o, proj, *consts)


def _rope(x, cos, sin_signed):
    n = x.shape[-1]
    lane = lax.broadcasted_iota(jnp.int32, x.shape, 1)
    partner = jnp.where((lane & 1) == 0, pltpu.roll(x, n - 1, axis=1), pltpu.roll(x, 1, axis=1))
    return x * cos + partner * sin_signed


def _attn_prep_body(q_ref, kv_ref, cs_ref, qn_ref, kn_ref, bd_ref, qo_ref, ko_ref, vo_ref,
                    *, q_heads, kv_heads, hdim):
    kvd = kv_heads * hdim
    cos = cs_ref[:, :kvd]
    sin = cs_ref[:, kvd:]
    reps = q_heads // kv_heads
    cos_q = jnp.concatenate([cos] * reps, axis=1)
    sin_q = jnp.concatenate([sin] * reps, axis=1)
    inv_n = 1.0 / hdim
    q = q_ref[...]
    q = q * lax.rsqrt(_segsum(q * q, bd_ref) * inv_n + RMS_EPS) * qn_ref[...]
    q = _rope(q, cos_q, sin_q) * (hdim ** -0.5)
    kv = kv_ref[...]
    k = kv[:, :kvd]
    kbd = bd_ref[:kvd, :kvd]
    hi, lo = _split2(k * k)
    ms = (jnp.dot(hi, kbd, preferred_element_type=F32) + jnp.dot(lo, kbd, preferred_element_type=F32)) * inv_n
    k = k * lax.rsqrt(ms + RMS_EPS) * kn_ref[...]
    k = _rope(k, cos, sin)
    v = kv[:, kvd:]
    for h in range(q_heads):
        qo_ref[h] = q[:, h * hdim:(h + 1) * hdim].astype(BF16)
    for h in range(kv_heads):
        ko_ref[h] = k[:, h * hdim:(h + 1) * hdim].astype(BF16)
        vo_ref[h] = v[:, h * hdim:(h + 1) * hdim].astype(BF16)


def _attn_prep(sq, proj, cs_tab, lw, q_heads, kv_heads, hdim, q_col_block, kv_col_block):
    qd, kvd = q_heads * hdim, kv_heads * hdim
    consts = [lw["q_norm"], lw["k_norm"], lw["bd_r"]]
    return pl.pallas_call(
        functools.partial(_attn_prep_body, q_heads=q_heads, kv_heads=kv_heads, hdim=hdim),
        grid=sq.grid,
        in_specs=[sq.rows(qd, q_col_block), sq.rows(2 * kvd, kv_col_block),
                  pl.BlockSpec((sq.TB, 2 * kvd), lambda b, j: (j, 0))]
        + [sq.const(c.shape) for c in consts],
        out_specs=[sq.heads(q_heads, hdim), sq.heads(kv_heads, hdim), sq.heads(kv_heads, hdim)],
        out_shape=[jax.ShapeDtypeStruct((sq.B, q_heads, sq.S, hdim), BF16),
                   jax.ShapeDtypeStruct((sq.B, kv_heads, sq.S, hdim), BF16),
                   jax.ShapeDtypeStruct((sq.B, kv_heads, sq.S, hdim), BF16)],
        compiler_params=_params(2),
        name="attn_prep",
    )(proj, proj, cs_tab, *consts)


def _attn_body(q_ref, k_ref, v_ref, o_ref, *, group, n_ctx_qblocks, n_ctx_kblocks, n_kblocks):
    tq, hdim = q_ref.shape[1], q_ref.shape[2]
    q = q_ref[...].reshape(group * tq, hdim)
    is_ctx = pl.program_id(2) < n_ctx_qblocks
    n_k = jnp.where(is_ctx, n_ctx_kblocks, n_kblocks)

    def step(i, carry):
        m, l, acc = carry
        off = pl.multiple_of(i * KEY_BLOCK, KEY_BLOCK)
        k = k_ref[pl.ds(off, KEY_BLOCK), :]
        v = v_ref[pl.ds(off, KEY_BLOCK), :]
        s = lax.dot_general(q, k, (((1,), (1,)), ((), ())), preferred_element_type=F32)
        m_new = jnp.maximum(m, jnp.max(s, axis=1, keepdims=True))
        alpha = jnp.exp(m - m_new)
        p = jnp.exp(s - m_new)
        l = alpha * l + jnp.sum(p, axis=1, keepdims=True)
        acc = alpha * acc + jnp.dot(p.astype(BF16), v, preferred_element_type=F32)
        return m_new, l, acc

    rows = group * tq
    init = (jnp.full((rows, 1), -jnp.inf, F32), jnp.zeros((rows, 1), F32), jnp.zeros((rows, hdim), F32))
    m, l, acc = lax.fori_loop(0, n_k, step, init)
    out = acc / l
    for g in range(group):
        o_ref[:, g * hdim:(g + 1) * hdim] = out[g * tq:(g + 1) * tq, :]


def _attention(sq, q, k, v):
    b, q_heads, s, hdim = q.shape
    kv_heads = k.shape[1]
    group = q_heads // kv_heads
    tq = _largest_divisor(sq.ctx, (128, 64))
    assert sq.ctx % KEY_BLOCK == 0 and s % KEY_BLOCK == 0
    return pl.pallas_call(
        functools.partial(_attn_body, group=group, n_ctx_qblocks=sq.ctx // tq,
                          n_ctx_kblocks=sq.ctx // KEY_BLOCK, n_kblocks=s // KEY_BLOCK),
        grid=(b, kv_heads, s // tq),
        in_specs=[pl.BlockSpec((None, group, tq, hdim), lambda bi, g, i: (bi, g, i, 0)),
                  pl.BlockSpec((None, None, s, hdim), lambda bi, g, i: (bi, g, 0, 0)),
                  pl.BlockSpec((None, None, s, hdim), lambda bi, g, i: (bi, g, 0, 0))],
        out_specs=pl.BlockSpec((None, tq, group * hdim), lambda bi, g, i: (bi, i, g)),
        out_shape=jax.ShapeDtypeStruct((b, s, q_heads * hdim), F32),
        compiler_params=_params(3),
        name="attention",
    )(q, k, v)


def _merge_body(x_ref, ya_ref, yb_ref, yc_ref, gate_ref, mod_ref, wa_ref, wb_ref, wc_ref, wo_ref, o_ref,
                *, gate_row):
    d = x_ref.shape[-1]
    m = gate_ref[:, :d] * _dot(ya_ref[...], wa_ref[...])
    m = m + gate_ref[:, d:2 * d] * _dot(yb_ref[...], wb_ref[...])
    m = m + gate_ref[:, 2 * d:] * _dot(yc_ref[...], wc_ref[...])
    y = _dot(m, wo_ref[...])
    o_ref[...] = x_ref[...] + mod_ref[gate_row:gate_row + 1, :] * y


def _merge(sq, xs, ya, yb, yc, gates, modtab, lw):
    d = xs.shape[-1]
    consts = [lw["w_up_a"], lw["w_up_b"], lw["w_up_c"], lw["w_out"]]
    return pl.pallas_call(
        functools.partial(_merge_body, gate_row=2),
        grid=sq.grid,
        in_specs=[sq.rows(d), sq.rows(ya.shape[-1]), sq.rows(yb.shape[-1]), sq.rows(yc.shape[-1]),
                  sq.rows(3 * d), sq.mod(d)] + [sq.const(c.shape) for c in consts],
        out_specs=sq.rows(d),
        out_shape=jax.ShapeDtypeStruct(xs.shape, F32),
        compiler_params=_params(2),
        name="merge",
    )(xs, ya, yb, yc, gates, modtab, *consts)


def _ffn_body(x_ref, g_ref, mod_ref, w1_ref, w3_ref, w2_ref, o_ref, *, h_chunk):
    x = x_ref[...]
    y = x * lax.rsqrt(jnp.mean(x * x, axis=-1, keepdims=True) + RMS_EPS) * g_ref[...]
    h = (y * (1.0 + mod_ref[4:5, :]) + mod_ref[3:4, :]).astype(BF16)
    hidden = w1_ref.shape[1]
    acc = jnp.zeros(x.shape, F32)
    for c0 in range(0, hidden, h_chunk):
        a = jnp.dot(h, w1_ref[:, c0:c0 + h_chunk], preferred_element_type=F32)
        b = jnp.dot(h, w3_ref[:, c0:c0 + h_chunk], preferred_element_type=F32)
        t = (a * _sigmoid(a) * b).astype(BF16)
        acc = acc + jnp.dot(t, w2_ref[c0:c0 + h_chunk, :], preferred_element_type=F32)
    o_ref[...] = x + mod_ref[5:6, :] * acc


def _ffn(sq, xs, g, modtab, lw):
    d = xs.shape[-1]
    consts = [lw["ffn_w1"], lw["ffn_w3"], lw["ffn_w2"]]
    h_chunk = _largest_divisor(lw["ffn_w1"].shape[1], (512, 256, 128))
    return pl.pallas_call(
        functools.partial(_ffn_body, h_chunk=h_chunk),
        grid=sq.grid,
        in_specs=[sq.rows(d), sq.const((1, d)), sq.mod(d)] + [sq.const(c.shape) for c in consts],
        out_specs=sq.rows(d),
        out_shape=jax.ShapeDtypeStruct(xs.shape, F32),
        compiler_params=_params(2),
        name="ffn",
    )(xs, g.reshape(1, d), modtab, *consts)


def _block_diag_ones(n, seg):
    idx = jnp.arange(n) // seg
    return (idx[:, None] == idx[None, :]).astype(BF16)


def _block_diag2(m):
    z = jnp.zeros_like(m[0])
    return jnp.concatenate([jnp.concatenate([m[0], z], axis=1), jnp.concatenate([z, m[1]], axis=1)], axis=0)


def _pad_to(x, axis, size):
    pad = [(0, 0)] * x.ndim
    pad[axis] = (0, size - x.shape[axis])
    return jnp.pad(x, pad)


def _rope_table(ctx_len, seq_len, hdim, kv_heads):
    rows = seq_len // GRID_W
    row = jnp.repeat(jnp.arange(rows), GRID_W).astype(F32)
    col = jnp.tile(jnp.arange(GRID_W), rows).astype(F32)
    half = hdim // 2
    inv = ROPE_THETA ** (-jnp.arange(0, half, 2, dtype=F32) / half)
    ang = jnp.concatenate([row[:, None] * inv, col[:, None] * inv], axis=-1)
    cos = jnp.repeat(jnp.cos(ang), 2, axis=1)
    sin = jnp.repeat(jnp.sin(ang), 2, axis=1) * jnp.tile(jnp.array([-1.0, 1.0], F32), half)
    cos = jnp.concatenate([jnp.ones((ctx_len, hdim), F32), cos], axis=0)
    sin = jnp.concatenate([jnp.zeros((ctx_len, hdim), F32), sin], axis=0)
    return jnp.concatenate([jnp.tile(cos, (1, kv_heads)), jnp.tile(sin, (1, kv_heads))], axis=1)


def kernel(x, c, ctx, c_ctx, ada_w, ada_b, norm1, norm2, w_in, rwkv_mu_x, rwkv_mu_rkv, rwkv_w0, rwkv_w1, rwkv_w2, rwkv_a0, rwkv_a1, rwkv_a2, rwkv_g1, rwkv_g2, rwkv_k_k, rwkv_k_a, rwkv_r_k, rwkv_lnx_w, rwkv_lnx_b, gdn_conv, gdn_w_alpha, gdn_dt_bias, gdn_A_log, gdn_w_beta, gdn_norm, attn_q_norm, attn_k_norm, w_up_a, w_up_b, w_up_c, w_gate, b_gate, w_out, ffn_w1, ffn_w3, ffn_w2, final_norm):
    batch, seq_len, d = x.shape
    ctx_len = ctx.shape[1]
    depth = ada_w.shape[0]
    sq = _Seq(batch, ctx_len, seq_len)

    r_heads, r_hdim = rwkv_r_k.shape[1], rwkv_r_k.shape[2]
    rdim = r_heads * r_hdim
    g_heads, g_hdim = gdn_w_alpha.shape[-1], gdn_norm.shape[-1]
    gdim = g_heads * g_hdim
    a_hdim = attn_q_norm.shape[-1]
    qd = w_up_c.shape[1]
    q_heads = qd // a_hdim
    kvd = (w_in.shape[-1] - 3 * rdim - 4 * gdim - qd) // 2
    kv_heads = kvd // a_hdim
    assert rdim == gdim == qd and r_hdim == a_hdim, "lane-segment constants are shared between mixers"
    assert (3 * rdim) % (3 * gdim) == 0 and (3 * rdim + 3 * gdim) % gdim == 0
    gdn_col = (3 * rdim) // (3 * gdim)
    z_col = (3 * rdim + 3 * gdim) // gdim
    q_col = (3 * rdim + 4 * gdim) // qd
    assert (3 * rdim + 4 * gdim + qd) % (2 * kvd) == 0
    kv_col = (3 * rdim + 4 * gdim + qd) // (2 * kvd)

    bd_r = _block_diag_ones(rdim, r_hdim)
    bd_g = _block_diag_ones(gdim, g_hdim)
    cs_tab = _rope_table(ctx_len, seq_len, a_hdim, kv_heads)
    n_beta = 2 * g_heads
    assert 2 * n_beta <= LANES

    cond = jnp.concatenate([c, c_ctx[None, :]], axis=0)
    cond = _pad_to(cond, 0, -(-(batch + 1) // SUBLANES) * SUBLANES)

    xs = jnp.concatenate([ctx, x], axis=1)
    m_rows = batch * sq.S
    for l in range(depth):
        g1w = _pad_to(rwkv_g1[l], 1, 2 * LANES)
        lw = {
            "mu": rwkv_mu_x[l], "mu_rkv": rwkv_mu_rkv[l],
            "w1": jnp.concatenate([rwkv_w1[l, 0], rwkv_w1[l, 1]], axis=1).astype(BF16),
            "w2": _block_diag2(rwkv_w2[l]).astype(BF16),
            "w0": rwkv_w0[l].reshape(1, 2 * rdim),
            "a1": jnp.concatenate([rwkv_a1[l, 0], rwkv_a1[l, 1]], axis=1).astype(BF16),
            "a2": _block_diag2(rwkv_a2[l]).astype(BF16),
            "a0": rwkv_a0[l].reshape(1, 2 * rdim),
            "g1": g1w.astype(BF16),
            "g2": _pad_to(rwkv_g2[l], 0, 2 * LANES).astype(BF16),
            "wab": _pad_to(jnp.concatenate([gdn_w_beta[l, 0], gdn_w_beta[l, 1],
                                            gdn_w_alpha[l, 0], gdn_w_alpha[l, 1]], axis=1), 1, LANES).astype(BF16),
            "abb": _pad_to(jnp.concatenate([jnp.zeros((n_beta,), F32), gdn_dt_bias[l].reshape(-1)]), 0, LANES).reshape(1, LANES),
            "alog": _pad_to(jnp.concatenate([jnp.zeros((n_beta,), F32), gdn_A_log[l].reshape(-1)]), 0, LANES).reshape(1, LANES),
            "n_beta": n_beta,
            "k_k": rwkv_k_k[l].reshape(1, rdim), "k_a": rwkv_k_a[l].reshape(1, rdim),
            "r_k": rwkv_r_k[l].reshape(1, rdim),
            "lnx_w": rwkv_lnx_w[l].reshape(1, rdim), "lnx_b": rwkv_lnx_b[l].reshape(1, rdim),
            "bd_r": bd_r, "bd_g": bd_g,
            "conv": jnp.transpose(gdn_conv[l]),
            "gdn_norm": jnp.tile(gdn_norm[l], g_heads).reshape(1, gdim),
            "q_norm": jnp.tile(attn_q_norm[l], q_heads).reshape(1, qd),
            "k_norm": jnp.tile(attn_k_norm[l], kv_heads).reshape(1, kvd),
            "w_up_a": w_up_a[l].astype(BF16), "w_up_b": w_up_b[l].astype(BF16),
            "w_up_c": w_up_c[l].astype(BF16), "w_out": w_out[l].astype(BF16),
            "ffn_w1": ffn_w1[l].astype(BF16), "ffn_w3": ffn_w3[l].astype(BF16), "ffn_w2": ffn_w2[l].astype(BF16),
        }
        mod = _matmul(cond, ada_w[l].astype(BF16), bias=ada_b[l], pre_act="silu")
        mod_x = mod[:batch].reshape(batch, 6, d)
        mod_c = jnp.broadcast_to(mod[batch].reshape(1, 6, d), (batch, 6, d))
        modtab = jnp.stack([mod_c, mod_x], axis=1)

        h = _normmod(sq, xs, norm1[l], modtab, shift_row=0, scale_row=1)
        h2 = h.reshape(m_rows, d)
        proj = _matmul(h2, w_in[l].astype(BF16)).reshape(batch, sq.S, -1)
        gates = _matmul(h2, w_gate[l].astype(BF16), bias=b_gate[l], act="sigmoid").reshape(batch, sq.S, -1)
        lw_dec, a_iclr, g_out, ab = _lora(sq, h, lw)

        r, v, aa, kd, bb = _rwkv_prep(sq, proj, a_iclr, lw)
        o_r = _rwkv_scan(sq, r, v, aa, kd, lw_dec, bb, r_heads)
        ya = _rwkv_post(sq, o_r, r, v, kd, g_out, lw, r_heads)

        gq, gk, gv = _gdn_prep(sq, proj, lw, gdim, gdn_col)
        o_g = _gdn_scan(sq, gq, gk, gv, ab, g_heads)
        yb = _gdn_post(sq, o_g, proj, lw, g_heads, z_col)

        aq, ak, av = _attn_prep(sq, proj, cs_tab, lw, q_heads, kv_heads, a_hdim, q_col, kv_col)
        yc = _attention(sq, aq, ak, av)

        xs = _merge(sq, xs, ya, yb, yc, gates, modtab, lw)
        xs = _ffn(sq, xs, norm2[l], modtab, lw)
    return _final_norm(sq, xs, final_norm)
```

```python
import functools

import jax
import jax.numpy as jnp
from jax import lax
from jax.experimental import pallas as pl
from jax.experimental.pallas import tpu as pltpu

F32 = jnp.float32
BF16 = jnp.bfloat16

RMS_EPS = 1e-6
RWKV_LNX_EPS = 64e-5
ROPE_THETA = 10000.0
GRID_W = 64

SUBLANES = 8
LANES = 128
CHUNK = 64
MAX_TOKEN_BLOCK = 256
MAX_MM_ROWS = 512
KEY_BLOCK = 256
RWKV_PASSES = 1
GDN_PASSES = 1
INV_PASSES = 1
VMEM_LIMIT = 56 * 1024 * 1024


def _sigmoid(x):
    return 1.0 / (1.0 + jnp.exp(-x))


def _softplus(x):
    return jnp.maximum(x, 0.0) + jnp.log(1.0 + jnp.exp(-jnp.abs(x)))


def _split2(x):
    hi = x.astype(BF16)
    lo = (x - hi.astype(F32)).astype(BF16)
    return hi, lo


def _mm(a, b, dims, passes):
    d = functools.partial(lax.dot_general, dimension_numbers=(dims, ((), ())), preferred_element_type=F32)
    if passes == 1:
        return d(a.astype(BF16), b.astype(BF16))
    ah, al = _split2(a)
    bh, bl = _split2(b)
    return d(ah, bh) + (d(ah, bl) + d(al, bh))


def _dot(a, b, passes=1):
    return _mm(a, b, ((1,), (0,)), passes)


def _dot_nt(a, b, passes=1):
    return _mm(a, b, ((1,), (1,)), passes)


def _dot_tn(a, b, passes=1):
    return _mm(a, b, ((0,), (0,)), passes)


def _split3(x):
    hi = x.astype(BF16)
    r1 = x - hi.astype(F32)
    mid = r1.astype(BF16)
    lo = (r1 - mid.astype(F32)).astype(BF16)
    return hi, mid, lo


def _dot_exact_lhs(m01, x):
    d = functools.partial(jnp.dot, preferred_element_type=F32)
    hi, mid, lo = _split3(x)
    return d(m01, hi) + (d(m01, mid) + d(m01, lo))


def _segsum(x, bd_ref):
    d = functools.partial(jnp.dot, preferred_element_type=F32)
    hi, lo = _split2(x)
    bd = bd_ref[...]
    return d(hi, bd) + d(lo, bd)


def _tri_inverse(l, eye):
    n = l.shape[0]
    row = lax.broadcasted_iota(jnp.int32, (n, n), 0)
    col = lax.broadcasted_iota(jnp.int32, (n, n), 1)
    t = eye - jnp.where((row >> 1) == (col >> 1), l, 0.0)
    for k in range(2, n.bit_length()):
        off = ((row >> k) == (col >> k)) & ((row >> (k - 1)) != (col >> (k - 1)))
        t = t - _dot(_dot(t, jnp.where(off, l, 0.0), INV_PASSES), t, INV_PASSES)
    return t


def _row_iota(shape):
    return lax.broadcasted_iota(jnp.int32, shape, 0)


def _shift_rows(x, prev_ref, next_ref, k, first, last):
    n = x.shape[0]
    row = _row_iota(x.shape)
    y = pltpu.roll(x, (-k) % n, axis=0)
    if k < 0:
        for i in range(-k):
            edge = prev_ref[SUBLANES + k + i:SUBLANES + k + i + 1, :]
            edge = jnp.where(first, 0.0, edge)
            y = jnp.where(row == i, edge, y)
    else:
        for i in range(k):
            edge = next_ref[i:i + 1, :]
            edge = jnp.where(last, 0.0, edge)
            y = jnp.where(row == n - k + i, edge, y)
    return y


def _largest_divisor(n, candidates):
    for c in candidates:
        if n % c == 0:
            return c
    raise ValueError(f"no block size among {candidates} divides {n}")


class _Seq:
    def __init__(self, batch, ctx_len, seq_len):
        self.B = batch
        self.ctx = ctx_len
        self.S = ctx_len + seq_len
        self.TB = _largest_divisor(ctx_len, (MAX_TOKEN_BLOCK, 128, 64))
        assert seq_len % self.TB == 0 and self.TB % CHUNK == 0
        self.ncb = ctx_len // self.TB
        self.nblk = self.S // self.TB
        self.ncc = ctx_len // CHUNK
        self.nchunk = self.S // CHUNK
        self.grid = (batch, self.nblk)

    def rows(self, width, col_block=0):
        return pl.BlockSpec((None, self.TB, width), lambda b, j: (b, j, col_block))

    def rows2(self, width):
        return pl.BlockSpec((2, None, self.TB, width), lambda b, j: (0, b, j, 0))

    def heads(self, n_heads, width):
        return pl.BlockSpec((None, n_heads, self.TB, width), lambda b, j: (b, 0, j, 0))

    def prev_rows(self, width, col_block=0):
        per = self.TB // SUBLANES
        return pl.BlockSpec((None, SUBLANES, width),
                            lambda b, j: (b, jnp.maximum(j * per - 1, 0), col_block))

    def next_rows(self, width, col_block=0):
        per = self.TB // SUBLANES
        top = self.S // SUBLANES - 1
        return pl.BlockSpec((None, SUBLANES, width),
                            lambda b, j: (b, jnp.minimum((j + 1) * per, top), col_block))

    def const(self, shape):
        zeros = (0,) * len(shape)
        return pl.BlockSpec(shape, lambda b, j: zeros)

    def mod(self, d_model):
        ncb = self.ncb
        return pl.BlockSpec((None, None, 6, d_model),
                            lambda b, j: (b, (j >= ncb).astype(jnp.int32), 0, 0))

    def edges(self):
        j = pl.program_id(1)
        first = (j == 0) | (j == self.ncb)
        last = (j == self.ncb - 1) | (j == self.nblk - 1)
        return first, last


def _params(n_axes):
    return pltpu.CompilerParams(dimension_semantics=("arbitrary",) * n_axes,
                                vmem_limit_bytes=VMEM_LIMIT)


def _mm_body(*refs, act, pre_act, n_chunk, has_bias):
    if has_bias:
        x_ref, w_ref, b_ref, o_ref = refs
    else:
        x_ref, w_ref, o_ref = refs
        b_ref = None
    x = x_ref[...]
    if pre_act == "silu":
        x = x * _sigmoid(x)
    xb = x.astype(BF16)
    n = o_ref.shape[-1]
    for n0 in range(0, n, n_chunk):
        y = jnp.dot(xb, w_ref[:, n0:n0 + n_chunk], preferred_element_type=F32)
        if b_ref is not None:
            y = y + b_ref[:, n0:n0 + n_chunk]
        if act == "sigmoid":
            y = _sigmoid(y)
        o_ref[:, n0:n0 + n_chunk] = y.astype(o_ref.dtype)


def _matmul(x, w, bias=None, act=None, pre_act=None, out_dtype=F32):
    m, k = x.shape
    n = w.shape[1]
    tm = m if m <= MAX_MM_ROWS else _largest_divisor(m, (MAX_MM_ROWS, 256, 128, 64, 32, 16, 8))
    n_chunk = _largest_divisor(n, (512, 256, 128))
    in_specs = [pl.BlockSpec((tm, k), lambda i: (i, 0)),
                pl.BlockSpec((k, n), lambda i: (0, 0))]
    args = [x, w]
    if bias is not None:
        in_specs.append(pl.BlockSpec((1, n), lambda i: (0, 0)))
        args.append(bias.reshape(1, n))
    return pl.pallas_call(
        functools.partial(_mm_body, act=act, pre_act=pre_act, n_chunk=n_chunk,
                          has_bias=bias is not None),
        grid=(m // tm,),
        in_specs=in_specs,
        out_specs=pl.BlockSpec((tm, n), lambda i: (i, 0)),
        out_shape=jax.ShapeDtypeStruct((m, n), out_dtype),
        compiler_params=_params(1),
        name="matmul",
    )(*args)


def _normmod_body(x_ref, g_ref, mod_ref, o_ref, *, shift_row, scale_row):
    x = x_ref[...]
    y = x * lax.rsqrt(jnp.mean(x * x, axis=-1, keepdims=True) + RMS_EPS)
    y = y * g_ref[...]
    o_ref[...] = y * (1.0 + mod_ref[scale_row:scale_row + 1, :]) + mod_ref[shift_row:shift_row + 1, :]


def _normmod(sq, xs, g, modtab, shift_row, scale_row):
    d = xs.shape[-1]
    return pl.pallas_call(
        functools.partial(_normmod_body, shift_row=shift_row, scale_row=scale_row),
        grid=sq.grid,
        in_specs=[sq.rows(d), sq.const((1, d)), sq.mod(d)],
        out_specs=sq.rows(d),
        out_shape=jax.ShapeDtypeStruct(xs.shape, F32),
        compiler_params=_params(2),
        name="normmod",
    )(xs, g.reshape(1, d), modtab)


def _final_norm_body(x_ref, g_ref, o_ref):
    x = x_ref[...]
    y = x * lax.rsqrt(jnp.mean(x * x, axis=-1, keepdims=True) + RMS_EPS)
    o_ref[...] = y * g_ref[...]


def _final_norm(sq, xs, g):
    d = xs.shape[-1]
    ncb = sq.ncb
    return pl.pallas_call(
        _final_norm_body,
        grid=(sq.B, sq.nblk - ncb),
        in_specs=[pl.BlockSpec((None, sq.TB, d), lambda b, j: (b, j + ncb, 0)),
                  pl.BlockSpec((1, d), lambda b, j: (0, 0))],
        out_specs=pl.BlockSpec((None, sq.TB, d), lambda b, j: (b, j, 0)),
        out_shape=jax.ShapeDtypeStruct((sq.B, sq.S - sq.ctx, d), F32),
        compiler_params=_params(2),
        name="final_norm",
    )(xs, g.reshape(1, d))


def _lora_body(h_ref, hp_ref, hn_ref, mu_ref, w1_ref, w2_ref, w0_ref, a1_ref, a2_ref, a0_ref,
               g1_ref, g2_ref, wab_ref, abb_ref, alog_ref,
               lw_ref, a_ref, g_ref, ab_ref, *, sq, rdim, n_beta):
    first, last = sq.edges()
    h = h_ref[...]
    nb = 0.5 * (_shift_rows(h, hp_ref, hn_ref, -1, first, last)
                + _shift_rows(h, hp_ref, hn_ref, 1, first, last))
    dlt = nb - h
    xw = h + dlt * mu_ref[0:1, :]
    xa = h + dlt * mu_ref[1:2, :]
    xg = h + dlt * mu_ref[2:3, :]
    wl = w0_ref[...] + _dot(jnp.tanh(_dot(xw, w1_ref[...])), w2_ref[...])
    w_log = -_softplus(-wl) - 0.5
    lw = -jnp.exp(w_log)
    lw_ref[0] = lw[:, :rdim]
    lw_ref[1] = lw[:, rdim:]
    a = _sigmoid(a0_ref[...] + _dot(_dot(xa, a1_ref[...]), a2_ref[...]))
    a_ref[0] = a[:, :rdim]
    a_ref[1] = a[:, rdim:]
    g_ref[...] = _dot(_sigmoid(_dot(xg, g1_ref[...])), g2_ref[...])
    z = _dot(h, wab_ref[...])
    col = lax.broadcasted_iota(jnp.int32, z.shape, 1)
    gl = -jnp.exp(alog_ref[...]) * _softplus(z + abb_ref[...])
    ab_ref[...] = jnp.where(col < n_beta, _sigmoid(z), gl)


def _lora(sq, h, lw):
    d = h.shape[-1]
    rdim = lw["w0"].shape[-1] // 2
    consts = [lw["mu"], lw["w1"], lw["w2"], lw["w0"], lw["a1"], lw["a2"], lw["a0"],
              lw["g1"], lw["g2"], lw["wab"], lw["abb"], lw["alog"]]
    bsd = (sq.B, sq.S)
    return pl.pallas_call(
        functools.partial(_lora_body, sq=sq, rdim=rdim, n_beta=lw["n_beta"]),
        grid=sq.grid,
        in_specs=[sq.rows(d), sq.prev_rows(d), sq.next_rows(d)] + [sq.const(c.shape) for c in consts],
        out_specs=[sq.rows2(rdim), sq.rows2(rdim), sq.rows(rdim), sq.rows(LANES)],
        out_shape=[jax.ShapeDtypeStruct((2,) + bsd + (rdim,), F32),
                   jax.ShapeDtypeStruct((2,) + bsd + (rdim,), F32),
                   jax.ShapeDtypeStruct(bsd + (rdim,), F32),
                   jax.ShapeDtypeStruct(bsd + (LANES,), F32)],
        compiler_params=_params(2),
        name="lora",
    )(h, h, h, *consts)


def _rwkv_prep_body(p_ref, pp_ref, pn_ref, a_ref, mu_ref, kk_ref, ka_ref, bd_ref,
                    r_ref, v_ref, aa_ref, kd_ref, bb_ref, *, sq, rdim):
    first, last = sq.edges()
    p = p_ref[...]
    nb = 0.5 * (_shift_rows(p, pp_ref, pn_ref, -1, first, last)
                + _shift_rows(p, pp_ref, pn_ref, 1, first, last))
    dlt = nb - p
    r = p[:, :rdim] + dlt[:, :rdim] * mu_ref[0:1, :]
    k = p[:, rdim:2 * rdim] + dlt[:, rdim:2 * rdim] * mu_ref[1:2, :]
    v = p[:, 2 * rdim:] + dlt[:, 2 * rdim:] * mu_ref[2:3, :]
    t = k * kk_ref[...]
    kk = t * lax.rsqrt(_segsum(t * t, bd_ref) + RMS_EPS)
    r_ref[...] = r
    v_ref[...] = v
    aa_ref[...] = -kk
    for d in range(2):
        a = a_ref[d]
        kd_ref[d] = k * (1.0 + (a - 1.0) * ka_ref[...])
        bb_ref[d] = kk * a


def _rwkv_prep(sq, proj, a, lw):
    rdim = a.shape[-1]
    consts = [lw["mu_rkv"], lw["k_k"], lw["k_a"], lw["bd_r"]]
    one = jax.ShapeDtypeStruct((sq.B, sq.S, rdim), F32)
    two = jax.ShapeDtypeStruct((2, sq.B, sq.S, rdim), F32)
    return pl.pallas_call(
        functools.partial(_rwkv_prep_body, sq=sq, rdim=rdim),
        grid=sq.grid,
        in_specs=[sq.rows(3 * rdim), sq.prev_rows(3 * rdim), sq.next_rows(3 * rdim), sq.rows2(rdim)]
        + [sq.const(c.shape) for c in consts],
        out_specs=[sq.rows(rdim), sq.rows(rdim), sq.rows(rdim), sq.rows2(rdim), sq.rows2(rdim)],
        out_shape=[one, one, one, two, two],
        compiler_params=_params(2),
        name="rwkv_prep",
    )(proj, proj, proj, a, *consts)


def _chunk_index(d, i, ncc, nchunk):
    back = jnp.where(i < ncc, ncc - 1 - i, nchunk + ncc - 1 - i)
    return jnp.where(d == 0, i, back)


def _causal_masks(d, c, reps=1):
    row = lax.broadcasted_iota(jnp.int32, (c, reps * c), 0)
    col = lax.broadcasted_iota(jnp.int32, (c, reps * c), 1) & (c - 1)
    delta = jnp.where(d == 0, row - col, col - row)
    eye = jnp.where(row == col, 1.0, 0.0).astype(F32)
    return delta >= 0, delta > 0, eye


def _stacked_mask(d, c):
    row = lax.broadcasted_iota(jnp.int32, (2 * c, c), 0)
    col = lax.broadcasted_iota(jnp.int32, (2 * c, c), 1)
    rr = row & (c - 1)
    delta = jnp.where(d == 0, rr - col, col - rr)
    return (delta > 0) | ((row >= c) & (delta == 0))


def _tri_inverse_many(ls, eye):
    n = ls[0].shape[0]
    row = lax.broadcasted_iota(jnp.int32, (n, n), 0)
    col = lax.broadcasted_iota(jnp.int32, (n, n), 1)
    same = (row >> 1) == (col >> 1)
    ts = [eye - jnp.where(same, l, 0.0) for l in ls]
    for k in range(2, n.bit_length()):
        off = ((row >> k) == (col >> k)) & ((row >> (k - 1)) != (col >> (k - 1)))
        tl = [_dot(t, jnp.where(off, l, 0.0), INV_PASSES) for t, l in zip(ts, ls)]
        ts = [t - _dot(x, t, INV_PASSES) for x, t in zip(tl, ts)]
    return ts


def _rwkv_intra_body(r_ref, v_ref, aa_ref, kd_ref, lw_ref, bb_ref, f_ref, g_ref, *, heads, hdim):
    d = pl.program_id(1)
    c = CHUNK
    incl, _, eye = _causal_masks(d, c)
    mask2 = _stacked_mask(d, c)
    lw = lw_ref[...]
    cum = _dot_exact_lhs(jnp.where(incl, 1.0, 0.0).astype(BF16), lw)
    tot = jnp.sum(lw, axis=0, keepdims=True)
    e_neg = jnp.exp(-cum)
    a_t = aa_ref[...] * jnp.exp(cum - lw)
    r_t = r_ref[...] * jnp.exp(cum)
    v = v_ref[...]
    b_tt = jnp.transpose(bb_ref[...] * e_neg)
    k_tt = jnp.transpose(kd_ref[...] * e_neg)
    e_col = jnp.transpose(jnp.broadcast_to(jnp.exp(tot), lw.shape))
    hs = range(heads)
    sl = [slice(h * hdim, (h + 1) * hdim) for h in hs]
    ar = [jnp.concatenate([a_t[:, sl[h]], r_t[:, sl[h]]], axis=0) for h in hs]
    vh = [v[:, sl[h]] for h in hs]
    mb = [jnp.where(mask2, _dot(ar[h], b_tt[sl[h]], RWKV_PASSES), 0.0) for h in hs]
    mk = [jnp.where(mask2, _dot(ar[h], k_tt[sl[h]], RWKV_PASSES), 0.0) for h in hs]
    x = [_dot(mk[h], vh[h], RWKV_PASSES) for h in hs]
    h0 = [_dot(e_col[sl[h]] * k_tt[sl[h]], vh[h], RWKV_PASSES) for h in hs]
    t = _tri_inverse_many([-mb[h][:c] for h in hs], eye)
    wt = [_dot(t[h], a_t[:, sl[h]], RWKV_PASSES) for h in hs]
    ut = [_dot(t[h], x[h][:c], RWKV_PASSES) for h in hs]
    for h in hs:
        f_ref[h, 0:c] = ut[h]
        f_ref[h, c:2 * c] = x[h][c:]
        f_ref[h, 2 * c:3 * c] = h0[h]
        f_ref[h, 3 * c:4 * c] = e_col[sl[h]]
        g_ref[h, 0:c] = wt[h].astype(BF16)
        g_ref[h, c:2 * c] = r_t[:, sl[h]].astype(BF16)
        g_ref[h, 2 * c:3 * c] = mb[h][c:].astype(BF16)
        g_ref[h, 3 * c:4 * c] = (e_col[sl[h]] * b_tt[sl[h]]).astype(BF16)


def _rwkv_intra(sq, r, v, aa, kd, lw, bb, heads):
    rdim = r.shape[-1]
    hdim = rdim // heads
    assert hdim == CHUNK, "the hand-over tiles share one [CHUNK, head_dim] shape"
    shared = pl.BlockSpec((None, CHUNK, rdim), lambda b, d, i: (b, i, 0))
    per_dir = pl.BlockSpec((None, None, CHUNK, rdim), lambda b, d, i: (d, b, i, 0))
    out = pl.BlockSpec((None, None, heads, 4 * CHUNK, hdim), lambda b, d, i: (d, b, 0, i, 0))
    shape = (2, sq.B, heads, sq.nchunk * 4 * CHUNK, hdim)
    return pl.pallas_call(
        functools.partial(_rwkv_intra_body, heads=heads, hdim=hdim),
        grid=(sq.B, 2, sq.nchunk),
        in_specs=[shared, shared, shared, per_dir, per_dir, per_dir],
        out_specs=[out, out],
        out_shape=[jax.ShapeDtypeStruct(shape, F32), jax.ShapeDtypeStruct(shape, BF16)],
        compiler_params=_params(3),
        name="rwkv_intra",
    )(r, v, aa, kd, lw, bb)


def _rwkv_scan_body(f0_ref, g0_ref, f1_ref, g1_ref, of_ref, ob_ref, h_ref, *, heads, hdim):
    @pl.when(pl.program_id(1) == 0)
    def _():
        h_ref[...] = jnp.zeros_like(h_ref)

    c = CHUNK
    fs, gs, outs = (f0_ref, f1_ref), (g0_ref, g1_ref), (of_ref, ob_ref)
    chains = [(d, h) for d in range(2) for h in range(heads)]
    dot = functools.partial(jnp.dot, preferred_element_type=F32)
    st = {ch: h_ref[ch[0], ch[1]] for ch in chains}
    m2 = {(d, h): dot(gs[d][h, 0:2 * c, :], st[d, h].astype(BF16)) for d, h in chains}
    ub = {(d, h): (fs[d][h, 0:c, :] + m2[d, h][:c]).astype(BF16) for d, h in chains}
    for d, h in chains:
        y = fs[d][h, c:2 * c, :] + m2[d, h][c:] + dot(gs[d][h, 2 * c:3 * c, :], ub[d, h])
        outs[d][:, h * hdim:(h + 1) * hdim] = y
        h_ref[d, h] = (fs[d][h, 3 * c:4 * c, :] * st[d, h] + fs[d][h, 2 * c:3 * c, :]
                       + dot(gs[d][h, 3 * c:4 * c, :], ub[d, h]))


def _backward_chunk(i, ncc, nchunk):
    return jnp.where(i < ncc, ncc - 1 - i, nchunk + ncc - 1 - i)


def _scan_specs(sq, heads, rows, width):
    ncc, nchunk = sq.ncc, sq.nchunk
    fwd = pl.BlockSpec((None, None, heads, rows, width), lambda b, i: (0, b, 0, i, 0))
    bwd = pl.BlockSpec((None, None, heads, rows, width),
                       lambda b, i: (1, b, 0, _backward_chunk(i, ncc, nchunk), 0))
    return fwd, bwd


def _scan_out_specs(sq, width):
    ncc, nchunk = sq.ncc, sq.nchunk
    fwd = pl.BlockSpec((None, CHUNK, width), lambda b, i: (b, i, 0))
    bwd = pl.BlockSpec((None, CHUNK, width), lambda b, i: (b, _backward_chunk(i, ncc, nchunk), 0))
    return [fwd, bwd]


def _rwkv_scan(sq, f, g, heads):
    hdim = f.shape[-1]
    rdim = heads * hdim
    ff, fb = _scan_specs(sq, heads, 4 * CHUNK, hdim)
    one = jax.ShapeDtypeStruct((sq.B, sq.S, rdim), F32)
    return pl.pallas_call(
        functools.partial(_rwkv_scan_body, heads=heads, hdim=hdim),
        grid=(sq.B, sq.nchunk),
        in_specs=[ff, ff, fb, fb],
        out_specs=_scan_out_specs(sq, rdim),
        out_shape=[one, one],
        scratch_shapes=[pltpu.VMEM((2, heads, hdim, hdim), F32)],
        compiler_params=_params(2),
        name="rwkv_scan",
    )(f, g, f, g)


def _rwkv_post_body(of_ref, ob_ref, r_ref, v_ref, kd_ref, g_ref, rk_ref, lnw_ref, lnb_ref, bd_ref, y_ref, *, hdim):
    o = of_ref[...] + ob_ref[...]
    inv_n = 1.0 / hdim
    mean = _segsum(o, bd_ref) * inv_n
    cen = o - mean
    var = _segsum(cen * cen, bd_ref) * inv_n
    o = cen * lax.rsqrt(var + RWKV_LNX_EPS) * lnw_ref[...] + lnb_ref[...]
    r = r_ref[...]
    v = v_ref[...]
    rk = rk_ref[...]
    for d in range(2):
        o = o + _segsum(r * kd_ref[d] * rk, bd_ref) * v
    y_ref[...] = o * g_ref[...]


def _rwkv_post(sq, o_f, o_b, r, v, kd, g, lw, heads):
    rdim = r.shape[-1]
    consts = [lw["r_k"], lw["lnx_w"], lw["lnx_b"], lw["bd_r"]]
    return pl.pallas_call(
        functools.partial(_rwkv_post_body, hdim=rdim // heads),
        grid=sq.grid,
        in_specs=[sq.rows(rdim), sq.rows(rdim), sq.rows(rdim), sq.rows(rdim), sq.rows2(rdim), sq.rows(rdim)]
        + [sq.const(c.shape) for c in consts],
        out_specs=sq.rows(rdim),
        out_shape=jax.ShapeDtypeStruct((sq.B, sq.S, rdim), F32),
        compiler_params=_params(2),
        name="rwkv_post",
    )(o_f, o_b, r, v, kd, g, *consts)


def _gdn_prep_body(p_ref, pp_ref, pn_ref, cw_ref, bd_ref, q_ref, k_ref, v_ref, *, sq, gdim):
    first, last = sq.edges()
    p = p_ref[...]
    width = cw_ref.shape[0]
    half = width // 2
    acc = p * cw_ref[half:half + 1, :]
    for j in range(width):
        if j != half:
            acc = acc + _shift_rows(p, pp_ref, pn_ref, j - half, first, last) * cw_ref[j:j + 1, :]
    y = acc * _sigmoid(acc)
    q = y[:, :gdim]
    k = y[:, gdim:2 * gdim]
    q_ref[...] = q * lax.rsqrt(_segsum(q * q, bd_ref) + RMS_EPS)
    k_ref[...] = k * lax.rsqrt(_segsum(k * k, bd_ref) + RMS_EPS)
    v_ref[...] = y[:, 2 * gdim:]


def _gdn_prep(sq, proj, lw, gdim, col_block):
    consts = [lw["conv"], lw["bd_g"]]
    one = jax.ShapeDtypeStruct((sq.B, sq.S, gdim), F32)
    w = 3 * gdim
    return pl.pallas_call(
        functools.partial(_gdn_prep_body, sq=sq, gdim=gdim),
        grid=sq.grid,
        in_specs=[sq.rows(w, col_block), sq.prev_rows(w, col_block), sq.next_rows(w, col_block)]
        + [sq.const(c.shape) for c in consts],
        out_specs=[sq.rows(gdim)] * 3,
        out_shape=[one, one, one],
        compiler_params=_params(2),
        name="gdn_prep",
    )(proj, proj, proj, *consts)


def _gdn_intra_body(q_ref, k_ref, v_ref, ab_ref, f_ref, ga_ref, gb_ref, *, heads, hdim):
    d = pl.program_id(1)
    c = CHUNK
    incl, strict, eye = _causal_masks(d, c)
    ab = ab_ref[...]
    gc_all = _dot_exact_lhs(jnp.where(incl, 1.0, 0.0).astype(BF16), ab)
    gc_all_t = jnp.transpose(gc_all)
    tot_all = jnp.sum(ab, axis=0, keepdims=True)
    lane = lax.broadcasted_iota(jnp.int32, ab.shape, 1)
    sub = lax.broadcasted_iota(jnp.int32, gc_all_t.shape, 0)
    k_all = k_ref[...]
    k_tt = jnp.transpose(k_all)
    scale = hdim ** -0.5
    hs = range(heads)
    sl = [slice(h * hdim, (h + 1) * hdim) for h in hs]
    beta, gc, g_last, gc_row, decay, kb, kq = [], [], [], [], [], [], []
    for h in hs:
        pick_b = lane == d * heads + h
        pick_g = lane == 2 * heads + d * heads + h
        beta.append(jnp.sum(jnp.where(pick_b, ab, 0.0), axis=1, keepdims=True))
        gc.append(jnp.sum(jnp.where(pick_g, gc_all, 0.0), axis=1, keepdims=True))
        g_last.append(jnp.sum(jnp.where(pick_g[:1], tot_all, 0.0), axis=1, keepdims=True))
        gc_row.append(jnp.sum(jnp.where(sub == 2 * heads + d * heads + h, gc_all_t, 0.0),
                              axis=0, keepdims=True))
        diff = gc[h] - gc_row[h]
        decay.append(jnp.where(incl, jnp.exp(jnp.where(incl, diff, 0.0)), 0.0))
        kb.append(k_all[:, sl[h]] * beta[h])
        kq.append(jnp.concatenate([kb[h], q_ref[:, sl[h]] * scale], axis=0))
    m = [_dot(kq[h], k_tt[sl[h]], GDN_PASSES) for h in hs]
    t = _tri_inverse_many([jnp.where(strict, m[h][:c] * decay[h], 0.0) for h in hs], eye)
    e_gc = [jnp.exp(gc[h]) for h in hs]
    sol = [_dot(t[h], jnp.concatenate([v_ref[:, sl[h]] * beta[h], kb[h] * e_gc[h]], axis=1), GDN_PASSES)
           for h in hs]
    for h in hs:
        f_ref[h, 0:c] = sol[h][:, :hdim]
        f_ref[h, c:c + SUBLANES] = jnp.broadcast_to(jnp.exp(g_last[h]), (SUBLANES, hdim))
        ga_ref[h, 0:c] = sol[h][:, hdim:].astype(BF16)
        ga_ref[h, c:2 * c] = (kq[h][c:] * e_gc[h]).astype(BF16)
        gb_ref[h, 0:c] = (m[h][c:] * decay[h]).astype(BF16)
        gb_ref[h, c:c + hdim] = (k_tt[sl[h]] * jnp.exp(g_last[h] - gc_row[h])).astype(BF16)


def _gdn_intra(sq, q, k, v, ab, heads):
    gdim = q.shape[-1]
    hdim = gdim // heads
    shared = pl.BlockSpec((None, CHUNK, gdim), lambda b, d, i: (b, i, 0))
    small = pl.BlockSpec((None, CHUNK, LANES), lambda b, d, i: (b, i, 0))
    rows = (CHUNK + SUBLANES, 2 * CHUNK, CHUNK + hdim)
    widths = (hdim, hdim, CHUNK)
    dtypes = (F32, BF16, BF16)
    outs = [pl.BlockSpec((None, None, heads, r, w), lambda b, d, i: (d, b, 0, i, 0)) for r, w in zip(rows, widths)]
    shapes = [jax.ShapeDtypeStruct((2, sq.B, heads, sq.nchunk * r, w), dt) for r, w, dt in zip(rows, widths, dtypes)]
    return pl.pallas_call(
        functools.partial(_gdn_intra_body, heads=heads, hdim=hdim),
        grid=(sq.B, 2, sq.nchunk),
        in_specs=[shared, shared, shared, small],
        out_specs=outs,
        out_shape=shapes,
        compiler_params=_params(3),
        name="gdn_intra",
    )(q, k, v, ab)


def _gdn_scan_body(f0_ref, ga0_ref, gb0_ref, f1_ref, ga1_ref, gb1_ref, of_ref, ob_ref, s_ref, *, heads, hdim):
    @pl.when(pl.program_id(1) == 0)
    def _():
        s_ref[...] = jnp.zeros_like(s_ref)

    c = CHUNK
    fs, gas, gbs, outs = (f0_ref, f1_ref), (ga0_ref, ga1_ref), (gb0_ref, gb1_ref), (of_ref, ob_ref)
    chains = [(d, h) for d in range(2) for h in range(heads)]
    dot = functools.partial(jnp.dot, preferred_element_type=F32)
    st = {ch: s_ref[ch[0], ch[1]] for ch in chains}
    m = {(d, h): dot(gas[d][h], st[d, h].astype(BF16)) for d, h in chains}
    vn = {(d, h): (fs[d][h, 0:c, :] - m[d, h][:c]).astype(BF16) for d, h in chains}
    for d, h in chains:
        outs[d][:, h * hdim:(h + 1) * hdim] = m[d, h][c:] + dot(gbs[d][h, 0:c, :], vn[d, h])
        s_ref[d, h] = st[d, h] * fs[d][h, c:c + 1, :] + dot(gbs[d][h, c:c + hdim, :], vn[d, h])


def _gdn_scan(sq, f, ga, gb, heads):
    hdim = f.shape[-1]
    gdim = heads * hdim
    specs = []
    for arr in (f, ga, gb):
        specs.append(_scan_specs(sq, heads, arr.shape[3] // sq.nchunk, arr.shape[4]))
    one = jax.ShapeDtypeStruct((sq.B, sq.S, gdim), F32)
    return pl.pallas_call(
        functools.partial(_gdn_scan_body, heads=heads, hdim=hdim),
        grid=(sq.B, sq.nchunk),
        in_specs=[s[0] for s in specs] + [s[1] for s in specs],
        out_specs=_scan_out_specs(sq, gdim),
        out_shape=[one, one],
        scratch_shapes=[pltpu.VMEM((2, heads, hdim, hdim), F32)],
        compiler_params=_params(2),
        name="gdn_scan",
    )(f, ga, gb, f, ga, gb)


def _gdn_post_body(of_ref, ob_ref, z_ref, gn_ref, bd_ref, y_ref, *, hdim):
    o = of_ref[...] + ob_ref[...]
    ms = _segsum(o * o, bd_ref) * (1.0 / hdim)
    z = z_ref[...]
    y_ref[...] = o * lax.rsqrt(ms + RMS_EPS) * gn_ref[...] * (z * _sigmoid(z))


def _gdn_post(sq, o_f, o_b, proj, lw, heads, z_col_block):
    gdim = o_f.shape[-1]
    consts = [lw["gdn_norm"], lw["bd_g"]]
    return pl.pallas_call(
        functools.partial(_gdn_post_body, hdim=gdim // heads),
        grid=sq.grid,
        in_specs=[sq.rows(gdim), sq.rows(gdim), sq.rows(gdim, z_col_block)] + [sq.const(c.shape) for c in consts],
        out_specs=sq.rows(gdim),
        out_shape=jax.ShapeDtypeStruct((sq.B, sq.S, gdim), F32),
        compiler_params=_params(2),
        name="gdn_post",
    )(o_f, o_b, proj, *consts)


def _rope(x, cos, sin_signed):
    n = x.shape[-1]
    lane = lax.broadcasted_iota(jnp.int32, x.shape, 1)
    partner = jnp.where((lane & 1) == 0, pltpu.roll(x, n - 1, axis=1), pltpu.roll(x, 1, axis=1))
    return x * cos + partner * sin_signed


def _attn_prep_body(q_ref, kv_ref, cs_ref, qn_ref, kn_ref, bd_ref, qo_ref, ko_ref, vo_ref,
                    *, q_heads, kv_heads, hdim):
    kvd = kv_heads * hdim
    cos = cs_ref[:, :kvd]
    sin = cs_ref[:, kvd:]
    reps = q_heads // kv_heads
    cos_q = jnp.concatenate([cos] * reps, axis=1)
    sin_q = jnp.concatenate([sin] * reps, axis=1)
    inv_n = 1.0 / hdim
    q = q_ref[...]
    q = q * lax.rsqrt(_segsum(q * q, bd_ref) * inv_n + RMS_EPS) * qn_ref[...]
    q = _rope(q, cos_q, sin_q) * (hdim ** -0.5)
    kv = kv_ref[...]
    k = kv[:, :kvd]
    kbd = bd_ref[:kvd, :kvd]
    hi, lo = _split2(k * k)
    ms = (jnp.dot(hi, kbd, preferred_element_type=F32) + jnp.dot(lo, kbd, preferred_element_type=F32)) * inv_n
    k = k * lax.rsqrt(ms + RMS_EPS) * kn_ref[...]
    k = _rope(k, cos, sin)
    v = kv[:, kvd:]
    for h in range(q_heads):
        qo_ref[h] = q[:, h * hdim:(h + 1) * hdim].astype(BF16)
    for h in range(kv_heads):
        ko_ref[h] = k[:, h * hdim:(h + 1) * hdim].astype(BF16)
        vo_ref[h] = v[:, h * hdim:(h + 1) * hdim].astype(BF16)


def _attn_prep(sq, proj, cs_tab, lw, q_heads, kv_heads, hdim, q_col_block, kv_col_block):
    qd, kvd = q_heads * hdim, kv_heads * hdim
    consts = [lw["q_norm"], lw["k_norm"], lw["bd_r"]]
    return pl.pallas_call(
        functools.partial(_attn_prep_body, q_heads=q_heads, kv_heads=kv_heads, hdim=hdim),
        grid=sq.grid,
        in_specs=[sq.rows(qd, q_col_block), sq.rows(2 * kvd, kv_col_block),
                  pl.BlockSpec((sq.TB, 2 * kvd), lambda b, j: (j, 0))]
        + [sq.const(c.shape) for c in consts],
        out_specs=[sq.heads(q_heads, hdim), sq.heads(kv_heads, hdim), sq.heads(kv_heads, hdim)],
        out_shape=[jax.ShapeDtypeStruct((sq.B, q_heads, sq.S, hdim), BF16),
                   jax.ShapeDtypeStruct((sq.B, kv_heads, sq.S, hdim), BF16),
                   jax.ShapeDtypeStruct((sq.B, kv_heads, sq.S, hdim), BF16)],
        compiler_params=_params(2),
        name="attn_prep",
    )(proj, proj, cs_tab, *consts)


def _attn_body(q_ref, k_ref, v_ref, o_ref, *, group, n_ctx_qblocks, n_ctx_kblocks, n_kblocks):
    tq, hdim = q_ref.shape[1], q_ref.shape[2]
    q = q_ref[...].reshape(group * tq, hdim)
    is_ctx = pl.program_id(2) < n_ctx_qblocks
    n_k = jnp.where(is_ctx, n_ctx_kblocks, n_kblocks)

    def step(i, carry):
        m, l, acc = carry
        off = pl.multiple_of(i * KEY_BLOCK, KEY_BLOCK)
        k = k_ref[pl.ds(off, KEY_BLOCK), :]
        v = v_ref[pl.ds(off, KEY_BLOCK), :]
        s = lax.dot_general(q, k, (((1,), (1,)), ((), ())), preferred_element_type=F32)
        m_new = jnp.maximum(m, jnp.max(s, axis=1, keepdims=True))
        alpha = jnp.exp(m - m_new)
        p = jnp.exp(s - m_new)
        l = alpha * l + jnp.sum(p, axis=1, keepdims=True)
        acc = alpha * acc + jnp.dot(p.astype(BF16), v, preferred_element_type=F32)
        return m_new, l, acc

    rows = group * tq
    init = (jnp.full((rows, 1), -jnp.inf, F32), jnp.zeros((rows, 1), F32), jnp.zeros((rows, hdim), F32))
    m, l, acc = lax.fori_loop(0, n_k, step, init)
    out = acc / l
    for g in range(group):
        o_ref[:, g * hdim:(g + 1) * hdim] = out[g * tq:(g + 1) * tq, :]


def _attention(sq, q, k, v):
    b, q_heads, s, hdim = q.shape
    kv_heads = k.shape[1]
    group = q_heads // kv_heads
    tq = _largest_divisor(sq.ctx, (128, 64))
    assert sq.ctx % KEY_BLOCK == 0 and s % KEY_BLOCK == 0
    return pl.pallas_call(
        functools.partial(_attn_body, group=group, n_ctx_qblocks=sq.ctx // tq,
                          n_ctx_kblocks=sq.ctx // KEY_BLOCK, n_kblocks=s // KEY_BLOCK),
        grid=(b, kv_heads, s // tq),
        in_specs=[pl.BlockSpec((None, group, tq, hdim), lambda bi, g, i: (bi, g, i, 0)),
                  pl.BlockSpec((None, None, s, hdim), lambda bi, g, i: (bi, g, 0, 0)),
                  pl.BlockSpec((None, None, s, hdim), lambda bi, g, i: (bi, g, 0, 0))],
        out_specs=pl.BlockSpec((None, tq, group * hdim), lambda bi, g, i: (bi, i, g)),
        out_shape=jax.ShapeDtypeStruct((b, s, q_heads * hdim), F32),
        compiler_params=_params(3),
        name="attention",
    )(q, k, v)


def _merge_body(x_ref, ya_ref, yb_ref, yc_ref, gate_ref, mod_ref, wa_ref, wb_ref, wc_ref, wo_ref, o_ref,
                *, gate_row):
    d = x_ref.shape[-1]
    m = gate_ref[:, :d] * _dot(ya_ref[...], wa_ref[...])
    m = m + gate_ref[:, d:2 * d] * _dot(yb_ref[...], wb_ref[...])
    m = m + gate_ref[:, 2 * d:] * _dot(yc_ref[...], wc_ref[...])
    y = _dot(m, wo_ref[...])
    o_ref[...] = x_ref[...] + mod_ref[gate_row:gate_row + 1, :] * y


def _merge(sq, xs, ya, yb, yc, gates, modtab, lw):
    d = xs.shape[-1]
    consts = [lw["w_up_a"], lw["w_up_b"], lw["w_up_c"], lw["w_out"]]
    return pl.pallas_call(
        functools.partial(_merge_body, gate_row=2),
        grid=sq.grid,
        in_specs=[sq.rows(d), sq.rows(ya.shape[-1]), sq.rows(yb.shape[-1]), sq.rows(yc.shape[-1]),
                  sq.rows(3 * d), sq.mod(d)] + [sq.const(c.shape) for c in consts],
        out_specs=sq.rows(d),
        out_shape=jax.ShapeDtypeStruct(xs.shape, F32),
        compiler_params=_params(2),
        name="merge",
    )(xs, ya, yb, yc, gates, modtab, *consts)


def _ffn_body(x_ref, g_ref, mod_ref, w1_ref, w3_ref, w2_ref, o_ref, *, h_chunk):
    x = x_ref[...]
    y = x * lax.rsqrt(jnp.mean(x * x, axis=-1, keepdims=True) + RMS_EPS) * g_ref[...]
    h = (y * (1.0 + mod_ref[4:5, :]) + mod_ref[3:4, :]).astype(BF16)
    hidden = w1_ref.shape[1]
    acc = jnp.zeros(x.shape, F32)
    for c0 in range(0, hidden, h_chunk):
        a = jnp.dot(h, w1_ref[:, c0:c0 + h_chunk], preferred_element_type=F32)
        b = jnp.dot(h, w3_ref[:, c0:c0 + h_chunk], preferred_element_type=F32)
        t = (a * _sigmoid(a) * b).astype(BF16)
        acc = acc + jnp.dot(t, w2_ref[c0:c0 + h_chunk, :], preferred_element_type=F32)
    o_ref[...] = x + mod_ref[5:6, :] * acc


def _ffn(sq, xs, g, modtab, lw):
    d = xs.shape[-1]
    consts = [lw["ffn_w1"], lw["ffn_w3"], lw["ffn_w2"]]
    h_chunk = _largest_divisor(lw["ffn_w1"].shape[1], (512, 256, 128))
    return pl.pallas_call(
        functools.partial(_ffn_body, h_chunk=h_chunk),
        grid=sq.grid,
        in_specs=[sq.rows(d), sq.const((1, d)), sq.mod(d)] + [sq.const(c.shape) for c in consts],
        out_specs=sq.rows(d),
        out_shape=jax.ShapeDtypeStruct(xs.shape, F32),
        compiler_params=_params(2),
        name="ffn",
    )(xs, g.reshape(1, d), modtab, *consts)


def _block_diag_ones(n, seg):
    idx = jnp.arange(n) // seg
    return (idx[:, None] == idx[None, :]).astype(BF16)


def _block_diag2(m):
    z = jnp.zeros_like(m[0])
    return jnp.concatenate([jnp.concatenate([m[0], z], axis=1), jnp.concatenate([z, m[1]], axis=1)], axis=0)


def _pad_to(x, axis, size):
    pad = [(0, 0)] * x.ndim
    pad[axis] = (0, size - x.shape[axis])
    return jnp.pad(x, pad)


def _rope_table(ctx_len, seq_len, hdim, kv_heads):
    rows = seq_len // GRID_W
    row = jnp.repeat(jnp.arange(rows), GRID_W).astype(F32)
    col = jnp.tile(jnp.arange(GRID_W), rows).astype(F32)
    half = hdim // 2
    inv = ROPE_THETA ** (-jnp.arange(0, half, 2, dtype=F32) / half)
    ang = jnp.concatenate([row[:, None] * inv, col[:, None] * inv], axis=-1)
    cos = jnp.repeat(jnp.cos(ang), 2, axis=1)
    sin = jnp.repeat(jnp.sin(ang), 2, axis=1) * jnp.tile(jnp.array([-1.0, 1.0], F32), half)
    cos = jnp.concatenate([jnp.ones((ctx_len, hdim), F32), cos], axis=0)
    sin = jnp.concatenate([jnp.zeros((ctx_len, hdim), F32), sin], axis=0)
    return jnp.concatenate([jnp.tile(cos, (1, kv_heads)), jnp.tile(sin, (1, kv_heads))], axis=1)


def kernel(x, c, ctx, c_ctx, ada_w, ada_b, norm1, norm2, w_in, rwkv_mu_x, rwkv_mu_rkv, rwkv_w0, rwkv_w1, rwkv_w2, rwkv_a0, rwkv_a1, rwkv_a2, rwkv_g1, rwkv_g2, rwkv_k_k, rwkv_k_a, rwkv_r_k, rwkv_lnx_w, rwkv_lnx_b, gdn_conv, gdn_w_alpha, gdn_dt_bias, gdn_A_log, gdn_w_beta, gdn_norm, attn_q_norm, attn_k_norm, w_up_a, w_up_b, w_up_c, w_gate, b_gate, w_out, ffn_w1, ffn_w3, ffn_w2, final_norm):
    batch, seq_len, d = x.shape
    ctx_len = ctx.shape[1]
    depth = ada_w.shape[0]
    sq = _Seq(batch, ctx_len, seq_len)

    r_heads, r_hdim = rwkv_r_k.shape[1], rwkv_r_k.shape[2]
    rdim = r_heads * r_hdim
    g_heads, g_hdim = gdn_w_alpha.shape[-1], gdn_norm.shape[-1]
    gdim = g_heads * g_hdim
    a_hdim = attn_q_norm.shape[-1]
    qd = w_up_c.shape[1]
    q_heads = qd // a_hdim
    kvd = (w_in.shape[-1] - 3 * rdim - 4 * gdim - qd) // 2
    kv_heads = kvd // a_hdim
    assert rdim == gdim == qd and r_hdim == a_hdim, "lane-segment constants are shared between mixers"
    assert (3 * rdim) % (3 * gdim) == 0 and (3 * rdim + 3 * gdim) % gdim == 0
    gdn_col = (3 * rdim) // (3 * gdim)
    z_col = (3 * rdim + 3 * gdim) // gdim
    q_col = (3 * rdim + 4 * gdim) // qd
    assert (3 * rdim + 4 * gdim + qd) % (2 * kvd) == 0
    kv_col = (3 * rdim + 4 * gdim + qd) // (2 * kvd)

    bd_r = _block_diag_ones(rdim, r_hdim)
    bd_g = _block_diag_ones(gdim, g_hdim)
    cs_tab = _rope_table(ctx_len, seq_len, a_hdim, kv_heads)
    n_beta = 2 * g_heads
    assert 2 * n_beta <= LANES

    cond = jnp.concatenate([c, c_ctx[None, :]], axis=0)
    cond = _pad_to(cond, 0, -(-(batch + 1) // SUBLANES) * SUBLANES)

    xs = jnp.concatenate([ctx, x], axis=1)
    m_rows = batch * sq.S
    for l in range(depth):
        g1w = _pad_to(rwkv_g1[l], 1, 2 * LANES)
        lw = {
            "mu": rwkv_mu_x[l], "mu_rkv": rwkv_mu_rkv[l],
            "w1": jnp.concatenate([rwkv_w1[l, 0], rwkv_w1[l, 1]], axis=1).astype(BF16),
            "w2": _block_diag2(rwkv_w2[l]).astype(BF16),
            "w0": rwkv_w0[l].reshape(1, 2 * rdim),
            "a1": jnp.concatenate([rwkv_a1[l, 0], rwkv_a1[l, 1]], axis=1).astype(BF16),
            "a2": _block_diag2(rwkv_a2[l]).astype(BF16),
            "a0": rwkv_a0[l].reshape(1, 2 * rdim),
            "g1": g1w.astype(BF16),
            "g2": _pad_to(rwkv_g2[l], 0, 2 * LANES).astype(BF16),
            "wab": _pad_to(jnp.concatenate([gdn_w_beta[l, 0], gdn_w_beta[l, 1],
                                            gdn_w_alpha[l, 0], gdn_w_alpha[l, 1]], axis=1), 1, LANES).astype(BF16),
            "abb": _pad_to(jnp.concatenate([jnp.zeros((n_beta,), F32), gdn_dt_bias[l].reshape(-1)]), 0, LANES).reshape(1, LANES),
            "alog": _pad_to(jnp.concatenate([jnp.zeros((n_beta,), F32), gdn_A_log[l].reshape(-1)]), 0, LANES).reshape(1, LANES),
            "n_beta": n_beta,
            "k_k": rwkv_k_k[l].reshape(1, rdim), "k_a": rwkv_k_a[l].reshape(1, rdim),
            "r_k": rwkv_r_k[l].reshape(1, rdim),
            "lnx_w": rwkv_lnx_w[l].reshape(1, rdim), "lnx_b": rwkv_lnx_b[l].reshape(1, rdim),
            "bd_r": bd_r, "bd_g": bd_g,
            "conv": jnp.transpose(gdn_conv[l]),
            "gdn_norm": jnp.tile(gdn_norm[l], g_heads).reshape(1, gdim),
            "q_norm": jnp.tile(attn_q_norm[l], q_heads).reshape(1, qd),
            "k_norm": jnp.tile(attn_k_norm[l], kv_heads).reshape(1, kvd),
            "w_up_a": w_up_a[l].astype(BF16), "w_up_b": w_up_b[l].astype(BF16),
            "w_up_c": w_up_c[l].astype(BF16), "w_out": w_out[l].astype(BF16),
            "ffn_w1": ffn_w1[l].astype(BF16), "ffn_w3": ffn_w3[l].astype(BF16), "ffn_w2": ffn_w2[l].astype(BF16),
        }
        mod = _matmul(cond, ada_w[l].astype(BF16), bias=ada_b[l], pre_act="silu")
        mod_x = mod[:batch].reshape(batch, 6, d)
        mod_c = jnp.broadcast_to(mod[batch].reshape(1, 6, d), (batch, 6, d))
        modtab = jnp.stack([mod_c, mod_x], axis=1)

        h = _normmod(sq, xs, norm1[l], modtab, shift_row=0, scale_row=1)
        h2 = h.reshape(m_rows, d)
        proj = _matmul(h2, w_in[l].astype(BF16)).reshape(batch, sq.S, -1)
        gates = _matmul(h2, w_gate[l].astype(BF16), bias=b_gate[l], act="sigmoid").reshape(batch, sq.S, -1)
        lw_dec, a_iclr, g_out, ab = _lora(sq, h, lw)

        r, v, aa, kd, bb = _rwkv_prep(sq, proj, a_iclr, lw)
        rf, rg = _rwkv_intra(sq, r, v, aa, kd, lw_dec, bb, r_heads)
        o_rf, o_rb = _rwkv_scan(sq, rf, rg, r_heads)
        ya = _rwkv_post(sq, o_rf, o_rb, r, v, kd, g_out, lw, r_heads)

        gq, gk, gv = _gdn_prep(sq, proj, lw, gdim, gdn_col)
        gf, gga, ggb = _gdn_intra(sq, gq, gk, gv, ab, g_heads)
        o_gf, o_gb = _gdn_scan(sq, gf, gga, ggb, g_heads)
        yb = _gdn_post(sq, o_gf, o_gb, proj, lw, g_heads, z_col)

        aq, ak, av = _attn_prep(sq, proj, cs_tab, lw, q_heads, kv_heads, a_hdim, q_col, kv_col)
        yc = _attention(sq, aq, ak, av)

        xs = _merge(sq, xs, ya, yb, yc, gates, modtab, lw)
        xs = _ffn(sq, xs, norm2[l], modtab, lw)
    return _final_norm(sq, xs, final_norm)
```

```python
import functools

import jax
import jax.numpy as jnp
from jax import lax
from jax.experimental import pallas as pl
from jax.experimental.pallas import tpu as pltpu

F32 = jnp.float32
BF16 = jnp.bfloat16

RMS_EPS = 1e-6
RWKV_LNX_EPS = 64e-5
ROPE_THETA = 10000.0
GRID_W = 64

SUBLANES = 8
LANES = 128
CHUNK = 64
MAX_TOKEN_BLOCK = 256
MAX_MM_ROWS = 512
KEY_BLOCK = 256
RWKV_PASSES = 1
GDN_PASSES = 1
INV_PASSES = 1
VMEM_LIMIT = 56 * 1024 * 1024


def _sigmoid(x):
    return 1.0 / (1.0 + jnp.exp(-x))


def _softplus(x):
    return jnp.maximum(x, 0.0) + jnp.log(1.0 + jnp.exp(-jnp.abs(x)))


def _split2(x):
    hi = x.astype(BF16)
    lo = (x - hi.astype(F32)).astype(BF16)
    return hi, lo


def _mm(a, b, dims, passes):
    d = functools.partial(lax.dot_general, dimension_numbers=(dims, ((), ())), preferred_element_type=F32)
    if passes == 1:
        return d(a.astype(BF16), b.astype(BF16))
    ah, al = _split2(a)
    bh, bl = _split2(b)
    return d(ah, bh) + (d(ah, bl) + d(al, bh))


def _dot(a, b, passes=1):
    return _mm(a, b, ((1,), (0,)), passes)


def _dot_nt(a, b, passes=1):
    return _mm(a, b, ((1,), (1,)), passes)


def _dot_tn(a, b, passes=1):
    return _mm(a, b, ((0,), (0,)), passes)


def _split3(x):
    hi = x.astype(BF16)
    r1 = x - hi.astype(F32)
    mid = r1.astype(BF16)
    lo = (r1 - mid.astype(F32)).astype(BF16)
    return hi, mid, lo


def _dot_exact_lhs(m01, x):
    d = functools.partial(jnp.dot, preferred_element_type=F32)
    hi, mid, lo = _split3(x)
    return d(m01, hi) + (d(m01, mid) + d(m01, lo))


def _segsum(x, bd_ref):
    d = functools.partial(jnp.dot, preferred_element_type=F32)
    hi, lo = _split2(x)
    bd = bd_ref[...]
    return d(hi, bd) + d(lo, bd)


def _tri_inverse(l, eye):
    n = l.shape[0]
    row = lax.broadcasted_iota(jnp.int32, (n, n), 0)
    col = lax.broadcasted_iota(jnp.int32, (n, n), 1)
    t = eye - jnp.where((row >> 1) == (col >> 1), l, 0.0)
    for k in range(2, n.bit_length()):
        off = ((row >> k) == (col >> k)) & ((row >> (k - 1)) != (col >> (k - 1)))
        t = t - _dot(_dot(t, jnp.where(off, l, 0.0), INV_PASSES), t, INV_PASSES)
    return t


def _row_iota(shape):
    return lax.broadcasted_iota(jnp.int32, shape, 0)


def _shift_rows(x, prev_ref, next_ref, k, first, last):
    n = x.shape[0]
    row = _row_iota(x.shape)
    y = pltpu.roll(x, (-k) % n, axis=0)
    if k < 0:
        for i in range(-k):
            edge = prev_ref[SUBLANES + k + i:SUBLANES + k + i + 1, :]
            edge = jnp.where(first, 0.0, edge)
            y = jnp.where(row == i, edge, y)
    else:
        for i in range(k):
            edge = next_ref[i:i + 1, :]
            edge = jnp.where(last, 0.0, edge)
            y = jnp.where(row == n - k + i, edge, y)
    return y


def _largest_divisor(n, candidates):
    for c in candidates:
        if n % c == 0:
            return c
    raise ValueError(f"no block size among {candidates} divides {n}")


class _Seq:
    def __init__(self, batch, ctx_len, seq_len):
        self.B = batch
        self.ctx = ctx_len
        self.S = ctx_len + seq_len
        self.TB = _largest_divisor(ctx_len, (MAX_TOKEN_BLOCK, 128, 64))
        assert seq_len % self.TB == 0 and self.TB % CHUNK == 0
        self.ncb = ctx_len // self.TB
        self.nblk = self.S // self.TB
        self.ncc = ctx_len // CHUNK
        self.nchunk = self.S // CHUNK
        self.grid = (batch, self.nblk)

    def rows(self, width, col_block=0):
        return pl.BlockSpec((None, self.TB, width), lambda b, j: (b, j, col_block))

    def rows2(self, width):
        return pl.BlockSpec((2, None, self.TB, width), lambda b, j: (0, b, j, 0))

    def heads(self, n_heads, width):
        return pl.BlockSpec((None, n_heads, self.TB, width), lambda b, j: (b, 0, j, 0))

    def prev_rows(self, width, col_block=0):
        per = self.TB // SUBLANES
        return pl.BlockSpec((None, SUBLANES, width),
                            lambda b, j: (b, jnp.maximum(j * per - 1, 0), col_block))

    def next_rows(self, width, col_block=0):
        per = self.TB // SUBLANES
        top = self.S // SUBLANES - 1
        return pl.BlockSpec((None, SUBLANES, width),
                            lambda b, j: (b, jnp.minimum((j + 1) * per, top), col_block))

    def const(self, shape):
        zeros = (0,) * len(shape)
        return pl.BlockSpec(shape, lambda b, j: zeros)

    def mod(self, d_model):
        ncb = self.ncb
        return pl.BlockSpec((None, None, 6, d_model),
                            lambda b, j: (b, (j >= ncb).astype(jnp.int32), 0, 0))

    def edges(self):
        j = pl.program_id(1)
        first = (j == 0) | (j == self.ncb)
        last = (j == self.ncb - 1) | (j == self.nblk - 1)
        return first, last


def _params(n_axes):
    return pltpu.CompilerParams(dimension_semantics=("arbitrary",) * n_axes,
                                vmem_limit_bytes=VMEM_LIMIT)


def _mm_body(*refs, act, pre_act, n_chunk, has_bias):
    if has_bias:
        x_ref, w_ref, b_ref, o_ref = refs
    else:
        x_ref, w_ref, o_ref = refs
        b_ref = None
    x = x_ref[...]
    if pre_act == "silu":
        x = x * _sigmoid(x)
    xb = x.astype(BF16)
    n = o_ref.shape[-1]
    for n0 in range(0, n, n_chunk):
        y = jnp.dot(xb, w_ref[:, n0:n0 + n_chunk], preferred_element_type=F32)
        if b_ref is not None:
            y = y + b_ref[:, n0:n0 + n_chunk]
        if act == "sigmoid":
            y = _sigmoid(y)
        o_ref[:, n0:n0 + n_chunk] = y.astype(o_ref.dtype)


def _matmul(x, w, bias=None, act=None, pre_act=None, out_dtype=F32):
    m, k = x.shape
    n = w.shape[1]
    tm = m if m <= MAX_MM_ROWS else _largest_divisor(m, (MAX_MM_ROWS, 256, 128, 64, 32, 16, 8))
    n_chunk = _largest_divisor(n, (512, 256, 128))
    in_specs = [pl.BlockSpec((tm, k), lambda i: (i, 0)),
                pl.BlockSpec((k, n), lambda i: (0, 0))]
    args = [x, w]
    if bias is not None:
        in_specs.append(pl.BlockSpec((1, n), lambda i: (0, 0)))
        args.append(bias.reshape(1, n))
    return pl.pallas_call(
        functools.partial(_mm_body, act=act, pre_act=pre_act, n_chunk=n_chunk,
                          has_bias=bias is not None),
        grid=(m // tm,),
        in_specs=in_specs,
        out_specs=pl.BlockSpec((tm, n), lambda i: (i, 0)),
        out_shape=jax.ShapeDtypeStruct((m, n), out_dtype),
        compiler_params=_params(1),
        name="matmul",
    )(*args)


def _normmod_body(x_ref, g_ref, mod_ref, o_ref, *, shift_row, scale_row):
    x = x_ref[...]
    y = x * lax.rsqrt(jnp.mean(x * x, axis=-1, keepdims=True) + RMS_EPS)
    y = y * g_ref[...]
    o_ref[...] = y * (1.0 + mod_ref[scale_row:scale_row + 1, :]) + mod_ref[shift_row:shift_row + 1, :]


def _normmod(sq, xs, g, modtab, shift_row, scale_row):
    d = xs.shape[-1]
    return pl.pallas_call(
        functools.partial(_normmod_body, shift_row=shift_row, scale_row=scale_row),
        grid=sq.grid,
        in_specs=[sq.rows(d), sq.const((1, d)), sq.mod(d)],
        out_specs=sq.rows(d),
        out_shape=jax.ShapeDtypeStruct(xs.shape, F32),
        compiler_params=_params(2),
        name="normmod",
    )(xs, g.reshape(1, d), modtab)


def _final_norm_body(x_ref, g_ref, o_ref):
    x = x_ref[...]
    y = x * lax.rsqrt(jnp.mean(x * x, axis=-1, keepdims=True) + RMS_EPS)
    o_ref[...] = y * g_ref[...]


def _final_norm(sq, xs, g):
    d = xs.shape[-1]
    ncb = sq.ncb
    return pl.pallas_call(
        _final_norm_body,
        grid=(sq.B, sq.nblk - ncb),
        in_specs=[pl.BlockSpec((None, sq.TB, d), lambda b, j: (b, j + ncb, 0)),
                  pl.BlockSpec((1, d), lambda b, j: (0, 0))],
        out_specs=pl.BlockSpec((None, sq.TB, d), lambda b, j: (b, j, 0)),
        out_shape=jax.ShapeDtypeStruct((sq.B, sq.S - sq.ctx, d), F32),
        compiler_params=_params(2),
        name="final_norm",
    )(xs, g.reshape(1, d))


def _lora_body(h_ref, hp_ref, hn_ref, mu_ref, w1_ref, w2_ref, w0_ref, a1_ref, a2_ref, a0_ref,
               g1_ref, g2_ref, wab_ref, abb_ref, alog_ref,
               lw_ref, a_ref, g_ref, ab_ref, *, sq, rdim, n_beta):
    first, last = sq.edges()
    h = h_ref[...]
    nb = 0.5 * (_shift_rows(h, hp_ref, hn_ref, -1, first, last)
                + _shift_rows(h, hp_ref, hn_ref, 1, first, last))
    dlt = nb - h
    xw = h + dlt * mu_ref[0:1, :]
    xa = h + dlt * mu_ref[1:2, :]
    xg = h + dlt * mu_ref[2:3, :]
    wl = w0_ref[...] + _dot(jnp.tanh(_dot(xw, w1_ref[...])), w2_ref[...])
    w_log = -_softplus(-wl) - 0.5
    lw = -jnp.exp(w_log)
    lw_ref[0] = lw[:, :rdim]
    lw_ref[1] = lw[:, rdim:]
    a = _sigmoid(a0_ref[...] + _dot(_dot(xa, a1_ref[...]), a2_ref[...]))
    a_ref[0] = a[:, :rdim]
    a_ref[1] = a[:, rdim:]
    g_ref[...] = _dot(_sigmoid(_dot(xg, g1_ref[...])), g2_ref[...])
    z = _dot(h, wab_ref[...])
    col = lax.broadcasted_iota(jnp.int32, z.shape, 1)
    gl = -jnp.exp(alog_ref[...]) * _softplus(z + abb_ref[...])
    ab_ref[...] = jnp.where(col < n_beta, _sigmoid(z), gl)


def _lora(sq, h, lw):
    d = h.shape[-1]
    rdim = lw["w0"].shape[-1] // 2
    consts = [lw["mu"], lw["w1"], lw["w2"], lw["w0"], lw["a1"], lw["a2"], lw["a0"],
              lw["g1"], lw["g2"], lw["wab"], lw["abb"], lw["alog"]]
    bsd = (sq.B, sq.S)
    return pl.pallas_call(
        functools.partial(_lora_body, sq=sq, rdim=rdim, n_beta=lw["n_beta"]),
        grid=sq.grid,
        in_specs=[sq.rows(d), sq.prev_rows(d), sq.next_rows(d)] + [sq.const(c.shape) for c in consts],
        out_specs=[sq.rows2(rdim), sq.rows2(rdim), sq.rows(rdim), sq.rows(LANES)],
        out_shape=[jax.ShapeDtypeStruct((2,) + bsd + (rdim,), F32),
                   jax.ShapeDtypeStruct((2,) + bsd + (rdim,), F32),
                   jax.ShapeDtypeStruct(bsd + (rdim,), F32),
                   jax.ShapeDtypeStruct(bsd + (LANES,), F32)],
        compiler_params=_params(2),
        name="lora",
    )(h, h, h, *consts)


def _rwkv_prep_body(p_ref, pp_ref, pn_ref, a_ref, mu_ref, kk_ref, ka_ref, bd_ref,
                    r_ref, v_ref, aa_ref, kd_ref, bb_ref, *, sq, rdim):
    first, last = sq.edges()
    p = p_ref[...]
    nb = 0.5 * (_shift_rows(p, pp_ref, pn_ref, -1, first, last)
                + _shift_rows(p, pp_ref, pn_ref, 1, first, last))
    dlt = nb - p
    r = p[:, :rdim] + dlt[:, :rdim] * mu_ref[0:1, :]
    k = p[:, rdim:2 * rdim] + dlt[:, rdim:2 * rdim] * mu_ref[1:2, :]
    v = p[:, 2 * rdim:] + dlt[:, 2 * rdim:] * mu_ref[2:3, :]
    t = k * kk_ref[...]
    kk = t * lax.rsqrt(_segsum(t * t, bd_ref) + RMS_EPS)
    r_ref[...] = r
    v_ref[...] = v
    aa_ref[...] = -kk
    for d in range(2):
        a = a_ref[d]
        kd_ref[d] = k * (1.0 + (a - 1.0) * ka_ref[...])
        bb_ref[d] = kk * a


def _rwkv_prep(sq, proj, a, lw):
    rdim = a.shape[-1]
    consts = [lw["mu_rkv"], lw["k_k"], lw["k_a"], lw["bd_r"]]
    one = jax.ShapeDtypeStruct((sq.B, sq.S, rdim), F32)
    two = jax.ShapeDtypeStruct((2, sq.B, sq.S, rdim), F32)
    return pl.pallas_call(
        functools.partial(_rwkv_prep_body, sq=sq, rdim=rdim),
        grid=sq.grid,
        in_specs=[sq.rows(3 * rdim), sq.prev_rows(3 * rdim), sq.next_rows(3 * rdim), sq.rows2(rdim)]
        + [sq.const(c.shape) for c in consts],
        out_specs=[sq.rows(rdim), sq.rows(rdim), sq.rows(rdim), sq.rows2(rdim), sq.rows2(rdim)],
        out_shape=[one, one, one, two, two],
        compiler_params=_params(2),
        name="rwkv_prep",
    )(proj, proj, proj, a, *consts)


def _chunk_index(d, i, ncc, nchunk):
    back = jnp.where(i < ncc, ncc - 1 - i, nchunk + ncc - 1 - i)
    return jnp.where(d == 0, i, back)


def _causal_masks(d, c, reps=1):
    row = lax.broadcasted_iota(jnp.int32, (c, reps * c), 0)
    col = lax.broadcasted_iota(jnp.int32, (c, reps * c), 1) & (c - 1)
    delta = jnp.where(d == 0, row - col, col - row)
    eye = jnp.where(row == col, 1.0, 0.0).astype(F32)
    return delta >= 0, delta > 0, eye


def _stacked_mask(d, c):
    row = lax.broadcasted_iota(jnp.int32, (2 * c, c), 0)
    col = lax.broadcasted_iota(jnp.int32, (2 * c, c), 1)
    rr = row & (c - 1)
    delta = jnp.where(d == 0, rr - col, col - rr)
    return (delta > 0) | ((row >= c) & (delta == 0))


def _tri_inverse_many(ls, eye):
    n = ls[0].shape[0]
    row = lax.broadcasted_iota(jnp.int32, (n, n), 0)
    col = lax.broadcasted_iota(jnp.int32, (n, n), 1)
    same = (row >> 1) == (col >> 1)
    ts = [eye - jnp.where(same, l, 0.0) for l in ls]
    for k in range(2, n.bit_length()):
        off = ((row >> k) == (col >> k)) & ((row >> (k - 1)) != (col >> (k - 1)))
        tl = [_dot(t, jnp.where(off, l, 0.0), INV_PASSES) for t, l in zip(ts, ls)]
        ts = [t - _dot(x, t, INV_PASSES) for x, t in zip(tl, ts)]
    return ts


def _rwkv_intra_body(r_ref, v_ref, aa_ref, kd_ref, lw_ref, bb_ref, f_ref, g_ref, *, heads, hdim):
    d = pl.program_id(1)
    c = CHUNK
    incl, _, eye = _causal_masks(d, c)
    mask2 = _stacked_mask(d, c)
    lw = lw_ref[...]
    cum = _dot_exact_lhs(jnp.where(incl, 1.0, 0.0).astype(BF16), lw)
    tot = jnp.sum(lw, axis=0, keepdims=True)
    e_neg = jnp.exp(-cum)
    a_t = aa_ref[...] * jnp.exp(cum - lw)
    r_t = r_ref[...] * jnp.exp(cum)
    v = v_ref[...]
    b_tt = jnp.transpose(bb_ref[...] * e_neg)
    k_tt = jnp.transpose(kd_ref[...] * e_neg)
    e_col = jnp.transpose(jnp.broadcast_to(jnp.exp(tot), lw.shape))
    hs = range(heads)
    sl = [slice(h * hdim, (h + 1) * hdim) for h in hs]
    ar = [jnp.concatenate([a_t[:, sl[h]], r_t[:, sl[h]]], axis=0) for h in hs]
    vh = [v[:, sl[h]] for h in hs]
    mb = [jnp.where(mask2, _dot(ar[h], b_tt[sl[h]], RWKV_PASSES), 0.0) for h in hs]
    mk = [jnp.where(mask2, _dot(ar[h], k_tt[sl[h]], RWKV_PASSES), 0.0) for h in hs]
    x = [_dot(mk[h], vh[h], RWKV_PASSES) for h in hs]
    h0 = [_dot(e_col[sl[h]] * k_tt[sl[h]], vh[h], RWKV_PASSES) for h in hs]
    t = _tri_inverse_many([-mb[h][:c] for h in hs], eye)
    wt = [_dot(t[h], a_t[:, sl[h]], RWKV_PASSES) for h in hs]
    ut = [_dot(t[h], x[h][:c], RWKV_PASSES) for h in hs]
    for h in hs:
        f_ref[h, 0:c] = ut[h]
        f_ref[h, c:2 * c] = x[h][c:]
        f_ref[h, 2 * c:3 * c] = h0[h]
        f_ref[h, 3 * c:4 * c] = e_col[sl[h]]
        g_ref[h, 0:c] = wt[h].astype(BF16)
        g_ref[h, c:2 * c] = r_t[:, sl[h]].astype(BF16)
        g_ref[h, 2 * c:3 * c] = mb[h][c:].astype(BF16)
        g_ref[h, 3 * c:4 * c] = (e_col[sl[h]] * b_tt[sl[h]]).astype(BF16)


def _rwkv_intra(sq, r, v, aa, kd, lw, bb, heads):
    rdim = r.shape[-1]
    hdim = rdim // heads
    assert hdim == CHUNK, "the hand-over tiles share one [CHUNK, head_dim] shape"
    shared = pl.BlockSpec((None, CHUNK, rdim), lambda b, d, i: (b, i, 0))
    per_dir = pl.BlockSpec((None, None, CHUNK, rdim), lambda b, d, i: (d, b, i, 0))
    out = pl.BlockSpec((None, None, heads, 4 * CHUNK, hdim), lambda b, d, i: (d, b, 0, i, 0))
    shape = (2, sq.B, heads, sq.nchunk * 4 * CHUNK, hdim)
    return pl.pallas_call(
        functools.partial(_rwkv_intra_body, heads=heads, hdim=hdim),
        grid=(sq.B, 2, sq.nchunk),
        in_specs=[shared, shared, shared, per_dir, per_dir, per_dir],
        out_specs=[out, out],
        out_shape=[jax.ShapeDtypeStruct(shape, F32), jax.ShapeDtypeStruct(shape, BF16)],
        compiler_params=_params(3),
        name="rwkv_intra",
    )(r, v, aa, kd, lw, bb)


def _rwkv_scan_body(f0_ref, g0_ref, f1_ref, g1_ref, of_ref, ob_ref, h_ref, *, heads, hdim):
    @pl.when(pl.program_id(1) == 0)
    def _():
        h_ref[...] = jnp.zeros_like(h_ref)

    c = CHUNK
    fs, gs, outs = (f0_ref, f1_ref), (g0_ref, g1_ref), (of_ref, ob_ref)
    chains = [(d, h) for d in range(2) for h in range(heads)]
    dot = functools.partial(jnp.dot, preferred_element_type=F32)
    st = {ch: h_ref[ch[0], ch[1]] for ch in chains}
    m2 = {(d, h): dot(gs[d][h, 0:2 * c, :], st[d, h].astype(BF16)) for d, h in chains}
    ub = {(d, h): (fs[d][h, 0:c, :] + m2[d, h][:c]).astype(BF16) for d, h in chains}
    for d, h in chains:
        y = fs[d][h, c:2 * c, :] + m2[d, h][c:] + dot(gs[d][h, 2 * c:3 * c, :], ub[d, h])
        outs[d][:, h * hdim:(h + 1) * hdim] = y
        h_ref[d, h] = (fs[d][h, 3 * c:4 * c, :] * st[d, h] + fs[d][h, 2 * c:3 * c, :]
                       + dot(gs[d][h, 3 * c:4 * c, :], ub[d, h]))


def _backward_chunk(i, ncc, nchunk):
    return jnp.where(i < ncc, ncc - 1 - i, nchunk + ncc - 1 - i)


def _scan_specs(sq, heads, rows, width):
    ncc, nchunk = sq.ncc, sq.nchunk
    fwd = pl.BlockSpec((None, None, heads, rows, width), lambda b, i: (0, b, 0, i, 0))
    bwd = pl.BlockSpec((None, None, heads, rows, width),
                       lambda b, i: (1, b, 0, _backward_chunk(i, ncc, nchunk), 0))
    return fwd, bwd


def _scan_out_specs(sq, width):
    ncc, nchunk = sq.ncc, sq.nchunk
    fwd = pl.BlockSpec((None, CHUNK, width), lambda b, i: (b, i, 0))
    bwd = pl.BlockSpec((None, CHUNK, width), lambda b, i: (b, _backward_chunk(i, ncc, nchunk), 0))
    return [fwd, bwd]


def _rwkv_scan(sq, f, g, heads):
    hdim = f.shape[-1]
    rdim = heads * hdim
    ff, fb = _scan_specs(sq, heads, 4 * CHUNK, hdim)
    one = jax.ShapeDtypeStruct((sq.B, sq.S, rdim), F32)
    return pl.pallas_call(
        functools.partial(_rwkv_scan_body, heads=heads, hdim=hdim),
        grid=(sq.B, sq.nchunk),
        in_specs=[ff, ff, fb, fb],
        out_specs=_scan_out_specs(sq, rdim),
        out_shape=[one, one],
        scratch_shapes=[pltpu.VMEM((2, heads, hdim, hdim), F32)],
        compiler_params=_params(2),
        name="rwkv_scan",
    )(f, g, f, g)


def _rwkv_post_body(of_ref, ob_ref, r_ref, v_ref, kd_ref, g_ref, rk_ref, lnw_ref, lnb_ref, bd_ref, y_ref, *, hdim):
    o = of_ref[...] + ob_ref[...]
    inv_n = 1.0 / hdim
    mean = _segsum(o, bd_ref) * inv_n
    cen = o - mean
    var = _segsum(cen * cen, bd_ref) * inv_n
    o = cen * lax.rsqrt(var + RWKV_LNX_EPS) * lnw_ref[...] + lnb_ref[...]
    r = r_ref[...]
    v = v_ref[...]
    rk = rk_ref[...]
    for d in range(2):
        o = o + _segsum(r * kd_ref[d] * rk, bd_ref) * v
    y_ref[...] = o * g_ref[...]


def _rwkv_post(sq, o_f, o_b, r, v, kd, g, lw, heads):
    rdim = r.shape[-1]
    consts = [lw["r_k"], lw["lnx_w"], lw["lnx_b"], lw["bd_r"]]
    return pl.pallas_call(
        functools.partial(_rwkv_post_body, hdim=rdim // heads),
        grid=sq.grid,
        in_specs=[sq.rows(rdim), sq.rows(rdim), sq.rows(rdim), sq.rows(rdim), sq.rows2(rdim), sq.rows(rdim)]
        + [sq.const(c.shape) for c in consts],
        out_specs=sq.rows(rdim),
        out_shape=jax.ShapeDtypeStruct((sq.B, sq.S, rdim), F32),
        compiler_params=_params(2),
        name="rwkv_post",
    )(o_f, o_b, r, v, kd, g, *consts)


def _gdn_prep_body(p_ref, pp_ref, pn_ref, cw_ref, bd_ref, q_ref, k_ref, v_ref, *, sq, gdim):
    first, last = sq.edges()
    p = p_ref[...]
    width = cw_ref.shape[0]
    half = width // 2
    acc = p * cw_ref[half:half + 1, :]
    for j in range(width):
        if j != half:
            acc = acc + _shift_rows(p, pp_ref, pn_ref, j - half, first, last) * cw_ref[j:j + 1, :]
    y = acc * _sigmoid(acc)
    q = y[:, :gdim]
    k = y[:, gdim:2 * gdim]
    q_ref[...] = q * lax.rsqrt(_segsum(q * q, bd_ref) + RMS_EPS)
    k_ref[...] = k * lax.rsqrt(_segsum(k * k, bd_ref) + RMS_EPS)
    v_ref[...] = y[:, 2 * gdim:]


def _gdn_prep(sq, proj, lw, gdim, col_block):
    consts = [lw["conv"], lw["bd_g"]]
    one = jax.ShapeDtypeStruct((sq.B, sq.S, gdim), F32)
    w = 3 * gdim
    return pl.pallas_call(
        functools.partial(_gdn_prep_body, sq=sq, gdim=gdim),
        grid=sq.grid,
        in_specs=[sq.rows(w, col_block), sq.prev_rows(w, col_block), sq.next_rows(w, col_block)]
        + [sq.const(c.shape) for c in consts],
        out_specs=[sq.rows(gdim)] * 3,
        out_shape=[one, one, one],
        compiler_params=_params(2),
        name="gdn_prep",
    )(proj, proj, proj, *consts)


def _gdn_intra_body(q_ref, k_ref, v_ref, ab_ref, f_ref, ga_ref, gb_ref, *, heads, hdim):
    d = pl.program_id(1)
    c = CHUNK
    incl, strict, eye = _causal_masks(d, c)
    ab = ab_ref[...]
    gc_all = _dot_exact_lhs(jnp.where(incl, 1.0, 0.0).astype(BF16), ab)
    gc_all_t = jnp.transpose(gc_all)
    tot_all = jnp.sum(ab, axis=0, keepdims=True)
    lane = lax.broadcasted_iota(jnp.int32, ab.shape, 1)
    sub = lax.broadcasted_iota(jnp.int32, gc_all_t.shape, 0)
    k_all = k_ref[...]
    k_tt = jnp.transpose(k_all)
    scale = hdim ** -0.5
    hs = range(heads)
    sl = [slice(h * hdim, (h + 1) * hdim) for h in hs]
    beta, gc, g_last, gc_row, decay, kb, kq = [], [], [], [], [], [], []
    for h in hs:
        pick_b = lane == d * heads + h
        pick_g = lane == 2 * heads + d * heads + h
        beta.append(jnp.sum(jnp.where(pick_b, ab, 0.0), axis=1, keepdims=True))
        gc.append(jnp.sum(jnp.where(pick_g, gc_all, 0.0), axis=1, keepdims=True))
        g_last.append(jnp.sum(jnp.where(pick_g[:1], tot_all, 0.0), axis=1, keepdims=True))
        gc_row.append(jnp.sum(jnp.where(sub == 2 * heads + d * heads + h, gc_all_t, 0.0),
                              axis=0, keepdims=True))
        diff = gc[h] - gc_row[h]
        decay.append(jnp.where(incl, jnp.exp(jnp.where(incl, diff, 0.0)), 0.0))
        kb.append(k_all[:, sl[h]] * beta[h])
        kq.append(jnp.concatenate([kb[h], q_ref[:, sl[h]] * scale], axis=0))
    m = [_dot(kq[h], k_tt[sl[h]], GDN_PASSES) for h in hs]
    t = _tri_inverse_many([jnp.where(strict, m[h][:c] * decay[h], 0.0) for h in hs], eye)
    e_gc = [jnp.exp(gc[h]) for h in hs]
    sol = [_dot(t[h], jnp.concatenate([v_ref[:, sl[h]] * beta[h], kb[h] * e_gc[h]], axis=1), GDN_PASSES)
           for h in hs]
    for h in hs:
        f_ref[h, 0:c] = sol[h][:, :hdim]
        f_ref[h, c:c + SUBLANES] = jnp.broadcast_to(jnp.exp(g_last[h]), (SUBLANES, hdim))
        ga_ref[h, 0:c] = sol[h][:, hdim:].astype(BF16)
        ga_ref[h, c:2 * c] = (kq[h][c:] * e_gc[h]).astype(BF16)
        gb_ref[h, 0:c] = (m[h][c:] * decay[h]).astype(BF16)
        gb_ref[h, c:c + hdim] = (k_tt[sl[h]] * jnp.exp(g_last[h] - gc_row[h])).astype(BF16)


def _gdn_intra(sq, q, k, v, ab, heads):
    gdim = q.shape[-1]
    hdim = gdim // heads
    shared = pl.BlockSpec((None, CHUNK, gdim), lambda b, d, i: (b, i, 0))
    small = pl.BlockSpec((None, CHUNK, LANES), lambda b, d, i: (b, i, 0))
    rows = (CHUNK + SUBLANES, 2 * CHUNK, CHUNK + hdim)
    widths = (hdim, hdim, CHUNK)
    dtypes = (F32, BF16, BF16)
    outs = [pl.BlockSpec((None, None, heads, r, w), lambda b, d, i: (d, b, 0, i, 0)) for r, w in zip(rows, widths)]
    shapes = [jax.ShapeDtypeStruct((2, sq.B, heads, sq.nchunk * r, w), dt) for r, w, dt in zip(rows, widths, dtypes)]
    return pl.pallas_call(
        functools.partial(_gdn_intra_body, heads=heads, hdim=hdim),
        grid=(sq.B, 2, sq.nchunk),
        in_specs=[shared, shared, shared, small],
        out_specs=outs,
        out_shape=shapes,
        compiler_params=_params(3),
        name="gdn_intra",
    )(q, k, v, ab)


def _gdn_scan_body(f0_ref, ga0_ref, gb0_ref, f1_ref, ga1_ref, gb1_ref, of_ref, ob_ref, s_ref, *, heads, hdim):
    @pl.when(pl.program_id(1) == 0)
    def _():
        s_ref[...] = jnp.zeros_like(s_ref)

    c = CHUNK
    fs, gas, gbs, outs = (f0_ref, f1_ref), (ga0_ref, ga1_ref), (gb0_ref, gb1_ref), (of_ref, ob_ref)
    chains = [(d, h) for d in range(2) for h in range(heads)]
    dot = functools.partial(jnp.dot, preferred_element_type=F32)
    st = {ch: s_ref[ch[0], ch[1]] for ch in chains}
    m = {(d, h): dot(gas[d][h], st[d, h].astype(BF16)) for d, h in chains}
    vn = {(d, h): (fs[d][h, 0:c, :] - m[d, h][:c]).astype(BF16) for d, h in chains}
    for d, h in chains:
        outs[d][:, h * hdim:(h + 1) * hdim] = m[d, h][c:] + dot(gbs[d][h, 0:c, :], vn[d, h])
        s_ref[d, h] = st[d, h] * fs[d][h, c:c + 1, :] + dot(gbs[d][h, c:c + hdim, :], vn[d, h])


def _gdn_scan(sq, f, ga, gb, heads):
    hdim = f.shape[-1]
    gdim = heads * hdim
    specs = []
    for arr in (f, ga, gb):
        specs.append(_scan_specs(sq, heads, arr.shape[3] // sq.nchunk, arr.shape[4]))
    one = jax.ShapeDtypeStruct((sq.B, sq.S, gdim), F32)
    return pl.pallas_call(
        functools.partial(_gdn_scan_body, heads=heads, hdim=hdim),
        grid=(sq.B, sq.nchunk),
        in_specs=[s[0] for s in specs] + [s[1] for s in specs],
        out_specs=_scan_out_specs(sq, gdim),
        out_shape=[one, one],
        scratch_shapes=[pltpu.VMEM((2, heads, hdim, hdim), F32)],
        compiler_params=_params(2),
        name="gdn_scan",
    )(f, ga, gb, f, ga, gb)


def _gdn_post_body(of_ref, ob_ref, z_ref, gn_ref, bd_ref, y_ref, *, hdim):
    o = of_ref[...] + ob_ref[...]
    ms = _segsum(o * o, bd_ref) * (1.0 / hdim)
    z = z_ref[...]
    y_ref[...] = o * lax.rsqrt(ms + RMS_EPS) * gn_ref[...] * (z * _sigmoid(z))


def _gdn_post(sq, o_f, o_b, proj, lw, heads, z_col_block):
    gdim = o_f.shape[-1]
    consts = [lw["gdn_norm"], lw["bd_g"]]
    return pl.pallas_call(
        functools.partial(_gdn_post_body, hdim=gdim // heads),
        grid=sq.grid,
        in_specs=[sq.rows(gdim), sq.rows(gdim), sq.rows(gdim, z_col_block)] + [sq.const(c.shape) for c in consts],
        out_specs=sq.rows(gdim),
        out_shape=jax.ShapeDtypeStruct((sq.B, sq.S, gdim), F32),
        compiler_params=_params(2),
        name="gdn_post",
    )(o_f, o_b, proj, *consts)


def _rope(x, cos, sin_signed):
    n = x.shape[-1]
    lane = lax.broadcasted_iota(jnp.int32, x.shape, 1)
    partner = jnp.where((lane & 1) == 0, pltpu.roll(x, n - 1, axis=1), pltpu.roll(x, 1, axis=1))
    return x * cos + partner * sin_signed


def _attn_prep_body(q_ref, kv_ref, cs_ref, qn_ref, kn_ref, bd_ref, qo_ref, ko_ref, vo_ref,
                    *, q_heads, kv_heads, hdim):
    kvd = kv_heads * hdim
    cos = cs_ref[:, :kvd]
    sin = cs_ref[:, kvd:]
    reps = q_heads // kv_heads
    cos_q = jnp.concatenate([cos] * reps, axis=1)
    sin_q = jnp.concatenate([sin] * reps, axis=1)
    inv_n = 1.0 / hdim
    q = q_ref[...]
    q = q * lax.rsqrt(_segsum(q * q, bd_ref) * inv_n + RMS_EPS) * qn_ref[...]
    q = _rope(q, cos_q, sin_q) * (hdim ** -0.5)
    kv = kv_ref[...]
    k = kv[:, :kvd]
    kbd = bd_ref[:kvd, :kvd]
    hi, lo = _split2(k * k)
    ms = (jnp.dot(hi, kbd, preferred_element_type=F32) + jnp.dot(lo, kbd, preferred_element_type=F32)) * inv_n
    k = k * lax.rsqrt(ms + RMS_EPS) * kn_ref[...]
    k = _rope(k, cos, sin)
    v = kv[:, kvd:]
    for h in range(q_heads):
        qo_ref[h] = q[:, h * hdim:(h + 1) * hdim].astype(BF16)
    k_t = jnp.transpose(k)
    for h in range(kv_heads):
        ko_ref[h] = k_t[h * hdim:(h + 1) * hdim].astype(BF16)
        vh = v[:, h * hdim:(h + 1) * hdim]
        vo_ref[h] = jnp.concatenate([vh, jnp.ones_like(vh)], axis=1).astype(BF16)


def _attn_prep(sq, proj, cs_tab, lw, q_heads, kv_heads, hdim, q_col_block, kv_col_block):
    qd, kvd = q_heads * hdim, kv_heads * hdim
    consts = [lw["q_norm"], lw["k_norm"], lw["bd_r"]]
    k_spec = pl.BlockSpec((None, kv_heads, hdim, sq.TB), lambda b, j: (b, 0, 0, j))
    return pl.pallas_call(
        functools.partial(_attn_prep_body, q_heads=q_heads, kv_heads=kv_heads, hdim=hdim),
        grid=sq.grid,
        in_specs=[sq.rows(qd, q_col_block), sq.rows(2 * kvd, kv_col_block),
                  pl.BlockSpec((sq.TB, 2 * kvd), lambda b, j: (j, 0))]
        + [sq.const(c.shape) for c in consts],
        out_specs=[sq.heads(q_heads, hdim), k_spec, sq.heads(kv_heads, 2 * hdim)],
        out_shape=[jax.ShapeDtypeStruct((sq.B, q_heads, sq.S, hdim), BF16),
                   jax.ShapeDtypeStruct((sq.B, kv_heads, hdim, sq.S), BF16),
                   jax.ShapeDtypeStruct((sq.B, kv_heads, sq.S, 2 * hdim), BF16)],
        compiler_params=_params(2),
        name="attn_prep",
    )(proj, proj, cs_tab, *consts)


def _attn_body(q_ref, kt_ref, v_ref, o_ref, s_ref, p_ref, *, group, n_ctx_qblocks, n_ctx_keys):
    tq, hdim = q_ref.shape[1], q_ref.shape[2]
    n_keys = kt_ref.shape[1]

    def attend(nk):
        tiles = range(0, nk, KEY_BLOCK)
        for g in range(group):
            q = q_ref[g]
            rows = slice(g * tq, (g + 1) * tq)
            mx = None
            for t in tiles:
                s = jnp.dot(q, kt_ref[:, t:t + KEY_BLOCK], preferred_element_type=F32)
                s_ref[:, t:t + KEY_BLOCK] = s
                for c0 in range(0, KEY_BLOCK, LANES):
                    part = s[:, c0:c0 + LANES]
                    mx = part if mx is None else jnp.maximum(mx, part)
            m = jnp.broadcast_to(jnp.max(mx, axis=1, keepdims=True), (tq, LANES))
            for t in tiles:
                for c0 in range(t, t + KEY_BLOCK, LANES):
                    p_ref[rows, c0:c0 + LANES] = jnp.exp(s_ref[:, c0:c0 + LANES] - m).astype(BF16)
        acc = jnp.dot(p_ref[:, 0:nk], v_ref[0:nk, :], preferred_element_type=F32)
        out = acc[:, :hdim] / acc[:, hdim:]
        for g in range(group):
            o_ref[:, g * hdim:(g + 1) * hdim] = out[g * tq:(g + 1) * tq, :]

    is_ctx = pl.program_id(2) < n_ctx_qblocks

    @pl.when(is_ctx)
    def _():
        attend(n_ctx_keys)

    @pl.when(jnp.logical_not(is_ctx))
    def _():
        attend(n_keys)


def _attention(sq, q, kt, v):
    b, q_heads, s, hdim = q.shape
    kv_heads = kt.shape[1]
    group = q_heads // kv_heads
    tq = _largest_divisor(sq.ctx, (128, 64))
    assert sq.ctx % KEY_BLOCK == 0 and s % KEY_BLOCK == 0
    return pl.pallas_call(
        functools.partial(_attn_body, group=group, n_ctx_qblocks=sq.ctx // tq, n_ctx_keys=sq.ctx),
        grid=(b, kv_heads, s // tq),
        in_specs=[pl.BlockSpec((None, group, tq, hdim), lambda bi, g, i: (bi, g, i, 0)),
                  pl.BlockSpec((None, None, hdim, s), lambda bi, g, i: (bi, g, 0, 0)),
                  pl.BlockSpec((None, None, s, 2 * hdim), lambda bi, g, i: (bi, g, 0, 0))],
        out_specs=pl.BlockSpec((None, tq, group * hdim), lambda bi, g, i: (bi, i, g)),
        out_shape=jax.ShapeDtypeStruct((b, s, q_heads * hdim), F32),
        scratch_shapes=[pltpu.VMEM((tq, s), F32), pltpu.VMEM((group * tq, s), BF16)],
        compiler_params=_params(3),
        name="attention",
    )(q, kt, v)


def _merge_body(x_ref, ya_ref, yb_ref, yc_ref, gate_ref, mod_ref, wa_ref, wb_ref, wc_ref, wo_ref, o_ref,
                *, gate_row):
    d = x_ref.shape[-1]
    m = gate_ref[:, :d] * _dot(ya_ref[...], wa_ref[...])
    m = m + gate_ref[:, d:2 * d] * _dot(yb_ref[...], wb_ref[...])
    m = m + gate_ref[:, 2 * d:] * _dot(yc_ref[...], wc_ref[...])
    y = _dot(m, wo_ref[...])
    o_ref[...] = x_ref[...] + mod_ref[gate_row:gate_row + 1, :] * y


def _merge(sq, xs, ya, yb, yc, gates, modtab, lw):
    d = xs.shape[-1]
    consts = [lw["w_up_a"], lw["w_up_b"], lw["w_up_c"], lw["w_out"]]
    return pl.pallas_call(
        functools.partial(_merge_body, gate_row=2),
        grid=sq.grid,
        in_specs=[sq.rows(d), sq.rows(ya.shape[-1]), sq.rows(yb.shape[-1]), sq.rows(yc.shape[-1]),
                  sq.rows(3 * d), sq.mod(d)] + [sq.const(c.shape) for c in consts],
        out_specs=sq.rows(d),
        out_shape=jax.ShapeDtypeStruct(xs.shape, F32),
        compiler_params=_params(2),
        name="merge",
    )(xs, ya, yb, yc, gates, modtab, *consts)


def _ffn_body(x_ref, g_ref, mod_ref, w1_ref, w3_ref, w2_ref, o_ref, *, h_chunk):
    x = x_ref[...]
    y = x * lax.rsqrt(jnp.mean(x * x, axis=-1, keepdims=True) + RMS_EPS) * g_ref[...]
    h = (y * (1.0 + mod_ref[4:5, :]) + mod_ref[3:4, :]).astype(BF16)
    hidden = w1_ref.shape[1]
    acc = jnp.zeros(x.shape, F32)
    for c0 in range(0, hidden, h_chunk):
        a = jnp.dot(h, w1_ref[:, c0:c0 + h_chunk], preferred_element_type=F32)
        b = jnp.dot(h, w3_ref[:, c0:c0 + h_chunk], preferred_element_type=F32)
        t = (a * _sigmoid(a) * b).astype(BF16)
        acc = acc + jnp.dot(t, w2_ref[c0:c0 + h_chunk, :], preferred_element_type=F32)
    o_ref[...] = x + mod_ref[5:6, :] * acc


def _ffn(sq, xs, g, modtab, lw):
    d = xs.shape[-1]
    consts = [lw["ffn_w1"], lw["ffn_w3"], lw["ffn_w2"]]
    h_chunk = _largest_divisor(lw["ffn_w1"].shape[1], (512, 256, 128))
    return pl.pallas_call(
        functools.partial(_ffn_body, h_chunk=h_chunk),
        grid=sq.grid,
        in_specs=[sq.rows(d), sq.const((1, d)), sq.mod(d)] + [sq.const(c.shape) for c in consts],
        out_specs=sq.rows(d),
        out_shape=jax.ShapeDtypeStruct(xs.shape, F32),
        compiler_params=_params(2),
        name="ffn",
    )(xs, g.reshape(1, d), modtab, *consts)


def _block_diag_ones(n, seg):
    idx = jnp.arange(n) // seg
    return (idx[:, None] == idx[None, :]).astype(BF16)


def _block_diag2(m):
    z = jnp.zeros_like(m[0])
    return jnp.concatenate([jnp.concatenate([m[0], z], axis=1), jnp.concatenate([z, m[1]], axis=1)], axis=0)


def _pad_to(x, axis, size):
    pad = [(0, 0)] * x.ndim
    pad[axis] = (0, size - x.shape[axis])
    return jnp.pad(x, pad)


def _rope_table(ctx_len, seq_len, hdim, kv_heads):
    rows = seq_len // GRID_W
    row = jnp.repeat(jnp.arange(rows), GRID_W).astype(F32)
    col = jnp.tile(jnp.arange(GRID_W), rows).astype(F32)
    half = hdim // 2
    inv = ROPE_THETA ** (-jnp.arange(0, half, 2, dtype=F32) / half)
    ang = jnp.concatenate([row[:, None] * inv, col[:, None] * inv], axis=-1)
    cos = jnp.repeat(jnp.cos(ang), 2, axis=1)
    sin = jnp.repeat(jnp.sin(ang), 2, axis=1) * jnp.tile(jnp.array([-1.0, 1.0], F32), half)
    cos = jnp.concatenate([jnp.ones((ctx_len, hdim), F32), cos], axis=0)
    sin = jnp.concatenate([jnp.zeros((ctx_len, hdim), F32), sin], axis=0)
    return jnp.concatenate([jnp.tile(cos, (1, kv_heads)), jnp.tile(sin, (1, kv_heads))], axis=1)


def kernel(x, c, ctx, c_ctx, ada_w, ada_b, norm1, norm2, w_in, rwkv_mu_x, rwkv_mu_rkv, rwkv_w0, rwkv_w1, rwkv_w2, rwkv_a0, rwkv_a1, rwkv_a2, rwkv_g1, rwkv_g2, rwkv_k_k, rwkv_k_a, rwkv_r_k, rwkv_lnx_w, rwkv_lnx_b, gdn_conv, gdn_w_alpha, gdn_dt_bias, gdn_A_log, gdn_w_beta, gdn_norm, attn_q_norm, attn_k_norm, w_up_a, w_up_b, w_up_c, w_gate, b_gate, w_out, ffn_w1, ffn_w3, ffn_w2, final_norm):
    batch, seq_len, d = x.shape
    ctx_len = ctx.shape[1]
    depth = ada_w.shape[0]
    sq = _Seq(batch, ctx_len, seq_len)

    r_heads, r_hdim = rwkv_r_k.shape[1], rwkv_r_k.shape[2]
    rdim = r_heads * r_hdim
    g_heads, g_hdim = gdn_w_alpha.shape[-1], gdn_norm.shape[-1]
    gdim = g_heads * g_hdim
    a_hdim = attn_q_norm.shape[-1]
    qd = w_up_c.shape[1]
    q_heads = qd // a_hdim
    kvd = (w_in.shape[-1] - 3 * rdim - 4 * gdim - qd) // 2
    kv_heads = kvd // a_hdim
    assert rdim == gdim == qd and r_hdim == a_hdim, "lane-segment constants are shared between mixers"
    assert (3 * rdim) % (3 * gdim) == 0 and (3 * rdim + 3 * gdim) % gdim == 0
    gdn_col = (3 * rdim) // (3 * gdim)
    z_col = (3 * rdim + 3 * gdim) // gdim
    q_col = (3 * rdim + 4 * gdim) // qd
    assert (3 * rdim + 4 * gdim + qd) % (2 * kvd) == 0
    kv_col = (3 * rdim + 4 * gdim + qd) // (2 * kvd)

    bd_r = _block_diag_ones(rdim, r_hdim)
    bd_g = _block_diag_ones(gdim, g_hdim)
    cs_tab = _rope_table(ctx_len, seq_len, a_hdim, kv_heads)
    n_beta = 2 * g_heads
    assert 2 * n_beta <= LANES

    cond = jnp.concatenate([c, c_ctx[None, :]], axis=0)
    cond = _pad_to(cond, 0, -(-(batch + 1) // SUBLANES) * SUBLANES)

    xs = jnp.concatenate([ctx, x], axis=1)
    m_rows = batch * sq.S
    for l in range(depth):
        g1w = _pad_to(rwkv_g1[l], 1, 2 * LANES)
        lw = {
            "mu": rwkv_mu_x[l], "mu_rkv": rwkv_mu_rkv[l],
            "w1": jnp.concatenate([rwkv_w1[l, 0], rwkv_w1[l, 1]], axis=1).astype(BF16),
            "w2": _block_diag2(rwkv_w2[l]).astype(BF16),
            "w0": rwkv_w0[l].reshape(1, 2 * rdim),
            "a1": jnp.concatenate([rwkv_a1[l, 0], rwkv_a1[l, 1]], axis=1).astype(BF16),
            "a2": _block_diag2(rwkv_a2[l]).astype(BF16),
            "a0": rwkv_a0[l].reshape(1, 2 * rdim),
            "g1": g1w.astype(BF16),
            "g2": _pad_to(rwkv_g2[l], 0, 2 * LANES).astype(BF16),
            "wab": _pad_to(jnp.concatenate([gdn_w_beta[l, 0], gdn_w_beta[l, 1],
                                            gdn_w_alpha[l, 0], gdn_w_alpha[l, 1]], axis=1), 1, LANES).astype(BF16),
            "abb": _pad_to(jnp.concatenate([jnp.zeros((n_beta,), F32), gdn_dt_bias[l].reshape(-1)]), 0, LANES).reshape(1, LANES),
            "alog": _pad_to(jnp.concatenate([jnp.zeros((n_beta,), F32), gdn_A_log[l].reshape(-1)]), 0, LANES).reshape(1, LANES),
            "n_beta": n_beta,
            "k_k": rwkv_k_k[l].reshape(1, rdim), "k_a": rwkv_k_a[l].reshape(1, rdim),
            "r_k": rwkv_r_k[l].reshape(1, rdim),
            "lnx_w": rwkv_lnx_w[l].reshape(1, rdim), "lnx_b": rwkv_lnx_b[l].reshape(1, rdim),
            "bd_r": bd_r, "bd_g": bd_g,
            "conv": jnp.transpose(gdn_conv[l]),
            "gdn_norm": jnp.tile(gdn_norm[l], g_heads).reshape(1, gdim),
            "q_norm": jnp.tile(attn_q_norm[l], q_heads).reshape(1, qd),
            "k_norm": jnp.tile(attn_k_norm[l], kv_heads).reshape(1, kvd),
            "w_up_a": w_up_a[l].astype(BF16), "w_up_b": w_up_b[l].astype(BF16),
            "w_up_c": w_up_c[l].astype(BF16), "w_out": w_out[l].astype(BF16),
            "ffn_w1": ffn_w1[l].astype(BF16), "ffn_w3": ffn_w3[l].astype(BF16), "ffn_w2": ffn_w2[l].astype(BF16),
        }
        mod = _matmul(cond, ada_w[l].astype(BF16), bias=ada_b[l], pre_act="silu")
        mod_x = mod[:batch].reshape(batch, 6, d)
        mod_c = jnp.broadcast_to(mod[batch].reshape(1, 6, d), (batch, 6, d))
        modtab = jnp.stack([mod_c, mod_x], axis=1)

        h = _normmod(sq, xs, norm1[l], modtab, shift_row=0, scale_row=1)
        h2 = h.reshape(m_rows, d)
        proj = _matmul(h2, w_in[l].astype(BF16)).reshape(batch, sq.S, -1)
        gates = _matmul(h2, w_gate[l].astype(BF16), bias=b_gate[l], act="sigmoid").reshape(batch, sq.S, -1)
        lw_dec, a_iclr, g_out, ab = _lora(sq, h, lw)

        r, v, aa, kd, bb = _rwkv_prep(sq, proj, a_iclr, lw)
        rf, rg = _rwkv_intra(sq, r, v, aa, kd, lw_dec, bb, r_heads)
        o_rf, o_rb = _rwkv_scan(sq, rf, rg, r_heads)
        ya = _rwkv_post(sq, o_rf, o_rb, r, v, kd, g_out, lw, r_heads)

        gq, gk, gv = _gdn_prep(sq, proj, lw, gdim, gdn_col)
        gf, gga, ggb = _gdn_intra(sq, gq, gk, gv, ab, g_heads)
        o_gf, o_gb = _gdn_scan(sq, gf, gga, ggb, g_heads)
        yb = _gdn_post(sq, o_gf, o_gb, proj, lw, g_heads, z_col)

        aq, ak, av = _attn_prep(sq, proj, cs_tab, lw, q_heads, kv_heads, a_hdim, q_col, kv_col)
        yc = _attention(sq, aq, ak, av)

        xs = _merge(sq, xs, ya, yb, yc, gates, modtab, lw)
        xs = _ffn(sq, xs, norm2[l], modtab, lw)
    return _final_norm(sq, xs, final_norm)
```

```python
import functools

import jax
import jax.numpy as jnp
from jax import lax
from jax.experimental import pallas as pl
from jax.experimental.pallas import tpu as pltpu

F32 = jnp.float32
BF16 = jnp.bfloat16

RMS_EPS = 1e-6
RWKV_LNX_EPS = 64e-5
ROPE_THETA = 10000.0
GRID_W = 64

SUBLANES = 8
LANES = 128
CHUNK = 64
MAX_TOKEN_BLOCK = 256
MAX_MM_ROWS = 512
KEY_BLOCK = 256
RWKV_PASSES = 1
GDN_PASSES = 1
INV_PASSES = 1
RWKV_INTRA_CHUNKS = 2
GDN_INTRA_CHUNKS = 4
VMEM_LIMIT = 56 * 1024 * 1024


def _sigmoid(x):
    return 1.0 / (1.0 + jnp.exp(-x))


def _softplus(x):
    return jnp.maximum(x, 0.0) + jnp.log(1.0 + jnp.exp(-jnp.abs(x)))


def _split2(x):
    hi = x.astype(BF16)
    lo = (x - hi.astype(F32)).astype(BF16)
    return hi, lo


def _mm(a, b, dims, passes):
    d = functools.partial(lax.dot_general, dimension_numbers=(dims, ((), ())), preferred_element_type=F32)
    if passes == 1:
        return d(a.astype(BF16), b.astype(BF16))
    ah, al = _split2(a)
    bh, bl = _split2(b)
    return d(ah, bh) + (d(ah, bl) + d(al, bh))


def _dot(a, b, passes=1):
    return _mm(a, b, ((1,), (0,)), passes)


def _dot_nt(a, b, passes=1):
    return _mm(a, b, ((1,), (1,)), passes)


def _dot_tn(a, b, passes=1):
    return _mm(a, b, ((0,), (0,)), passes)


def _split3(x):
    hi = x.astype(BF16)
    r1 = x - hi.astype(F32)
    mid = r1.astype(BF16)
    lo = (r1 - mid.astype(F32)).astype(BF16)
    return hi, mid, lo


def _dot_exact_lhs(m01, x):
    d = functools.partial(jnp.dot, preferred_element_type=F32)
    hi, mid, lo = _split3(x)
    return d(m01, hi) + (d(m01, mid) + d(m01, lo))


def _segsum(x, bd_ref):
    d = functools.partial(jnp.dot, preferred_element_type=F32)
    hi, lo = _split2(x)
    bd = bd_ref[...]
    return d(hi, bd) + d(lo, bd)


def _tri_inverse(l, eye):
    n = l.shape[0]
    row = lax.broadcasted_iota(jnp.int32, (n, n), 0)
    col = lax.broadcasted_iota(jnp.int32, (n, n), 1)
    t = eye - jnp.where((row >> 1) == (col >> 1), l, 0.0)
    for k in range(2, n.bit_length()):
        off = ((row >> k) == (col >> k)) & ((row >> (k - 1)) != (col >> (k - 1)))
        t = t - _dot(_dot(t, jnp.where(off, l, 0.0), INV_PASSES), t, INV_PASSES)
    return t


def _row_iota(shape):
    return lax.broadcasted_iota(jnp.int32, shape, 0)


def _shift_rows(x, prev_ref, next_ref, k, first, last):
    n = x.shape[0]
    row = _row_iota(x.shape)
    y = pltpu.roll(x, (-k) % n, axis=0)
    if k < 0:
        for i in range(-k):
            edge = prev_ref[SUBLANES + k + i:SUBLANES + k + i + 1, :]
            edge = jnp.where(first, 0.0, edge)
            y = jnp.where(row == i, edge, y)
    else:
        for i in range(k):
            edge = next_ref[i:i + 1, :]
            edge = jnp.where(last, 0.0, edge)
            y = jnp.where(row == n - k + i, edge, y)
    return y


def _largest_divisor(n, candidates):
    for c in candidates:
        if n % c == 0:
            return c
    raise ValueError(f"no block size among {candidates} divides {n}")


class _Seq:
    def __init__(self, batch, ctx_len, seq_len):
        self.B = batch
        self.ctx = ctx_len
        self.S = ctx_len + seq_len
        self.TB = _largest_divisor(ctx_len, (MAX_TOKEN_BLOCK, 128, 64))
        assert seq_len % self.TB == 0 and self.TB % CHUNK == 0
        self.ncb = ctx_len // self.TB
        self.nblk = self.S // self.TB
        self.ncc = ctx_len // CHUNK
        self.nchunk = self.S // CHUNK
        self.grid = (batch, self.nblk)

    def rows(self, width, col_block=0):
        return pl.BlockSpec((None, self.TB, width), lambda b, j: (b, j, col_block))

    def rows2(self, width):
        return pl.BlockSpec((2, None, self.TB, width), lambda b, j: (0, b, j, 0))

    def heads(self, n_heads, width):
        return pl.BlockSpec((None, n_heads, self.TB, width), lambda b, j: (b, 0, j, 0))

    def prev_rows(self, width, col_block=0):
        per = self.TB // SUBLANES
        return pl.BlockSpec((None, SUBLANES, width),
                            lambda b, j: (b, jnp.maximum(j * per - 1, 0), col_block))

    def next_rows(self, width, col_block=0):
        per = self.TB // SUBLANES
        top = self.S // SUBLANES - 1
        return pl.BlockSpec((None, SUBLANES, width),
                            lambda b, j: (b, jnp.minimum((j + 1) * per, top), col_block))

    def const(self, shape):
        zeros = (0,) * len(shape)
        return pl.BlockSpec(shape, lambda b, j: zeros)

    def mod(self, d_model):
        ncb = self.ncb
        return pl.BlockSpec((None, None, 6, d_model),
                            lambda b, j: (b, (j >= ncb).astype(jnp.int32), 0, 0))

    def edges(self):
        j = pl.program_id(1)
        first = (j == 0) | (j == self.ncb)
        last = (j == self.ncb - 1) | (j == self.nblk - 1)
        return first, last


def _params(n_axes):
    return pltpu.CompilerParams(dimension_semantics=("arbitrary",) * n_axes,
                                vmem_limit_bytes=VMEM_LIMIT)


def _mm_body(*refs, act, pre_act, n_chunk, has_bias):
    if has_bias:
        x_ref, w_ref, b_ref, o_ref = refs
    else:
        x_ref, w_ref, o_ref = refs
        b_ref = None
    x = x_ref[...]
    if pre_act == "silu":
        x = x * _sigmoid(x)
    xb = x.astype(BF16)
    n = o_ref.shape[-1]
    for n0 in range(0, n, n_chunk):
        y = jnp.dot(xb, w_ref[:, n0:n0 + n_chunk], preferred_element_type=F32)
        if b_ref is not None:
            y = y + b_ref[:, n0:n0 + n_chunk]
        if act == "sigmoid":
            y = _sigmoid(y)
        o_ref[:, n0:n0 + n_chunk] = y.astype(o_ref.dtype)


def _matmul(x, w, bias=None, act=None, pre_act=None, out_dtype=F32):
    m, k = x.shape
    n = w.shape[1]
    tm = m if m <= MAX_MM_ROWS else _largest_divisor(m, (MAX_MM_ROWS, 256, 128, 64, 32, 16, 8))
    n_chunk = _largest_divisor(n, (512, 256, 128))
    in_specs = [pl.BlockSpec((tm, k), lambda i: (i, 0)),
                pl.BlockSpec((k, n), lambda i: (0, 0))]
    args = [x, w]
    if bias is not None:
        in_specs.append(pl.BlockSpec((1, n), lambda i: (0, 0)))
        args.append(bias.reshape(1, n))
    return pl.pallas_call(
        functools.partial(_mm_body, act=act, pre_act=pre_act, n_chunk=n_chunk,
                          has_bias=bias is not None),
        grid=(m // tm,),
        in_specs=in_specs,
        out_specs=pl.BlockSpec((tm, n), lambda i: (i, 0)),
        out_shape=jax.ShapeDtypeStruct((m, n), out_dtype),
        compiler_params=_params(1),
        name="matmul",
    )(*args)


def _normmod_body(x_ref, g_ref, mod_ref, o_ref, *, shift_row, scale_row):
    x = x_ref[...]
    y = x * lax.rsqrt(jnp.mean(x * x, axis=-1, keepdims=True) + RMS_EPS)
    y = y * g_ref[...]
    o_ref[...] = y * (1.0 + mod_ref[scale_row:scale_row + 1, :]) + mod_ref[shift_row:shift_row + 1, :]


def _normmod(sq, xs, g, modtab, shift_row, scale_row):
    d = xs.shape[-1]
    return pl.pallas_call(
        functools.partial(_normmod_body, shift_row=shift_row, scale_row=scale_row),
        grid=sq.grid,
        in_specs=[sq.rows(d), sq.const((1, d)), sq.mod(d)],
        out_specs=sq.rows(d),
        out_shape=jax.ShapeDtypeStruct(xs.shape, F32),
        compiler_params=_params(2),
        name="normmod",
    )(xs, g.reshape(1, d), modtab)


def _final_norm_body(x_ref, g_ref, o_ref):
    x = x_ref[...]
    y = x * lax.rsqrt(jnp.mean(x * x, axis=-1, keepdims=True) + RMS_EPS)
    o_ref[...] = y * g_ref[...]


def _final_norm(sq, xs, g):
    d = xs.shape[-1]
    ncb = sq.ncb
    return pl.pallas_call(
        _final_norm_body,
        grid=(sq.B, sq.nblk - ncb),
        in_specs=[pl.BlockSpec((None, sq.TB, d), lambda b, j: (b, j + ncb, 0)),
                  pl.BlockSpec((1, d), lambda b, j: (0, 0))],
        out_specs=pl.BlockSpec((None, sq.TB, d), lambda b, j: (b, j, 0)),
        out_shape=jax.ShapeDtypeStruct((sq.B, sq.S - sq.ctx, d), F32),
        compiler_params=_params(2),
        name="final_norm",
    )(xs, g.reshape(1, d))


def _lora_body(h_ref, hp_ref, hn_ref, mu_ref, w1_ref, w2_ref, w0_ref, a1_ref, a2_ref, a0_ref,
               g1_ref, g2_ref, wab_ref, abb_ref, alog_ref,
               lw_ref, a_ref, g_ref, ab_ref, *, sq, rdim, n_beta):
    first, last = sq.edges()
    h = h_ref[...]
    nb = 0.5 * (_shift_rows(h, hp_ref, hn_ref, -1, first, last)
                + _shift_rows(h, hp_ref, hn_ref, 1, first, last))
    dlt = nb - h
    xw = h + dlt * mu_ref[0:1, :]
    xa = h + dlt * mu_ref[1:2, :]
    xg = h + dlt * mu_ref[2:3, :]
    wl = w0_ref[...] + _dot(jnp.tanh(_dot(xw, w1_ref[...])), w2_ref[...])
    w_log = -_softplus(-wl) - 0.5
    lw = -jnp.exp(w_log)
    lw_ref[0] = lw[:, :rdim]
    lw_ref[1] = lw[:, rdim:]
    a = _sigmoid(a0_ref[...] + _dot(_dot(xa, a1_ref[...]), a2_ref[...]))
    a_ref[0] = a[:, :rdim]
    a_ref[1] = a[:, rdim:]
    g_ref[...] = _dot(_sigmoid(_dot(xg, g1_ref[...])), g2_ref[...])
    z = _dot(h, wab_ref[...])
    col = lax.broadcasted_iota(jnp.int32, z.shape, 1)
    gl = -jnp.exp(alog_ref[...]) * _softplus(z + abb_ref[...])
    ab_ref[...] = jnp.where(col < n_beta, _sigmoid(z), gl)


def _lora(sq, h, lw):
    d = h.shape[-1]
    rdim = lw["w0"].shape[-1] // 2
    consts = [lw["mu"], lw["w1"], lw["w2"], lw["w0"], lw["a1"], lw["a2"], lw["a0"],
              lw["g1"], lw["g2"], lw["wab"], lw["abb"], lw["alog"]]
    bsd = (sq.B, sq.S)
    return pl.pallas_call(
        functools.partial(_lora_body, sq=sq, rdim=rdim, n_beta=lw["n_beta"]),
        grid=sq.grid,
        in_specs=[sq.rows(d), sq.prev_rows(d), sq.next_rows(d)] + [sq.const(c.shape) for c in consts],
        out_specs=[sq.rows2(rdim), sq.rows2(rdim), sq.rows(rdim), sq.rows(LANES)],
        out_shape=[jax.ShapeDtypeStruct((2,) + bsd + (rdim,), F32),
                   jax.ShapeDtypeStruct((2,) + bsd + (rdim,), F32),
                   jax.ShapeDtypeStruct(bsd + (rdim,), F32),
                   jax.ShapeDtypeStruct(bsd + (LANES,), F32)],
        compiler_params=_params(2),
        name="lora",
    )(h, h, h, *consts)


def _rwkv_prep_body(p_ref, pp_ref, pn_ref, a_ref, mu_ref, kk_ref, ka_ref, bd_ref,
                    r_ref, v_ref, aa_ref, kd_ref, bb_ref, *, sq, rdim):
    first, last = sq.edges()
    p = p_ref[...]
    nb = 0.5 * (_shift_rows(p, pp_ref, pn_ref, -1, first, last)
                + _shift_rows(p, pp_ref, pn_ref, 1, first, last))
    dlt = nb - p
    r = p[:, :rdim] + dlt[:, :rdim] * mu_ref[0:1, :]
    k = p[:, rdim:2 * rdim] + dlt[:, rdim:2 * rdim] * mu_ref[1:2, :]
    v = p[:, 2 * rdim:] + dlt[:, 2 * rdim:] * mu_ref[2:3, :]
    t = k * kk_ref[...]
    kk = t * lax.rsqrt(_segsum(t * t, bd_ref) + RMS_EPS)
    r_ref[...] = r
    v_ref[...] = v
    aa_ref[...] = -kk
    for d in range(2):
        a = a_ref[d]
        kd_ref[d] = k * (1.0 + (a - 1.0) * ka_ref[...])
        bb_ref[d] = kk * a


def _rwkv_prep(sq, proj, a, lw):
    rdim = a.shape[-1]
    consts = [lw["mu_rkv"], lw["k_k"], lw["k_a"], lw["bd_r"]]
    one = jax.ShapeDtypeStruct((sq.B, sq.S, rdim), F32)
    two = jax.ShapeDtypeStruct((2, sq.B, sq.S, rdim), F32)
    return pl.pallas_call(
        functools.partial(_rwkv_prep_body, sq=sq, rdim=rdim),
        grid=sq.grid,
        in_specs=[sq.rows(3 * rdim), sq.prev_rows(3 * rdim), sq.next_rows(3 * rdim), sq.rows2(rdim)]
        + [sq.const(c.shape) for c in consts],
        out_specs=[sq.rows(rdim), sq.rows(rdim), sq.rows(rdim), sq.rows2(rdim), sq.rows2(rdim)],
        out_shape=[one, one, one, two, two],
        compiler_params=_params(2),
        name="rwkv_prep",
    )(proj, proj, proj, a, *consts)


def _chunk_index(d, i, ncc, nchunk):
    back = jnp.where(i < ncc, ncc - 1 - i, nchunk + ncc - 1 - i)
    return jnp.where(d == 0, i, back)


def _causal_masks(d, c, reps=1):
    row = lax.broadcasted_iota(jnp.int32, (c, reps * c), 0)
    col = lax.broadcasted_iota(jnp.int32, (c, reps * c), 1) & (c - 1)
    delta = jnp.where(d == 0, row - col, col - row)
    eye = jnp.where(row == col, 1.0, 0.0).astype(F32)
    return delta >= 0, delta > 0, eye


def _stacked_mask(d, c):
    row = lax.broadcasted_iota(jnp.int32, (2 * c, c), 0)
    col = lax.broadcasted_iota(jnp.int32, (2 * c, c), 1)
    rr = row & (c - 1)
    delta = jnp.where(d == 0, rr - col, col - rr)
    return (delta > 0) | ((row >= c) & (delta == 0))


def _tri_inverse_many(ls, eye):
    n = ls[0].shape[0]
    row = lax.broadcasted_iota(jnp.int32, (n, n), 0)
    col = lax.broadcasted_iota(jnp.int32, (n, n), 1)
    same = (row >> 1) == (col >> 1)
    ts = [eye - jnp.where(same, l, 0.0) for l in ls]
    for k in range(2, n.bit_length()):
        off = ((row >> k) == (col >> k)) & ((row >> (k - 1)) != (col >> (k - 1)))
        tl = [_dot(t, jnp.where(off, l, 0.0), INV_PASSES) for t, l in zip(ts, ls)]
        ts = [t - _dot(x, t, INV_PASSES) for x, t in zip(tl, ts)]
    return ts


def _rwkv_intra_body(r_ref, v_ref, aa_ref, kd_ref, lw_ref, bb_ref, f_ref, g_ref, *, heads, hdim, cps):
    c = CHUNK
    sl = [slice(h * hdim, (h + 1) * hdim) for h in range(heads)]
    keys, a_h, r_h, v_h, bt_h, kt_h, ec_h, mask2, eye = [], {}, {}, {}, {}, {}, {}, {}, None
    for d in range(2):
        incl, _, eye = _causal_masks(d, c)
        m01 = jnp.where(incl, 1.0, 0.0).astype(BF16)
        mask2[d] = _stacked_mask(d, c)
        for j in range(cps):
            rows = slice(j * c, (j + 1) * c)
            lw = lw_ref[d, rows, :]
            cum = _dot_exact_lhs(m01, lw)
            tot = jnp.sum(lw, axis=0, keepdims=True)
            e_neg = jnp.exp(-cum)
            a_t = aa_ref[rows, :] * jnp.exp(cum - lw)
            r_t = r_ref[rows, :] * jnp.exp(cum)
            v = v_ref[rows, :]
            b_tt = jnp.transpose(bb_ref[d, rows, :] * e_neg)
            k_tt = jnp.transpose(kd_ref[d, rows, :] * e_neg)
            e_col = jnp.transpose(jnp.broadcast_to(jnp.exp(tot), lw.shape))
            for h in range(heads):
                key = (d, j, h)
                keys.append(key)
                a_h[key], r_h[key], v_h[key] = a_t[:, sl[h]], r_t[:, sl[h]], v[:, sl[h]]
                bt_h[key], kt_h[key], ec_h[key] = b_tt[sl[h]], k_tt[sl[h]], e_col[sl[h]]
    ar = {k: jnp.concatenate([a_h[k], r_h[k]], axis=0) for k in keys}
    mb = {k: jnp.where(mask2[k[0]], _dot(ar[k], bt_h[k], RWKV_PASSES), 0.0) for k in keys}
    mk = {k: jnp.where(mask2[k[0]], _dot(ar[k], kt_h[k], RWKV_PASSES), 0.0) for k in keys}
    x = {k: _dot(mk[k], v_h[k], RWKV_PASSES) for k in keys}
    h0 = {k: _dot(ec_h[k] * kt_h[k], v_h[k], RWKV_PASSES) for k in keys}
    t = dict(zip(keys, _tri_inverse_many([-mb[k][:c] for k in keys], eye)))
    wt = {k: _dot(t[k], a_h[k], RWKV_PASSES) for k in keys}
    ut = {k: _dot(t[k], x[k][:c], RWKV_PASSES) for k in keys}
    for k in keys:
        d, j, h = k
        o = j * 4 * c
        f_ref[d, h, o:o + c] = ut[k]
        f_ref[d, h, o + c:o + 2 * c] = x[k][c:]
        f_ref[d, h, o + 2 * c:o + 3 * c] = h0[k]
        f_ref[d, h, o + 3 * c:o + 4 * c] = ec_h[k]
        g_ref[d, h, o:o + c] = wt[k].astype(BF16)
        g_ref[d, h, o + c:o + 2 * c] = r_h[k].astype(BF16)
        g_ref[d, h, o + 2 * c:o + 3 * c] = mb[k][c:].astype(BF16)
        g_ref[d, h, o + 3 * c:o + 4 * c] = (ec_h[k] * bt_h[k]).astype(BF16)


def _rwkv_intra(sq, r, v, aa, kd, lw, bb, heads):
    rdim = r.shape[-1]
    hdim = rdim // heads
    assert hdim == CHUNK, "the hand-over tiles share one [CHUNK, head_dim] shape"
    cps = _largest_divisor(sq.nchunk, (RWKV_INTRA_CHUNKS, 1))
    rows = cps * CHUNK
    shared = pl.BlockSpec((None, rows, rdim), lambda b, i: (b, i, 0))
    per_dir = pl.BlockSpec((2, None, rows, rdim), lambda b, i: (0, b, i, 0))
    out = pl.BlockSpec((2, None, heads, 4 * rows, hdim), lambda b, i: (0, b, 0, i, 0))
    shape = (2, sq.B, heads, sq.nchunk * 4 * CHUNK, hdim)
    return pl.pallas_call(
        functools.partial(_rwkv_intra_body, heads=heads, hdim=hdim, cps=cps),
        grid=(sq.B, sq.nchunk // cps),
        in_specs=[shared, shared, shared, per_dir, per_dir, per_dir],
        out_specs=[out, out],
        out_shape=[jax.ShapeDtypeStruct(shape, F32), jax.ShapeDtypeStruct(shape, BF16)],
        compiler_params=_params(2),
        name="rwkv_intra",
    )(r, v, aa, kd, lw, bb)


def _rwkv_scan_body(f0_ref, g0_ref, f1_ref, g1_ref, of_ref, ob_ref, h_ref, *, heads, hdim):
    @pl.when(pl.program_id(1) == 0)
    def _():
        h_ref[...] = jnp.zeros_like(h_ref)

    c = CHUNK
    fs, gs, outs = (f0_ref, f1_ref), (g0_ref, g1_ref), (of_ref, ob_ref)
    chains = [(d, h) for d in range(2) for h in range(heads)]
    dot = functools.partial(jnp.dot, preferred_element_type=F32)
    st = {ch: h_ref[ch[0], ch[1]] for ch in chains}
    m2 = {(d, h): dot(gs[d][h, 0:2 * c, :], st[d, h].astype(BF16)) for d, h in chains}
    ub = {(d, h): (fs[d][h, 0:c, :] + m2[d, h][:c]).astype(BF16) for d, h in chains}
    for d, h in chains:
        y = fs[d][h, c:2 * c, :] + m2[d, h][c:] + dot(gs[d][h, 2 * c:3 * c, :], ub[d, h])
        outs[d][:, h * hdim:(h + 1) * hdim] = y
        h_ref[d, h] = (fs[d][h, 3 * c:4 * c, :] * st[d, h] + fs[d][h, 2 * c:3 * c, :]
                       + dot(gs[d][h, 3 * c:4 * c, :], ub[d, h]))


def _backward_chunk(i, ncc, nchunk):
    return jnp.where(i < ncc, ncc - 1 - i, nchunk + ncc - 1 - i)


def _scan_specs(sq, heads, rows, width):
    ncc, nchunk = sq.ncc, sq.nchunk
    fwd = pl.BlockSpec((None, None, heads, rows, width), lambda b, i: (0, b, 0, i, 0))
    bwd = pl.BlockSpec((None, None, heads, rows, width),
                       lambda b, i: (1, b, 0, _backward_chunk(i, ncc, nchunk), 0))
    return fwd, bwd


def _scan_out_specs(sq, width):
    ncc, nchunk = sq.ncc, sq.nchunk
    fwd = pl.BlockSpec((None, CHUNK, width), lambda b, i: (b, i, 0))
    bwd = pl.BlockSpec((None, CHUNK, width), lambda b, i: (b, _backward_chunk(i, ncc, nchunk), 0))
    return [fwd, bwd]


def _rwkv_scan(sq, f, g, heads):
    hdim = f.shape[-1]
    rdim = heads * hdim
    ff, fb = _scan_specs(sq, heads, 4 * CHUNK, hdim)
    one = jax.ShapeDtypeStruct((sq.B, sq.S, rdim), F32)
    return pl.pallas_call(
        functools.partial(_rwkv_scan_body, heads=heads, hdim=hdim),
        grid=(sq.B, sq.nchunk),
        in_specs=[ff, ff, fb, fb],
        out_specs=_scan_out_specs(sq, rdim),
        out_shape=[one, one],
        scratch_shapes=[pltpu.VMEM((2, heads, hdim, hdim), F32)],
        compiler_params=_params(2),
        name="rwkv_scan",
    )(f, g, f, g)


def _rwkv_post_body(of_ref, ob_ref, r_ref, v_ref, kd_ref, g_ref, rk_ref, lnw_ref, lnb_ref, bd_ref, y_ref, *, hdim):
    o = of_ref[...] + ob_ref[...]
    inv_n = 1.0 / hdim
    mean = _segsum(o, bd_ref) * inv_n
    cen = o - mean
    var = _segsum(cen * cen, bd_ref) * inv_n
    o = cen * lax.rsqrt(var + RWKV_LNX_EPS) * lnw_ref[...] + lnb_ref[...]
    r = r_ref[...]
    v = v_ref[...]
    rk = rk_ref[...]
    for d in range(2):
        o = o + _segsum(r * kd_ref[d] * rk, bd_ref) * v
    y_ref[...] = o * g_ref[...]


def _rwkv_post(sq, o_f, o_b, r, v, kd, g, lw, heads):
    rdim = r.shape[-1]
    consts = [lw["r_k"], lw["lnx_w"], lw["lnx_b"], lw["bd_r"]]
    return pl.pallas_call(
        functools.partial(_rwkv_post_body, hdim=rdim // heads),
        grid=sq.grid,
        in_specs=[sq.rows(rdim), sq.rows(rdim), sq.rows(rdim), sq.rows(rdim), sq.rows2(rdim), sq.rows(rdim)]
        + [sq.const(c.shape) for c in consts],
        out_specs=sq.rows(rdim),
        out_shape=jax.ShapeDtypeStruct((sq.B, sq.S, rdim), F32),
        compiler_params=_params(2),
        name="rwkv_post",
    )(o_f, o_b, r, v, kd, g, *consts)


def _gdn_prep_body(p_ref, pp_ref, pn_ref, cw_ref, bd_ref, q_ref, k_ref, v_ref, *, sq, gdim):
    first, last = sq.edges()
    p = p_ref[...]
    width = cw_ref.shape[0]
    half = width // 2
    acc = p * cw_ref[half:half + 1, :]
    for j in range(width):
        if j != half:
            acc = acc + _shift_rows(p, pp_ref, pn_ref, j - half, first, last) * cw_ref[j:j + 1, :]
    y = acc * _sigmoid(acc)
    q = y[:, :gdim]
    k = y[:, gdim:2 * gdim]
    q_ref[...] = q * lax.rsqrt(_segsum(q * q, bd_ref) + RMS_EPS)
    k_ref[...] = k * lax.rsqrt(_segsum(k * k, bd_ref) + RMS_EPS)
    v_ref[...] = y[:, 2 * gdim:]


def _gdn_prep(sq, proj, lw, gdim, col_block):
    consts = [lw["conv"], lw["bd_g"]]
    one = jax.ShapeDtypeStruct((sq.B, sq.S, gdim), F32)
    w = 3 * gdim
    return pl.pallas_call(
        functools.partial(_gdn_prep_body, sq=sq, gdim=gdim),
        grid=sq.grid,
        in_specs=[sq.rows(w, col_block), sq.prev_rows(w, col_block), sq.next_rows(w, col_block)]
        + [sq.const(c.shape) for c in consts],
        out_specs=[sq.rows(gdim)] * 3,
        out_shape=[one, one, one],
        compiler_params=_params(2),
        name="gdn_prep",
    )(proj, proj, proj, *consts)


def _gdn_intra_body(q_ref, k_ref, v_ref, ab_ref, f_ref, ga_ref, gb_ref, *, heads, hdim, cps):
    c = CHUNK
    scale = hdim ** -0.5
    sl = [slice(h * hdim, (h + 1) * hdim) for h in range(heads)]
    keys = []
    beta, gc, g_last, gc_row, decay, kb, kq, kt_h, vb, strict_of, eye = {}, {}, {}, {}, {}, {}, {}, {}, {}, {}, None
    for j in range(cps):
        rows = slice(j * c, (j + 1) * c)
        ab = ab_ref[rows, :]
        tot_all = jnp.sum(ab, axis=0, keepdims=True)
        lane = lax.broadcasted_iota(jnp.int32, ab.shape, 1)
        k_all = k_ref[rows, :]
        k_tt = jnp.transpose(k_all)
        for d in range(2):
            incl, strict, eye = _causal_masks(d, c)
            strict_of[d] = strict
            gc_all = _dot_exact_lhs(jnp.where(incl, 1.0, 0.0).astype(BF16), ab)
            gc_all_t = jnp.transpose(gc_all)
            sub = lax.broadcasted_iota(jnp.int32, gc_all_t.shape, 0)
            for h in range(heads):
                key = (d, j, h)
                keys.append(key)
                pick_b = lane == d * heads + h
                pick_g = lane == 2 * heads + d * heads + h
                beta[key] = jnp.sum(jnp.where(pick_b, ab, 0.0), axis=1, keepdims=True)
                gc[key] = jnp.sum(jnp.where(pick_g, gc_all, 0.0), axis=1, keepdims=True)
                g_last[key] = jnp.sum(jnp.where(pick_g[:1], tot_all, 0.0), axis=1, keepdims=True)
                gc_row[key] = jnp.sum(jnp.where(sub == 2 * heads + d * heads + h, gc_all_t, 0.0),
                                      axis=0, keepdims=True)
                diff = gc[key] - gc_row[key]
                decay[key] = jnp.where(incl, jnp.exp(jnp.where(incl, diff, 0.0)), 0.0)
                kb[key] = k_all[:, sl[h]] * beta[key]
                kq[key] = jnp.concatenate([kb[key], q_ref[rows, sl[h]] * scale], axis=0)
                kt_h[key] = k_tt[sl[h]]
                vb[key] = v_ref[rows, sl[h]] * beta[key]
    m = {k: _dot(kq[k], kt_h[k], GDN_PASSES) for k in keys}
    lower = [jnp.where(strict_of[k[0]], m[k][:c] * decay[k], 0.0) for k in keys]
    t = dict(zip(keys, _tri_inverse_many(lower, eye)))
    e_gc = {k: jnp.exp(gc[k]) for k in keys}
    sol = {k: _dot(t[k], jnp.concatenate([vb[k], kb[k] * e_gc[k]], axis=1), GDN_PASSES) for k in keys}
    for k in keys:
        d, j, h = k
        of, oa, ob = j * (c + SUBLANES), j * 2 * c, j * (c + hdim)
        f_ref[d, h, of:of + c] = sol[k][:, :hdim]
        f_ref[d, h, of + c:of + c + SUBLANES] = jnp.broadcast_to(jnp.exp(g_last[k]), (SUBLANES, hdim))
        ga_ref[d, h, oa:oa + c] = sol[k][:, hdim:].astype(BF16)
        ga_ref[d, h, oa + c:oa + 2 * c] = (kq[k][c:] * e_gc[k]).astype(BF16)
        gb_ref[d, h, ob:ob + c] = (m[k][c:] * decay[k]).astype(BF16)
        gb_ref[d, h, ob + c:ob + c + hdim] = (kt_h[k] * jnp.exp(g_last[k] - gc_row[k])).astype(BF16)


def _gdn_intra(sq, q, k, v, ab, heads):
    gdim = q.shape[-1]
    hdim = gdim // heads
    cps = _largest_divisor(sq.nchunk, (GDN_INTRA_CHUNKS, 2, 1))
    shared = pl.BlockSpec((None, cps * CHUNK, gdim), lambda b, i: (b, i, 0))
    small = pl.BlockSpec((None, cps * CHUNK, LANES), lambda b, i: (b, i, 0))
    rows = (CHUNK + SUBLANES, 2 * CHUNK, CHUNK + hdim)
    widths = (hdim, hdim, CHUNK)
    dtypes = (F32, BF16, BF16)
    outs = [pl.BlockSpec((2, None, heads, cps * r, w), lambda b, i: (0, b, 0, i, 0)) for r, w in zip(rows, widths)]
    shapes = [jax.ShapeDtypeStruct((2, sq.B, heads, sq.nchunk * r, w), dt) for r, w, dt in zip(rows, widths, dtypes)]
    return pl.pallas_call(
        functools.partial(_gdn_intra_body, heads=heads, hdim=hdim, cps=cps),
        grid=(sq.B, sq.nchunk // cps),
        in_specs=[shared, shared, shared, small],
        out_specs=outs,
        out_shape=shapes,
        compiler_params=_params(2),
        name="gdn_intra",
    )(q, k, v, ab)


def _gdn_scan_body(f0_ref, ga0_ref, gb0_ref, f1_ref, ga1_ref, gb1_ref, of_ref, ob_ref, s_ref, *, heads, hdim):
    @pl.when(pl.program_id(1) == 0)
    def _():
        s_ref[...] = jnp.zeros_like(s_ref)

    c = CHUNK
    fs, gas, gbs, outs = (f0_ref, f1_ref), (ga0_ref, ga1_ref), (gb0_ref, gb1_ref), (of_ref, ob_ref)
    chains = [(d, h) for d in range(2) for h in range(heads)]
    dot = functools.partial(jnp.dot, preferred_element_type=F32)
    st = {ch: s_ref[ch[0], ch[1]] for ch in chains}
    m = {(d, h): dot(gas[d][h], st[d, h].astype(BF16)) for d, h in chains}
    vn = {(d, h): (fs[d][h, 0:c, :] - m[d, h][:c]).astype(BF16) for d, h in chains}
    for d, h in chains:
        outs[d][:, h * hdim:(h + 1) * hdim] = m[d, h][c:] + dot(gbs[d][h, 0:c, :], vn[d, h])
        s_ref[d, h] = st[d, h] * fs[d][h, c:c + 1, :] + dot(gbs[d][h, c:c + hdim, :], vn[d, h])


def _gdn_scan(sq, f, ga, gb, heads):
    hdim = f.shape[-1]
    gdim = heads * hdim
    specs = []
    for arr in (f, ga, gb):
        specs.append(_scan_specs(sq, heads, arr.shape[3] // sq.nchunk, arr.shape[4]))
    one = jax.ShapeDtypeStruct((sq.B, sq.S, gdim), F32)
    return pl.pallas_call(
        functools.partial(_gdn_scan_body, heads=heads, hdim=hdim),
        grid=(sq.B, sq.nchunk),
        in_specs=[s[0] for s in specs] + [s[1] for s in specs],
        out_specs=_scan_out_specs(sq, gdim),
        out_shape=[one, one],
        scratch_shapes=[pltpu.VMEM((2, heads, hdim, hdim), F32)],
        compiler_params=_params(2),
        name="gdn_scan",
    )(f, ga, gb, f, ga, gb)


def _gdn_post_body(of_ref, ob_ref, z_ref, gn_ref, bd_ref, y_ref, *, hdim):
    o = of_ref[...] + ob_ref[...]
    ms = _segsum(o * o, bd_ref) * (1.0 / hdim)
    z = z_ref[...]
    y_ref[...] = o * lax.rsqrt(ms + RMS_EPS) * gn_ref[...] * (z * _sigmoid(z))


def _gdn_post(sq, o_f, o_b, proj, lw, heads, z_col_block):
    gdim = o_f.shape[-1]
    consts = [lw["gdn_norm"], lw["bd_g"]]
    return pl.pallas_call(
        functools.partial(_gdn_post_body, hdim=gdim // heads),
        grid=sq.grid,
        in_specs=[sq.rows(gdim), sq.rows(gdim), sq.rows(gdim, z_col_block)] + [sq.const(c.shape) for c in consts],
        out_specs=sq.rows(gdim),
        out_shape=jax.ShapeDtypeStruct((sq.B, sq.S, gdim), F32),
        compiler_params=_params(2),
        name="gdn_post",
    )(o_f, o_b, proj, *consts)


def _rope(x, cos, sin_signed):
    n = x.shape[-1]
    lane = lax.broadcasted_iota(jnp.int32, x.shape, 1)
    partner = jnp.where((lane & 1) == 0, pltpu.roll(x, n - 1, axis=1), pltpu.roll(x, 1, axis=1))
    return x * cos + partner * sin_signed


def _attn_prep_body(q_ref, kv_ref, cs_ref, qn_ref, kn_ref, bd_ref, qo_ref, ko_ref, vo_ref,
                    *, q_heads, kv_heads, hdim):
    kvd = kv_heads * hdim
    cos = cs_ref[:, :kvd]
    sin = cs_ref[:, kvd:]
    reps = q_heads // kv_heads
    cos_q = jnp.concatenate([cos] * reps, axis=1)
    sin_q = jnp.concatenate([sin] * reps, axis=1)
    inv_n = 1.0 / hdim
    q = q_ref[...]
    q = q * lax.rsqrt(_segsum(q * q, bd_ref) * inv_n + RMS_EPS) * qn_ref[...]
    q = _rope(q, cos_q, sin_q) * (hdim ** -0.5)
    kv = kv_ref[...]
    k = kv[:, :kvd]
    kbd = bd_ref[:kvd, :kvd]
    hi, lo = _split2(k * k)
    ms = (jnp.dot(hi, kbd, preferred_element_type=F32) + jnp.dot(lo, kbd, preferred_element_type=F32)) * inv_n
    k = k * lax.rsqrt(ms + RMS_EPS) * kn_ref[...]
    k = _rope(k, cos, sin)
    v = kv[:, kvd:]
    for h in range(q_heads):
        qo_ref[h] = q[:, h * hdim:(h + 1) * hdim].astype(BF16)
    k_t = jnp.transpose(k)
    for h in range(kv_heads):
        ko_ref[h] = k_t[h * hdim:(h + 1) * hdim].astype(BF16)
        vh = v[:, h * hdim:(h + 1) * hdim]
        vo_ref[h] = jnp.concatenate([vh, jnp.ones_like(vh)], axis=1).astype(BF16)


def _attn_prep(sq, proj, cs_tab, lw, q_heads, kv_heads, hdim, q_col_block, kv_col_block):
    qd, kvd = q_heads * hdim, kv_heads * hdim
    consts = [lw["q_norm"], lw["k_norm"], lw["bd_r"]]
    k_spec = pl.BlockSpec((None, kv_heads, hdim, sq.TB), lambda b, j: (b, 0, 0, j))
    return pl.pallas_call(
        functools.partial(_attn_prep_body, q_heads=q_heads, kv_heads=kv_heads, hdim=hdim),
        grid=sq.grid,
        in_specs=[sq.rows(qd, q_col_block), sq.rows(2 * kvd, kv_col_block),
                  pl.BlockSpec((sq.TB, 2 * kvd), lambda b, j: (j, 0))]
        + [sq.const(c.shape) for c in consts],
        out_specs=[sq.heads(q_heads, hdim), k_spec, sq.heads(kv_heads, 2 * hdim)],
        out_shape=[jax.ShapeDtypeStruct((sq.B, q_heads, sq.S, hdim), BF16),
                   jax.ShapeDtypeStruct((sq.B, kv_heads, hdim, sq.S), BF16),
                   jax.ShapeDtypeStruct((sq.B, kv_heads, sq.S, 2 * hdim), BF16)],
        compiler_params=_params(2),
        name="attn_prep",
    )(proj, proj, cs_tab, *consts)


def _attn_body(q_ref, kt_ref, v_ref, o_ref, s_ref, p_ref, *, group, n_ctx_qblocks, n_ctx_keys):
    tq, hdim = q_ref.shape[1], q_ref.shape[2]
    n_keys = kt_ref.shape[1]

    def attend(nk):
        tiles = range(0, nk, KEY_BLOCK)
        for g in range(group):
            q = q_ref[g]
            rows = slice(g * tq, (g + 1) * tq)
            mx = None
            for t in tiles:
                s = jnp.dot(q, kt_ref[:, t:t + KEY_BLOCK], preferred_element_type=F32)
                s_ref[:, t:t + KEY_BLOCK] = s
                for c0 in range(0, KEY_BLOCK, LANES):
                    part = s[:, c0:c0 + LANES]
                    mx = part if mx is None else jnp.maximum(mx, part)
            m = jnp.broadcast_to(jnp.max(mx, axis=1, keepdims=True), (tq, LANES))
            for t in tiles:
                for c0 in range(t, t + KEY_BLOCK, LANES):
                    p_ref[rows, c0:c0 + LANES] = jnp.exp(s_ref[:, c0:c0 + LANES] - m).astype(BF16)
        acc = jnp.dot(p_ref[:, 0:nk], v_ref[0:nk, :], preferred_element_type=F32)
        out = acc[:, :hdim] / acc[:, hdim:]
        for g in range(group):
            o_ref[:, g * hdim:(g + 1) * hdim] = out[g * tq:(g + 1) * tq, :]

    is_ctx = pl.program_id(2) < n_ctx_qblocks

    @pl.when(is_ctx)
    def _():
        attend(n_ctx_keys)

    @pl.when(jnp.logical_not(is_ctx))
    def _():
        attend(n_keys)


def _attention(sq, q, kt, v):
    b, q_heads, s, hdim = q.shape
    kv_heads = kt.shape[1]
    group = q_heads // kv_heads
    tq = _largest_divisor(sq.ctx, (128, 64))
    assert sq.ctx % KEY_BLOCK == 0 and s % KEY_BLOCK == 0
    return pl.pallas_call(
        functools.partial(_attn_body, group=group, n_ctx_qblocks=sq.ctx // tq, n_ctx_keys=sq.ctx),
        grid=(b, kv_heads, s // tq),
        in_specs=[pl.BlockSpec((None, group, tq, hdim), lambda bi, g, i: (bi, g, i, 0)),
                  pl.BlockSpec((None, None, hdim, s), lambda bi, g, i: (bi, g, 0, 0)),
                  pl.BlockSpec((None, None, s, 2 * hdim), lambda bi, g, i: (bi, g, 0, 0))],
        out_specs=pl.BlockSpec((None, tq, group * hdim), lambda bi, g, i: (bi, i, g)),
        out_shape=jax.ShapeDtypeStruct((b, s, q_heads * hdim), F32),
        scratch_shapes=[pltpu.VMEM((tq, s), F32), pltpu.VMEM((group * tq, s), BF16)],
        compiler_params=_params(3),
        name="attention",
    )(q, kt, v)


def _merge_body(x_ref, ya_ref, yb_ref, yc_ref, gate_ref, mod_ref, wa_ref, wb_ref, wc_ref, wo_ref, o_ref,
                *, gate_row):
    d = x_ref.shape[-1]
    m = gate_ref[:, :d] * _dot(ya_ref[...], wa_ref[...])
    m = m + gate_ref[:, d:2 * d] * _dot(yb_ref[...], wb_ref[...])
    m = m + gate_ref[:, 2 * d:] * _dot(yc_ref[...], wc_ref[...])
    y = _dot(m, wo_ref[...])
    o_ref[...] = x_ref[...] + mod_ref[gate_row:gate_row + 1, :] * y


def _merge(sq, xs, ya, yb, yc, gates, modtab, lw):
    d = xs.shape[-1]
    consts = [lw["w_up_a"], lw["w_up_b"], lw["w_up_c"], lw["w_out"]]
    return pl.pallas_call(
        functools.partial(_merge_body, gate_row=2),
        grid=sq.grid,
        in_specs=[sq.rows(d), sq.rows(ya.shape[-1]), sq.rows(yb.shape[-1]), sq.rows(yc.shape[-1]),
                  sq.rows(3 * d), sq.mod(d)] + [sq.const(c.shape) for c in consts],
        out_specs=sq.rows(d),
        out_shape=jax.ShapeDtypeStruct(xs.shape, F32),
        compiler_params=_params(2),
        name="merge",
    )(xs, ya, yb, yc, gates, modtab, *consts)


def _ffn_body(x_ref, g_ref, mod_ref, w1_ref, w3_ref, w2_ref, o_ref, *, h_chunk):
    x = x_ref[...]
    y = x * lax.rsqrt(jnp.mean(x * x, axis=-1, keepdims=True) + RMS_EPS) * g_ref[...]
    h = (y * (1.0 + mod_ref[4:5, :]) + mod_ref[3:4, :]).astype(BF16)
    hidden = w1_ref.shape[1]
    acc = jnp.zeros(x.shape, F32)
    for c0 in range(0, hidden, h_chunk):
        a = jnp.dot(h, w1_ref[:, c0:c0 + h_chunk], preferred_element_type=F32)
        b = jnp.dot(h, w3_ref[:, c0:c0 + h_chunk], preferred_element_type=F32)
        t = (a * _sigmoid(a) * b).astype(BF16)
        acc = acc + jnp.dot(t, w2_ref[c0:c0 + h_chunk, :], preferred_element_type=F32)
    o_ref[...] = x + mod_ref[5:6, :] * acc


def _ffn(sq, xs, g, modtab, lw):
    d = xs.shape[-1]
    consts = [lw["ffn_w1"], lw["ffn_w3"], lw["ffn_w2"]]
    h_chunk = _largest_divisor(lw["ffn_w1"].shape[1], (512, 256, 128))
    return pl.pallas_call(
        functools.partial(_ffn_body, h_chunk=h_chunk),
        grid=sq.grid,
        in_specs=[sq.rows(d), sq.const((1, d)), sq.mod(d)] + [sq.const(c.shape) for c in consts],
        out_specs=sq.rows(d),
        out_shape=jax.ShapeDtypeStruct(xs.shape, F32),
        compiler_params=_params(2),
        name="ffn",
    )(xs, g.reshape(1, d), modtab, *consts)


def _block_diag_ones(n, seg):
    idx = jnp.arange(n) // seg
    return (idx[:, None] == idx[None, :]).astype(BF16)


def _block_diag2(m):
    z = jnp.zeros_like(m[0])
    return jnp.concatenate([jnp.concatenate([m[0], z], axis=1), jnp.concatenate([z, m[1]], axis=1)], axis=0)


def _pad_to(x, axis, size):
    pad = [(0, 0)] * x.ndim
    pad[axis] = (0, size - x.shape[axis])
    return jnp.pad(x, pad)


def _rope_table(ctx_len, seq_len, hdim, kv_heads):
    rows = seq_len // GRID_W
    row = jnp.repeat(jnp.arange(rows), GRID_W).astype(F32)
    col = jnp.tile(jnp.arange(GRID_W), rows).astype(F32)
    half = hdim // 2
    inv = ROPE_THETA ** (-jnp.arange(0, half, 2, dtype=F32) / half)
    ang = jnp.concatenate([row[:, None] * inv, col[:, None] * inv], axis=-1)
    cos = jnp.repeat(jnp.cos(ang), 2, axis=1)
    sin = jnp.repeat(jnp.sin(ang), 2, axis=1) * jnp.tile(jnp.array([-1.0, 1.0], F32), half)
    cos = jnp.concatenate([jnp.ones((ctx_len, hdim), F32), cos], axis=0)
    sin = jnp.concatenate([jnp.zeros((ctx_len, hdim), F32), sin], axis=0)
    return jnp.concatenate([jnp.tile(cos, (1, kv_heads)), jnp.tile(sin, (1, kv_heads))], axis=1)


def kernel(x, c, ctx, c_ctx, ada_w, ada_b, norm1, norm2, w_in, rwkv_mu_x, rwkv_mu_rkv, rwkv_w0, rwkv_w1, rwkv_w2, rwkv_a0, rwkv_a1, rwkv_a2, rwkv_g1, rwkv_g2, rwkv_k_k, rwkv_k_a, rwkv_r_k, rwkv_lnx_w, rwkv_lnx_b, gdn_conv, gdn_w_alpha, gdn_dt_bias, gdn_A_log, gdn_w_beta, gdn_norm, attn_q_norm, attn_k_norm, w_up_a, w_up_b, w_up_c, w_gate, b_gate, w_out, ffn_w1, ffn_w3, ffn_w2, final_norm):
    batch, seq_len, d = x.shape
    ctx_len = ctx.shape[1]
    depth = ada_w.shape[0]
    sq = _Seq(batch, ctx_len, seq_len)

    r_heads, r_hdim = rwkv_r_k.shape[1], rwkv_r_k.shape[2]
    rdim = r_heads * r_hdim
    g_heads, g_hdim = gdn_w_alpha.shape[-1], gdn_norm.shape[-1]
    gdim = g_heads * g_hdim
    a_hdim = attn_q_norm.shape[-1]
    qd = w_up_c.shape[1]
    q_heads = qd // a_hdim
    kvd = (w_in.shape[-1] - 3 * rdim - 4 * gdim - qd) // 2
    kv_heads = kvd // a_hdim
    assert rdim == gdim == qd and r_hdim == a_hdim, "lane-segment constants are shared between mixers"
    assert (3 * rdim) % (3 * gdim) == 0 and (3 * rdim + 3 * gdim) % gdim == 0
    gdn_col = (3 * rdim) // (3 * gdim)
    z_col = (3 * rdim + 3 * gdim) // gdim
    q_col = (3 * rdim + 4 * gdim) // qd
    assert (3 * rdim + 4 * gdim + qd) % (2 * kvd) == 0
    kv_col = (3 * rdim + 4 * gdim + qd) // (2 * kvd)

    bd_r = _block_diag_ones(rdim, r_hdim)
    bd_g = _block_diag_ones(gdim, g_hdim)
    cs_tab = _rope_table(ctx_len, seq_len, a_hdim, kv_heads)
    n_beta = 2 * g_heads
    assert 2 * n_beta <= LANES

    cond = jnp.concatenate([c, c_ctx[None, :]], axis=0)
    cond = _pad_to(cond, 0, -(-(batch + 1) // SUBLANES) * SUBLANES)

    xs = jnp.concatenate([ctx, x], axis=1)
    m_rows = batch * sq.S
    for l in range(depth):
        g1w = _pad_to(rwkv_g1[l], 1, 2 * LANES)
        lw = {
            "mu": rwkv_mu_x[l], "mu_rkv": rwkv_mu_rkv[l],
            "w1": jnp.concatenate([rwkv_w1[l, 0], rwkv_w1[l, 1]], axis=1).astype(BF16),
            "w2": _block_diag2(rwkv_w2[l]).astype(BF16),
            "w0": rwkv_w0[l].reshape(1, 2 * rdim),
            "a1": jnp.concatenate([rwkv_a1[l, 0], rwkv_a1[l, 1]], axis=1).astype(BF16),
            "a2": _block_diag2(rwkv_a2[l]).astype(BF16),
            "a0": rwkv_a0[l].reshape(1, 2 * rdim),
            "g1": g1w.astype(BF16),
            "g2": _pad_to(rwkv_g2[l], 0, 2 * LANES).astype(BF16),
            "wab": _pad_to(jnp.concatenate([gdn_w_beta[l, 0], gdn_w_beta[l, 1],
                                            gdn_w_alpha[l, 0], gdn_w_alpha[l, 1]], axis=1), 1, LANES).astype(BF16),
            "abb": _pad_to(jnp.concatenate([jnp.zeros((n_beta,), F32), gdn_dt_bias[l].reshape(-1)]), 0, LANES).reshape(1, LANES),
            "alog": _pad_to(jnp.concatenate([jnp.zeros((n_beta,), F32), gdn_A_log[l].reshape(-1)]), 0, LANES).reshape(1, LANES),
            "n_beta": n_beta,
            "k_k": rwkv_k_k[l].reshape(1, rdim), "k_a": rwkv_k_a[l].reshape(1, rdim),
            "r_k": rwkv_r_k[l].reshape(1, rdim),
            "lnx_w": rwkv_lnx_w[l].reshape(1, rdim), "lnx_b": rwkv_lnx_b[l].reshape(1, rdim),
            "bd_r": bd_r, "bd_g": bd_g,
            "conv": jnp.transpose(gdn_conv[l]),
            "gdn_norm": jnp.tile(gdn_norm[l], g_heads).reshape(1, gdim),
            "q_norm": jnp.tile(attn_q_norm[l], q_heads).reshape(1, qd),
            "k_norm": jnp.tile(attn_k_norm[l], kv_heads).reshape(1, kvd),
            "w_up_a": w_up_a[l].astype(BF16), "w_up_b": w_up_b[l].astype(BF16),
            "w_up_c": w_up_c[l].astype(BF16), "w_out": w_out[l].astype(BF16),
            "ffn_w1": ffn_w1[l].astype(BF16), "ffn_w3": ffn_w3[l].astype(BF16), "ffn_w2": ffn_w2[l].astype(BF16),
        }
        mod = _matmul(cond, ada_w[l].astype(BF16), bias=ada_b[l], pre_act="silu")
        mod_x = mod[:batch].reshape(batch, 6, d)
        mod_c = jnp.broadcast_to(mod[batch].reshape(1, 6, d), (batch, 6, d))
        modtab = jnp.stack([mod_c, mod_x], axis=1)

        h = _normmod(sq, xs, norm1[l], modtab, shift_row=0, scale_row=1)
        h2 = h.reshape(m_rows, d)
        proj = _matmul(h2, w_in[l].astype(BF16)).reshape(batch, sq.S, -1)
        gates = _matmul(h2, w_gate[l].astype(BF16), bias=b_gate[l], act="sigmoid").reshape(batch, sq.S, -1)
        lw_dec, a_iclr, g_out, ab = _lora(sq, h, lw)

        r, v, aa, kd, bb = _rwkv_prep(sq, proj, a_iclr, lw)
        rf, rg = _rwkv_intra(sq, r, v, aa, kd, lw_dec, bb, r_heads)
        o_rf, o_rb = _rwkv_scan(sq, rf, rg, r_heads)
        ya = _rwkv_post(sq, o_rf, o_rb, r, v, kd, g_out, lw, r_heads)

        gq, gk, gv = _gdn_prep(sq, proj, lw, gdim, gdn_col)
        gf, gga, ggb = _gdn_intra(sq, gq, gk, gv, ab, g_heads)
        o_gf, o_gb = _gdn_scan(sq, gf, gga, ggb, g_heads)
        yb = _gdn_post(sq, o_gf, o_gb, proj, lw, g_heads, z_col)

        aq, ak, av = _attn_prep(sq, proj, cs_tab, lw, q_heads, kv_heads, a_hdim, q_col, kv_col)
        yc = _attention(sq, aq, ak, av)

        xs = _merge(sq, xs, ya, yb, yc, gates, modtab, lw)
        xs = _ffn(sq, xs, norm2[l], modtab, lw)
    return _final_norm(sq, xs, final_norm)
```

```python
import functools

import jax
import jax.numpy as jnp
from jax import lax
from jax.experimental import pallas as pl
from jax.experimental.pallas import tpu as pltpu

F32 = jnp.float32
BF16 = jnp.bfloat16

RMS_EPS = 1e-6
RWKV_LNX_EPS = 64e-5
ROPE_THETA = 10000.0
GRID_W = 64

SUBLANES = 8
LANES = 128
CHUNK = 64
MAX_TOKEN_BLOCK = 256
MAX_MM_ROWS = 512
KEY_BLOCK = 256
RWKV_PASSES = 1
GDN_PASSES = 1
INV_PASSES = 1
RWKV_INTRA_CHUNKS = 2
GDN_INTRA_CHUNKS = 4
SCAN_CHUNKS = 4
VMEM_LIMIT = 56 * 1024 * 1024


def _sigmoid(x):
    return 1.0 / (1.0 + jnp.exp(-x))


def _softplus(x):
    return jnp.maximum(x, 0.0) + jnp.log(1.0 + jnp.exp(-jnp.abs(x)))


def _split2(x):
    hi = x.astype(BF16)
    lo = (x - hi.astype(F32)).astype(BF16)
    return hi, lo


def _mm(a, b, dims, passes):
    d = functools.partial(lax.dot_general, dimension_numbers=(dims, ((), ())), preferred_element_type=F32)
    if passes == 1:
        return d(a.astype(BF16), b.astype(BF16))
    ah, al = _split2(a)
    bh, bl = _split2(b)
    return d(ah, bh) + (d(ah, bl) + d(al, bh))


def _dot(a, b, passes=1):
    return _mm(a, b, ((1,), (0,)), passes)


def _dot_nt(a, b, passes=1):
    return _mm(a, b, ((1,), (1,)), passes)


def _dot_tn(a, b, passes=1):
    return _mm(a, b, ((0,), (0,)), passes)


def _split3(x):
    hi = x.astype(BF16)
    r1 = x - hi.astype(F32)
    mid = r1.astype(BF16)
    lo = (r1 - mid.astype(F32)).astype(BF16)
    return hi, mid, lo


def _dot_exact_lhs(m01, x):
    d = functools.partial(jnp.dot, preferred_element_type=F32)
    hi, mid, lo = _split3(x)
    return d(m01, hi) + (d(m01, mid) + d(m01, lo))


def _segsum(x, bd_ref):
    d = functools.partial(jnp.dot, preferred_element_type=F32)
    hi, lo = _split2(x)
    bd = bd_ref[...]
    return d(hi, bd) + d(lo, bd)


def _tri_inverse(l, eye):
    n = l.shape[0]
    row = lax.broadcasted_iota(jnp.int32, (n, n), 0)
    col = lax.broadcasted_iota(jnp.int32, (n, n), 1)
    t = eye - jnp.where((row >> 1) == (col >> 1), l, 0.0)
    for k in range(2, n.bit_length()):
        off = ((row >> k) == (col >> k)) & ((row >> (k - 1)) != (col >> (k - 1)))
        t = t - _dot(_dot(t, jnp.where(off, l, 0.0), INV_PASSES), t, INV_PASSES)
    return t


def _row_iota(shape):
    return lax.broadcasted_iota(jnp.int32, shape, 0)


def _shift_rows(x, prev_ref, next_ref, k, first, last):
    n = x.shape[0]
    row = _row_iota(x.shape)
    y = pltpu.roll(x, (-k) % n, axis=0)
    if k < 0:
        for i in range(-k):
            edge = prev_ref[SUBLANES + k + i:SUBLANES + k + i + 1, :]
            edge = jnp.where(first, 0.0, edge)
            y = jnp.where(row == i, edge, y)
    else:
        for i in range(k):
            edge = next_ref[i:i + 1, :]
            edge = jnp.where(last, 0.0, edge)
            y = jnp.where(row == n - k + i, edge, y)
    return y


def _largest_divisor(n, candidates):
    for c in candidates:
        if n % c == 0:
            return c
    raise ValueError(f"no block size among {candidates} divides {n}")


class _Seq:
    def __init__(self, batch, ctx_len, seq_len):
        self.B = batch
        self.ctx = ctx_len
        self.S = ctx_len + seq_len
        self.TB = _largest_divisor(ctx_len, (MAX_TOKEN_BLOCK, 128, 64))
        assert seq_len % self.TB == 0 and self.TB % CHUNK == 0
        self.ncb = ctx_len // self.TB
        self.nblk = self.S // self.TB
        self.ncc = ctx_len // CHUNK
        self.nchunk = self.S // CHUNK
        self.grid = (batch, self.nblk)

    def rows(self, width, col_block=0):
        return pl.BlockSpec((None, self.TB, width), lambda b, j: (b, j, col_block))

    def rows2(self, width):
        return pl.BlockSpec((2, None, self.TB, width), lambda b, j: (0, b, j, 0))

    def heads(self, n_heads, width):
        return pl.BlockSpec((None, n_heads, self.TB, width), lambda b, j: (b, 0, j, 0))

    def prev_rows(self, width, col_block=0):
        per = self.TB // SUBLANES
        return pl.BlockSpec((None, SUBLANES, width),
                            lambda b, j: (b, jnp.maximum(j * per - 1, 0), col_block))

    def next_rows(self, width, col_block=0):
        per = self.TB // SUBLANES
        top = self.S // SUBLANES - 1
        return pl.BlockSpec((None, SUBLANES, width),
                            lambda b, j: (b, jnp.minimum((j + 1) * per, top), col_block))

    def const(self, shape):
        zeros = (0,) * len(shape)
        return pl.BlockSpec(shape, lambda b, j: zeros)

    def mod(self, d_model):
        ncb = self.ncb
        return pl.BlockSpec((None, None, 6, d_model),
                            lambda b, j: (b, (j >= ncb).astype(jnp.int32), 0, 0))

    def edges(self):
        j = pl.program_id(1)
        first = (j == 0) | (j == self.ncb)
        last = (j == self.ncb - 1) | (j == self.nblk - 1)
        return first, last


def _params(n_axes):
    return pltpu.CompilerParams(dimension_semantics=("arbitrary",) * n_axes,
                                vmem_limit_bytes=VMEM_LIMIT)


def _mm_body(*refs, act, pre_act, n_chunk, has_bias):
    if has_bias:
        x_ref, w_ref, b_ref, o_ref = refs
    else:
        x_ref, w_ref, o_ref = refs
        b_ref = None
    x = x_ref[...]
    if pre_act == "silu":
        x = x * _sigmoid(x)
    xb = x.astype(BF16)
    n = o_ref.shape[-1]
    for n0 in range(0, n, n_chunk):
        y = jnp.dot(xb, w_ref[:, n0:n0 + n_chunk], preferred_element_type=F32)
        if b_ref is not None:
            y = y + b_ref[:, n0:n0 + n_chunk]
        if act == "sigmoid":
            y = _sigmoid(y)
        o_ref[:, n0:n0 + n_chunk] = y.astype(o_ref.dtype)


def _matmul(x, w, bias=None, act=None, pre_act=None, out_dtype=F32):
    m, k = x.shape
    n = w.shape[1]
    tm = m if m <= MAX_MM_ROWS else _largest_divisor(m, (MAX_MM_ROWS, 256, 128, 64, 32, 16, 8))
    n_chunk = _largest_divisor(n, (512, 256, 128))
    in_specs = [pl.BlockSpec((tm, k), lambda i: (i, 0)),
                pl.BlockSpec((k, n), lambda i: (0, 0))]
    args = [x, w]
    if bias is not None:
        in_specs.append(pl.BlockSpec((1, n), lambda i: (0, 0)))
        args.append(bias.reshape(1, n))
    return pl.pallas_call(
        functools.partial(_mm_body, act=act, pre_act=pre_act, n_chunk=n_chunk,
                          has_bias=bias is not None),
        grid=(m // tm,),
        in_specs=in_specs,
        out_specs=pl.BlockSpec((tm, n), lambda i: (i, 0)),
        out_shape=jax.ShapeDtypeStruct((m, n), out_dtype),
        compiler_params=_params(1),
        name="matmul",
    )(*args)


def _normmod_body(x_ref, g_ref, mod_ref, o_ref, *, shift_row, scale_row):
    x = x_ref[...]
    y = x * lax.rsqrt(jnp.mean(x * x, axis=-1, keepdims=True) + RMS_EPS)
    y = y * g_ref[...]
    o_ref[...] = y * (1.0 + mod_ref[scale_row:scale_row + 1, :]) + mod_ref[shift_row:shift_row + 1, :]


def _normmod(sq, xs, g, modtab, shift_row, scale_row):
    d = xs.shape[-1]
    return pl.pallas_call(
        functools.partial(_normmod_body, shift_row=shift_row, scale_row=scale_row),
        grid=sq.grid,
        in_specs=[sq.rows(d), sq.const((1, d)), sq.mod(d)],
        out_specs=sq.rows(d),
        out_shape=jax.ShapeDtypeStruct(xs.shape, F32),
        compiler_params=_params(2),
        name="normmod",
    )(xs, g.reshape(1, d), modtab)


def _final_norm_body(x_ref, g_ref, o_ref):
    x = x_ref[...]
    y = x * lax.rsqrt(jnp.mean(x * x, axis=-1, keepdims=True) + RMS_EPS)
    o_ref[...] = y * g_ref[...]


def _final_norm(sq, xs, g):
    d = xs.shape[-1]
    ncb = sq.ncb
    return pl.pallas_call(
        _final_norm_body,
        grid=(sq.B, sq.nblk - ncb),
        in_specs=[pl.BlockSpec((None, sq.TB, d), lambda b, j: (b, j + ncb, 0)),
                  pl.BlockSpec((1, d), lambda b, j: (0, 0))],
        out_specs=pl.BlockSpec((None, sq.TB, d), lambda b, j: (b, j, 0)),
        out_shape=jax.ShapeDtypeStruct((sq.B, sq.S - sq.ctx, d), F32),
        compiler_params=_params(2),
        name="final_norm",
    )(xs, g.reshape(1, d))


def _lora_body(h_ref, hp_ref, hn_ref, mu_ref, w1_ref, w2_ref, w0_ref, a1_ref, a2_ref, a0_ref,
               g1_ref, g2_ref, wab_ref, abb_ref, alog_ref,
               lw_ref, a_ref, g_ref, ab_ref, *, sq, rdim, n_beta):
    first, last = sq.edges()
    h = h_ref[...]
    nb = 0.5 * (_shift_rows(h, hp_ref, hn_ref, -1, first, last)
                + _shift_rows(h, hp_ref, hn_ref, 1, first, last))
    dlt = nb - h
    xw = h + dlt * mu_ref[0:1, :]
    xa = h + dlt * mu_ref[1:2, :]
    xg = h + dlt * mu_ref[2:3, :]
    wl = w0_ref[...] + _dot(jnp.tanh(_dot(xw, w1_ref[...])), w2_ref[...])
    w_log = -_softplus(-wl) - 0.5
    lw = -jnp.exp(w_log)
    lw_ref[0] = lw[:, :rdim]
    lw_ref[1] = lw[:, rdim:]
    a = _sigmoid(a0_ref[...] + _dot(_dot(xa, a1_ref[...]), a2_ref[...]))
    a_ref[0] = a[:, :rdim]
    a_ref[1] = a[:, rdim:]
    g_ref[...] = _dot(_sigmoid(_dot(xg, g1_ref[...])), g2_ref[...])
    z = _dot(h, wab_ref[...])
    col = lax.broadcasted_iota(jnp.int32, z.shape, 1)
    gl = -jnp.exp(alog_ref[...]) * _softplus(z + abb_ref[...])
    ab_ref[...] = jnp.where(col < n_beta, _sigmoid(z), gl)


def _lora(sq, h, lw):
    d = h.shape[-1]
    rdim = lw["w0"].shape[-1] // 2
    consts = [lw["mu"], lw["w1"], lw["w2"], lw["w0"], lw["a1"], lw["a2"], lw["a0"],
              lw["g1"], lw["g2"], lw["wab"], lw["abb"], lw["alog"]]
    bsd = (sq.B, sq.S)
    return pl.pallas_call(
        functools.partial(_lora_body, sq=sq, rdim=rdim, n_beta=lw["n_beta"]),
        grid=sq.grid,
        in_specs=[sq.rows(d), sq.prev_rows(d), sq.next_rows(d)] + [sq.const(c.shape) for c in consts],
        out_specs=[sq.rows2(rdim), sq.rows2(rdim), sq.rows(rdim), sq.rows(LANES)],
        out_shape=[jax.ShapeDtypeStruct((2,) + bsd + (rdim,), F32),
                   jax.ShapeDtypeStruct((2,) + bsd + (rdim,), F32),
                   jax.ShapeDtypeStruct(bsd + (rdim,), F32),
                   jax.ShapeDtypeStruct(bsd + (LANES,), F32)],
        compiler_params=_params(2),
        name="lora",
    )(h, h, h, *consts)


def _rwkv_prep_body(p_ref, pp_ref, pn_ref, a_ref, mu_ref, kk_ref, ka_ref, bd_ref,
                    r_ref, v_ref, aa_ref, kd_ref, bb_ref, *, sq, rdim):
    first, last = sq.edges()
    p = p_ref[...]
    nb = 0.5 * (_shift_rows(p, pp_ref, pn_ref, -1, first, last)
                + _shift_rows(p, pp_ref, pn_ref, 1, first, last))
    dlt = nb - p
    r = p[:, :rdim] + dlt[:, :rdim] * mu_ref[0:1, :]
    k = p[:, rdim:2 * rdim] + dlt[:, rdim:2 * rdim] * mu_ref[1:2, :]
    v = p[:, 2 * rdim:] + dlt[:, 2 * rdim:] * mu_ref[2:3, :]
    t = k * kk_ref[...]
    kk = t * lax.rsqrt(_segsum(t * t, bd_ref) + RMS_EPS)
    r_ref[...] = r
    v_ref[...] = v
    aa_ref[...] = -kk
    for d in range(2):
        a = a_ref[d]
        kd_ref[d] = k * (1.0 + (a - 1.0) * ka_ref[...])
        bb_ref[d] = kk * a


def _rwkv_prep(sq, proj, a, lw):
    rdim = a.shape[-1]
    consts = [lw["mu_rkv"], lw["k_k"], lw["k_a"], lw["bd_r"]]
    one = jax.ShapeDtypeStruct((sq.B, sq.S, rdim), F32)
    two = jax.ShapeDtypeStruct((2, sq.B, sq.S, rdim), F32)
    return pl.pallas_call(
        functools.partial(_rwkv_prep_body, sq=sq, rdim=rdim),
        grid=sq.grid,
        in_specs=[sq.rows(3 * rdim), sq.prev_rows(3 * rdim), sq.next_rows(3 * rdim), sq.rows2(rdim)]
        + [sq.const(c.shape) for c in consts],
        out_specs=[sq.rows(rdim), sq.rows(rdim), sq.rows(rdim), sq.rows2(rdim), sq.rows2(rdim)],
        out_shape=[one, one, one, two, two],
        compiler_params=_params(2),
        name="rwkv_prep",
    )(proj, proj, proj, a, *consts)


def _chunk_index(d, i, ncc, nchunk):
    back = jnp.where(i < ncc, ncc - 1 - i, nchunk + ncc - 1 - i)
    return jnp.where(d == 0, i, back)


def _causal_masks(d, c, reps=1):
    row = lax.broadcasted_iota(jnp.int32, (c, reps * c), 0)
    col = lax.broadcasted_iota(jnp.int32, (c, reps * c), 1) & (c - 1)
    delta = jnp.where(d == 0, row - col, col - row)
    eye = jnp.where(row == col, 1.0, 0.0).astype(F32)
    return delta >= 0, delta > 0, eye


def _stacked_mask(d, c):
    row = lax.broadcasted_iota(jnp.int32, (2 * c, c), 0)
    col = lax.broadcasted_iota(jnp.int32, (2 * c, c), 1)
    rr = row & (c - 1)
    delta = jnp.where(d == 0, rr - col, col - rr)
    return (delta > 0) | ((row >= c) & (delta == 0))


def _tri_inverse_many(ls, eye):
    n = ls[0].shape[0]
    row = lax.broadcasted_iota(jnp.int32, (n, n), 0)
    col = lax.broadcasted_iota(jnp.int32, (n, n), 1)
    same = (row >> 1) == (col >> 1)
    ts = [eye - jnp.where(same, l, 0.0) for l in ls]
    for k in range(2, n.bit_length()):
        off = ((row >> k) == (col >> k)) & ((row >> (k - 1)) != (col >> (k - 1)))
        tl = [_dot(t, jnp.where(off, l, 0.0), INV_PASSES) for t, l in zip(ts, ls)]
        ts = [t - _dot(x, t, INV_PASSES) for x, t in zip(tl, ts)]
    return ts


def _rwkv_intra_body(r_ref, v_ref, aa_ref, kd_ref, lw_ref, bb_ref, f_ref, g_ref, *, heads, hdim, cps):
    c = CHUNK
    sl = [slice(h * hdim, (h + 1) * hdim) for h in range(heads)]
    keys, a_h, r_h, v_h, bt_h, kt_h, ec_h, mask2, eye = [], {}, {}, {}, {}, {}, {}, {}, None
    for d in range(2):
        incl, _, eye = _causal_masks(d, c)
        m01 = jnp.where(incl, 1.0, 0.0).astype(BF16)
        mask2[d] = _stacked_mask(d, c)
        for j in range(cps):
            rows = slice(j * c, (j + 1) * c)
            lw = lw_ref[d, rows, :]
            cum = _dot_exact_lhs(m01, lw)
            tot = jnp.sum(lw, axis=0, keepdims=True)
            e_neg = jnp.exp(-cum)
            a_t = aa_ref[rows, :] * jnp.exp(cum - lw)
            r_t = r_ref[rows, :] * jnp.exp(cum)
            v = v_ref[rows, :]
            b_tt = jnp.transpose(bb_ref[d, rows, :] * e_neg)
            k_tt = jnp.transpose(kd_ref[d, rows, :] * e_neg)
            e_col = jnp.transpose(jnp.broadcast_to(jnp.exp(tot), lw.shape))
            for h in range(heads):
                key = (d, j, h)
                keys.append(key)
                a_h[key], r_h[key], v_h[key] = a_t[:, sl[h]], r_t[:, sl[h]], v[:, sl[h]]
                bt_h[key], kt_h[key], ec_h[key] = b_tt[sl[h]], k_tt[sl[h]], e_col[sl[h]]
    ar = {k: jnp.concatenate([a_h[k], r_h[k]], axis=0) for k in keys}
    mb = {k: jnp.where(mask2[k[0]], _dot(ar[k], bt_h[k], RWKV_PASSES), 0.0) for k in keys}
    mk = {k: jnp.where(mask2[k[0]], _dot(ar[k], kt_h[k], RWKV_PASSES), 0.0) for k in keys}
    x = {k: _dot(mk[k], v_h[k], RWKV_PASSES) for k in keys}
    h0 = {k: _dot(ec_h[k] * kt_h[k], v_h[k], RWKV_PASSES) for k in keys}
    t = dict(zip(keys, _tri_inverse_many([-mb[k][:c] for k in keys], eye)))
    wt = {k: _dot(t[k], a_h[k], RWKV_PASSES) for k in keys}
    ut = {k: _dot(t[k], x[k][:c], RWKV_PASSES) for k in keys}
    for k in keys:
        d, j, h = k
        o = j * 4 * c
        f_ref[d, h, o:o + c] = ut[k]
        f_ref[d, h, o + c:o + 2 * c] = x[k][c:]
        f_ref[d, h, o + 2 * c:o + 3 * c] = h0[k]
        f_ref[d, h, o + 3 * c:o + 4 * c] = ec_h[k]
        g_ref[d, h, o:o + c] = wt[k].astype(BF16)
        g_ref[d, h, o + c:o + 2 * c] = r_h[k].astype(BF16)
        g_ref[d, h, o + 2 * c:o + 3 * c] = mb[k][c:].astype(BF16)
        g_ref[d, h, o + 3 * c:o + 4 * c] = (ec_h[k] * bt_h[k]).astype(BF16)


def _rwkv_intra(sq, r, v, aa, kd, lw, bb, heads):
    rdim = r.shape[-1]
    hdim = rdim // heads
    assert hdim == CHUNK, "the hand-over tiles share one [CHUNK, head_dim] shape"
    cps = _largest_divisor(sq.nchunk, (RWKV_INTRA_CHUNKS, 1))
    rows = cps * CHUNK
    shared = pl.BlockSpec((None, rows, rdim), lambda b, i: (b, i, 0))
    per_dir = pl.BlockSpec((2, None, rows, rdim), lambda b, i: (0, b, i, 0))
    out = pl.BlockSpec((2, None, heads, 4 * rows, hdim), lambda b, i: (0, b, 0, i, 0))
    shape = (2, sq.B, heads, sq.nchunk * 4 * CHUNK, hdim)
    return pl.pallas_call(
        functools.partial(_rwkv_intra_body, heads=heads, hdim=hdim, cps=cps),
        grid=(sq.B, sq.nchunk // cps),
        in_specs=[shared, shared, shared, per_dir, per_dir, per_dir],
        out_specs=[out, out],
        out_shape=[jax.ShapeDtypeStruct(shape, F32), jax.ShapeDtypeStruct(shape, BF16)],
        compiler_params=_params(2),
        name="rwkv_intra",
    )(r, v, aa, kd, lw, bb)


def _rwkv_scan_body(f0_ref, g0_ref, f1_ref, g1_ref, of_ref, ob_ref, h_ref, *, heads, hdim, cps):
    @pl.when(pl.program_id(1) == 0)
    def _():
        h_ref[...] = jnp.zeros_like(h_ref)

    c = CHUNK
    fs, gs, outs = (f0_ref, f1_ref), (g0_ref, g1_ref), (of_ref, ob_ref)
    chains = [(d, h) for d in range(2) for h in range(heads)]
    dot = functools.partial(jnp.dot, preferred_element_type=F32)
    st = {ch: h_ref[ch[0], ch[1]] for ch in chains}
    for step in range(cps):
        sub = (step, cps - 1 - step)
        o = [4 * c * sub[d] for d in range(2)]
        m2 = {(d, h): dot(gs[d][h, o[d]:o[d] + 2 * c, :], st[d, h].astype(BF16)) for d, h in chains}
        ub = {(d, h): (fs[d][h, o[d]:o[d] + c, :] + m2[d, h][:c]).astype(BF16) for d, h in chains}
        for d, h in chains:
            y = fs[d][h, o[d] + c:o[d] + 2 * c, :] + m2[d, h][c:] + dot(gs[d][h, o[d] + 2 * c:o[d] + 3 * c, :], ub[d, h])
            outs[d][sub[d] * c:(sub[d] + 1) * c, h * hdim:(h + 1) * hdim] = y
            st[d, h] = (fs[d][h, o[d] + 3 * c:o[d] + 4 * c, :] * st[d, h] + fs[d][h, o[d] + 2 * c:o[d] + 3 * c, :]
                        + dot(gs[d][h, o[d] + 3 * c:o[d] + 4 * c, :], ub[d, h]))
    for d, h in chains:
        h_ref[d, h] = st[d, h]


def _backward_block(i, ncb, nblk):
    return jnp.where(i < ncb, ncb - 1 - i, nblk + ncb - 1 - i)


def _scan_chunks_per_step(sq):
    cps = _largest_divisor(sq.ncc, (SCAN_CHUNKS, 2, 1))
    assert sq.nchunk % cps == 0
    return cps


def _scan_specs(sq, heads, rows, width):
    cps = _scan_chunks_per_step(sq)
    ncb, nblk = sq.ncc // cps, sq.nchunk // cps
    fwd = pl.BlockSpec((None, None, heads, cps * rows, width), lambda b, i: (0, b, 0, i, 0))
    bwd = pl.BlockSpec((None, None, heads, cps * rows, width),
                       lambda b, i: (1, b, 0, _backward_block(i, ncb, nblk), 0))
    return fwd, bwd


def _scan_out_specs(sq, width):
    cps = _scan_chunks_per_step(sq)
    ncb, nblk = sq.ncc // cps, sq.nchunk // cps
    fwd = pl.BlockSpec((None, cps * CHUNK, width), lambda b, i: (b, i, 0))
    bwd = pl.BlockSpec((None, cps * CHUNK, width), lambda b, i: (b, _backward_block(i, ncb, nblk), 0))
    return [fwd, bwd]


def _rwkv_scan(sq, f, g, heads):
    hdim = f.shape[-1]
    rdim = heads * hdim
    cps = _scan_chunks_per_step(sq)
    ff, fb = _scan_specs(sq, heads, 4 * CHUNK, hdim)
    one = jax.ShapeDtypeStruct((sq.B, sq.S, rdim), F32)
    return pl.pallas_call(
        functools.partial(_rwkv_scan_body, heads=heads, hdim=hdim, cps=cps),
        grid=(sq.B, sq.nchunk // cps),
        in_specs=[ff, ff, fb, fb],
        out_specs=_scan_out_specs(sq, rdim),
        out_shape=[one, one],
        scratch_shapes=[pltpu.VMEM((2, heads, hdim, hdim), F32)],
        compiler_params=_params(2),
        name="rwkv_scan",
    )(f, g, f, g)


def _rwkv_post_body(of_ref, ob_ref, r_ref, v_ref, kd_ref, g_ref, rk_ref, lnw_ref, lnb_ref, bd_ref, y_ref, *, hdim):
    o = of_ref[...] + ob_ref[...]
    inv_n = 1.0 / hdim
    mean = _segsum(o, bd_ref) * inv_n
    cen = o - mean
    var = _segsum(cen * cen, bd_ref) * inv_n
    o = cen * lax.rsqrt(var + RWKV_LNX_EPS) * lnw_ref[...] + lnb_ref[...]
    r = r_ref[...]
    v = v_ref[...]
    rk = rk_ref[...]
    for d in range(2):
        o = o + _segsum(r * kd_ref[d] * rk, bd_ref) * v
    y_ref[...] = o * g_ref[...]


def _rwkv_post(sq, o_f, o_b, r, v, kd, g, lw, heads):
    rdim = r.shape[-1]
    consts = [lw["r_k"], lw["lnx_w"], lw["lnx_b"], lw["bd_r"]]
    return pl.pallas_call(
        functools.partial(_rwkv_post_body, hdim=rdim // heads),
        grid=sq.grid,
        in_specs=[sq.rows(rdim), sq.rows(rdim), sq.rows(rdim), sq.rows(rdim), sq.rows2(rdim), sq.rows(rdim)]
        + [sq.const(c.shape) for c in consts],
        out_specs=sq.rows(rdim),
        out_shape=jax.ShapeDtypeStruct((sq.B, sq.S, rdim), F32),
        compiler_params=_params(2),
        name="rwkv_post",
    )(o_f, o_b, r, v, kd, g, *consts)


def _gdn_prep_body(p_ref, pp_ref, pn_ref, cw_ref, bd_ref, q_ref, k_ref, v_ref, *, sq, gdim):
    first, last = sq.edges()
    p = p_ref[...]
    width = cw_ref.shape[0]
    half = width // 2
    acc = p * cw_ref[half:half + 1, :]
    for j in range(width):
        if j != half:
            acc = acc + _shift_rows(p, pp_ref, pn_ref, j - half, first, last) * cw_ref[j:j + 1, :]
    y = acc * _sigmoid(acc)
    q = y[:, :gdim]
    k = y[:, gdim:2 * gdim]
    q_ref[...] = q * lax.rsqrt(_segsum(q * q, bd_ref) + RMS_EPS)
    k_ref[...] = k * lax.rsqrt(_segsum(k * k, bd_ref) + RMS_EPS)
    v_ref[...] = y[:, 2 * gdim:]


def _gdn_prep(sq, proj, lw, gdim, col_block):
    consts = [lw["conv"], lw["bd_g"]]
    one = jax.ShapeDtypeStruct((sq.B, sq.S, gdim), F32)
    w = 3 * gdim
    return pl.pallas_call(
        functools.partial(_gdn_prep_body, sq=sq, gdim=gdim),
        grid=sq.grid,
        in_specs=[sq.rows(w, col_block), sq.prev_rows(w, col_block), sq.next_rows(w, col_block)]
        + [sq.const(c.shape) for c in consts],
        out_specs=[sq.rows(gdim)] * 3,
        out_shape=[one, one, one],
        compiler_params=_params(2),
        name="gdn_prep",
    )(proj, proj, proj, *consts)


def _gdn_intra_body(q_ref, k_ref, v_ref, ab_ref, f_ref, ga_ref, gb_ref, *, heads, hdim, cps):
    c = CHUNK
    scale = hdim ** -0.5
    sl = [slice(h * hdim, (h + 1) * hdim) for h in range(heads)]
    keys = []
    beta, gc, g_last, gc_row, decay, kb, kq, kt_h, vb, strict_of, eye = {}, {}, {}, {}, {}, {}, {}, {}, {}, {}, None
    for j in range(cps):
        rows = slice(j * c, (j + 1) * c)
        ab = ab_ref[rows, :]
        tot_all = jnp.sum(ab, axis=0, keepdims=True)
        lane = lax.broadcasted_iota(jnp.int32, ab.shape, 1)
        k_all = k_ref[rows, :]
        k_tt = jnp.transpose(k_all)
        for d in range(2):
            incl, strict, eye = _causal_masks(d, c)
            strict_of[d] = strict
            gc_all = _dot_exact_lhs(jnp.where(incl, 1.0, 0.0).astype(BF16), ab)
            gc_all_t = jnp.transpose(gc_all)
            sub = lax.broadcasted_iota(jnp.int32, gc_all_t.shape, 0)
            for h in range(heads):
                key = (d, j, h)
                keys.append(key)
                pick_b = lane == d * heads + h
                pick_g = lane == 2 * heads + d * heads + h
                beta[key] = jnp.sum(jnp.where(pick_b, ab, 0.0), axis=1, keepdims=True)
                gc[key] = jnp.sum(jnp.where(pick_g, gc_all, 0.0), axis=1, keepdims=True)
                g_last[key] = jnp.sum(jnp.where(pick_g[:1], tot_all, 0.0), axis=1, keepdims=True)
                gc_row[key] = jnp.sum(jnp.where(sub == 2 * heads + d * heads + h, gc_all_t, 0.0),
                                      axis=0, keepdims=True)
                diff = gc[key] - gc_row[key]
                decay[key] = jnp.where(incl, jnp.exp(jnp.where(incl, diff, 0.0)), 0.0)
                kb[key] = k_all[:, sl[h]] * beta[key]
                kq[key] = jnp.concatenate([kb[key], q_ref[rows, sl[h]] * scale], axis=0)
                kt_h[key] = k_tt[sl[h]]
                vb[key] = v_ref[rows, sl[h]] * beta[key]
    m = {k: _dot(kq[k], kt_h[k], GDN_PASSES) for k in keys}
    lower = [jnp.where(strict_of[k[0]], m[k][:c] * decay[k], 0.0) for k in keys]
    t = dict(zip(keys, _tri_inverse_many(lower, eye)))
    e_gc = {k: jnp.exp(gc[k]) for k in keys}
    sol = {k: _dot(t[k], jnp.concatenate([vb[k], kb[k] * e_gc[k]], axis=1), GDN_PASSES) for k in keys}
    for k in keys:
        d, j, h = k
        of, oa, ob = j * (c + SUBLANES), j * 2 * c, j * (c + hdim)
        f_ref[d, h, of:of + c] = sol[k][:, :hdim]
        f_ref[d, h, of + c:of + c + SUBLANES] = jnp.broadcast_to(jnp.exp(g_last[k]), (SUBLANES, hdim))
        ga_ref[d, h, oa:oa + c] = sol[k][:, hdim:].astype(BF16)
        ga_ref[d, h, oa + c:oa + 2 * c] = (kq[k][c:] * e_gc[k]).astype(BF16)
        gb_ref[d, h, ob:ob + c] = (m[k][c:] * decay[k]).astype(BF16)
        gb_ref[d, h, ob + c:ob + c + hdim] = (kt_h[k] * jnp.exp(g_last[k] - gc_row[k])).astype(BF16)


def _gdn_intra(sq, q, k, v, ab, heads):
    gdim = q.shape[-1]
    hdim = gdim // heads
    cps = _largest_divisor(sq.nchunk, (GDN_INTRA_CHUNKS, 2, 1))
    shared = pl.BlockSpec((None, cps * CHUNK, gdim), lambda b, i: (b, i, 0))
    small = pl.BlockSpec((None, cps * CHUNK, LANES), lambda b, i: (b, i, 0))
    rows = (CHUNK + SUBLANES, 2 * CHUNK, CHUNK + hdim)
    widths = (hdim, hdim, CHUNK)
    dtypes = (F32, BF16, BF16)
    outs = [pl.BlockSpec((2, None, heads, cps * r, w), lambda b, i: (0, b, 0, i, 0)) for r, w in zip(rows, widths)]
    shapes = [jax.ShapeDtypeStruct((2, sq.B, heads, sq.nchunk * r, w), dt) for r, w, dt in zip(rows, widths, dtypes)]
    return pl.pallas_call(
        functools.partial(_gdn_intra_body, heads=heads, hdim=hdim, cps=cps),
        grid=(sq.B, sq.nchunk // cps),
        in_specs=[shared, shared, shared, small],
        out_specs=outs,
        out_shape=shapes,
        compiler_params=_params(2),
        name="gdn_intra",
    )(q, k, v, ab)


def _gdn_scan_body(f0_ref, ga0_ref, gb0_ref, f1_ref, ga1_ref, gb1_ref, of_ref, ob_ref, s_ref, *, heads, hdim, cps):
    @pl.when(pl.program_id(1) == 0)
    def _():
        s_ref[...] = jnp.zeros_like(s_ref)

    c = CHUNK
    fs, gas, gbs, outs = (f0_ref, f1_ref), (ga0_ref, ga1_ref), (gb0_ref, gb1_ref), (of_ref, ob_ref)
    chains = [(d, h) for d in range(2) for h in range(heads)]
    dot = functools.partial(jnp.dot, preferred_element_type=F32)
    st = {ch: s_ref[ch[0], ch[1]] for ch in chains}
    for step in range(cps):
        sub = (step, cps - 1 - step)
        of = [(c + SUBLANES) * sub[d] for d in range(2)]
        oa = [2 * c * sub[d] for d in range(2)]
        ob = [(c + hdim) * sub[d] for d in range(2)]
        m = {(d, h): dot(gas[d][h, oa[d]:oa[d] + 2 * c, :], st[d, h].astype(BF16)) for d, h in chains}
        vn = {(d, h): (fs[d][h, of[d]:of[d] + c, :] - m[d, h][:c]).astype(BF16) for d, h in chains}
        for d, h in chains:
            outs[d][sub[d] * c:(sub[d] + 1) * c, h * hdim:(h + 1) * hdim] = (
                m[d, h][c:] + dot(gbs[d][h, ob[d]:ob[d] + c, :], vn[d, h]))
            st[d, h] = (st[d, h] * fs[d][h, of[d] + c:of[d] + c + 1, :]
                        + dot(gbs[d][h, ob[d] + c:ob[d] + c + hdim, :], vn[d, h]))
    for d, h in chains:
        s_ref[d, h] = st[d, h]


def _gdn_scan(sq, f, ga, gb, heads):
    hdim = f.shape[-1]
    gdim = heads * hdim
    cps = _scan_chunks_per_step(sq)
    specs = []
    for arr in (f, ga, gb):
        specs.append(_scan_specs(sq, heads, arr.shape[3] // sq.nchunk, arr.shape[4]))
    one = jax.ShapeDtypeStruct((sq.B, sq.S, gdim), F32)
    return pl.pallas_call(
        functools.partial(_gdn_scan_body, heads=heads, hdim=hdim, cps=cps),
        grid=(sq.B, sq.nchunk // cps),
        in_specs=[s[0] for s in specs] + [s[1] for s in specs],
        out_specs=_scan_out_specs(sq, gdim),
        out_shape=[one, one],
        scratch_shapes=[pltpu.VMEM((2, heads, hdim, hdim), F32)],
        compiler_params=_params(2),
        name="gdn_scan",
    )(f, ga, gb, f, ga, gb)


def _gdn_post_body(of_ref, ob_ref, z_ref, gn_ref, bd_ref, y_ref, *, hdim):
    o = of_ref[...] + ob_ref[...]
    ms = _segsum(o * o, bd_ref) * (1.0 / hdim)
    z = z_ref[...]
    y_ref[...] = o * lax.rsqrt(ms + RMS_EPS) * gn_ref[...] * (z * _sigmoid(z))


def _gdn_post(sq, o_f, o_b, proj, lw, heads, z_col_block):
    gdim = o_f.shape[-1]
    consts = [lw["gdn_norm"], lw["bd_g"]]
    return pl.pallas_call(
        functools.partial(_gdn_post_body, hdim=gdim // heads),
        grid=sq.grid,
        in_specs=[sq.rows(gdim), sq.rows(gdim), sq.rows(gdim, z_col_block)] + [sq.const(c.shape) for c in consts],
        out_specs=sq.rows(gdim),
        out_shape=jax.ShapeDtypeStruct((sq.B, sq.S, gdim), F32),
        compiler_params=_params(2),
        name="gdn_post",
    )(o_f, o_b, proj, *consts)


def _rope(x, cos, sin_signed):
    n = x.shape[-1]
    lane = lax.broadcasted_iota(jnp.int32, x.shape, 1)
    partner = jnp.where((lane & 1) == 0, pltpu.roll(x, n - 1, axis=1), pltpu.roll(x, 1, axis=1))
    return x * cos + partner * sin_signed


def _attn_prep_body(q_ref, kv_ref, cs_ref, qn_ref, kn_ref, bd_ref, qo_ref, ko_ref, vo_ref,
                    *, q_heads, kv_heads, hdim):
    kvd = kv_heads * hdim
    cos = cs_ref[:, :kvd]
    sin = cs_ref[:, kvd:]
    reps = q_heads // kv_heads
    cos_q = jnp.concatenate([cos] * reps, axis=1)
    sin_q = jnp.concatenate([sin] * reps, axis=1)
    inv_n = 1.0 / hdim
    q = q_ref[...]
    q = q * lax.rsqrt(_segsum(q * q, bd_ref) * inv_n + RMS_EPS) * qn_ref[...]
    q = _rope(q, cos_q, sin_q) * (hdim ** -0.5)
    kv = kv_ref[...]
    k = kv[:, :kvd]
    kbd = bd_ref[:kvd, :kvd]
    hi, lo = _split2(k * k)
    ms = (jnp.dot(hi, kbd, preferred_element_type=F32) + jnp.dot(lo, kbd, preferred_element_type=F32)) * inv_n
    k = k * lax.rsqrt(ms + RMS_EPS) * kn_ref[...]
    k = _rope(k, cos, sin)
    v = kv[:, kvd:]
    for h in range(q_heads):
        qo_ref[h] = q[:, h * hdim:(h + 1) * hdim].astype(BF16)
    k_t = jnp.transpose(k)
    for h in range(kv_heads):
        ko_ref[h] = k_t[h * hdim:(h + 1) * hdim].astype(BF16)
        vh = v[:, h * hdim:(h + 1) * hdim]
        vo_ref[h] = jnp.concatenate([vh, jnp.ones_like(vh)], axis=1).astype(BF16)


def _attn_prep(sq, proj, cs_tab, lw, q_heads, kv_heads, hdim, q_col_block, kv_col_block):
    qd, kvd = q_heads * hdim, kv_heads * hdim
    consts = [lw["q_norm"], lw["k_norm"], lw["bd_r"]]
    k_spec = pl.BlockSpec((None, kv_heads, hdim, sq.TB), lambda b, j: (b, 0, 0, j))
    return pl.pallas_call(
        functools.partial(_attn_prep_body, q_heads=q_heads, kv_heads=kv_heads, hdim=hdim),
        grid=sq.grid,
        in_specs=[sq.rows(qd, q_col_block), sq.rows(2 * kvd, kv_col_block),
                  pl.BlockSpec((sq.TB, 2 * kvd), lambda b, j: (j, 0))]
        + [sq.const(c.shape) for c in consts],
        out_specs=[sq.heads(q_heads, hdim), k_spec, sq.heads(kv_heads, 2 * hdim)],
        out_shape=[jax.ShapeDtypeStruct((sq.B, q_heads, sq.S, hdim), BF16),
                   jax.ShapeDtypeStruct((sq.B, kv_heads, hdim, sq.S), BF16),
                   jax.ShapeDtypeStruct((sq.B, kv_heads, sq.S, 2 * hdim), BF16)],
        compiler_params=_params(2),
        name="attn_prep",
    )(proj, proj, cs_tab, *consts)


def _attn_body(q_ref, kt_ref, v_ref, o_ref, s_ref, p_ref, *, group, n_ctx_qblocks, n_ctx_keys):
    tq, hdim = q_ref.shape[1], q_ref.shape[2]
    n_keys = kt_ref.shape[1]

    def attend(nk):
        tiles = range(0, nk, KEY_BLOCK)
        for g in range(group):
            q = q_ref[g]
            rows = slice(g * tq, (g + 1) * tq)
            mx = None
            for t in tiles:
                s = jnp.dot(q, kt_ref[:, t:t + KEY_BLOCK], preferred_element_type=F32)
                s_ref[:, t:t + KEY_BLOCK] = s
                for c0 in range(0, KEY_BLOCK, LANES):
                    part = s[:, c0:c0 + LANES]
                    mx = part if mx is None else jnp.maximum(mx, part)
            m = jnp.broadcast_to(jnp.max(mx, axis=1, keepdims=True), (tq, LANES))
            for t in tiles:
                for c0 in range(t, t + KEY_BLOCK, LANES):
                    p_ref[rows, c0:c0 + LANES] = jnp.exp(s_ref[:, c0:c0 + LANES] - m).astype(BF16)
        acc = jnp.dot(p_ref[:, 0:nk], v_ref[0:nk, :], preferred_element_type=F32)
        out = acc[:, :hdim] / acc[:, hdim:]
        for g in range(group):
            o_ref[:, g * hdim:(g + 1) * hdim] = out[g * tq:(g + 1) * tq, :]

    is_ctx = pl.program_id(2) < n_ctx_qblocks

    @pl.when(is_ctx)
    def _():
        attend(n_ctx_keys)

    @pl.when(jnp.logical_not(is_ctx))
    def _():
        attend(n_keys)


def _attention(sq, q, kt, v):
    b, q_heads, s, hdim = q.shape
    kv_heads = kt.shape[1]
    group = q_heads // kv_heads
    tq = _largest_divisor(sq.ctx, (128, 64))
    assert sq.ctx % KEY_BLOCK == 0 and s % KEY_BLOCK == 0
    return pl.pallas_call(
        functools.partial(_attn_body, group=group, n_ctx_qblocks=sq.ctx // tq, n_ctx_keys=sq.ctx),
        grid=(b, kv_heads, s // tq),
        in_specs=[pl.BlockSpec((None, group, tq, hdim), lambda bi, g, i: (bi, g, i, 0)),
                  pl.BlockSpec((None, None, hdim, s), lambda bi, g, i: (bi, g, 0, 0)),
                  pl.BlockSpec((None, None, s, 2 * hdim), lambda bi, g, i: (bi, g, 0, 0))],
        out_specs=pl.BlockSpec((None, tq, group * hdim), lambda bi, g, i: (bi, i, g)),
        out_shape=jax.ShapeDtypeStruct((b, s, q_heads * hdim), F32),
        scratch_shapes=[pltpu.VMEM((tq, s), F32), pltpu.VMEM((group * tq, s), BF16)],
        compiler_params=_params(3),
        name="attention",
    )(q, kt, v)


def _merge_body(x_ref, ya_ref, yb_ref, yc_ref, gate_ref, mod_ref, wa_ref, wb_ref, wc_ref, wo_ref, o_ref,
                *, gate_row):
    d = x_ref.shape[-1]
    m = gate_ref[:, :d] * _dot(ya_ref[...], wa_ref[...])
    m = m + gate_ref[:, d:2 * d] * _dot(yb_ref[...], wb_ref[...])
    m = m + gate_ref[:, 2 * d:] * _dot(yc_ref[...], wc_ref[...])
    y = _dot(m, wo_ref[...])
    o_ref[...] = x_ref[...] + mod_ref[gate_row:gate_row + 1, :] * y


def _merge(sq, xs, ya, yb, yc, gates, modtab, lw):
    d = xs.shape[-1]
    consts = [lw["w_up_a"], lw["w_up_b"], lw["w_up_c"], lw["w_out"]]
    return pl.pallas_call(
        functools.partial(_merge_body, gate_row=2),
        grid=sq.grid,
        in_specs=[sq.rows(d), sq.rows(ya.shape[-1]), sq.rows(yb.shape[-1]), sq.rows(yc.shape[-1]),
                  sq.rows(3 * d), sq.mod(d)] + [sq.const(c.shape) for c in consts],
        out_specs=sq.rows(d),
        out_shape=jax.ShapeDtypeStruct(xs.shape, F32),
        compiler_params=_params(2),
        name="merge",
    )(xs, ya, yb, yc, gates, modtab, *consts)


def _ffn_body(x_ref, g_ref, mod_ref, w1_ref, w3_ref, w2_ref, o_ref, *, h_chunk):
    x = x_ref[...]
    y = x * lax.rsqrt(jnp.mean(x * x, axis=-1, keepdims=True) + RMS_EPS) * g_ref[...]
    h = (y * (1.0 + mod_ref[4:5, :]) + mod_ref[3:4, :]).astype(BF16)
    hidden = w1_ref.shape[1]
    acc = jnp.zeros(x.shape, F32)
    for c0 in range(0, hidden, h_chunk):
        a = jnp.dot(h, w1_ref[:, c0:c0 + h_chunk], preferred_element_type=F32)
        b = jnp.dot(h, w3_ref[:, c0:c0 + h_chunk], preferred_element_type=F32)
        t = (a * _sigmoid(a) * b).astype(BF16)
        acc = acc + jnp.dot(t, w2_ref[c0:c0 + h_chunk, :], preferred_element_type=F32)
    o_ref[...] = x + mod_ref[5:6, :] * acc


def _ffn(sq, xs, g, modtab, lw):
    d = xs.shape[-1]
    consts = [lw["ffn_w1"], lw["ffn_w3"], lw["ffn_w2"]]
    h_chunk = _largest_divisor(lw["ffn_w1"].shape[1], (512, 256, 128))
    return pl.pallas_call(
        functools.partial(_ffn_body, h_chunk=h_chunk),
        grid=sq.grid,
        in_specs=[sq.rows(d), sq.const((1, d)), sq.mod(d)] + [sq.const(c.shape) for c in consts],
        out_specs=sq.rows(d),
        out_shape=jax.ShapeDtypeStruct(xs.shape, F32),
        compiler_params=_params(2),
        name="ffn",
    )(xs, g.reshape(1, d), modtab, *consts)


def _block_diag_ones(n, seg):
    idx = jnp.arange(n) // seg
    return (idx[:, None] == idx[None, :]).astype(BF16)


def _block_diag2(m):
    z = jnp.zeros_like(m[0])
    return jnp.concatenate([jnp.concatenate([m[0], z], axis=1), jnp.concatenate([z, m[1]], axis=1)], axis=0)


def _pad_to(x, axis, size):
    pad = [(0, 0)] * x.ndim
    pad[axis] = (0, size - x.shape[axis])
    return jnp.pad(x, pad)


def _rope_table(ctx_len, seq_len, hdim, kv_heads):
    rows = seq_len // GRID_W
    row = jnp.repeat(jnp.arange(rows), GRID_W).astype(F32)
    col = jnp.tile(jnp.arange(GRID_W), rows).astype(F32)
    half = hdim // 2
    inv = ROPE_THETA ** (-jnp.arange(0, half, 2, dtype=F32) / half)
    ang = jnp.concatenate([row[:, None] * inv, col[:, None] * inv], axis=-1)
    cos = jnp.repeat(jnp.cos(ang), 2, axis=1)
    sin = jnp.repeat(jnp.sin(ang), 2, axis=1) * jnp.tile(jnp.array([-1.0, 1.0], F32), half)
    cos = jnp.concatenate([jnp.ones((ctx_len, hdim), F32), cos], axis=0)
    sin = jnp.concatenate([jnp.zeros((ctx_len, hdim), F32), sin], axis=0)
    return jnp.concatenate([jnp.tile(cos, (1, kv_heads)), jnp.tile(sin, (1, kv_heads))], axis=1)


def kernel(x, c, ctx, c_ctx, ada_w, ada_b, norm1, norm2, w_in, rwkv_mu_x, rwkv_mu_rkv, rwkv_w0, rwkv_w1, rwkv_w2, rwkv_a0, rwkv_a1, rwkv_a2, rwkv_g1, rwkv_g2, rwkv_k_k, rwkv_k_a, rwkv_r_k, rwkv_lnx_w, rwkv_lnx_b, gdn_conv, gdn_w_alpha, gdn_dt_bias, gdn_A_log, gdn_w_beta, gdn_norm, attn_q_norm, attn_k_norm, w_up_a, w_up_b, w_up_c, w_gate, b_gate, w_out, ffn_w1, ffn_w3, ffn_w2, final_norm):
    batch, seq_len, d = x.shape
    ctx_len = ctx.shape[1]
    depth = ada_w.shape[0]
    sq = _Seq(batch, ctx_len, seq_len)

    r_heads, r_hdim = rwkv_r_k.shape[1], rwkv_r_k.shape[2]
    rdim = r_heads * r_hdim
    g_heads, g_hdim = gdn_w_alpha.shape[-1], gdn_norm.shape[-1]
    gdim = g_heads * g_hdim
    a_hdim = attn_q_norm.shape[-1]
    qd = w_up_c.shape[1]
    q_heads = qd // a_hdim
    kvd = (w_in.shape[-1] - 3 * rdim - 4 * gdim - qd) // 2
    kv_heads = kvd // a_hdim
    assert rdim == gdim == qd and r_hdim == a_hdim, "lane-segment constants are shared between mixers"
    assert (3 * rdim) % (3 * gdim) == 0 and (3 * rdim + 3 * gdim) % gdim == 0
    gdn_col = (3 * rdim) // (3 * gdim)
    z_col = (3 * rdim + 3 * gdim) // gdim
    q_col = (3 * rdim + 4 * gdim) // qd
    assert (3 * rdim + 4 * gdim + qd) % (2 * kvd) == 0
    kv_col = (3 * rdim + 4 * gdim + qd) // (2 * kvd)

    bd_r = _block_diag_ones(rdim, r_hdim)
    bd_g = _block_diag_ones(gdim, g_hdim)
    cs_tab = _rope_table(ctx_len, seq_len, a_hdim, kv_heads)
    n_beta = 2 * g_heads
    assert 2 * n_beta <= LANES

    cond = jnp.concatenate([c, c_ctx[None, :]], axis=0)
    cond = _pad_to(cond, 0, -(-(batch + 1) // SUBLANES) * SUBLANES)

    xs = jnp.concatenate([ctx, x], axis=1)
    m_rows = batch * sq.S
    for l in range(depth):
        g1w = _pad_to(rwkv_g1[l], 1, 2 * LANES)
        lw = {
            "mu": rwkv_mu_x[l], "mu_rkv": rwkv_mu_rkv[l],
            "w1": jnp.concatenate([rwkv_w1[l, 0], rwkv_w1[l, 1]], axis=1).astype(BF16),
            "w2": _block_diag2(rwkv_w2[l]).astype(BF16),
            "w0": rwkv_w0[l].reshape(1, 2 * rdim),
            "a1": jnp.concatenate([rwkv_a1[l, 0], rwkv_a1[l, 1]], axis=1).astype(BF16),
            "a2": _block_diag2(rwkv_a2[l]).astype(BF16),
            "a0": rwkv_a0[l].reshape(1, 2 * rdim),
            "g1": g1w.astype(BF16),
            "g2": _pad_to(rwkv_g2[l], 0, 2 * LANES).astype(BF16),
            "wab": _pad_to(jnp.concatenate([gdn_w_beta[l, 0], gdn_w_beta[l, 1],
                                            gdn_w_alpha[l, 0], gdn_w_alpha[l, 1]], axis=1), 1, LANES).astype(BF16),
            "abb": _pad_to(jnp.concatenate([jnp.zeros((n_beta,), F32), gdn_dt_bias[l].reshape(-1)]), 0, LANES).reshape(1, LANES),
            "alog": _pad_to(jnp.concatenate([jnp.zeros((n_beta,), F32), gdn_A_log[l].reshape(-1)]), 0, LANES).reshape(1, LANES),
            "n_beta": n_beta,
            "k_k": rwkv_k_k[l].reshape(1, rdim), "k_a": rwkv_k_a[l].reshape(1, rdim),
            "r_k": rwkv_r_k[l].reshape(1, rdim),
            "lnx_w": rwkv_lnx_w[l].reshape(1, rdim), "lnx_b": rwkv_lnx_b[l].reshape(1, rdim),
            "bd_r": bd_r, "bd_g": bd_g,
            "conv": jnp.transpose(gdn_conv[l]),
            "gdn_norm": jnp.tile(gdn_norm[l], g_heads).reshape(1, gdim),
            "q_norm": jnp.tile(attn_q_norm[l], q_heads).reshape(1, qd),
            "k_norm": jnp.tile(attn_k_norm[l], kv_heads).reshape(1, kvd),
            "w_up_a": w_up_a[l].astype(BF16), "w_up_b": w_up_b[l].astype(BF16),
            "w_up_c": w_up_c[l].astype(BF16), "w_out": w_out[l].astype(BF16),
            "ffn_w1": ffn_w1[l].astype(BF16), "ffn_w3": ffn_w3[l].astype(BF16), "ffn_w2": ffn_w2[l].astype(BF16),
        }
        mod = _matmul(cond, ada_w[l].astype(BF16), bias=ada_b[l], pre_act="silu")
        mod_x = mod[:batch].reshape(batch, 6, d)
        mod_c = jnp.broadcast_to(mod[batch].reshape(1, 6, d), (batch, 6, d))
        modtab = jnp.stack([mod_c, mod_x], axis=1)

        h = _normmod(sq, xs, norm1[l], modtab, shift_row=0, scale_row=1)
        h2 = h.reshape(m_rows, d)
        proj = _matmul(h2, w_in[l].astype(BF16)).reshape(batch, sq.S, -1)
        gates = _matmul(h2, w_gate[l].astype(BF16), bias=b_gate[l], act="sigmoid").reshape(batch, sq.S, -1)
        lw_dec, a_iclr, g_out, ab = _lora(sq, h, lw)

        r, v, aa, kd, bb = _rwkv_prep(sq, proj, a_iclr, lw)
        rf, rg = _rwkv_intra(sq, r, v, aa, kd, lw_dec, bb, r_heads)
        o_rf, o_rb = _rwkv_scan(sq, rf, rg, r_heads)
        ya = _rwkv_post(sq, o_rf, o_rb, r, v, kd, g_out, lw, r_heads)

        gq, gk, gv = _gdn_prep(sq, proj, lw, gdim, gdn_col)
        gf, gga, ggb = _gdn_intra(sq, gq, gk, gv, ab, g_heads)
        o_gf, o_gb = _gdn_scan(sq, gf, gga, ggb, g_heads)
        yb = _gdn_post(sq, o_gf, o_gb, proj, lw, g_heads, z_col)

        aq, ak, av = _attn_prep(sq, proj, cs_tab, lw, q_heads, kv_heads, a_hdim, q_col, kv_col)
        yc = _attention(sq, aq, ak, av)

        xs = _merge(sq, xs, ya, yb, yc, gates, modtab, lw)
        xs = _ffn(sq, xs, norm2[l], modtab, lw)
    return _final_norm(sq, xs, final_norm)
```

```python
import functools

import jax
import jax.numpy as jnp
from jax import lax
from jax.experimental import pallas as pl
from jax.experimental.pallas import tpu as pltpu

F32 = jnp.float32
BF16 = jnp.bfloat16

RMS_EPS = 1e-6
RWKV_LNX_EPS = 64e-5
ROPE_THETA = 10000.0
GRID_W = 64

SUBLANES = 8
LANES = 128
CHUNK = 64
MAX_TOKEN_BLOCK = 256
MAX_MM_ROWS = 512
KEY_BLOCK = 256
RWKV_PASSES = 1
GDN_PASSES = 1
INV_PASSES = 1
RWKV_INTRA_CHUNKS = 2
GDN_INTRA_CHUNKS = 4
SCAN_CHUNKS = 4
VMEM_LIMIT = 56 * 1024 * 1024


def _sigmoid(x):
    return 1.0 / (1.0 + jnp.exp(-x))


def _softplus(x):
    return jnp.maximum(x, 0.0) + jnp.log(1.0 + jnp.exp(-jnp.abs(x)))


def _split2(x):
    hi = x.astype(BF16)
    lo = (x - hi.astype(F32)).astype(BF16)
    return hi, lo


def _mm(a, b, dims, passes):
    d = functools.partial(lax.dot_general, dimension_numbers=(dims, ((), ())), preferred_element_type=F32)
    if passes == 1:
        return d(a.astype(BF16), b.astype(BF16))
    ah, al = _split2(a)
    bh, bl = _split2(b)
    return d(ah, bh) + (d(ah, bl) + d(al, bh))


def _dot(a, b, passes=1):
    return _mm(a, b, ((1,), (0,)), passes)


def _dot_nt(a, b, passes=1):
    return _mm(a, b, ((1,), (1,)), passes)


def _dot_tn(a, b, passes=1):
    return _mm(a, b, ((0,), (0,)), passes)


def _split3(x):
    hi = x.astype(BF16)
    r1 = x - hi.astype(F32)
    mid = r1.astype(BF16)
    lo = (r1 - mid.astype(F32)).astype(BF16)
    return hi, mid, lo


def _dot_exact_lhs(m01, x):
    d = functools.partial(jnp.dot, preferred_element_type=F32)
    hi, mid, lo = _split3(x)
    return d(m01, hi) + (d(m01, mid) + d(m01, lo))


def _segsum(x, bd_ref):
    d = functools.partial(jnp.dot, preferred_element_type=F32)
    hi, lo = _split2(x)
    bd = bd_ref[...]
    return d(hi, bd) + d(lo, bd)


def _tri_inverse(l, eye):
    n = l.shape[0]
    row = lax.broadcasted_iota(jnp.int32, (n, n), 0)
    col = lax.broadcasted_iota(jnp.int32, (n, n), 1)
    t = eye - jnp.where((row >> 1) == (col >> 1), l, 0.0)
    for k in range(2, n.bit_length()):
        off = ((row >> k) == (col >> k)) & ((row >> (k - 1)) != (col >> (k - 1)))
        t = t - _dot(_dot(t, jnp.where(off, l, 0.0), INV_PASSES), t, INV_PASSES)
    return t


def _row_iota(shape):
    return lax.broadcasted_iota(jnp.int32, shape, 0)


def _shift_rows(x, prev_ref, next_ref, k, first, last):
    n = x.shape[0]
    row = _row_iota(x.shape)
    y = pltpu.roll(x, (-k) % n, axis=0)
    if k < 0:
        for i in range(-k):
            edge = prev_ref[SUBLANES + k + i:SUBLANES + k + i + 1, :]
            edge = jnp.where(first, 0.0, edge)
            y = jnp.where(row == i, edge, y)
    else:
        for i in range(k):
            edge = next_ref[i:i + 1, :]
            edge = jnp.where(last, 0.0, edge)
            y = jnp.where(row == n - k + i, edge, y)
    return y


def _largest_divisor(n, candidates):
    for c in candidates:
        if n % c == 0:
            return c
    raise ValueError(f"no block size among {candidates} divides {n}")


class _Seq:
    def __init__(self, batch, ctx_len, seq_len):
        self.B = batch
        self.ctx = ctx_len
        self.S = ctx_len + seq_len
        self.TB = _largest_divisor(ctx_len, (MAX_TOKEN_BLOCK, 128, 64))
        assert seq_len % self.TB == 0 and self.TB % CHUNK == 0
        self.ncb = ctx_len // self.TB
        self.nblk = self.S // self.TB
        self.ncc = ctx_len // CHUNK
        self.nchunk = self.S // CHUNK
        self.grid = (batch, self.nblk)

    def rows(self, width, col_block=0):
        return pl.BlockSpec((None, self.TB, width), lambda b, j: (b, j, col_block))

    def rows2(self, width):
        return pl.BlockSpec((2, None, self.TB, width), lambda b, j: (0, b, j, 0))

    def heads(self, n_heads, width):
        return pl.BlockSpec((None, n_heads, self.TB, width), lambda b, j: (b, 0, j, 0))

    def prev_rows(self, width, col_block=0):
        per = self.TB // SUBLANES
        return pl.BlockSpec((None, SUBLANES, width),
                            lambda b, j: (b, jnp.maximum(j * per - 1, 0), col_block))

    def next_rows(self, width, col_block=0):
        per = self.TB // SUBLANES
        top = self.S // SUBLANES - 1
        return pl.BlockSpec((None, SUBLANES, width),
                            lambda b, j: (b, jnp.minimum((j + 1) * per, top), col_block))

    def const(self, shape):
        zeros = (0,) * len(shape)
        return pl.BlockSpec(shape, lambda b, j: zeros)

    def mod(self, d_model):
        ncb = self.ncb
        return pl.BlockSpec((None, None, 6, d_model),
                            lambda b, j: (b, (j >= ncb).astype(jnp.int32), 0, 0))

    def edges(self):
        j = pl.program_id(1)
        first = (j == 0) | (j == self.ncb)
        last = (j == self.ncb - 1) | (j == self.nblk - 1)
        return first, last


def _params(n_axes):
    return pltpu.CompilerParams(dimension_semantics=("arbitrary",) * n_axes,
                                vmem_limit_bytes=VMEM_LIMIT)


def _mm_body(*refs, act, pre_act, n_chunk, has_bias):
    if has_bias:
        x_ref, w_ref, b_ref, o_ref = refs
    else:
        x_ref, w_ref, o_ref = refs
        b_ref = None
    x = x_ref[...]
    if pre_act == "silu":
        x = x * _sigmoid(x)
    xb = x.astype(BF16)
    n = o_ref.shape[-1]
    for n0 in range(0, n, n_chunk):
        y = jnp.dot(xb, w_ref[:, n0:n0 + n_chunk], preferred_element_type=F32)
        if b_ref is not None:
            y = y + b_ref[:, n0:n0 + n_chunk]
        if act == "sigmoid":
            y = _sigmoid(y)
        o_ref[:, n0:n0 + n_chunk] = y.astype(o_ref.dtype)


def _matmul(x, w, bias=None, act=None, pre_act=None, out_dtype=F32):
    m, k = x.shape
    n = w.shape[1]
    tm = m if m <= MAX_MM_ROWS else _largest_divisor(m, (MAX_MM_ROWS, 256, 128, 64, 32, 16, 8))
    n_chunk = _largest_divisor(n, (512, 256, 128))
    in_specs = [pl.BlockSpec((tm, k), lambda i: (i, 0)),
                pl.BlockSpec((k, n), lambda i: (0, 0))]
    args = [x, w]
    if bias is not None:
        in_specs.append(pl.BlockSpec((1, n), lambda i: (0, 0)))
        args.append(bias.reshape(1, n))
    return pl.pallas_call(
        functools.partial(_mm_body, act=act, pre_act=pre_act, n_chunk=n_chunk,
                          has_bias=bias is not None),
        grid=(m // tm,),
        in_specs=in_specs,
        out_specs=pl.BlockSpec((tm, n), lambda i: (i, 0)),
        out_shape=jax.ShapeDtypeStruct((m, n), out_dtype),
        compiler_params=_params(1),
        name="matmul",
    )(*args)


def _normmod_body(x_ref, g_ref, mod_ref, o_ref, *, shift_row, scale_row):
    x = x_ref[...]
    y = x * lax.rsqrt(jnp.mean(x * x, axis=-1, keepdims=True) + RMS_EPS)
    y = y * g_ref[...]
    o_ref[...] = y * (1.0 + mod_ref[scale_row:scale_row + 1, :]) + mod_ref[shift_row:shift_row + 1, :]


def _normmod(sq, xs, g, modtab, shift_row, scale_row):
    d = xs.shape[-1]
    return pl.pallas_call(
        functools.partial(_normmod_body, shift_row=shift_row, scale_row=scale_row),
        grid=sq.grid,
        in_specs=[sq.rows(d), sq.const((1, d)), sq.mod(d)],
        out_specs=sq.rows(d),
        out_shape=jax.ShapeDtypeStruct(xs.shape, F32),
        compiler_params=_params(2),
        name="normmod",
    )(xs, g.reshape(1, d), modtab)


def _final_norm_body(x_ref, g_ref, o_ref):
    x = x_ref[...]
    y = x * lax.rsqrt(jnp.mean(x * x, axis=-1, keepdims=True) + RMS_EPS)
    o_ref[...] = y * g_ref[...]


def _final_norm(sq, xs, g):
    d = xs.shape[-1]
    ncb = sq.ncb
    return pl.pallas_call(
        _final_norm_body,
        grid=(sq.B, sq.nblk - ncb),
        in_specs=[pl.BlockSpec((None, sq.TB, d), lambda b, j: (b, j + ncb, 0)),
                  pl.BlockSpec((1, d), lambda b, j: (0, 0))],
        out_specs=pl.BlockSpec((None, sq.TB, d), lambda b, j: (b, j, 0)),
        out_shape=jax.ShapeDtypeStruct((sq.B, sq.S - sq.ctx, d), F32),
        compiler_params=_params(2),
        name="final_norm",
    )(xs, g.reshape(1, d))


def _lora_body(h_ref, hp_ref, hn_ref, mu_ref, w1_ref, w2_ref, w0_ref, a1_ref, a2_ref, a0_ref,
               g1_ref, g2_ref, wab_ref, abb_ref, alog_ref,
               lw_ref, a_ref, g_ref, ab_ref, *, sq, rdim, n_beta):
    first, last = sq.edges()
    h = h_ref[...]
    nb = 0.5 * (_shift_rows(h, hp_ref, hn_ref, -1, first, last)
                + _shift_rows(h, hp_ref, hn_ref, 1, first, last))
    dlt = nb - h
    xw = h + dlt * mu_ref[0:1, :]
    xa = h + dlt * mu_ref[1:2, :]
    xg = h + dlt * mu_ref[2:3, :]
    wl = w0_ref[...] + _dot(jnp.tanh(_dot(xw, w1_ref[...])), w2_ref[...])
    w_log = -_softplus(-wl) - 0.5
    lw = -jnp.exp(w_log)
    lw_ref[0] = lw[:, :rdim]
    lw_ref[1] = lw[:, rdim:]
    a = _sigmoid(a0_ref[...] + _dot(_dot(xa, a1_ref[...]), a2_ref[...]))
    a_ref[0] = a[:, :rdim]
    a_ref[1] = a[:, rdim:]
    g_ref[...] = _dot(_sigmoid(_dot(xg, g1_ref[...])), g2_ref[...])
    z = _dot(h, wab_ref[...])
    col = lax.broadcasted_iota(jnp.int32, z.shape, 1)
    gl = -jnp.exp(alog_ref[...]) * _softplus(z + abb_ref[...])
    ab_ref[...] = jnp.where(col < n_beta, _sigmoid(z), gl)


def _lora(sq, h, lw):
    d = h.shape[-1]
    rdim = lw["w0"].shape[-1] // 2
    consts = [lw["mu"], lw["w1"], lw["w2"], lw["w0"], lw["a1"], lw["a2"], lw["a0"],
              lw["g1"], lw["g2"], lw["wab"], lw["abb"], lw["alog"]]
    bsd = (sq.B, sq.S)
    return pl.pallas_call(
        functools.partial(_lora_body, sq=sq, rdim=rdim, n_beta=lw["n_beta"]),
        grid=sq.grid,
        in_specs=[sq.rows(d), sq.prev_rows(d), sq.next_rows(d)] + [sq.const(c.shape) for c in consts],
        out_specs=[sq.rows2(rdim), sq.rows2(rdim), sq.rows(rdim), sq.rows(LANES)],
        out_shape=[jax.ShapeDtypeStruct((2,) + bsd + (rdim,), F32),
                   jax.ShapeDtypeStruct((2,) + bsd + (rdim,), F32),
                   jax.ShapeDtypeStruct(bsd + (rdim,), F32),
                   jax.ShapeDtypeStruct(bsd + (LANES,), F32)],
        compiler_params=_params(2),
        name="lora",
    )(h, h, h, *consts)


def _rwkv_prep_body(p_ref, pp_ref, pn_ref, a_ref, mu_ref, kk_ref, ka_ref, bd_ref,
                    r_ref, v_ref, aa_ref, kd_ref, bb_ref, *, sq, rdim):
    first, last = sq.edges()
    p = p_ref[...]
    nb = 0.5 * (_shift_rows(p, pp_ref, pn_ref, -1, first, last)
                + _shift_rows(p, pp_ref, pn_ref, 1, first, last))
    dlt = nb - p
    r = p[:, :rdim] + dlt[:, :rdim] * mu_ref[0:1, :]
    k = p[:, rdim:2 * rdim] + dlt[:, rdim:2 * rdim] * mu_ref[1:2, :]
    v = p[:, 2 * rdim:] + dlt[:, 2 * rdim:] * mu_ref[2:3, :]
    t = k * kk_ref[...]
    kk = t * lax.rsqrt(_segsum(t * t, bd_ref) + RMS_EPS)
    r_ref[...] = r
    v_ref[...] = v
    aa_ref[...] = -kk
    for d in range(2):
        a = a_ref[d]
        kd_ref[d] = k * (1.0 + (a - 1.0) * ka_ref[...])
        bb_ref[d] = kk * a


def _rwkv_prep(sq, proj, a, lw):
    rdim = a.shape[-1]
    consts = [lw["mu_rkv"], lw["k_k"], lw["k_a"], lw["bd_r"]]
    one = jax.ShapeDtypeStruct((sq.B, sq.S, rdim), F32)
    two = jax.ShapeDtypeStruct((2, sq.B, sq.S, rdim), F32)
    return pl.pallas_call(
        functools.partial(_rwkv_prep_body, sq=sq, rdim=rdim),
        grid=sq.grid,
        in_specs=[sq.rows(3 * rdim), sq.prev_rows(3 * rdim), sq.next_rows(3 * rdim), sq.rows2(rdim)]
        + [sq.const(c.shape) for c in consts],
        out_specs=[sq.rows(rdim), sq.rows(rdim), sq.rows(rdim), sq.rows2(rdim), sq.rows2(rdim)],
        out_shape=[one, one, one, two, two],
        compiler_params=_params(2),
        name="rwkv_prep",
    )(proj, proj, proj, a, *consts)


def _chunk_index(d, i, ncc, nchunk):
    back = jnp.where(i < ncc, ncc - 1 - i, nchunk + ncc - 1 - i)
    return jnp.where(d == 0, i, back)


def _causal_masks(d, c, reps=1):
    row = lax.broadcasted_iota(jnp.int32, (c, reps * c), 0)
    col = lax.broadcasted_iota(jnp.int32, (c, reps * c), 1) & (c - 1)
    delta = jnp.where(d == 0, row - col, col - row)
    eye = jnp.where(row == col, 1.0, 0.0).astype(F32)
    return delta >= 0, delta > 0, eye


def _stacked_mask(d, c, reps=1):
    row = lax.broadcasted_iota(jnp.int32, (2 * c, reps * c), 0)
    col = lax.broadcasted_iota(jnp.int32, (2 * c, reps * c), 1) & (c - 1)
    rr = row & (c - 1)
    delta = jnp.where(d == 0, rr - col, col - rr)
    return (delta > 0) | ((row >= c) & (delta == 0))


def _block_rows(x, w):
    left = lax.broadcasted_iota(jnp.int32, x.shape, 1) < w
    zero = jnp.zeros_like(x)
    return jnp.concatenate([jnp.where(left, x, zero), jnp.where(left, zero, x)], axis=0)


def _fold_rows(y):
    n = y.shape[0] // 2
    return y[:n] + y[n:]


def _tri_inverse_pairs(ls):
    n = ls[0].shape[0]
    row = lax.broadcasted_iota(jnp.int32, (n, 2 * n), 0)
    col = lax.broadcasted_iota(jnp.int32, (n, 2 * n), 1) & (n - 1)
    eye = jnp.where(row == col, 1.0, 0.0).astype(F32)
    same = (row >> 1) == (col >> 1)
    ts = [eye - jnp.where(same, l, 0.0) for l in ls]
    for k in range(2, n.bit_length()):
        off = ((row >> k) == (col >> k)) & ((row >> (k - 1)) != (col >> (k - 1)))
        tl = [_dot(t, _block_rows(jnp.where(off, l, 0.0), n), INV_PASSES) for t, l in zip(ts, ls)]
        ts = [t - _dot(x, _block_rows(t, n), INV_PASSES) for x, t in zip(tl, ts)]
    return ts


def _tri_inverse_many(ls, eye):
    n = ls[0].shape[0]
    row = lax.broadcasted_iota(jnp.int32, (n, n), 0)
    col = lax.broadcasted_iota(jnp.int32, (n, n), 1)
    same = (row >> 1) == (col >> 1)
    ts = [eye - jnp.where(same, l, 0.0) for l in ls]
    for k in range(2, n.bit_length()):
        off = ((row >> k) == (col >> k)) & ((row >> (k - 1)) != (col >> (k - 1)))
        tl = [_dot(t, jnp.where(off, l, 0.0), INV_PASSES) for t, l in zip(ts, ls)]
        ts = [t - _dot(x, t, INV_PASSES) for x, t in zip(tl, ts)]
    return ts


def _rwkv_intra_body(r_ref, v_ref, aa_ref, kd_ref, lw_ref, bb_ref, f_ref, g_ref, *, heads, hdim, cps):
    c = CHUNK
    pw = 2 * hdim
    keys, a_p, r_p, v_p, bt_p, kt_p, ec_p, mask2 = [], {}, {}, {}, {}, {}, {}, {}
    for d in range(2):
        incl, _, _ = _causal_masks(d, c)
        m01 = jnp.where(incl, 1.0, 0.0).astype(BF16)
        mask2[d] = _stacked_mask(d, c, reps=2)
        for j in range(cps):
            rows = slice(j * c, (j + 1) * c)
            lw = lw_ref[d, rows, :]
            cum = _dot_exact_lhs(m01, lw)
            e_neg = jnp.exp(-cum)
            e_row = jnp.broadcast_to(jnp.exp(jnp.sum(lw, axis=0, keepdims=True)), lw.shape)
            a_t = aa_ref[rows, :] * jnp.exp(cum - lw)
            r_t = r_ref[rows, :] * jnp.exp(cum)
            b_t = bb_ref[d, rows, :] * e_neg
            k_t = kd_ref[d, rows, :] * e_neg
            v = v_ref[rows, :]
            for p in range(heads // 2):
                key = (d, j, p)
                keys.append(key)
                lanes = slice(p * pw, (p + 1) * pw)
                a_p[key], r_p[key], v_p[key] = a_t[:, lanes], r_t[:, lanes], v[:, lanes]
                bt_p[key] = jnp.transpose(_block_rows(b_t[:, lanes], hdim))
                kt_p[key] = jnp.transpose(_block_rows(k_t[:, lanes], hdim))
                ec_p[key] = jnp.transpose(_block_rows(e_row[:, lanes], hdim))
    ar = {k: jnp.concatenate([a_p[k], r_p[k]], axis=0) for k in keys}
    mb = {k: jnp.where(mask2[k[0]], _dot(ar[k], bt_p[k], RWKV_PASSES), 0.0) for k in keys}
    mk = {k: jnp.where(mask2[k[0]], _dot(ar[k], kt_p[k], RWKV_PASSES), 0.0) for k in keys}
    v_bd = {k: _block_rows(v_p[k], hdim) for k in keys}
    x = {k: _dot(mk[k], v_bd[k], RWKV_PASSES) for k in keys}
    h0 = {k: _dot(_fold_rows(ec_p[k] * kt_p[k]), v_bd[k], RWKV_PASSES) for k in keys}
    t = dict(zip(keys, _tri_inverse_pairs([-mb[k][:c] for k in keys])))
    wt = {k: _dot(t[k], _block_rows(a_p[k], hdim), RWKV_PASSES) for k in keys}
    ut = {k: _dot(t[k], _block_rows(x[k][:c], hdim), RWKV_PASSES) for k in keys}
    for k in keys:
        d, j, p = k
        o = j * 4 * c
        lanes = slice(p * pw, (p + 1) * pw)
        f_ref[d, o:o + c, lanes] = ut[k]
        f_ref[d, o + c:o + 2 * c, lanes] = x[k][c:]
        f_ref[d, o + 2 * c:o + 3 * c, lanes] = h0[k]
        f_ref[d, o + 3 * c:o + 4 * c, lanes] = _fold_rows(ec_p[k])
        g_ref[d, o:o + c, lanes] = wt[k].astype(BF16)
        g_ref[d, o + c:o + 2 * c, lanes] = r_p[k].astype(BF16)
        g_ref[d, o + 2 * c:o + 3 * c, lanes] = mb[k][c:].astype(BF16)
        g_ref[d, o + 3 * c:o + 4 * c, lanes] = _fold_rows(ec_p[k] * bt_p[k]).astype(BF16)


def _rwkv_intra(sq, r, v, aa, kd, lw, bb, heads):
    rdim = r.shape[-1]
    hdim = rdim // heads
    assert hdim == CHUNK and heads % 2 == 0 and 2 * hdim == LANES, "pair tiles are [CHUNK, 128 lanes]"
    cps = _largest_divisor(sq.nchunk, (RWKV_INTRA_CHUNKS, 1))
    rows = cps * CHUNK
    shared = pl.BlockSpec((None, rows, rdim), lambda b, i: (b, i, 0))
    per_dir = pl.BlockSpec((2, None, rows, rdim), lambda b, i: (0, b, i, 0))
    out = pl.BlockSpec((2, None, 4 * rows, rdim), lambda b, i: (0, b, i, 0))
    shape = (2, sq.B, sq.nchunk * 4 * CHUNK, rdim)
    return pl.pallas_call(
        functools.partial(_rwkv_intra_body, heads=heads, hdim=hdim, cps=cps),
        grid=(sq.B, sq.nchunk // cps),
        in_specs=[shared, shared, shared, per_dir, per_dir, per_dir],
        out_specs=[out, out],
        out_shape=[jax.ShapeDtypeStruct(shape, F32), jax.ShapeDtypeStruct(shape, BF16)],
        compiler_params=_params(2),
        name="rwkv_intra",
    )(r, v, aa, kd, lw, bb)


def _rwkv_scan_body(f0_ref, g0_ref, f1_ref, g1_ref, of_ref, ob_ref, h_ref, *, heads, hdim, cps):
    @pl.when(pl.program_id(1) == 0)
    def _():
        h_ref[...] = jnp.zeros_like(h_ref)

    c = CHUNK
    pw = 2 * hdim
    fs, gs, outs = (f0_ref, f1_ref), (g0_ref, g1_ref), (of_ref, ob_ref)
    chains = [(d, p) for d in range(2) for p in range(heads // 2)]
    lanes = [slice(p * pw, (p + 1) * pw) for p in range(heads // 2)]
    dot = functools.partial(jnp.dot, preferred_element_type=F32)
    st = {ch: h_ref[ch[0], ch[1]] for ch in chains}
    for step in range(cps):
        sub = (step, cps - 1 - step)
        o = [4 * c * sub[d] for d in range(2)]
        m2 = {(d, p): dot(gs[d][o[d]:o[d] + 2 * c, lanes[p]], st[d, p].astype(BF16)) for d, p in chains}
        ub = {(d, p): _block_rows((fs[d][o[d]:o[d] + c, lanes[p]] + m2[d, p][:c]).astype(BF16), hdim)
              for d, p in chains}
        for d, p in chains:
            y = (fs[d][o[d] + c:o[d] + 2 * c, lanes[p]] + m2[d, p][c:]
                 + dot(gs[d][o[d] + 2 * c:o[d] + 3 * c, lanes[p]], ub[d, p]))
            outs[d][sub[d] * c:(sub[d] + 1) * c, lanes[p]] = y
            st[d, p] = (_block_rows(fs[d][o[d] + 3 * c:o[d] + 4 * c, lanes[p]], hdim) * st[d, p]
                        + _block_rows(fs[d][o[d] + 2 * c:o[d] + 3 * c, lanes[p]], hdim)
                        + dot(_block_rows(gs[d][o[d] + 3 * c:o[d] + 4 * c, lanes[p]], hdim), ub[d, p]))
    for d, p in chains:
        h_ref[d, p] = st[d, p]


def _backward_block(i, ncb, nblk):
    return jnp.where(i < ncb, ncb - 1 - i, nblk + ncb - 1 - i)


def _scan_chunks_per_step(sq):
    cps = _largest_divisor(sq.ncc, (SCAN_CHUNKS, 2, 1))
    assert sq.nchunk % cps == 0
    return cps


def _scan_specs(sq, heads, rows, width):
    cps = _scan_chunks_per_step(sq)
    ncb, nblk = sq.ncc // cps, sq.nchunk // cps
    fwd = pl.BlockSpec((None, None, heads, cps * rows, width), lambda b, i: (0, b, 0, i, 0))
    bwd = pl.BlockSpec((None, None, heads, cps * rows, width),
                       lambda b, i: (1, b, 0, _backward_block(i, ncb, nblk), 0))
    return fwd, bwd


def _scan_out_specs(sq, width):
    cps = _scan_chunks_per_step(sq)
    ncb, nblk = sq.ncc // cps, sq.nchunk // cps
    fwd = pl.BlockSpec((None, cps * CHUNK, width), lambda b, i: (b, i, 0))
    bwd = pl.BlockSpec((None, cps * CHUNK, width), lambda b, i: (b, _backward_block(i, ncb, nblk), 0))
    return [fwd, bwd]


def _rwkv_scan(sq, f, g, heads):
    rdim = f.shape[-1]
    hdim = rdim // heads
    cps = _scan_chunks_per_step(sq)
    ncb, nblk = sq.ncc // cps, sq.nchunk // cps
    rows = cps * 4 * CHUNK
    ff = pl.BlockSpec((None, None, rows, rdim), lambda b, i: (0, b, i, 0))
    fb = pl.BlockSpec((None, None, rows, rdim), lambda b, i: (1, b, _backward_block(i, ncb, nblk), 0))
    one = jax.ShapeDtypeStruct((sq.B, sq.S, rdim), F32)
    return pl.pallas_call(
        functools.partial(_rwkv_scan_body, heads=heads, hdim=hdim, cps=cps),
        grid=(sq.B, sq.nchunk // cps),
        in_specs=[ff, ff, fb, fb],
        out_specs=_scan_out_specs(sq, rdim),
        out_shape=[one, one],
        scratch_shapes=[pltpu.VMEM((2, heads // 2, 2 * hdim, 2 * hdim), F32)],
        compiler_params=_params(2),
        name="rwkv_scan",
    )(f, g, f, g)


def _rwkv_post_body(of_ref, ob_ref, r_ref, v_ref, kd_ref, g_ref, rk_ref, lnw_ref, lnb_ref, bd_ref, y_ref, *, hdim):
    o = of_ref[...] + ob_ref[...]
    inv_n = 1.0 / hdim
    mean = _segsum(o, bd_ref) * inv_n
    cen = o - mean
    var = _segsum(cen * cen, bd_ref) * inv_n
    o = cen * lax.rsqrt(var + RWKV_LNX_EPS) * lnw_ref[...] + lnb_ref[...]
    r = r_ref[...]
    v = v_ref[...]
    rk = rk_ref[...]
    for d in range(2):
        o = o + _segsum(r * kd_ref[d] * rk, bd_ref) * v
    y_ref[...] = o * g_ref[...]


def _rwkv_post(sq, o_f, o_b, r, v, kd, g, lw, heads):
    rdim = r.shape[-1]
    consts = [lw["r_k"], lw["lnx_w"], lw["lnx_b"], lw["bd_r"]]
    return pl.pallas_call(
        functools.partial(_rwkv_post_body, hdim=rdim // heads),
        grid=sq.grid,
        in_specs=[sq.rows(rdim), sq.rows(rdim), sq.rows(rdim), sq.rows(rdim), sq.rows2(rdim), sq.rows(rdim)]
        + [sq.const(c.shape) for c in consts],
        out_specs=sq.rows(rdim),
        out_shape=jax.ShapeDtypeStruct((sq.B, sq.S, rdim), F32),
        compiler_params=_params(2),
        name="rwkv_post",
    )(o_f, o_b, r, v, kd, g, *consts)


def _gdn_prep_body(p_ref, pp_ref, pn_ref, cw_ref, bd_ref, q_ref, k_ref, v_ref, *, sq, gdim):
    first, last = sq.edges()
    p = p_ref[...]
    width = cw_ref.shape[0]
    half = width // 2
    acc = p * cw_ref[half:half + 1, :]
    for j in range(width):
        if j != half:
            acc = acc + _shift_rows(p, pp_ref, pn_ref, j - half, first, last) * cw_ref[j:j + 1, :]
    y = acc * _sigmoid(acc)
    q = y[:, :gdim]
    k = y[:, gdim:2 * gdim]
    q_ref[...] = q * lax.rsqrt(_segsum(q * q, bd_ref) + RMS_EPS)
    k_ref[...] = k * lax.rsqrt(_segsum(k * k, bd_ref) + RMS_EPS)
    v_ref[...] = y[:, 2 * gdim:]


def _gdn_prep(sq, proj, lw, gdim, col_block):
    consts = [lw["conv"], lw["bd_g"]]
    one = jax.ShapeDtypeStruct((sq.B, sq.S, gdim), F32)
    w = 3 * gdim
    return pl.pallas_call(
        functools.partial(_gdn_prep_body, sq=sq, gdim=gdim),
        grid=sq.grid,
        in_specs=[sq.rows(w, col_block), sq.prev_rows(w, col_block), sq.next_rows(w, col_block)]
        + [sq.const(c.shape) for c in consts],
        out_specs=[sq.rows(gdim)] * 3,
        out_shape=[one, one, one],
        compiler_params=_params(2),
        name="gdn_prep",
    )(proj, proj, proj, *consts)


def _gdn_intra_body(q_ref, k_ref, v_ref, ab_ref, f_ref, ga_ref, gb_ref, *, heads, hdim, cps):
    c = CHUNK
    scale = hdim ** -0.5
    sl = [slice(h * hdim, (h + 1) * hdim) for h in range(heads)]
    keys = []
    beta, gc, g_last, gc_row, decay, kb, kq, kt_h, vb, strict_of, eye = {}, {}, {}, {}, {}, {}, {}, {}, {}, {}, None
    for j in range(cps):
        rows = slice(j * c, (j + 1) * c)
        ab = ab_ref[rows, :]
        tot_all = jnp.sum(ab, axis=0, keepdims=True)
        lane = lax.broadcasted_iota(jnp.int32, ab.shape, 1)
        k_all = k_ref[rows, :]
        k_tt = jnp.transpose(k_all)
        for d in range(2):
            incl, strict, eye = _causal_masks(d, c)
            strict_of[d] = strict
            gc_all = _dot_exact_lhs(jnp.where(incl, 1.0, 0.0).astype(BF16), ab)
            gc_all_t = jnp.transpose(gc_all)
            sub = lax.broadcasted_iota(jnp.int32, gc_all_t.shape, 0)
            for h in range(heads):
                key = (d, j, h)
                keys.append(key)
                pick_b = lane == d * heads + h
                pick_g = lane == 2 * heads + d * heads + h
                beta[key] = jnp.sum(jnp.where(pick_b, ab, 0.0), axis=1, keepdims=True)
                gc[key] = jnp.sum(jnp.where(pick_g, gc_all, 0.0), axis=1, keepdims=True)
                g_last[key] = jnp.sum(jnp.where(pick_g[:1], tot_all, 0.0), axis=1, keepdims=True)
                gc_row[key] = jnp.sum(jnp.where(sub == 2 * heads + d * heads + h, gc_all_t, 0.0),
                                      axis=0, keepdims=True)
                diff = gc[key] - gc_row[key]
                decay[key] = jnp.where(incl, jnp.exp(jnp.where(incl, diff, 0.0)), 0.0)
                kb[key] = k_all[:, sl[h]] * beta[key]
                kq[key] = jnp.concatenate([kb[key], q_ref[rows, sl[h]] * scale], axis=0)
                kt_h[key] = k_tt[sl[h]]
                vb[key] = v_ref[rows, sl[h]] * beta[key]
    m = {k: _dot(kq[k], kt_h[k], GDN_PASSES) for k in keys}
    lower = [jnp.where(strict_of[k[0]], m[k][:c] * decay[k], 0.0) for k in keys]
    t = dict(zip(keys, _tri_inverse_many(lower, eye)))
    e_gc = {k: jnp.exp(gc[k]) for k in keys}
    sol = {k: _dot(t[k], jnp.concatenate([vb[k], kb[k] * e_gc[k]], axis=1), GDN_PASSES) for k in keys}
    for k in keys:
        d, j, h = k
        of, oa, ob = j * (c + SUBLANES), j * 2 * c, j * (c + hdim)
        f_ref[d, h, of:of + c] = sol[k][:, :hdim]
        f_ref[d, h, of + c:of + c + SUBLANES] = jnp.broadcast_to(jnp.exp(g_last[k]), (SUBLANES, hdim))
        ga_ref[d, h, oa:oa + c] = sol[k][:, hdim:].astype(BF16)
        ga_ref[d, h, oa + c:oa + 2 * c] = (kq[k][c:] * e_gc[k]).astype(BF16)
        gb_ref[d, h, ob:ob + c] = (m[k][c:] * decay[k]).astype(BF16)
        gb_ref[d, h, ob + c:ob + c + hdim] = (kt_h[k] * jnp.exp(g_last[k] - gc_row[k])).astype(BF16)


def _gdn_intra(sq, q, k, v, ab, heads):
    gdim = q.shape[-1]
    hdim = gdim // heads
    cps = _largest_divisor(sq.nchunk, (GDN_INTRA_CHUNKS, 2, 1))
    shared = pl.BlockSpec((None, cps * CHUNK, gdim), lambda b, i: (b, i, 0))
    small = pl.BlockSpec((None, cps * CHUNK, LANES), lambda b, i: (b, i, 0))
    rows = (CHUNK + SUBLANES, 2 * CHUNK, CHUNK + hdim)
    widths = (hdim, hdim, CHUNK)
    dtypes = (F32, BF16, BF16)
    outs = [pl.BlockSpec((2, None, heads, cps * r, w), lambda b, i: (0, b, 0, i, 0)) for r, w in zip(rows, widths)]
    shapes = [jax.ShapeDtypeStruct((2, sq.B, heads, sq.nchunk * r, w), dt) for r, w, dt in zip(rows, widths, dtypes)]
    return pl.pallas_call(
        functools.partial(_gdn_intra_body, heads=heads, hdim=hdim, cps=cps),
        grid=(sq.B, sq.nchunk // cps),
        in_specs=[shared, shared, shared, small],
        out_specs=outs,
        out_shape=shapes,
        compiler_params=_params(2),
        name="gdn_intra",
    )(q, k, v, ab)


def _gdn_scan_body(f0_ref, ga0_ref, gb0_ref, f1_ref, ga1_ref, gb1_ref, of_ref, ob_ref, s_ref, *, heads, hdim, cps):
    @pl.when(pl.program_id(1) == 0)
    def _():
        s_ref[...] = jnp.zeros_like(s_ref)

    c = CHUNK
    fs, gas, gbs, outs = (f0_ref, f1_ref), (ga0_ref, ga1_ref), (gb0_ref, gb1_ref), (of_ref, ob_ref)
    chains = [(d, h) for d in range(2) for h in range(heads)]
    dot = functools.partial(jnp.dot, preferred_element_type=F32)
    st = {ch: s_ref[ch[0], ch[1]] for ch in chains}
    for step in range(cps):
        sub = (step, cps - 1 - step)
        of = [(c + SUBLANES) * sub[d] for d in range(2)]
        oa = [2 * c * sub[d] for d in range(2)]
        ob = [(c + hdim) * sub[d] for d in range(2)]
        m = {(d, h): dot(gas[d][h, oa[d]:oa[d] + 2 * c, :], st[d, h].astype(BF16)) for d, h in chains}
        vn = {(d, h): (fs[d][h, of[d]:of[d] + c, :] - m[d, h][:c]).astype(BF16) for d, h in chains}
        for d, h in chains:
            outs[d][sub[d] * c:(sub[d] + 1) * c, h * hdim:(h + 1) * hdim] = (
                m[d, h][c:] + dot(gbs[d][h, ob[d]:ob[d] + c, :], vn[d, h]))
            st[d, h] = (st[d, h] * fs[d][h, of[d] + c:of[d] + c + 1, :]
                        + dot(gbs[d][h, ob[d] + c:ob[d] + c + hdim, :], vn[d, h]))
    for d, h in chains:
        s_ref[d, h] = st[d, h]


def _gdn_scan(sq, f, ga, gb, heads):
    hdim = f.shape[-1]
    gdim = heads * hdim
    cps = _scan_chunks_per_step(sq)
    specs = []
    for arr in (f, ga, gb):
        specs.append(_scan_specs(sq, heads, arr.shape[3] // sq.nchunk, arr.shape[4]))
    one = jax.ShapeDtypeStruct((sq.B, sq.S, gdim), F32)
    return pl.pallas_call(
        functools.partial(_gdn_scan_body, heads=heads, hdim=hdim, cps=cps),
        grid=(sq.B, sq.nchunk // cps),
        in_specs=[s[0] for s in specs] + [s[1] for s in specs],
        out_specs=_scan_out_specs(sq, gdim),
        out_shape=[one, one],
        scratch_shapes=[pltpu.VMEM((2, heads, hdim, hdim), F32)],
        compiler_params=_params(2),
        name="gdn_scan",
    )(f, ga, gb, f, ga, gb)


def _gdn_post_body(of_ref, ob_ref, z_ref, gn_ref, bd_ref, y_ref, *, hdim):
    o = of_ref[...] + ob_ref[...]
    ms = _segsum(o * o, bd_ref) * (1.0 / hdim)
    z = z_ref[...]
    y_ref[...] = o * lax.rsqrt(ms + RMS_EPS) * gn_ref[...] * (z * _sigmoid(z))


def _gdn_post(sq, o_f, o_b, proj, lw, heads, z_col_block):
    gdim = o_f.shape[-1]
    consts = [lw["gdn_norm"], lw["bd_g"]]
    return pl.pallas_call(
        functools.partial(_gdn_post_body, hdim=gdim // heads),
        grid=sq.grid,
        in_specs=[sq.rows(gdim), sq.rows(gdim), sq.rows(gdim, z_col_block)] + [sq.const(c.shape) for c in consts],
        out_specs=sq.rows(gdim),
        out_shape=jax.ShapeDtypeStruct((sq.B, sq.S, gdim), F32),
        compiler_params=_params(2),
        name="gdn_post",
    )(o_f, o_b, proj, *consts)


def _rope(x, cos, sin_signed):
    n = x.shape[-1]
    lane = lax.broadcasted_iota(jnp.int32, x.shape, 1)
    partner = jnp.where((lane & 1) == 0, pltpu.roll(x, n - 1, axis=1), pltpu.roll(x, 1, axis=1))
    return x * cos + partner * sin_signed


def _attn_prep_body(q_ref, kv_ref, cs_ref, qn_ref, kn_ref, bd_ref, qo_ref, ko_ref, vo_ref,
                    *, q_heads, kv_heads, hdim):
    kvd = kv_heads * hdim
    cos = cs_ref[:, :kvd]
    sin = cs_ref[:, kvd:]
    reps = q_heads // kv_heads
    cos_q = jnp.concatenate([cos] * reps, axis=1)
    sin_q = jnp.concatenate([sin] * reps, axis=1)
    inv_n = 1.0 / hdim
    q = q_ref[...]
    q = q * lax.rsqrt(_segsum(q * q, bd_ref) * inv_n + RMS_EPS) * qn_ref[...]
    q = _rope(q, cos_q, sin_q) * (hdim ** -0.5)
    kv = kv_ref[...]
    k = kv[:, :kvd]
    kbd = bd_ref[:kvd, :kvd]
    hi, lo = _split2(k * k)
    ms = (jnp.dot(hi, kbd, preferred_element_type=F32) + jnp.dot(lo, kbd, preferred_element_type=F32)) * inv_n
    k = k * lax.rsqrt(ms + RMS_EPS) * kn_ref[...]
    k = _rope(k, cos, sin)
    v = kv[:, kvd:]
    for h in range(q_heads):
        qo_ref[h] = q[:, h * hdim:(h + 1) * hdim].astype(BF16)
    k_t = jnp.transpose(k)
    for h in range(kv_heads):
        ko_ref[h] = k_t[h * hdim:(h + 1) * hdim].astype(BF16)
        vh = v[:, h * hdim:(h + 1) * hdim]
        vo_ref[h] = jnp.concatenate([vh, jnp.ones_like(vh)], axis=1).astype(BF16)


def _attn_prep(sq, proj, cs_tab, lw, q_heads, kv_heads, hdim, q_col_block, kv_col_block):
    qd, kvd = q_heads * hdim, kv_heads * hdim
    consts = [lw["q_norm"], lw["k_norm"], lw["bd_r"]]
    k_spec = pl.BlockSpec((None, kv_heads, hdim, sq.TB), lambda b, j: (b, 0, 0, j))
    return pl.pallas_call(
        functools.partial(_attn_prep_body, q_heads=q_heads, kv_heads=kv_heads, hdim=hdim),
        grid=sq.grid,
        in_specs=[sq.rows(qd, q_col_block), sq.rows(2 * kvd, kv_col_block),
                  pl.BlockSpec((sq.TB, 2 * kvd), lambda b, j: (j, 0))]
        + [sq.const(c.shape) for c in consts],
        out_specs=[sq.heads(q_heads, hdim), k_spec, sq.heads(kv_heads, 2 * hdim)],
        out_shape=[jax.ShapeDtypeStruct((sq.B, q_heads, sq.S, hdim), BF16),
                   jax.ShapeDtypeStruct((sq.B, kv_heads, hdim, sq.S), BF16),
                   jax.ShapeDtypeStruct((sq.B, kv_heads, sq.S, 2 * hdim), BF16)],
        compiler_params=_params(2),
        name="attn_prep",
    )(proj, proj, cs_tab, *consts)


def _attn_body(q_ref, kt_ref, v_ref, o_ref, s_ref, p_ref, *, group, n_ctx_qblocks, n_ctx_keys):
    tq, hdim = q_ref.shape[1], q_ref.shape[2]
    n_keys = kt_ref.shape[1]

    def attend(nk):
        tiles = range(0, nk, KEY_BLOCK)
        for g in range(group):
            q = q_ref[g]
            rows = slice(g * tq, (g + 1) * tq)
            mx = None
            for t in tiles:
                s = jnp.dot(q, kt_ref[:, t:t + KEY_BLOCK], preferred_element_type=F32)
                s_ref[:, t:t + KEY_BLOCK] = s
                for c0 in range(0, KEY_BLOCK, LANES):
                    part = s[:, c0:c0 + LANES]
                    mx = part if mx is None else jnp.maximum(mx, part)
            m = jnp.broadcast_to(jnp.max(mx, axis=1, keepdims=True), (tq, LANES))
            for t in tiles:
                for c0 in range(t, t + KEY_BLOCK, LANES):
                    p_ref[rows, c0:c0 + LANES] = jnp.exp(s_ref[:, c0:c0 + LANES] - m).astype(BF16)
        acc = jnp.dot(p_ref[:, 0:nk], v_ref[0:nk, :], preferred_element_type=F32)
        out = acc[:, :hdim] / acc[:, hdim:]
        for g in range(group):
            o_ref[:, g * hdim:(g + 1) * hdim] = out[g * tq:(g + 1) * tq, :]

    is_ctx = pl.program_id(2) < n_ctx_qblocks

    @pl.when(is_ctx)
    def _():
        attend(n_ctx_keys)

    @pl.when(jnp.logical_not(is_ctx))
    def _():
        attend(n_keys)


def _attention(sq, q, kt, v):
    b, q_heads, s, hdim = q.shape
    kv_heads = kt.shape[1]
    group = q_heads // kv_heads
    tq = _largest_divisor(sq.ctx, (128, 64))
    assert sq.ctx % KEY_BLOCK == 0 and s % KEY_BLOCK == 0
    return pl.pallas_call(
        functools.partial(_attn_body, group=group, n_ctx_qblocks=sq.ctx // tq, n_ctx_keys=sq.ctx),
        grid=(b, kv_heads, s // tq),
        in_specs=[pl.BlockSpec((None, group, tq, hdim), lambda bi, g, i: (bi, g, i, 0)),
                  pl.BlockSpec((None, None, hdim, s), lambda bi, g, i: (bi, g, 0, 0)),
                  pl.BlockSpec((None, None, s, 2 * hdim), lambda bi, g, i: (bi, g, 0, 0))],
        out_specs=pl.BlockSpec((None, tq, group * hdim), lambda bi, g, i: (bi, i, g)),
        out_shape=jax.ShapeDtypeStruct((b, s, q_heads * hdim), F32),
        scratch_shapes=[pltpu.VMEM((tq, s), F32), pltpu.VMEM((group * tq, s), BF16)],
        compiler_params=_params(3),
        name="attention",
    )(q, kt, v)


def _merge_body(x_ref, ya_ref, yb_ref, yc_ref, gate_ref, mod_ref, wa_ref, wb_ref, wc_ref, wo_ref, o_ref,
                *, gate_row):
    d = x_ref.shape[-1]
    m = gate_ref[:, :d] * _dot(ya_ref[...], wa_ref[...])
    m = m + gate_ref[:, d:2 * d] * _dot(yb_ref[...], wb_ref[...])
    m = m + gate_ref[:, 2 * d:] * _dot(yc_ref[...], wc_ref[...])
    y = _dot(m, wo_ref[...])
    o_ref[...] = x_ref[...] + mod_ref[gate_row:gate_row + 1, :] * y


def _merge(sq, xs, ya, yb, yc, gates, modtab, lw):
    d = xs.shape[-1]
    consts = [lw["w_up_a"], lw["w_up_b"], lw["w_up_c"], lw["w_out"]]
    return pl.pallas_call(
        functools.partial(_merge_body, gate_row=2),
        grid=sq.grid,
        in_specs=[sq.rows(d), sq.rows(ya.shape[-1]), sq.rows(yb.shape[-1]), sq.rows(yc.shape[-1]),
                  sq.rows(3 * d), sq.mod(d)] + [sq.const(c.shape) for c in consts],
        out_specs=sq.rows(d),
        out_shape=jax.ShapeDtypeStruct(xs.shape, F32),
        compiler_params=_params(2),
        name="merge",
    )(xs, ya, yb, yc, gates, modtab, *consts)


def _ffn_body(x_ref, g_ref, mod_ref, w1_ref, w3_ref, w2_ref, o_ref, *, h_chunk):
    x = x_ref[...]
    y = x * lax.rsqrt(jnp.mean(x * x, axis=-1, keepdims=True) + RMS_EPS) * g_ref[...]
    h = (y * (1.0 + mod_ref[4:5, :]) + mod_ref[3:4, :]).astype(BF16)
    hidden = w1_ref.shape[1]
    acc = jnp.zeros(x.shape, F32)
    for c0 in range(0, hidden, h_chunk):
        a = jnp.dot(h, w1_ref[:, c0:c0 + h_chunk], preferred_element_type=F32)
        b = jnp.dot(h, w3_ref[:, c0:c0 + h_chunk], preferred_element_type=F32)
        t = (a * _sigmoid(a) * b).astype(BF16)
        acc = acc + jnp.dot(t, w2_ref[c0:c0 + h_chunk, :], preferred_element_type=F32)
    o_ref[...] = x + mod_ref[5:6, :] * acc


def _ffn(sq, xs, g, modtab, lw):
    d = xs.shape[-1]
    consts = [lw["ffn_w1"], lw["ffn_w3"], lw["ffn_w2"]]
    h_chunk = _largest_divisor(lw["ffn_w1"].shape[1], (512, 256, 128))
    return pl.pallas_call(
        functools.partial(_ffn_body, h_chunk=h_chunk),
        grid=sq.grid,
        in_specs=[sq.rows(d), sq.const((1, d)), sq.mod(d)] + [sq.const(c.shape) for c in consts],
        out_specs=sq.rows(d),
        out_shape=jax.ShapeDtypeStruct(xs.shape, F32),
        compiler_params=_params(2),
        name="ffn",
    )(xs, g.reshape(1, d), modtab, *consts)


def _block_diag_ones(n, seg):
    idx = jnp.arange(n) // seg
    return (idx[:, None] == idx[None, :]).astype(BF16)


def _block_diag2(m):
    z = jnp.zeros_like(m[0])
    return jnp.concatenate([jnp.concatenate([m[0], z], axis=1), jnp.concatenate([z, m[1]], axis=1)], axis=0)


def _pad_to(x, axis, size):
    pad = [(0, 0)] * x.ndim
    pad[axis] = (0, size - x.shape[axis])
    return jnp.pad(x, pad)


def _rope_table(ctx_len, seq_len, hdim, kv_heads):
    rows = seq_len // GRID_W
    row = jnp.repeat(jnp.arange(rows), GRID_W).astype(F32)
    col = jnp.tile(jnp.arange(GRID_W), rows).astype(F32)
    half = hdim // 2
    inv = ROPE_THETA ** (-jnp.arange(0, half, 2, dtype=F32) / half)
    ang = jnp.concatenate([row[:, None] * inv, col[:, None] * inv], axis=-1)
    cos = jnp.repeat(jnp.cos(ang), 2, axis=1)
    sin = jnp.repeat(jnp.sin(ang), 2, axis=1) * jnp.tile(jnp.array([-1.0, 1.0], F32), half)
    cos = jnp.concatenate([jnp.ones((ctx_len, hdim), F32), cos], axis=0)
    sin = jnp.concatenate([jnp.zeros((ctx_len, hdim), F32), sin], axis=0)
    return jnp.concatenate([jnp.tile(cos, (1, kv_heads)), jnp.tile(sin, (1, kv_heads))], axis=1)


def kernel(x, c, ctx, c_ctx, ada_w, ada_b, norm1, norm2, w_in, rwkv_mu_x, rwkv_mu_rkv, rwkv_w0, rwkv_w1, rwkv_w2, rwkv_a0, rwkv_a1, rwkv_a2, rwkv_g1, rwkv_g2, rwkv_k_k, rwkv_k_a, rwkv_r_k, rwkv_lnx_w, rwkv_lnx_b, gdn_conv, gdn_w_alpha, gdn_dt_bias, gdn_A_log, gdn_w_beta, gdn_norm, attn_q_norm, attn_k_norm, w_up_a, w_up_b, w_up_c, w_gate, b_gate, w_out, ffn_w1, ffn_w3, ffn_w2, final_norm):
    batch, seq_len, d = x.shape
    ctx_len = ctx.shape[1]
    depth = ada_w.shape[0]
    sq = _Seq(batch, ctx_len, seq_len)

    r_heads, r_hdim = rwkv_r_k.shape[1], rwkv_r_k.shape[2]
    rdim = r_heads * r_hdim
    g_heads, g_hdim = gdn_w_alpha.shape[-1], gdn_norm.shape[-1]
    gdim = g_heads * g_hdim
    a_hdim = attn_q_norm.shape[-1]
    qd = w_up_c.shape[1]
    q_heads = qd // a_hdim
    kvd = (w_in.shape[-1] - 3 * rdim - 4 * gdim - qd) // 2
    kv_heads = kvd // a_hdim
    assert rdim == gdim == qd and r_hdim == a_hdim, "lane-segment constants are shared between mixers"
    assert (3 * rdim) % (3 * gdim) == 0 and (3 * rdim + 3 * gdim) % gdim == 0
    gdn_col = (3 * rdim) // (3 * gdim)
    z_col = (3 * rdim + 3 * gdim) // gdim
    q_col = (3 * rdim + 4 * gdim) // qd
    assert (3 * rdim + 4 * gdim + qd) % (2 * kvd) == 0
    kv_col = (3 * rdim + 4 * gdim + qd) // (2 * kvd)

    bd_r = _block_diag_ones(rdim, r_hdim)
    bd_g = _block_diag_ones(gdim, g_hdim)
    cs_tab = _rope_table(ctx_len, seq_len, a_hdim, kv_heads)
    n_beta = 2 * g_heads
    assert 2 * n_beta <= LANES

    cond = jnp.concatenate([c, c_ctx[None, :]], axis=0)
    cond = _pad_to(cond, 0, -(-(batch + 1) // SUBLANES) * SUBLANES)

    xs = jnp.concatenate([ctx, x], axis=1)
    m_rows = batch * sq.S
    for l in range(depth):
        g1w = _pad_to(rwkv_g1[l], 1, 2 * LANES)
        lw = {
            "mu": rwkv_mu_x[l], "mu_rkv": rwkv_mu_rkv[l],
            "w1": jnp.concatenate([rwkv_w1[l, 0], rwkv_w1[l, 1]], axis=1).astype(BF16),
            "w2": _block_diag2(rwkv_w2[l]).astype(BF16),
            "w0": rwkv_w0[l].reshape(1, 2 * rdim),
            "a1": jnp.concatenate([rwkv_a1[l, 0], rwkv_a1[l, 1]], axis=1).astype(BF16),
            "a2": _block_diag2(rwkv_a2[l]).astype(BF16),
            "a0": rwkv_a0[l].reshape(1, 2 * rdim),
            "g1": g1w.astype(BF16),
            "g2": _pad_to(rwkv_g2[l], 0, 2 * LANES).astype(BF16),
            "wab": _pad_to(jnp.concatenate([gdn_w_beta[l, 0], gdn_w_beta[l, 1],
                                            gdn_w_alpha[l, 0], gdn_w_alpha[l, 1]], axis=1), 1, LANES).astype(BF16),
            "abb": _pad_to(jnp.concatenate([jnp.zeros((n_beta,), F32), gdn_dt_bias[l].reshape(-1)]), 0, LANES).reshape(1, LANES),
            "alog": _pad_to(jnp.concatenate([jnp.zeros((n_beta,), F32), gdn_A_log[l].reshape(-1)]), 0, LANES).reshape(1, LANES),
            "n_beta": n_beta,
            "k_k": rwkv_k_k[l].reshape(1, rdim), "k_a": rwkv_k_a[l].reshape(1, rdim),
            "r_k": rwkv_r_k[l].reshape(1, rdim),
            "lnx_w": rwkv_lnx_w[l].reshape(1, rdim), "lnx_b": rwkv_lnx_b[l].reshape(1, rdim),
            "bd_r": bd_r, "bd_g": bd_g,
            "conv": jnp.transpose(gdn_conv[l]),
            "gdn_norm": jnp.tile(gdn_norm[l], g_heads).reshape(1, gdim),
            "q_norm": jnp.tile(attn_q_norm[l], q_heads).reshape(1, qd),
            "k_norm": jnp.tile(attn_k_norm[l], kv_heads).reshape(1, kvd),
            "w_up_a": w_up_a[l].astype(BF16), "w_up_b": w_up_b[l].astype(BF16),
            "w_up_c": w_up_c[l].astype(BF16), "w_out": w_out[l].astype(BF16),
            "ffn_w1": ffn_w1[l].astype(BF16), "ffn_w3": ffn_w3[l].astype(BF16), "ffn_w2": ffn_w2[l].astype(BF16),
        }
        mod = _matmul(cond, ada_w[l].astype(BF16), bias=ada_b[l], pre_act="silu")
        mod_x = mod[:batch].reshape(batch, 6, d)
        mod_c = jnp.broadcast_to(mod[batch].reshape(1, 6, d), (batch, 6, d))
        modtab = jnp.stack([mod_c, mod_x], axis=1)

        h = _normmod(sq, xs, norm1[l], modtab, shift_row=0, scale_row=1)
        h2 = h.reshape(m_rows, d)
        proj = _matmul(h2, w_in[l].astype(BF16)).reshape(batch, sq.S, -1)
        gates = _matmul(h2, w_gate[l].astype(BF16), bias=b_gate[l], act="sigmoid",
                        out_dtype=BF16).reshape(batch, sq.S, -1)
        lw_dec, a_iclr, g_out, ab = _lora(sq, h, lw)

        r, v, aa, kd, bb = _rwkv_prep(sq, proj, a_iclr, lw)
        rf, rg = _rwkv_intra(sq, r, v, aa, kd, lw_dec, bb, r_heads)
        o_rf, o_rb = _rwkv_scan(sq, rf, rg, r_heads)
        ya = _rwkv_post(sq, o_rf, o_rb, r, v, kd, g_out, lw, r_heads)

        gq, gk, gv = _gdn_prep(sq, proj, lw, gdim, gdn_col)
        gf, gga, ggb = _gdn_intra(sq, gq, gk, gv, ab, g_heads)
        o_gf, o_gb = _gdn_scan(sq, gf, gga, ggb, g_heads)
        yb = _gdn_post(sq, o_gf, o_gb, proj, lw, g_heads, z_col)

        aq, ak, av = _attn_prep(sq, proj, cs_tab, lw, q_heads, kv_heads, a_hdim, q_col, kv_col)
        yc = _attention(sq, aq, ak, av)

        xs = _merge(sq, xs, ya, yb, yc, gates, modtab, lw)
        xs = _ffn(sq, xs, norm2[l], modtab, lw)
    return _final_norm(sq, xs, final_norm)
```

```python
import functools

import jax
import jax.numpy as jnp
from jax import lax
from jax.experimental import pallas as pl
from jax.experimental.pallas import tpu as pltpu

F32 = jnp.float32
BF16 = jnp.bfloat16

RMS_EPS = 1e-6
RWKV_LNX_EPS = 64e-5
ROPE_THETA = 10000.0
GRID_W = 64

SUBLANES = 8
LANES = 128
CHUNK = 64
MAX_TOKEN_BLOCK = 256
MAX_MM_ROWS = 512
FFN_ROWS = (544, 512, 256, 128)
KEY_BLOCK = 256
RWKV_PASSES = 1
GDN_PASSES = 1
INV_PASSES = 1
RWKV_INTRA_CHUNKS = 2
GDN_INTRA_CHUNKS = 4
SCAN_CHUNKS = 4
VMEM_LIMIT = 56 * 1024 * 1024


def _sigmoid(x):
    return 1.0 / (1.0 + jnp.exp(-x))


def _softplus(x):
    return jnp.maximum(x, 0.0) + jnp.log(1.0 + jnp.exp(-jnp.abs(x)))


def _split2(x):
    hi = x.astype(BF16)
    lo = (x - hi.astype(F32)).astype(BF16)
    return hi, lo


def _mm(a, b, dims, passes):
    d = functools.partial(lax.dot_general, dimension_numbers=(dims, ((), ())), preferred_element_type=F32)
    if passes == 1:
        return d(a.astype(BF16), b.astype(BF16))
    ah, al = _split2(a)
    bh, bl = _split2(b)
    return d(ah, bh) + (d(ah, bl) + d(al, bh))


def _dot(a, b, passes=1):
    return _mm(a, b, ((1,), (0,)), passes)


def _dot_nt(a, b, passes=1):
    return _mm(a, b, ((1,), (1,)), passes)


def _dot_tn(a, b, passes=1):
    return _mm(a, b, ((0,), (0,)), passes)


def _split3(x):
    hi = x.astype(BF16)
    r1 = x - hi.astype(F32)
    mid = r1.astype(BF16)
    lo = (r1 - mid.astype(F32)).astype(BF16)
    return hi, mid, lo


def _dot_exact_lhs(m01, x):
    d = functools.partial(jnp.dot, preferred_element_type=F32)
    hi, mid, lo = _split3(x)
    return d(m01, hi) + (d(m01, mid) + d(m01, lo))


def _segsum(x, bd_ref):
    d = functools.partial(jnp.dot, preferred_element_type=F32)
    hi, lo = _split2(x)
    bd = bd_ref[...]
    return d(hi, bd) + d(lo, bd)


def _tri_inverse(l, eye):
    n = l.shape[0]
    row = lax.broadcasted_iota(jnp.int32, (n, n), 0)
    col = lax.broadcasted_iota(jnp.int32, (n, n), 1)
    t = eye - jnp.where((row >> 1) == (col >> 1), l, 0.0)
    for k in range(2, n.bit_length()):
        off = ((row >> k) == (col >> k)) & ((row >> (k - 1)) != (col >> (k - 1)))
        t = t - _dot(_dot(t, jnp.where(off, l, 0.0), INV_PASSES), t, INV_PASSES)
    return t


def _row_iota(shape):
    return lax.broadcasted_iota(jnp.int32, shape, 0)


def _shift_rows(x, prev_ref, next_ref, k, first, last):
    n = x.shape[0]
    row = _row_iota(x.shape)
    y = pltpu.roll(x, (-k) % n, axis=0)
    if k < 0:
        for i in range(-k):
            edge = prev_ref[SUBLANES + k + i:SUBLANES + k + i + 1, :]
            edge = jnp.where(first, 0.0, edge)
            y = jnp.where(row == i, edge, y)
    else:
        for i in range(k):
            edge = next_ref[i:i + 1, :]
            edge = jnp.where(last, 0.0, edge)
            y = jnp.where(row == n - k + i, edge, y)
    return y


def _largest_divisor(n, candidates):
    for c in candidates:
        if n % c == 0:
            return c
    raise ValueError(f"no block size among {candidates} divides {n}")


class _Seq:
    def __init__(self, batch, ctx_len, seq_len):
        self.B = batch
        self.ctx = ctx_len
        self.S = ctx_len + seq_len
        self.TB = _largest_divisor(ctx_len, (MAX_TOKEN_BLOCK, 128, 64))
        assert seq_len % self.TB == 0 and self.TB % CHUNK == 0
        self.ncb = ctx_len // self.TB
        self.nblk = self.S // self.TB
        self.ncc = ctx_len // CHUNK
        self.nchunk = self.S // CHUNK
        self.grid = (batch, self.nblk)

    def rows(self, width, col_block=0):
        return pl.BlockSpec((None, self.TB, width), lambda b, j: (b, j, col_block))

    def rows2(self, width):
        return pl.BlockSpec((2, None, self.TB, width), lambda b, j: (0, b, j, 0))

    def heads(self, n_heads, width):
        return pl.BlockSpec((None, n_heads, self.TB, width), lambda b, j: (b, 0, j, 0))

    def prev_rows(self, width, col_block=0, tb=None):
        per = (tb or self.TB) // SUBLANES
        return pl.BlockSpec((None, SUBLANES, width),
                            lambda b, j: (b, jnp.maximum(j * per - 1, 0), col_block))

    def next_rows(self, width, col_block=0, tb=None):
        per = (tb or self.TB) // SUBLANES
        top = self.S // SUBLANES - 1
        return pl.BlockSpec((None, SUBLANES, width),
                            lambda b, j: (b, jnp.minimum((j + 1) * per, top), col_block))

    def const(self, shape):
        zeros = (0,) * len(shape)
        return pl.BlockSpec(shape, lambda b, j: zeros)

    def mod(self, d_model):
        ncb = self.ncb
        return pl.BlockSpec((None, None, 6, d_model),
                            lambda b, j: (b, (j >= ncb).astype(jnp.int32), 0, 0))

    def edges(self, tb=None):
        tb = tb or self.TB
        j = pl.program_id(1)
        ncb, nblk = self.ctx // tb, self.S // tb
        first = (j == 0) | (j == ncb)
        last = (j == ncb - 1) | (j == nblk - 1)
        return first, last


def _params(n_axes):
    return pltpu.CompilerParams(dimension_semantics=("arbitrary",) * n_axes,
                                vmem_limit_bytes=VMEM_LIMIT)


def _mm_body(*refs, act, pre_act, n_chunk, has_bias):
    if has_bias:
        x_ref, w_ref, b_ref, o_ref = refs
    else:
        x_ref, w_ref, o_ref = refs
        b_ref = None
    x = x_ref[...]
    if pre_act == "silu":
        x = x * _sigmoid(x)
    xb = x.astype(BF16)
    n = o_ref.shape[-1]
    for n0 in range(0, n, n_chunk):
        y = jnp.dot(xb, w_ref[:, n0:n0 + n_chunk], preferred_element_type=F32)
        if b_ref is not None:
            y = y + b_ref[:, n0:n0 + n_chunk]
        if act == "sigmoid":
            y = _sigmoid(y)
        o_ref[:, n0:n0 + n_chunk] = y.astype(o_ref.dtype)


def _matmul(x, w, bias=None, act=None, pre_act=None, out_dtype=F32):
    m, k = x.shape
    n = w.shape[1]
    tm = m if m <= MAX_MM_ROWS else _largest_divisor(m, (MAX_MM_ROWS, 256, 128, 64, 32, 16, 8))
    n_chunk = _largest_divisor(n, (512, 256, 128))
    in_specs = [pl.BlockSpec((tm, k), lambda i: (i, 0)),
                pl.BlockSpec((k, n), lambda i: (0, 0))]
    args = [x, w]
    if bias is not None:
        in_specs.append(pl.BlockSpec((1, n), lambda i: (0, 0)))
        args.append(bias.reshape(1, n))
    return pl.pallas_call(
        functools.partial(_mm_body, act=act, pre_act=pre_act, n_chunk=n_chunk,
                          has_bias=bias is not None),
        grid=(m // tm,),
        in_specs=in_specs,
        out_specs=pl.BlockSpec((tm, n), lambda i: (i, 0)),
        out_shape=jax.ShapeDtypeStruct((m, n), out_dtype),
        compiler_params=_params(1),
        name="matmul",
    )(*args)


def _normmod_body(x_ref, g_ref, mod_ref, o_ref, *, shift_row, scale_row):
    x = x_ref[...]
    y = x * lax.rsqrt(jnp.mean(x * x, axis=-1, keepdims=True) + RMS_EPS)
    y = y * g_ref[...]
    o_ref[...] = y * (1.0 + mod_ref[scale_row:scale_row + 1, :]) + mod_ref[shift_row:shift_row + 1, :]


def _normmod(sq, xs, g, modtab, shift_row, scale_row):
    d = xs.shape[-1]
    return pl.pallas_call(
        functools.partial(_normmod_body, shift_row=shift_row, scale_row=scale_row),
        grid=sq.grid,
        in_specs=[sq.rows(d), sq.const((1, d)), sq.mod(d)],
        out_specs=sq.rows(d),
        out_shape=jax.ShapeDtypeStruct(xs.shape, F32),
        compiler_params=_params(2),
        name="normmod",
    )(xs, g.reshape(1, d), modtab)


def _final_norm_body(x_ref, g_ref, o_ref):
    x = x_ref[...]
    y = x * lax.rsqrt(jnp.mean(x * x, axis=-1, keepdims=True) + RMS_EPS)
    o_ref[...] = y * g_ref[...]


def _final_norm(sq, xs, g):
    d = xs.shape[-1]
    ncb = sq.ncb
    return pl.pallas_call(
        _final_norm_body,
        grid=(sq.B, sq.nblk - ncb),
        in_specs=[pl.BlockSpec((None, sq.TB, d), lambda b, j: (b, j + ncb, 0)),
                  pl.BlockSpec((1, d), lambda b, j: (0, 0))],
        out_specs=pl.BlockSpec((None, sq.TB, d), lambda b, j: (b, j, 0)),
        out_shape=jax.ShapeDtypeStruct((sq.B, sq.S - sq.ctx, d), F32),
        compiler_params=_params(2),
        name="final_norm",
    )(xs, g.reshape(1, d))


def _lora_body(h_ref, hp_ref, hn_ref, mu_ref, w1_ref, w2_ref, w0_ref, a1_ref, a2_ref, a0_ref,
               g1_ref, g2_ref, wab_ref, abb_ref, alog_ref,
               lw_ref, a_ref, g_ref, ab_ref, *, sq, rdim, n_beta):
    first, last = sq.edges()
    h = h_ref[...]
    nb = 0.5 * (_shift_rows(h, hp_ref, hn_ref, -1, first, last)
                + _shift_rows(h, hp_ref, hn_ref, 1, first, last))
    dlt = nb - h
    xw = h + dlt * mu_ref[0:1, :]
    xa = h + dlt * mu_ref[1:2, :]
    xg = h + dlt * mu_ref[2:3, :]
    wl = w0_ref[...] + _dot(jnp.tanh(_dot(xw, w1_ref[...])), w2_ref[...])
    w_log = -_softplus(-wl) - 0.5
    lw = -jnp.exp(w_log)
    lw_ref[0] = lw[:, :rdim]
    lw_ref[1] = lw[:, rdim:]
    a = _sigmoid(a0_ref[...] + _dot(_dot(xa, a1_ref[...]), a2_ref[...]))
    a_ref[0] = a[:, :rdim]
    a_ref[1] = a[:, rdim:]
    g_ref[...] = _dot(_sigmoid(_dot(xg, g1_ref[...])), g2_ref[...])
    z = _dot(h, wab_ref[...])
    col = lax.broadcasted_iota(jnp.int32, z.shape, 1)
    gl = -jnp.exp(alog_ref[...]) * _softplus(z + abb_ref[...])
    ab_ref[...] = jnp.where(col < n_beta, _sigmoid(z), gl)


def _lora(sq, h, lw):
    d = h.shape[-1]
    rdim = lw["w0"].shape[-1] // 2
    consts = [lw["mu"], lw["w1"], lw["w2"], lw["w0"], lw["a1"], lw["a2"], lw["a0"],
              lw["g1"], lw["g2"], lw["wab"], lw["abb"], lw["alog"]]
    bsd = (sq.B, sq.S)
    return pl.pallas_call(
        functools.partial(_lora_body, sq=sq, rdim=rdim, n_beta=lw["n_beta"]),
        grid=sq.grid,
        in_specs=[sq.rows(d), sq.prev_rows(d), sq.next_rows(d)] + [sq.const(c.shape) for c in consts],
        out_specs=[sq.rows2(rdim), sq.rows2(rdim), sq.rows(rdim), sq.rows(LANES)],
        out_shape=[jax.ShapeDtypeStruct((2,) + bsd + (rdim,), F32),
                   jax.ShapeDtypeStruct((2,) + bsd + (rdim,), F32),
                   jax.ShapeDtypeStruct(bsd + (rdim,), F32),
                   jax.ShapeDtypeStruct(bsd + (LANES,), F32)],
        compiler_params=_params(2),
        name="lora",
    )(h, h, h, *consts)


def _chunk_index(d, i, ncc, nchunk):
    back = jnp.where(i < ncc, ncc - 1 - i, nchunk + ncc - 1 - i)
    return jnp.where(d == 0, i, back)


def _causal_masks(d, c, reps=1):
    row = lax.broadcasted_iota(jnp.int32, (c, reps * c), 0)
    col = lax.broadcasted_iota(jnp.int32, (c, reps * c), 1) & (c - 1)
    delta = jnp.where(d == 0, row - col, col - row)
    eye = jnp.where(row == col, 1.0, 0.0).astype(F32)
    return delta >= 0, delta > 0, eye


def _stacked_mask(d, c, reps=1):
    row = lax.broadcasted_iota(jnp.int32, (2 * c, reps * c), 0)
    col = lax.broadcasted_iota(jnp.int32, (2 * c, reps * c), 1) & (c - 1)
    rr = row & (c - 1)
    delta = jnp.where(d == 0, rr - col, col - rr)
    return (delta > 0) | ((row >= c) & (delta == 0))


def _block_rows(x, w):
    left = lax.broadcasted_iota(jnp.int32, x.shape, 1) < w
    zero = jnp.zeros_like(x)
    return jnp.concatenate([jnp.where(left, x, zero), jnp.where(left, zero, x)], axis=0)


def _fold_rows(y):
    n = y.shape[0] // 2
    return y[:n] + y[n:]


def _tri_inverse_pairs(ls):
    n = ls[0].shape[0]
    row = lax.broadcasted_iota(jnp.int32, (n, 2 * n), 0)
    col = lax.broadcasted_iota(jnp.int32, (n, 2 * n), 1) & (n - 1)
    eye = jnp.where(row == col, 1.0, 0.0).astype(F32)
    same = (row >> 1) == (col >> 1)
    ts = [eye - jnp.where(same, l, 0.0) for l in ls]
    for k in range(2, n.bit_length()):
        off = ((row >> k) == (col >> k)) & ((row >> (k - 1)) != (col >> (k - 1)))
        tl = [_dot(t, _block_rows(jnp.where(off, l, 0.0), n), INV_PASSES) for t, l in zip(ts, ls)]
        ts = [t - _dot(x, _block_rows(t, n), INV_PASSES) for x, t in zip(tl, ts)]
    return ts


def _tri_inverse_many(ls, eye):
    n = ls[0].shape[0]
    row = lax.broadcasted_iota(jnp.int32, (n, n), 0)
    col = lax.broadcasted_iota(jnp.int32, (n, n), 1)
    same = (row >> 1) == (col >> 1)
    ts = [eye - jnp.where(same, l, 0.0) for l in ls]
    for k in range(2, n.bit_length()):
        off = ((row >> k) == (col >> k)) & ((row >> (k - 1)) != (col >> (k - 1)))
        tl = [_dot(t, jnp.where(off, l, 0.0), INV_PASSES) for t, l in zip(ts, ls)]
        ts = [t - _dot(x, t, INV_PASSES) for x, t in zip(tl, ts)]
    return ts


def _rwkv_intra_body(p_ref, pp_ref, pn_ref, a_ref, lw_ref, mu_ref, kk_ref, ka_ref, rk_ref, bd_ref,
                     f_ref, g_ref, bonus_ref, *, sq, heads, hdim, cps):
    c = CHUNK
    pw = 2 * hdim
    rdim = heads * hdim
    first, last = sq.edges(cps * c)
    proj = p_ref[...]
    nb = 0.5 * (_shift_rows(proj, pp_ref, pn_ref, -1, first, last)
                + _shift_rows(proj, pp_ref, pn_ref, 1, first, last))
    dlt = nb - proj
    r_all = proj[:, :rdim] + dlt[:, :rdim] * mu_ref[0:1, :]
    k_all = proj[:, rdim:2 * rdim] + dlt[:, rdim:2 * rdim] * mu_ref[1:2, :]
    v_all = proj[:, 2 * rdim:] + dlt[:, 2 * rdim:] * mu_ref[2:3, :]
    kn = k_all * kk_ref[...]
    kk_all = kn * lax.rsqrt(_segsum(kn * kn, bd_ref) + RMS_EPS)
    kd_all = [k_all * (1.0 + (a_ref[d] - 1.0) * ka_ref[...]) for d in range(2)]
    rk = r_all * rk_ref[...]
    bonus_ref[...] = (_segsum(rk * kd_all[0], bd_ref) + _segsum(rk * kd_all[1], bd_ref)) * v_all
    keys, a_p, r_p, v_p, bt_p, kt_p, ec_p, mask2 = [], {}, {}, {}, {}, {}, {}, {}
    for d in range(2):
        incl, _, _ = _causal_masks(d, c)
        m01 = jnp.where(incl, 1.0, 0.0).astype(BF16)
        mask2[d] = _stacked_mask(d, c, reps=2)
        for j in range(cps):
            rows = slice(j * c, (j + 1) * c)
            lw = lw_ref[d, rows, :]
            cum = _dot_exact_lhs(m01, lw)
            e_neg = jnp.exp(-cum)
            e_row = jnp.broadcast_to(jnp.exp(jnp.sum(lw, axis=0, keepdims=True)), lw.shape)
            a_t = -kk_all[rows] * jnp.exp(cum - lw)
            r_t = r_all[rows] * jnp.exp(cum)
            b_t = kk_all[rows] * a_ref[d, rows, :] * e_neg
            k_t = kd_all[d][rows] * e_neg
            v = v_all[rows]
            for p in range(heads // 2):
                key = (d, j, p)
                keys.append(key)
                lanes = slice(p * pw, (p + 1) * pw)
                a_p[key], r_p[key], v_p[key] = a_t[:, lanes], r_t[:, lanes], v[:, lanes]
                bt_p[key] = jnp.transpose(_block_rows(b_t[:, lanes], hdim))
                kt_p[key] = jnp.transpose(_block_rows(k_t[:, lanes], hdim))
                ec_p[key] = jnp.transpose(_block_rows(e_row[:, lanes], hdim))
    ar = {k: jnp.concatenate([a_p[k], r_p[k]], axis=0) for k in keys}
    mb = {k: jnp.where(mask2[k[0]], _dot(ar[k], bt_p[k], RWKV_PASSES), 0.0) for k in keys}
    mk = {k: jnp.where(mask2[k[0]], _dot(ar[k], kt_p[k], RWKV_PASSES), 0.0) for k in keys}
    v_bd = {k: _block_rows(v_p[k], hdim) for k in keys}
    x = {k: _dot(mk[k], v_bd[k], RWKV_PASSES) for k in keys}
    h0 = {k: _dot(_fold_rows(ec_p[k] * kt_p[k]), v_bd[k], RWKV_PASSES) for k in keys}
    t = dict(zip(keys, _tri_inverse_pairs([-mb[k][:c] for k in keys])))
    wt = {k: _dot(t[k], _block_rows(a_p[k], hdim), RWKV_PASSES) for k in keys}
    ut = {k: _dot(t[k], _block_rows(x[k][:c], hdim), RWKV_PASSES) for k in keys}
    for k in keys:
        d, j, p = k
        o = j * 4 * c
        lanes = slice(p * pw, (p + 1) * pw)
        f_ref[d, o:o + c, lanes] = ut[k]
        f_ref[d, o + c:o + 2 * c, lanes] = x[k][c:]
        f_ref[d, o + 2 * c:o + 3 * c, lanes] = h0[k]
        f_ref[d, o + 3 * c:o + 4 * c, lanes] = _fold_rows(ec_p[k])
        g_ref[d, o:o + c, lanes] = wt[k].astype(BF16)
        g_ref[d, o + c:o + 2 * c, lanes] = r_p[k].astype(BF16)
        g_ref[d, o + 2 * c:o + 3 * c, lanes] = mb[k][c:].astype(BF16)
        g_ref[d, o + 3 * c:o + 4 * c, lanes] = _fold_rows(ec_p[k] * bt_p[k]).astype(BF16)


def _rwkv_intra(sq, proj, a, lw_dec, lw, heads):
    rdim = a.shape[-1]
    hdim = rdim // heads
    assert hdim == CHUNK and heads % 2 == 0 and 2 * hdim == LANES, "pair tiles are [CHUNK, 128 lanes]"
    cps = _largest_divisor(sq.ncc, (RWKV_INTRA_CHUNKS, 1))
    rows = cps * CHUNK
    consts = [lw["mu_rkv"], lw["k_k"], lw["k_a"], lw["r_k"], lw["bd_r"]]
    const = lambda shape: pl.BlockSpec(shape, lambda b, i: (0,) * len(shape))
    per_dir = pl.BlockSpec((2, None, rows, rdim), lambda b, i: (0, b, i, 0))
    out = pl.BlockSpec((2, None, 4 * rows, rdim), lambda b, i: (0, b, i, 0))
    shape = (2, sq.B, sq.nchunk * 4 * CHUNK, rdim)
    return pl.pallas_call(
        functools.partial(_rwkv_intra_body, sq=sq, heads=heads, hdim=hdim, cps=cps),
        grid=(sq.B, sq.nchunk // cps),
        in_specs=[pl.BlockSpec((None, rows, 3 * rdim), lambda b, i: (b, i, 0)),
                  sq.prev_rows(3 * rdim, tb=rows), sq.next_rows(3 * rdim, tb=rows), per_dir, per_dir]
        + [const(c.shape) for c in consts],
        out_specs=[out, out, pl.BlockSpec((None, rows, rdim), lambda b, i: (b, i, 0))],
        out_shape=[jax.ShapeDtypeStruct(shape, F32), jax.ShapeDtypeStruct(shape, BF16),
                   jax.ShapeDtypeStruct((sq.B, sq.S, rdim), F32)],
        compiler_params=_params(2),
        name="rwkv_intra",
    )(proj, proj, proj, a, lw_dec, *consts)


def _rwkv_scan_body(f0_ref, g0_ref, f1_ref, g1_ref, of_ref, ob_ref, h_ref, *, heads, hdim, cps):
    @pl.when(pl.program_id(1) == 0)
    def _():
        h_ref[...] = jnp.zeros_like(h_ref)

    c = CHUNK
    pw = 2 * hdim
    fs, gs, outs = (f0_ref, f1_ref), (g0_ref, g1_ref), (of_ref, ob_ref)
    chains = [(d, p) for d in range(2) for p in range(heads // 2)]
    lanes = [slice(p * pw, (p + 1) * pw) for p in range(heads // 2)]
    dot = functools.partial(jnp.dot, preferred_element_type=F32)
    st = {ch: h_ref[ch[0], ch[1]] for ch in chains}
    for step in range(cps):
        sub = (step, cps - 1 - step)
        o = [4 * c * sub[d] for d in range(2)]
        m2 = {(d, p): dot(gs[d][o[d]:o[d] + 2 * c, lanes[p]], st[d, p].astype(BF16)) for d, p in chains}
        ub = {(d, p): _block_rows((fs[d][o[d]:o[d] + c, lanes[p]] + m2[d, p][:c]).astype(BF16), hdim)
              for d, p in chains}
        for d, p in chains:
            y = (fs[d][o[d] + c:o[d] + 2 * c, lanes[p]] + m2[d, p][c:]
                 + dot(gs[d][o[d] + 2 * c:o[d] + 3 * c, lanes[p]], ub[d, p]))
            outs[d][sub[d] * c:(sub[d] + 1) * c, lanes[p]] = y
            st[d, p] = (_block_rows(fs[d][o[d] + 3 * c:o[d] + 4 * c, lanes[p]], hdim) * st[d, p]
                        + _block_rows(fs[d][o[d] + 2 * c:o[d] + 3 * c, lanes[p]], hdim)
                        + dot(_block_rows(gs[d][o[d] + 3 * c:o[d] + 4 * c, lanes[p]], hdim), ub[d, p]))
    for d, p in chains:
        h_ref[d, p] = st[d, p]


def _backward_block(i, ncb, nblk):
    return jnp.where(i < ncb, ncb - 1 - i, nblk + ncb - 1 - i)


def _scan_chunks_per_step(sq):
    cps = _largest_divisor(sq.ncc, (SCAN_CHUNKS, 2, 1))
    assert sq.nchunk % cps == 0
    return cps


def _scan_specs(sq, heads, rows, width):
    cps = _scan_chunks_per_step(sq)
    ncb, nblk = sq.ncc // cps, sq.nchunk // cps
    fwd = pl.BlockSpec((None, None, heads, cps * rows, width), lambda b, i: (0, b, 0, i, 0))
    bwd = pl.BlockSpec((None, None, heads, cps * rows, width),
                       lambda b, i: (1, b, 0, _backward_block(i, ncb, nblk), 0))
    return fwd, bwd


def _scan_out_specs(sq, width):
    cps = _scan_chunks_per_step(sq)
    ncb, nblk = sq.ncc // cps, sq.nchunk // cps
    fwd = pl.BlockSpec((None, cps * CHUNK, width), lambda b, i: (b, i, 0))
    bwd = pl.BlockSpec((None, cps * CHUNK, width), lambda b, i: (b, _backward_block(i, ncb, nblk), 0))
    return [fwd, bwd]


def _rwkv_scan(sq, f, g, heads):
    rdim = f.shape[-1]
    hdim = rdim // heads
    cps = _scan_chunks_per_step(sq)
    ncb, nblk = sq.ncc // cps, sq.nchunk // cps
    rows = cps * 4 * CHUNK
    ff = pl.BlockSpec((None, None, rows, rdim), lambda b, i: (0, b, i, 0))
    fb = pl.BlockSpec((None, None, rows, rdim), lambda b, i: (1, b, _backward_block(i, ncb, nblk), 0))
    one = jax.ShapeDtypeStruct((sq.B, sq.S, rdim), F32)
    return pl.pallas_call(
        functools.partial(_rwkv_scan_body, heads=heads, hdim=hdim, cps=cps),
        grid=(sq.B, sq.nchunk // cps),
        in_specs=[ff, ff, fb, fb],
        out_specs=_scan_out_specs(sq, rdim),
        out_shape=[one, one],
        scratch_shapes=[pltpu.VMEM((2, heads // 2, 2 * hdim, 2 * hdim), F32)],
        compiler_params=_params(2),
        name="rwkv_scan",
    )(f, g, f, g)


def _rwkv_post_body(of_ref, ob_ref, bonus_ref, g_ref, lnw_ref, lnb_ref, bd_ref, y_ref, *, hdim):
    o = of_ref[...] + ob_ref[...]
    inv_n = 1.0 / hdim
    mean = _segsum(o, bd_ref) * inv_n
    cen = o - mean
    var = _segsum(cen * cen, bd_ref) * inv_n
    o = cen * lax.rsqrt(var + RWKV_LNX_EPS) * lnw_ref[...] + lnb_ref[...]
    y_ref[...] = (o + bonus_ref[...]) * g_ref[...]


def _rwkv_post(sq, o_f, o_b, bonus, g, lw, heads):
    rdim = g.shape[-1]
    consts = [lw["lnx_w"], lw["lnx_b"], lw["bd_r"]]
    return pl.pallas_call(
        functools.partial(_rwkv_post_body, hdim=rdim // heads),
        grid=sq.grid,
        in_specs=[sq.rows(rdim)] * 4 + [sq.const(c.shape) for c in consts],
        out_specs=sq.rows(rdim),
        out_shape=jax.ShapeDtypeStruct((sq.B, sq.S, rdim), F32),
        compiler_params=_params(2),
        name="rwkv_post",
    )(o_f, o_b, bonus, g, *consts)


def _gdn_intra_body(p_ref, pp_ref, pn_ref, ab_ref, cw_ref, bd_ref, f_ref, ga_ref, gb_ref, *, sq, heads, hdim, cps):
    c = CHUNK
    gdim = heads * hdim
    scale = hdim ** -0.5
    first, last = sq.edges(cps * c)
    proj = p_ref[...]
    width = cw_ref.shape[0]
    half = width // 2
    acc = proj * cw_ref[half:half + 1, :]
    for j in range(width):
        if j != half:
            acc = acc + _shift_rows(proj, pp_ref, pn_ref, j - half, first, last) * cw_ref[j:j + 1, :]
    y = acc * _sigmoid(acc)
    q_blk = y[:, :gdim]
    k_blk = y[:, gdim:2 * gdim]
    q_blk = q_blk * lax.rsqrt(_segsum(q_blk * q_blk, bd_ref) + RMS_EPS)
    k_blk = k_blk * lax.rsqrt(_segsum(k_blk * k_blk, bd_ref) + RMS_EPS)
    v_blk = y[:, 2 * gdim:]
    sl = [slice(h * hdim, (h + 1) * hdim) for h in range(heads)]
    keys = []
    beta, gc, g_last, gc_row, decay, kb, kq, kt_h, vb, strict_of, eye = {}, {}, {}, {}, {}, {}, {}, {}, {}, {}, None
    for j in range(cps):
        rows = slice(j * c, (j + 1) * c)
        ab = ab_ref[rows, :]
        tot_all = jnp.sum(ab, axis=0, keepdims=True)
        lane = lax.broadcasted_iota(jnp.int32, ab.shape, 1)
        k_all = k_blk[rows]
        k_tt = jnp.transpose(k_all)
        for d in range(2):
            incl, strict, eye = _causal_masks(d, c)
            strict_of[d] = strict
            gc_all = _dot_exact_lhs(jnp.where(incl, 1.0, 0.0).astype(BF16), ab)
            gc_all_t = jnp.transpose(gc_all)
            sub = lax.broadcasted_iota(jnp.int32, gc_all_t.shape, 0)
            for h in range(heads):
                key = (d, j, h)
                keys.append(key)
                pick_b = lane == d * heads + h
                pick_g = lane == 2 * heads + d * heads + h
                beta[key] = jnp.sum(jnp.where(pick_b, ab, 0.0), axis=1, keepdims=True)
                gc[key] = jnp.sum(jnp.where(pick_g, gc_all, 0.0), axis=1, keepdims=True)
                g_last[key] = jnp.sum(jnp.where(pick_g[:1], tot_all, 0.0), axis=1, keepdims=True)
                gc_row[key] = jnp.sum(jnp.where(sub == 2 * heads + d * heads + h, gc_all_t, 0.0),
                                      axis=0, keepdims=True)
                diff = gc[key] - gc_row[key]
                decay[key] = jnp.where(incl, jnp.exp(jnp.where(incl, diff, 0.0)), 0.0)
                kb[key] = k_all[:, sl[h]] * beta[key]
                kq[key] = jnp.concatenate([kb[key], q_blk[rows, sl[h]] * scale], axis=0)
                kt_h[key] = k_tt[sl[h]]
                vb[key] = v_blk[rows, sl[h]] * beta[key]
    m = {k: _dot(kq[k], kt_h[k], GDN_PASSES) for k in keys}
    lower = [jnp.where(strict_of[k[0]], m[k][:c] * decay[k], 0.0) for k in keys]
    t = dict(zip(keys, _tri_inverse_many(lower, eye)))
    e_gc = {k: jnp.exp(gc[k]) for k in keys}
    sol = {k: _dot(t[k], jnp.concatenate([vb[k], kb[k] * e_gc[k]], axis=1), GDN_PASSES) for k in keys}
    for k in keys:
        d, j, h = k
        of, oa, ob = j * (c + SUBLANES), j * 2 * c, j * (c + hdim)
        f_ref[d, h, of:of + c] = sol[k][:, :hdim]
        f_ref[d, h, of + c:of + c + SUBLANES] = jnp.broadcast_to(jnp.exp(g_last[k]), (SUBLANES, hdim))
        ga_ref[d, h, oa:oa + c] = sol[k][:, hdim:].astype(BF16)
        ga_ref[d, h, oa + c:oa + 2 * c] = (kq[k][c:] * e_gc[k]).astype(BF16)
        gb_ref[d, h, ob:ob + c] = (m[k][c:] * decay[k]).astype(BF16)
        gb_ref[d, h, ob + c:ob + c + hdim] = (kt_h[k] * jnp.exp(g_last[k] - gc_row[k])).astype(BF16)


def _gdn_intra(sq, proj, ab, lw, heads, gdim, col_block):
    hdim = gdim // heads
    cps = _largest_divisor(sq.ncc, (GDN_INTRA_CHUNKS, 2, 1))
    tb = cps * CHUNK
    w = 3 * gdim
    consts = [lw["conv"], lw["bd_g"]]
    const = lambda shape: pl.BlockSpec(shape, lambda b, i: (0,) * len(shape))
    small = pl.BlockSpec((None, tb, LANES), lambda b, i: (b, i, 0))
    rows = (CHUNK + SUBLANES, 2 * CHUNK, CHUNK + hdim)
    widths = (hdim, hdim, CHUNK)
    dtypes = (F32, BF16, BF16)
    outs = [pl.BlockSpec((2, None, heads, cps * r, wd), lambda b, i: (0, b, 0, i, 0)) for r, wd in zip(rows, widths)]
    shapes = [jax.ShapeDtypeStruct((2, sq.B, heads, sq.nchunk * r, wd), dt) for r, wd, dt in zip(rows, widths, dtypes)]
    return pl.pallas_call(
        functools.partial(_gdn_intra_body, sq=sq, heads=heads, hdim=hdim, cps=cps),
        grid=(sq.B, sq.nchunk // cps),
        in_specs=[pl.BlockSpec((None, tb, w), lambda b, i: (b, i, col_block)),
                  sq.prev_rows(w, col_block, tb=tb), sq.next_rows(w, col_block, tb=tb), small]
        + [const(c.shape) for c in consts],
        out_specs=outs,
        out_shape=shapes,
        compiler_params=_params(2),
        name="gdn_intra",
    )(proj, proj, proj, ab, *consts)


def _gdn_scan_body(f0_ref, ga0_ref, gb0_ref, f1_ref, ga1_ref, gb1_ref, of_ref, ob_ref, s_ref, *, heads, hdim, cps):
    @pl.when(pl.program_id(1) == 0)
    def _():
        s_ref[...] = jnp.zeros_like(s_ref)

    c = CHUNK
    fs, gas, gbs, outs = (f0_ref, f1_ref), (ga0_ref, ga1_ref), (gb0_ref, gb1_ref), (of_ref, ob_ref)
    chains = [(d, h) for d in range(2) for h in range(heads)]
    dot = functools.partial(jnp.dot, preferred_element_type=F32)
    st = {ch: s_ref[ch[0], ch[1]] for ch in chains}
    for step in range(cps):
        sub = (step, cps - 1 - step)
        of = [(c + SUBLANES) * sub[d] for d in range(2)]
        oa = [2 * c * sub[d] for d in range(2)]
        ob = [(c + hdim) * sub[d] for d in range(2)]
        m = {(d, h): dot(gas[d][h, oa[d]:oa[d] + 2 * c, :], st[d, h].astype(BF16)) for d, h in chains}
        vn = {(d, h): (fs[d][h, of[d]:of[d] + c, :] - m[d, h][:c]).astype(BF16) for d, h in chains}
        for d, h in chains:
            outs[d][sub[d] * c:(sub[d] + 1) * c, h * hdim:(h + 1) * hdim] = (
                m[d, h][c:] + dot(gbs[d][h, ob[d]:ob[d] + c, :], vn[d, h]))
            st[d, h] = (st[d, h] * fs[d][h, of[d] + c:of[d] + c + 1, :]
                        + dot(gbs[d][h, ob[d] + c:ob[d] + c + hdim, :], vn[d, h]))
    for d, h in chains:
        s_ref[d, h] = st[d, h]


def _gdn_scan(sq, f, ga, gb, heads):
    hdim = f.shape[-1]
    gdim = heads * hdim
    cps = _scan_chunks_per_step(sq)
    specs = []
    for arr in (f, ga, gb):
        specs.append(_scan_specs(sq, heads, arr.shape[3] // sq.nchunk, arr.shape[4]))
    one = jax.ShapeDtypeStruct((sq.B, sq.S, gdim), F32)
    return pl.pallas_call(
        functools.partial(_gdn_scan_body, heads=heads, hdim=hdim, cps=cps),
        grid=(sq.B, sq.nchunk // cps),
        in_specs=[s[0] for s in specs] + [s[1] for s in specs],
        out_specs=_scan_out_specs(sq, gdim),
        out_shape=[one, one],
        scratch_shapes=[pltpu.VMEM((2, heads, hdim, hdim), F32)],
        compiler_params=_params(2),
        name="gdn_scan",
    )(f, ga, gb, f, ga, gb)


def _gdn_post_body(of_ref, ob_ref, z_ref, gn_ref, bd_ref, y_ref, *, hdim):
    o = of_ref[...] + ob_ref[...]
    ms = _segsum(o * o, bd_ref) * (1.0 / hdim)
    z = z_ref[...]
    y_ref[...] = o * lax.rsqrt(ms + RMS_EPS) * gn_ref[...] * (z * _sigmoid(z))


def _gdn_post(sq, o_f, o_b, proj, lw, heads, z_col_block):
    gdim = o_f.shape[-1]
    consts = [lw["gdn_norm"], lw["bd_g"]]
    return pl.pallas_call(
        functools.partial(_gdn_post_body, hdim=gdim // heads),
        grid=sq.grid,
        in_specs=[sq.rows(gdim), sq.rows(gdim), sq.rows(gdim, z_col_block)] + [sq.const(c.shape) for c in consts],
        out_specs=sq.rows(gdim),
        out_shape=jax.ShapeDtypeStruct((sq.B, sq.S, gdim), F32),
        compiler_params=_params(2),
        name="gdn_post",
    )(o_f, o_b, proj, *consts)


def _rope(x, cos, sin_signed):
    n = x.shape[-1]
    lane = lax.broadcasted_iota(jnp.int32, x.shape, 1)
    partner = jnp.where((lane & 1) == 0, pltpu.roll(x, n - 1, axis=1), pltpu.roll(x, 1, axis=1))
    return x * cos + partner * sin_signed


def _attn_prep_body(q_ref, kv_ref, cs_ref, qn_ref, kn_ref, bd_ref, qo_ref, ko_ref, vo_ref,
                    *, q_heads, kv_heads, hdim):
    kvd = kv_heads * hdim
    cos = cs_ref[:, :kvd]
    sin = cs_ref[:, kvd:]
    reps = q_heads // kv_heads
    cos_q = jnp.concatenate([cos] * reps, axis=1)
    sin_q = jnp.concatenate([sin] * reps, axis=1)
    inv_n = 1.0 / hdim
    q = q_ref[...]
    q = q * lax.rsqrt(_segsum(q * q, bd_ref) * inv_n + RMS_EPS) * qn_ref[...]
    q = _rope(q, cos_q, sin_q) * (hdim ** -0.5)
    kv = kv_ref[...]
    k = kv[:, :kvd]
    kbd = bd_ref[:kvd, :kvd]
    hi, lo = _split2(k * k)
    ms = (jnp.dot(hi, kbd, preferred_element_type=F32) + jnp.dot(lo, kbd, preferred_element_type=F32)) * inv_n
    k = k * lax.rsqrt(ms + RMS_EPS) * kn_ref[...]
    k = _rope(k, cos, sin)
    v = kv[:, kvd:]
    for h in range(q_heads):
        qo_ref[h] = q[:, h * hdim:(h + 1) * hdim].astype(BF16)
    k_t = jnp.transpose(k)
    for h in range(kv_heads):
        ko_ref[h] = k_t[h * hdim:(h + 1) * hdim].astype(BF16)
        vh = v[:, h * hdim:(h + 1) * hdim]
        vo_ref[h] = jnp.concatenate([vh, jnp.ones_like(vh)], axis=1).astype(BF16)


def _attn_prep(sq, proj, cs_tab, lw, q_heads, kv_heads, hdim, q_col_block, kv_col_block):
    qd, kvd = q_heads * hdim, kv_heads * hdim
    consts = [lw["q_norm"], lw["k_norm"], lw["bd_r"]]
    k_spec = pl.BlockSpec((None, kv_heads, hdim, sq.TB), lambda b, j: (b, 0, 0, j))
    return pl.pallas_call(
        functools.partial(_attn_prep_body, q_heads=q_heads, kv_heads=kv_heads, hdim=hdim),
        grid=sq.grid,
        in_specs=[sq.rows(qd, q_col_block), sq.rows(2 * kvd, kv_col_block),
                  pl.BlockSpec((sq.TB, 2 * kvd), lambda b, j: (j, 0))]
        + [sq.const(c.shape) for c in consts],
        out_specs=[sq.heads(q_heads, hdim), k_spec, sq.heads(kv_heads, 2 * hdim)],
        out_shape=[jax.ShapeDtypeStruct((sq.B, q_heads, sq.S, hdim), BF16),
                   jax.ShapeDtypeStruct((sq.B, kv_heads, hdim, sq.S), BF16),
                   jax.ShapeDtypeStruct((sq.B, kv_heads, sq.S, 2 * hdim), BF16)],
        compiler_params=_params(2),
        name="attn_prep",
    )(proj, proj, cs_tab, *consts)


def _attn_body(q_ref, kt_ref, v_ref, o_ref, s_ref, p_ref, *, group, n_ctx_qblocks, n_ctx_keys):
    tq, hdim = q_ref.shape[1], q_ref.shape[2]
    n_keys = kt_ref.shape[1]

    def attend(nk):
        tiles = range(0, nk, KEY_BLOCK)
        for g in range(group):
            q = q_ref[g]
            rows = slice(g * tq, (g + 1) * tq)
            mx = None
            for t in tiles:
                s = jnp.dot(q, kt_ref[:, t:t + KEY_BLOCK], preferred_element_type=F32)
                s_ref[:, t:t + KEY_BLOCK] = s
                for c0 in range(0, KEY_BLOCK, LANES):
                    part = s[:, c0:c0 + LANES]
                    mx = part if mx is None else jnp.maximum(mx, part)
            m = jnp.broadcast_to(jnp.max(mx, axis=1, keepdims=True), (tq, LANES))
            for t in tiles:
                for c0 in range(t, t + KEY_BLOCK, LANES):
                    p_ref[rows, c0:c0 + LANES] = jnp.exp(s_ref[:, c0:c0 + LANES] - m).astype(BF16)
        acc = jnp.dot(p_ref[:, 0:nk], v_ref[0:nk, :], preferred_element_type=F32)
        out = acc[:, :hdim] / acc[:, hdim:]
        for g in range(group):
            o_ref[:, g * hdim:(g + 1) * hdim] = out[g * tq:(g + 1) * tq, :]

    is_ctx = pl.program_id(2) < n_ctx_qblocks

    @pl.when(is_ctx)
    def _():
        attend(n_ctx_keys)

    @pl.when(jnp.logical_not(is_ctx))
    def _():
        attend(n_keys)


def _attention(sq, q, kt, v):
    b, q_heads, s, hdim = q.shape
    kv_heads = kt.shape[1]
    group = q_heads // kv_heads
    tq = _largest_divisor(sq.ctx, (128, 64))
    assert sq.ctx % KEY_BLOCK == 0 and s % KEY_BLOCK == 0
    return pl.pallas_call(
        functools.partial(_attn_body, group=group, n_ctx_qblocks=sq.ctx // tq, n_ctx_keys=sq.ctx),
        grid=(b, kv_heads, s // tq),
        in_specs=[pl.BlockSpec((None, group, tq, hdim), lambda bi, g, i: (bi, g, i, 0)),
                  pl.BlockSpec((None, None, hdim, s), lambda bi, g, i: (bi, g, 0, 0)),
                  pl.BlockSpec((None, None, s, 2 * hdim), lambda bi, g, i: (bi, g, 0, 0))],
        out_specs=pl.BlockSpec((None, tq, group * hdim), lambda bi, g, i: (bi, i, g)),
        out_shape=jax.ShapeDtypeStruct((b, s, q_heads * hdim), F32),
        scratch_shapes=[pltpu.VMEM((tq, s), F32), pltpu.VMEM((group * tq, s), BF16)],
        compiler_params=_params(3),
        name="attention",
    )(q, kt, v)


def _merge_body(x_ref, ya_ref, yb_ref, yc_ref, gate_ref, mod_ref, wa_ref, wb_ref, wc_ref, wo_ref, o_ref,
                *, gate_row):
    d = x_ref.shape[-1]
    m = gate_ref[:, :d] * _dot(ya_ref[...], wa_ref[...])
    m = m + gate_ref[:, d:2 * d] * _dot(yb_ref[...], wb_ref[...])
    m = m + gate_ref[:, 2 * d:] * _dot(yc_ref[...], wc_ref[...])
    y = _dot(m, wo_ref[...])
    o_ref[...] = x_ref[...] + mod_ref[gate_row:gate_row + 1, :] * y


def _merge(sq, xs, ya, yb, yc, gates, modtab, lw):
    d = xs.shape[-1]
    consts = [lw["w_up_a"], lw["w_up_b"], lw["w_up_c"], lw["w_out"]]
    return pl.pallas_call(
        functools.partial(_merge_body, gate_row=2),
        grid=sq.grid,
        in_specs=[sq.rows(d), sq.rows(ya.shape[-1]), sq.rows(yb.shape[-1]), sq.rows(yc.shape[-1]),
                  sq.rows(3 * d), sq.mod(d)] + [sq.const(c.shape) for c in consts],
        out_specs=sq.rows(d),
        out_shape=jax.ShapeDtypeStruct(xs.shape, F32),
        compiler_params=_params(2),
        name="merge",
    )(xs, ya, yb, yc, gates, modtab, *consts)


def _ffn_body(x_ref, g_ref, mod_ref, w1_ref, w3_ref, w2_ref, o_ref, *, h_chunk, ctx_len):
    rows = x_ref.shape[0]
    row = pl.program_id(1) * rows + lax.broadcasted_iota(jnp.int32, (rows, 1), 0)
    is_ctx = row < ctx_len
    pick = lambda i: jnp.where(is_ctx, mod_ref[0, i:i + 1, :], mod_ref[1, i:i + 1, :])
    x = x_ref[...]
    y = x * lax.rsqrt(jnp.mean(x * x, axis=-1, keepdims=True) + RMS_EPS) * g_ref[...]
    h = (y * (1.0 + pick(4)) + pick(3)).astype(BF16)
    hidden = w1_ref.shape[1]
    acc = jnp.zeros(x.shape, F32)
    for c0 in range(0, hidden, h_chunk):
        a = jnp.dot(h, w1_ref[:, c0:c0 + h_chunk], preferred_element_type=F32)
        b = jnp.dot(h, w3_ref[:, c0:c0 + h_chunk], preferred_element_type=F32)
        t = (a * _sigmoid(a) * b).astype(BF16)
        acc = acc + jnp.dot(t, w2_ref[c0:c0 + h_chunk, :], preferred_element_type=F32)
    o_ref[...] = x + pick(5) * acc


def _ffn(sq, xs, g, modtab, lw):
    d = xs.shape[-1]
    consts = [lw["ffn_w1"], lw["ffn_w3"], lw["ffn_w2"]]
    h_chunk = _largest_divisor(lw["ffn_w1"].shape[1], (512, 256, 128))
    tb = _largest_divisor(sq.S, FFN_ROWS)
    rows = pl.BlockSpec((None, tb, d), lambda b, j: (b, j, 0))
    const = lambda shape: pl.BlockSpec(shape, lambda b, j: (0,) * len(shape))
    return pl.pallas_call(
        functools.partial(_ffn_body, h_chunk=h_chunk, ctx_len=sq.ctx),
        grid=(sq.B, sq.S // tb),
        in_specs=[rows, const((1, d)), pl.BlockSpec((None, 2, 6, d), lambda b, j: (b, 0, 0, 0))]
        + [const(c.shape) for c in consts],
        out_specs=rows,
        out_shape=jax.ShapeDtypeStruct(xs.shape, F32),
        compiler_params=_params(2),
        name="ffn",
    )(xs, g.reshape(1, d), modtab, *consts)


def _block_diag_ones(n, seg):
    idx = jnp.arange(n) // seg
    return (idx[:, None] == idx[None, :]).astype(BF16)


def _block_diag2(m):
    z = jnp.zeros_like(m[0])
    return jnp.concatenate([jnp.concatenate([m[0], z], axis=1), jnp.concatenate([z, m[1]], axis=1)], axis=0)


def _pad_to(x, axis, size):
    pad = [(0, 0)] * x.ndim
    pad[axis] = (0, size - x.shape[axis])
    return jnp.pad(x, pad)


def _rope_table(ctx_len, seq_len, hdim, kv_heads):
    rows = seq_len // GRID_W
    row = jnp.repeat(jnp.arange(rows), GRID_W).astype(F32)
    col = jnp.tile(jnp.arange(GRID_W), rows).astype(F32)
    half = hdim // 2
    inv = ROPE_THETA ** (-jnp.arange(0, half, 2, dtype=F32) / half)
    ang = jnp.concatenate([row[:, None] * inv, col[:, None] * inv], axis=-1)
    cos = jnp.repeat(jnp.cos(ang), 2, axis=1)
    sin = jnp.repeat(jnp.sin(ang), 2, axis=1) * jnp.tile(jnp.array([-1.0, 1.0], F32), half)
    cos = jnp.concatenate([jnp.ones((ctx_len, hdim), F32), cos], axis=0)
    sin = jnp.concatenate([jnp.zeros((ctx_len, hdim), F32), sin], axis=0)
    return jnp.concatenate([jnp.tile(cos, (1, kv_heads)), jnp.tile(sin, (1, kv_heads))], axis=1)


def kernel(x, c, ctx, c_ctx, ada_w, ada_b, norm1, norm2, w_in, rwkv_mu_x, rwkv_mu_rkv, rwkv_w0, rwkv_w1, rwkv_w2, rwkv_a0, rwkv_a1, rwkv_a2, rwkv_g1, rwkv_g2, rwkv_k_k, rwkv_k_a, rwkv_r_k, rwkv_lnx_w, rwkv_lnx_b, gdn_conv, gdn_w_alpha, gdn_dt_bias, gdn_A_log, gdn_w_beta, gdn_norm, attn_q_norm, attn_k_norm, w_up_a, w_up_b, w_up_c, w_gate, b_gate, w_out, ffn_w1, ffn_w3, ffn_w2, final_norm):
    batch, seq_len, d = x.shape
    ctx_len = ctx.shape[1]
    depth = ada_w.shape[0]
    sq = _Seq(batch, ctx_len, seq_len)

    r_heads, r_hdim = rwkv_r_k.shape[1], rwkv_r_k.shape[2]
    rdim = r_heads * r_hdim
    g_heads, g_hdim = gdn_w_alpha.shape[-1], gdn_norm.shape[-1]
    gdim = g_heads * g_hdim
    a_hdim = attn_q_norm.shape[-1]
    qd = w_up_c.shape[1]
    q_heads = qd // a_hdim
    kvd = (w_in.shape[-1] - 3 * rdim - 4 * gdim - qd) // 2
    kv_heads = kvd // a_hdim
    assert rdim == gdim == qd and r_hdim == a_hdim, "lane-segment constants are shared between mixers"
    assert (3 * rdim) % (3 * gdim) == 0 and (3 * rdim + 3 * gdim) % gdim == 0
    gdn_col = (3 * rdim) // (3 * gdim)
    z_col = (3 * rdim + 3 * gdim) // gdim
    q_col = (3 * rdim + 4 * gdim) // qd
    assert (3 * rdim + 4 * gdim + qd) % (2 * kvd) == 0
    kv_col = (3 * rdim + 4 * gdim + qd) // (2 * kvd)

    bd_r = _block_diag_ones(rdim, r_hdim)
    bd_g = _block_diag_ones(gdim, g_hdim)
    cs_tab = _rope_table(ctx_len, seq_len, a_hdim, kv_heads)
    n_beta = 2 * g_heads
    assert 2 * n_beta <= LANES

    cond = jnp.concatenate([c, c_ctx[None, :]], axis=0)
    cond = _pad_to(cond, 0, -(-(batch + 1) // SUBLANES) * SUBLANES)

    xs = jnp.concatenate([ctx, x], axis=1)
    m_rows = batch * sq.S
    for l in range(depth):
        g1w = _pad_to(rwkv_g1[l], 1, 2 * LANES)
        lw = {
            "mu": rwkv_mu_x[l], "mu_rkv": rwkv_mu_rkv[l],
            "w1": jnp.concatenate([rwkv_w1[l, 0], rwkv_w1[l, 1]], axis=1).astype(BF16),
            "w2": _block_diag2(rwkv_w2[l]).astype(BF16),
            "w0": rwkv_w0[l].reshape(1, 2 * rdim),
            "a1": jnp.concatenate([rwkv_a1[l, 0], rwkv_a1[l, 1]], axis=1).astype(BF16),
            "a2": _block_diag2(rwkv_a2[l]).astype(BF16),
            "a0": rwkv_a0[l].reshape(1, 2 * rdim),
            "g1": g1w.astype(BF16),
            "g2": _pad_to(rwkv_g2[l], 0, 2 * LANES).astype(BF16),
            "wab": _pad_to(jnp.concatenate([gdn_w_beta[l, 0], gdn_w_beta[l, 1],
                                            gdn_w_alpha[l, 0], gdn_w_alpha[l, 1]], axis=1), 1, LANES).astype(BF16),
            "abb": _pad_to(jnp.concatenate([jnp.zeros((n_beta,), F32), gdn_dt_bias[l].reshape(-1)]), 0, LANES).reshape(1, LANES),
            "alog": _pad_to(jnp.concatenate([jnp.zeros((n_beta,), F32), gdn_A_log[l].reshape(-1)]), 0, LANES).reshape(1, LANES),
            "n_beta": n_beta,
            "k_k": rwkv_k_k[l].reshape(1, rdim), "k_a": rwkv_k_a[l].reshape(1, rdim),
            "r_k": rwkv_r_k[l].reshape(1, rdim),
            "lnx_w": rwkv_lnx_w[l].reshape(1, rdim), "lnx_b": rwkv_lnx_b[l].reshape(1, rdim),
            "bd_r": bd_r, "bd_g": bd_g,
            "conv": jnp.transpose(gdn_conv[l]),
            "gdn_norm": jnp.tile(gdn_norm[l], g_heads).reshape(1, gdim),
            "q_norm": jnp.tile(attn_q_norm[l], q_heads).reshape(1, qd),
            "k_norm": jnp.tile(attn_k_norm[l], kv_heads).reshape(1, kvd),
            "w_up_a": w_up_a[l].astype(BF16), "w_up_b": w_up_b[l].astype(BF16),
            "w_up_c": w_up_c[l].astype(BF16), "w_out": w_out[l].astype(BF16),
            "ffn_w1": ffn_w1[l].astype(BF16), "ffn_w3": ffn_w3[l].astype(BF16), "ffn_w2": ffn_w2[l].astype(BF16),
        }
        mod = _matmul(cond, ada_w[l].astype(BF16), bias=ada_b[l], pre_act="silu")
        mod_x = mod[:batch].reshape(batch, 6, d)
        mod_c = jnp.broadcast_to(mod[batch].reshape(1, 6, d), (batch, 6, d))
        modtab = jnp.stack([mod_c, mod_x], axis=1)

        h = _normmod(sq, xs, norm1[l], modtab, shift_row=0, scale_row=1)
        h2 = h.reshape(m_rows, d)
        proj = _matmul(h2, w_in[l].astype(BF16)).reshape(batch, sq.S, -1)
        gates = _matmul(h2, w_gate[l].astype(BF16), bias=b_gate[l], act="sigmoid",
                        out_dtype=BF16).reshape(batch, sq.S, -1)
        lw_dec, a_iclr, g_out, ab = _lora(sq, h, lw)

        rf, rg, bonus = _rwkv_intra(sq, proj, a_iclr, lw_dec, lw, r_heads)
        o_rf, o_rb = _rwkv_scan(sq, rf, rg, r_heads)
        ya = _rwkv_post(sq, o_rf, o_rb, bonus, g_out, lw, r_heads)

        gf, gga, ggb = _gdn_intra(sq, proj, ab, lw, g_heads, gdim, gdn_col)
        o_gf, o_gb = _gdn_scan(sq, gf, gga, ggb, g_heads)
        yb = _gdn_post(sq, o_gf, o_gb, proj, lw, g_heads, z_col)

        aq, ak, av = _attn_prep(sq, proj, cs_tab, lw, q_heads, kv_heads, a_hdim, q_col, kv_col)
        yc = _attention(sq, aq, ak, av)

        xs = _merge(sq, xs, ya, yb, yc, gates, modtab, lw)
        xs = _ffn(sq, xs, norm2[l], modtab, lw)
    return _final_norm(sq, xs, final_norm)
```

```python
import functools

import jax
import jax.numpy as jnp
from jax import lax
from jax.experimental import pallas as pl
from jax.experimental.pallas import tpu as pltpu

F32 = jnp.float32
BF16 = jnp.bfloat16

RMS_EPS = 1e-6
RWKV_LNX_EPS = 64e-5
ROPE_THETA = 10000.0
GRID_W = 64

SUBLANES = 8
LANES = 128
CHUNK = 64
MAX_TOKEN_BLOCK = 256
MAX_MM_ROWS = 512
FFN_ROWS = (544, 512, 256, 128)
KEY_BLOCK = 256
RWKV_PASSES = 1
GDN_PASSES = 1
INV_PASSES = 1
RWKV_INTRA_CHUNKS = 2
GDN_INTRA_CHUNKS = 4
SCAN_CHUNKS = 4
VMEM_LIMIT = 56 * 1024 * 1024


def _sigmoid(x):
    return 1.0 / (1.0 + jnp.exp(-x))


def _softplus(x):
    return jnp.maximum(x, 0.0) + jnp.log(1.0 + jnp.exp(-jnp.abs(x)))


def _split2(x):
    hi = x.astype(BF16)
    lo = (x - hi.astype(F32)).astype(BF16)
    return hi, lo


def _mm(a, b, dims, passes):
    d = functools.partial(lax.dot_general, dimension_numbers=(dims, ((), ())), preferred_element_type=F32)
    if passes == 1:
        return d(a.astype(BF16), b.astype(BF16))
    ah, al = _split2(a)
    bh, bl = _split2(b)
    return d(ah, bh) + (d(ah, bl) + d(al, bh))


def _dot(a, b, passes=1):
    return _mm(a, b, ((1,), (0,)), passes)


def _split3(x):
    hi = x.astype(BF16)
    r1 = x - hi.astype(F32)
    mid = r1.astype(BF16)
    lo = (r1 - mid.astype(F32)).astype(BF16)
    return hi, mid, lo


def _dot_exact_lhs(m01, x):
    d = functools.partial(jnp.dot, preferred_element_type=F32)
    hi, mid, lo = _split3(x)
    return d(m01, hi) + (d(m01, mid) + d(m01, lo))


def _segsum(x, bd_ref):
    d = functools.partial(jnp.dot, preferred_element_type=F32)
    hi, lo = _split2(x)
    bd = bd_ref[...]
    return d(hi, bd) + d(lo, bd)


def _row_iota(shape):
    return lax.broadcasted_iota(jnp.int32, shape, 0)


def _shift_rows(x, prev_ref, next_ref, k, first, last):
    n = x.shape[0]
    y = pltpu.roll(x, (-k) % n, axis=0)
    row = _row_iota((SUBLANES, x.shape[1]))
    if k < 0:
        tile = y[:SUBLANES]
        for i in range(-k):
            edge = prev_ref[SUBLANES + k + i:SUBLANES + k + i + 1, :]
            edge = jnp.where(first, 0.0, edge)
            tile = jnp.where(row == i, edge, tile)
        return jnp.concatenate([tile, y[SUBLANES:]], axis=0)
    tile = y[n - SUBLANES:]
    for i in range(k):
        edge = next_ref[i:i + 1, :]
        edge = jnp.where(last, 0.0, edge)
        tile = jnp.where(row == SUBLANES - k + i, edge, tile)
    return jnp.concatenate([y[:n - SUBLANES], tile], axis=0)


def _largest_divisor(n, candidates):
    for c in candidates:
        if n % c == 0:
            return c
    raise ValueError(f"no block size among {candidates} divides {n}")


class _Seq:
    def __init__(self, batch, ctx_len, seq_len):
        self.B = batch
        self.ctx = ctx_len
        self.S = ctx_len + seq_len
        self.TB = _largest_divisor(ctx_len, (MAX_TOKEN_BLOCK, 128, 64))
        assert seq_len % self.TB == 0 and self.TB % CHUNK == 0
        self.ncb = ctx_len // self.TB
        self.nblk = self.S // self.TB
        self.ncc = ctx_len // CHUNK
        self.nchunk = self.S // CHUNK
        self.grid = (batch, self.nblk)

    def rows(self, width, col_block=0):
        return pl.BlockSpec((None, self.TB, width), lambda b, j: (b, j, col_block))

    def rows2(self, width):
        return pl.BlockSpec((2, None, self.TB, width), lambda b, j: (0, b, j, 0))

    def heads(self, n_heads, width):
        return pl.BlockSpec((None, n_heads, self.TB, width), lambda b, j: (b, 0, j, 0))

    def prev_rows(self, width, col_block=0, tb=None):
        per = (tb or self.TB) // SUBLANES
        return pl.BlockSpec((None, SUBLANES, width),
                            lambda b, j: (b, jnp.maximum(j * per - 1, 0), col_block))

    def next_rows(self, width, col_block=0, tb=None):
        per = (tb or self.TB) // SUBLANES
        top = self.S // SUBLANES - 1
        return pl.BlockSpec((None, SUBLANES, width),
                            lambda b, j: (b, jnp.minimum((j + 1) * per, top), col_block))

    def const(self, shape):
        zeros = (0,) * len(shape)
        return pl.BlockSpec(shape, lambda b, j: zeros)

    def mod(self, d_model):
        ncb = self.ncb
        return pl.BlockSpec((None, None, 6, d_model),
                            lambda b, j: (b, (j >= ncb).astype(jnp.int32), 0, 0))

    def edges(self, tb=None):
        tb = tb or self.TB
        j = pl.program_id(1)
        ncb, nblk = self.ctx // tb, self.S // tb
        first = (j == 0) | (j == ncb)
        last = (j == ncb - 1) | (j == nblk - 1)
        return first, last


def _params(n_axes):
    return pltpu.CompilerParams(dimension_semantics=("arbitrary",) * n_axes,
                                vmem_limit_bytes=VMEM_LIMIT)


def _mm_body(*refs, act, pre_act, n_chunk, has_bias):
    if has_bias:
        x_ref, w_ref, b_ref, o_ref = refs
    else:
        x_ref, w_ref, o_ref = refs
        b_ref = None
    x = x_ref[...]
    if pre_act == "silu":
        x = x * _sigmoid(x)
    xb = x.astype(BF16)
    n = o_ref.shape[-1]
    for n0 in range(0, n, n_chunk):
        y = jnp.dot(xb, w_ref[:, n0:n0 + n_chunk], preferred_element_type=F32)
        if b_ref is not None:
            y = y + b_ref[:, n0:n0 + n_chunk]
        if act == "sigmoid":
            y = _sigmoid(y)
        o_ref[:, n0:n0 + n_chunk] = y.astype(o_ref.dtype)


def _matmul(x, w, bias=None, act=None, pre_act=None, out_dtype=F32):
    m, k = x.shape
    n = w.shape[1]
    tm = m if m <= MAX_MM_ROWS else _largest_divisor(m, (MAX_MM_ROWS, 256, 128, 64, 32, 16, 8))
    n_chunk = _largest_divisor(n, (512, 256, 128))
    in_specs = [pl.BlockSpec((tm, k), lambda i: (i, 0)),
                pl.BlockSpec((k, n), lambda i: (0, 0))]
    args = [x, w]
    if bias is not None:
        in_specs.append(pl.BlockSpec((1, n), lambda i: (0, 0)))
        args.append(bias.reshape(1, n))
    return pl.pallas_call(
        functools.partial(_mm_body, act=act, pre_act=pre_act, n_chunk=n_chunk,
                          has_bias=bias is not None),
        grid=(m // tm,),
        in_specs=in_specs,
        out_specs=pl.BlockSpec((tm, n), lambda i: (i, 0)),
        out_shape=jax.ShapeDtypeStruct((m, n), out_dtype),
        compiler_params=_params(1),
        name="matmul",
    )(*args)


def _normmod_body(x_ref, g_ref, mod_ref, o_ref, *, shift_row, scale_row):
    x = x_ref[...]
    y = x * lax.rsqrt(jnp.mean(x * x, axis=-1, keepdims=True) + RMS_EPS)
    y = y * g_ref[...]
    o_ref[...] = y * (1.0 + mod_ref[scale_row:scale_row + 1, :]) + mod_ref[shift_row:shift_row + 1, :]


def _normmod(sq, xs, g, modtab, shift_row, scale_row):
    d = xs.shape[-1]
    return pl.pallas_call(
        functools.partial(_normmod_body, shift_row=shift_row, scale_row=scale_row),
        grid=sq.grid,
        in_specs=[sq.rows(d), sq.const((1, d)), sq.mod(d)],
        out_specs=sq.rows(d),
        out_shape=jax.ShapeDtypeStruct(xs.shape, F32),
        compiler_params=_params(2),
        name="normmod",
    )(xs, g.reshape(1, d), modtab)


def _final_norm_body(x_ref, g_ref, o_ref):
    x = x_ref[...]
    y = x * lax.rsqrt(jnp.mean(x * x, axis=-1, keepdims=True) + RMS_EPS)
    o_ref[...] = y * g_ref[...]


def _final_norm(sq, xs, g):
    d = xs.shape[-1]
    ncb = sq.ncb
    return pl.pallas_call(
        _final_norm_body,
        grid=(sq.B, sq.nblk - ncb),
        in_specs=[pl.BlockSpec((None, sq.TB, d), lambda b, j: (b, j + ncb, 0)),
                  pl.BlockSpec((1, d), lambda b, j: (0, 0))],
        out_specs=pl.BlockSpec((None, sq.TB, d), lambda b, j: (b, j, 0)),
        out_shape=jax.ShapeDtypeStruct((sq.B, sq.S - sq.ctx, d), F32),
        compiler_params=_params(2),
        name="final_norm",
    )(xs, g.reshape(1, d))


def _lora_body(h_ref, hp_ref, hn_ref, mu_ref, w1_ref, w2_ref, w0_ref, a1_ref, a2_ref, a0_ref,
               g1_ref, g2_ref, wab_ref, abb_ref, alog_ref,
               lw_ref, a_ref, g_ref, ab_ref, *, sq, rdim, n_beta):
    first, last = sq.edges()
    h = h_ref[...]
    nb = 0.5 * (_shift_rows(h, hp_ref, hn_ref, -1, first, last)
                + _shift_rows(h, hp_ref, hn_ref, 1, first, last))
    dlt = nb - h
    xw = h + dlt * mu_ref[0:1, :]
    xa = h + dlt * mu_ref[1:2, :]
    xg = h + dlt * mu_ref[2:3, :]
    wl = w0_ref[...] + _dot(jnp.tanh(_dot(xw, w1_ref[...])), w2_ref[...])
    w_log = -_softplus(-wl) - 0.5
    lw = -jnp.exp(w_log)
    lw_ref[0] = lw[:, :rdim]
    lw_ref[1] = lw[:, rdim:]
    a = _sigmoid(a0_ref[...] + _dot(_dot(xa, a1_ref[...]), a2_ref[...]))
    a_ref[0] = a[:, :rdim]
    a_ref[1] = a[:, rdim:]
    g_ref[...] = _dot(_sigmoid(_dot(xg, g1_ref[...])), g2_ref[...])
    z = _dot(h, wab_ref[...])
    col = lax.broadcasted_iota(jnp.int32, z.shape, 1)
    gl = -jnp.exp(alog_ref[...]) * _softplus(z + abb_ref[...])
    ab_ref[...] = jnp.where(col < n_beta, _sigmoid(z), gl)


def _lora(sq, h, lw):
    d = h.shape[-1]
    rdim = lw["w0"].shape[-1] // 2
    consts = [lw["mu"], lw["w1"], lw["w2"], lw["w0"], lw["a1"], lw["a2"], lw["a0"],
              lw["g1"], lw["g2"], lw["wab"], lw["abb"], lw["alog"]]
    bsd = (sq.B, sq.S)
    return pl.pallas_call(
        functools.partial(_lora_body, sq=sq, rdim=rdim, n_beta=lw["n_beta"]),
        grid=sq.grid,
        in_specs=[sq.rows(d), sq.prev_rows(d), sq.next_rows(d)] + [sq.const(c.shape) for c in consts],
        out_specs=[sq.rows2(rdim), sq.rows2(rdim), sq.rows(rdim), sq.rows(LANES)],
        out_shape=[jax.ShapeDtypeStruct((2,) + bsd + (rdim,), F32),
                   jax.ShapeDtypeStruct((2,) + bsd + (rdim,), F32),
                   jax.ShapeDtypeStruct(bsd + (rdim,), F32),
                   jax.ShapeDtypeStruct(bsd + (LANES,), F32)],
        compiler_params=_params(2),
        name="lora",
    )(h, h, h, *consts)


def _causal_masks(d, c, reps=1):
    row = lax.broadcasted_iota(jnp.int32, (c, reps * c), 0)
    col = lax.broadcasted_iota(jnp.int32, (c, reps * c), 1) & (c - 1)
    delta = jnp.where(d == 0, row - col, col - row)
    eye = jnp.where(row == col, 1.0, 0.0).astype(F32)
    return delta >= 0, delta > 0, eye


def _stacked_mask(d, c, reps=1):
    row = lax.broadcasted_iota(jnp.int32, (2 * c, reps * c), 0)
    col = lax.broadcasted_iota(jnp.int32, (2 * c, reps * c), 1) & (c - 1)
    rr = row & (c - 1)
    delta = jnp.where(d == 0, rr - col, col - rr)
    return (delta > 0) | ((row >= c) & (delta == 0))


def _block_rows(x, w):
    left = lax.broadcasted_iota(jnp.int32, x.shape, 1) < w
    zero = jnp.zeros_like(x)
    return jnp.concatenate([jnp.where(left, x, zero), jnp.where(left, zero, x)], axis=0)


def _fold_rows(y):
    n = y.shape[0] // 2
    return y[:n] + y[n:]


def _tri_inverse_pairs(ls):
    n = ls[0].shape[0]
    row = lax.broadcasted_iota(jnp.int32, (n, 2 * n), 0)
    col = lax.broadcasted_iota(jnp.int32, (n, 2 * n), 1) & (n - 1)
    eye = jnp.where(row == col, 1.0, 0.0).astype(F32)
    same = (row >> 1) == (col >> 1)
    ts = [eye - jnp.where(same, l, 0.0) for l in ls]
    for k in range(2, n.bit_length()):
        off = ((row >> k) == (col >> k)) & ((row >> (k - 1)) != (col >> (k - 1)))
        tl = [_dot(t, _block_rows(jnp.where(off, l, 0.0), n), INV_PASSES) for t, l in zip(ts, ls)]
        ts = [t - _dot(x, _block_rows(t, n), INV_PASSES) for x, t in zip(tl, ts)]
    return ts


def _tri_inverse_many(ls, eye):
    n = ls[0].shape[0]
    row = lax.broadcasted_iota(jnp.int32, (n, n), 0)
    col = lax.broadcasted_iota(jnp.int32, (n, n), 1)
    same = (row >> 1) == (col >> 1)
    ts = [eye - jnp.where(same, l, 0.0) for l in ls]
    for k in range(2, n.bit_length()):
        off = ((row >> k) == (col >> k)) & ((row >> (k - 1)) != (col >> (k - 1)))
        tl = [_dot(t, jnp.where(off, l, 0.0), INV_PASSES) for t, l in zip(ts, ls)]
        ts = [t - _dot(x, t, INV_PASSES) for x, t in zip(tl, ts)]
    return ts


RWKV_TILES = 7

def _rwkv_intra_body(p_ref, pp_ref, pn_ref, a_ref, lw_ref, mu_ref, kk_ref, ka_ref, rk_ref, bd_ref,
                     f_ref, g_ref, bonus_ref, *, sq, heads, hdim, cps):
    c = CHUNK
    pw = 2 * hdim
    rdim = heads * hdim
    first, last = sq.edges(cps * c)
    proj = p_ref[...]
    nb = 0.5 * (_shift_rows(proj, pp_ref, pn_ref, -1, first, last)
                + _shift_rows(proj, pp_ref, pn_ref, 1, first, last))
    dlt = nb - proj
    r_all = proj[:, :rdim] + dlt[:, :rdim] * mu_ref[0:1, :]
    k_all = proj[:, rdim:2 * rdim] + dlt[:, rdim:2 * rdim] * mu_ref[1:2, :]
    v_all = proj[:, 2 * rdim:] + dlt[:, 2 * rdim:] * mu_ref[2:3, :]
    kn = k_all * kk_ref[...]
    kk_all = kn * lax.rsqrt(_segsum(kn * kn, bd_ref) + RMS_EPS)
    kd_all = [k_all * (1.0 + (a_ref[d] - 1.0) * ka_ref[...]) for d in range(2)]
    rk = r_all * rk_ref[...]
    bonus_ref[...] = (_segsum(rk * kd_all[0], bd_ref) + _segsum(rk * kd_all[1], bd_ref)) * v_all
    keys, a_p, r_p, v_p, bt_p, kt_p, ec_p, mask2 = [], {}, {}, {}, {}, {}, {}, {}
    for d in range(2):
        incl, _, _ = _causal_masks(d, c)
        m01 = jnp.where(incl, 1.0, 0.0).astype(BF16)
        mask2[d] = _stacked_mask(d, c, reps=2)
        for j in range(cps):
            rows = slice(j * c, (j + 1) * c)
            lw = lw_ref[d, rows, :]
            cum = _dot_exact_lhs(m01, lw)
            e_neg = jnp.exp(-cum)
            e_row = jnp.broadcast_to(jnp.exp(jnp.sum(lw, axis=0, keepdims=True)), lw.shape)
            a_t = -kk_all[rows] * jnp.exp(cum - lw)
            r_t = r_all[rows] * jnp.exp(cum)
            b_t = kk_all[rows] * a_ref[d, rows, :] * e_neg
            k_t = kd_all[d][rows] * e_neg
            v = v_all[rows]
            for p in range(heads // 2):
                key = (d, j, p)
                keys.append(key)
                lanes = slice(p * pw, (p + 1) * pw)
                a_p[key], r_p[key], v_p[key] = a_t[:, lanes], r_t[:, lanes], v[:, lanes]
                bt_p[key] = jnp.transpose(_block_rows(b_t[:, lanes], hdim))
                kt_p[key] = jnp.transpose(_block_rows(k_t[:, lanes], hdim))
                ec_p[key] = jnp.transpose(_block_rows(e_row[:, lanes], hdim))
    ar = {k: jnp.concatenate([a_p[k], r_p[k]], axis=0) for k in keys}
    mb = {k: jnp.where(mask2[k[0]], _dot(ar[k], bt_p[k], RWKV_PASSES), 0.0) for k in keys}
    mk = {k: jnp.where(mask2[k[0]], _dot(ar[k], kt_p[k], RWKV_PASSES), 0.0) for k in keys}
    v_bd = {k: _block_rows(v_p[k], hdim) for k in keys}
    x = {k: _dot(mk[k], v_bd[k], RWKV_PASSES) for k in keys}
    h0 = {k: _dot(_fold_rows(ec_p[k] * kt_p[k]), v_bd[k], RWKV_PASSES) for k in keys}
    t = dict(zip(keys, _tri_inverse_pairs([-mb[k][:c] for k in keys])))
    wt = {k: _dot(t[k], _block_rows(a_p[k], hdim), RWKV_PASSES) for k in keys}
    ut = {k: _dot(t[k], _block_rows(x[k][:c], hdim), RWKV_PASSES) for k in keys}
    for k in keys:
        d, j, p = k
        lanes = slice(p * pw, (p + 1) * pw)
        f_ref[d, j * c:(j + 1) * c, lanes] = _fold_rows(ec_p[k])
        tiles = (wt[k], r_p[k], mb[k][c:], _fold_rows(ec_p[k] * bt_p[k]), ut[k], x[k][c:], h0[k])
        for i, tile in enumerate(tiles):
            o = (j * RWKV_TILES + i) * c
            g_ref[d, o:o + c, lanes] = tile.astype(BF16)


def _rwkv_intra(sq, proj, a, lw_dec, lw, heads):
    rdim = a.shape[-1]
    hdim = rdim // heads
    assert hdim == CHUNK and heads % 2 == 0 and 2 * hdim == LANES, "pair tiles are [CHUNK, 128 lanes]"
    cps = _largest_divisor(sq.ncc, (RWKV_INTRA_CHUNKS, 1))
    rows = cps * CHUNK
    consts = [lw["mu_rkv"], lw["k_k"], lw["k_a"], lw["r_k"], lw["bd_r"]]
    const = lambda shape: pl.BlockSpec(shape, lambda b, i: (0,) * len(shape))
    per_dir = pl.BlockSpec((2, None, rows, rdim), lambda b, i: (0, b, i, 0))
    out_g = pl.BlockSpec((2, None, RWKV_TILES * rows, rdim), lambda b, i: (0, b, i, 0))
    return pl.pallas_call(
        functools.partial(_rwkv_intra_body, sq=sq, heads=heads, hdim=hdim, cps=cps),
        grid=(sq.B, sq.nchunk // cps),
        in_specs=[pl.BlockSpec((None, rows, 3 * rdim), lambda b, i: (b, i, 0)),
                  sq.prev_rows(3 * rdim, tb=rows), sq.next_rows(3 * rdim, tb=rows), per_dir, per_dir]
        + [const(c.shape) for c in consts],
        out_specs=[per_dir, out_g, pl.BlockSpec((None, rows, rdim), lambda b, i: (b, i, 0))],
        out_shape=[jax.ShapeDtypeStruct((2, sq.B, sq.S, rdim), F32),
                   jax.ShapeDtypeStruct((2, sq.B, RWKV_TILES * sq.S, rdim), BF16),
                   jax.ShapeDtypeStruct((sq.B, sq.S, rdim), F32)],
        compiler_params=_params(2),
        name="rwkv_intra",
    )(proj, proj, proj, a, lw_dec, *consts)


def _rwkv_scan_body(f0_ref, g0_ref, f1_ref, g1_ref, of_ref, ob_ref, h_ref, *, heads, hdim, cps):
    @pl.when(pl.program_id(1) == 0)
    def _():
        h_ref[...] = jnp.zeros_like(h_ref)

    c = CHUNK
    pw = 2 * hdim
    fs, gs, outs = (f0_ref, f1_ref), (g0_ref, g1_ref), (of_ref, ob_ref)
    chains = [(d, p) for d in range(2) for p in range(heads // 2)]
    lanes = [slice(p * pw, (p + 1) * pw) for p in range(heads // 2)]
    dot = functools.partial(jnp.dot, preferred_element_type=F32)
    st = {ch: h_ref[ch[0], ch[1]] for ch in chains}
    for step in range(cps):
        sub = (step, cps - 1 - step)

        def tile(d, p, i, n=1):
            o = (sub[d] * RWKV_TILES + i) * c
            return gs[d][o:o + n * c, lanes[p]]

        m2 = {(d, p): dot(tile(d, p, 0, 2), st[d, p].astype(BF16)) for d, p in chains}
        ub = {(d, p): _block_rows((tile(d, p, 4) + m2[d, p][:c]).astype(BF16), hdim) for d, p in chains}
        for d, p in chains:
            y = tile(d, p, 5) + m2[d, p][c:] + dot(tile(d, p, 2), ub[d, p])
            outs[d][sub[d] * c:(sub[d] + 1) * c, lanes[p]] = y
            decay = _block_rows(fs[d][sub[d] * c:(sub[d] + 1) * c, lanes[p]], hdim)
            st[d, p] = (decay * st[d, p] + _block_rows(tile(d, p, 6), hdim)
                        + dot(_block_rows(tile(d, p, 3), hdim), ub[d, p]))
    for d, p in chains:
        h_ref[d, p] = st[d, p]


def _backward_block(i, ncb, nblk):
    return jnp.where(i < ncb, ncb - 1 - i, nblk + ncb - 1 - i)


def _scan_chunks_per_step(sq):
    cps = _largest_divisor(sq.ncc, (SCAN_CHUNKS, 2, 1))
    assert sq.nchunk % cps == 0
    return cps


def _scan_specs(sq, heads, rows, width):
    cps = _scan_chunks_per_step(sq)
    ncb, nblk = sq.ncc // cps, sq.nchunk // cps
    fwd = pl.BlockSpec((None, None, heads, cps * rows, width), lambda b, i: (0, b, 0, i, 0))
    bwd = pl.BlockSpec((None, None, heads, cps * rows, width),
                       lambda b, i: (1, b, 0, _backward_block(i, ncb, nblk), 0))
    return fwd, bwd


def _scan_out_specs(sq, width):
    cps = _scan_chunks_per_step(sq)
    ncb, nblk = sq.ncc // cps, sq.nchunk // cps
    fwd = pl.BlockSpec((None, cps * CHUNK, width), lambda b, i: (b, i, 0))
    bwd = pl.BlockSpec((None, cps * CHUNK, width), lambda b, i: (b, _backward_block(i, ncb, nblk), 0))
    return [fwd, bwd]


def _rwkv_scan(sq, f, g, heads):
    rdim = f.shape[-1]
    hdim = rdim // heads
    cps = _scan_chunks_per_step(sq)
    ncb, nblk = sq.ncc // cps, sq.nchunk // cps
    fwd = lambda rows: pl.BlockSpec((None, None, rows, rdim), lambda b, i: (0, b, i, 0))
    bwd = lambda rows: pl.BlockSpec((None, None, rows, rdim),
                                    lambda b, i: (1, b, _backward_block(i, ncb, nblk), 0))
    rows_f, rows_g = cps * CHUNK, cps * RWKV_TILES * CHUNK
    one = jax.ShapeDtypeStruct((sq.B, sq.S, rdim), F32)
    return pl.pallas_call(
        functools.partial(_rwkv_scan_body, heads=heads, hdim=hdim, cps=cps),
        grid=(sq.B, sq.nchunk // cps),
        in_specs=[fwd(rows_f), fwd(rows_g), bwd(rows_f), bwd(rows_g)],
        out_specs=_scan_out_specs(sq, rdim),
        out_shape=[one, one],
        scratch_shapes=[pltpu.VMEM((2, heads // 2, 2 * hdim, 2 * hdim), F32)],
        compiler_params=_params(2),
        name="rwkv_scan",
    )(f, g, f, g)


def _gdn_intra_body(p_ref, pp_ref, pn_ref, ab_ref, cw_ref, bd_ref, f_ref, ga_ref, gb_ref, *, sq, heads, hdim, cps):
    c = CHUNK
    gdim = heads * hdim
    scale = hdim ** -0.5
    first, last = sq.edges(cps * c)
    proj = p_ref[...]
    width = cw_ref.shape[0]
    half = width // 2
    acc = proj * cw_ref[half:half + 1, :]
    for j in range(width):
        if j != half:
            acc = acc + _shift_rows(proj, pp_ref, pn_ref, j - half, first, last) * cw_ref[j:j + 1, :]
    y = acc * _sigmoid(acc)
    q_blk = y[:, :gdim]
    k_blk = y[:, gdim:2 * gdim]
    q_blk = q_blk * lax.rsqrt(_segsum(q_blk * q_blk, bd_ref) + RMS_EPS)
    k_blk = k_blk * lax.rsqrt(_segsum(k_blk * k_blk, bd_ref) + RMS_EPS)
    v_blk = y[:, 2 * gdim:]
    sl = [slice(h * hdim, (h + 1) * hdim) for h in range(heads)]
    keys = []
    beta, gc, g_last, gc_row, decay, kb, kq, kt_h, vb, strict_of, eye = {}, {}, {}, {}, {}, {}, {}, {}, {}, {}, None
    for j in range(cps):
        rows = slice(j * c, (j + 1) * c)
        ab = ab_ref[rows, :]
        tot_all = jnp.sum(ab, axis=0, keepdims=True)
        lane = lax.broadcasted_iota(jnp.int32, ab.shape, 1)
        k_all = k_blk[rows]
        k_tt = jnp.transpose(k_all)
        for d in range(2):
            incl, strict, eye = _causal_masks(d, c)
            strict_of[d] = strict
            gc_all = _dot_exact_lhs(jnp.where(incl, 1.0, 0.0).astype(BF16), ab)
            gc_all_t = jnp.transpose(gc_all)
            sub = lax.broadcasted_iota(jnp.int32, gc_all_t.shape, 0)
            for h in range(heads):
                key = (d, j, h)
                keys.append(key)
                pick_b = lane == d * heads + h
                pick_g = lane == 2 * heads + d * heads + h
                beta[key] = jnp.sum(jnp.where(pick_b, ab, 0.0), axis=1, keepdims=True)
                gc[key] = jnp.sum(jnp.where(pick_g, gc_all, 0.0), axis=1, keepdims=True)
                g_last[key] = jnp.sum(jnp.where(pick_g[:1], tot_all, 0.0), axis=1, keepdims=True)
                gc_row[key] = jnp.sum(jnp.where(sub == 2 * heads + d * heads + h, gc_all_t, 0.0),
                                      axis=0, keepdims=True)
                diff = gc[key] - gc_row[key]
                decay[key] = jnp.where(incl, jnp.exp(jnp.where(incl, diff, 0.0)), 0.0)
                kb[key] = k_all[:, sl[h]] * beta[key]
                kq[key] = jnp.concatenate([kb[key], q_blk[rows, sl[h]] * scale], axis=0)
                kt_h[key] = k_tt[sl[h]]
                vb[key] = v_blk[rows, sl[h]] * beta[key]
    m = {k: _dot(kq[k], kt_h[k], GDN_PASSES) for k in keys}
    lower = [jnp.where(strict_of[k[0]], m[k][:c] * decay[k], 0.0) for k in keys]
    t = dict(zip(keys, _tri_inverse_many(lower, eye)))
    e_gc = {k: jnp.exp(gc[k]) for k in keys}
    sol = {k: _dot(t[k], jnp.concatenate([vb[k], kb[k] * e_gc[k]], axis=1), GDN_PASSES) for k in keys}
    for k in keys:
        d, j, h = k
        of, oa, ob = j * (c + SUBLANES), j * 2 * c, j * (c + hdim)
        f_ref[d, h, of:of + c] = sol[k][:, :hdim]
        f_ref[d, h, of + c:of + c + SUBLANES] = jnp.broadcast_to(jnp.exp(g_last[k]), (SUBLANES, hdim))
        ga_ref[d, h, oa:oa + c] = sol[k][:, hdim:].astype(BF16)
        ga_ref[d, h, oa + c:oa + 2 * c] = (kq[k][c:] * e_gc[k]).astype(BF16)
        gb_ref[d, h, ob:ob + c] = (m[k][c:] * decay[k]).astype(BF16)
        gb_ref[d, h, ob + c:ob + c + hdim] = (kt_h[k] * jnp.exp(g_last[k] - gc_row[k])).astype(BF16)


def _gdn_intra(sq, proj, ab, lw, heads, gdim, col_block):
    hdim = gdim // heads
    cps = _largest_divisor(sq.ncc, (GDN_INTRA_CHUNKS, 2, 1))
    tb = cps * CHUNK
    w = 3 * gdim
    consts = [lw["conv"], lw["bd_g"]]
    const = lambda shape: pl.BlockSpec(shape, lambda b, i: (0,) * len(shape))
    small = pl.BlockSpec((None, tb, LANES), lambda b, i: (b, i, 0))
    rows = (CHUNK + SUBLANES, 2 * CHUNK, CHUNK + hdim)
    widths = (hdim, hdim, CHUNK)
    dtypes = (F32, BF16, BF16)
    outs = [pl.BlockSpec((2, None, heads, cps * r, wd), lambda b, i: (0, b, 0, i, 0)) for r, wd in zip(rows, widths)]
    shapes = [jax.ShapeDtypeStruct((2, sq.B, heads, sq.nchunk * r, wd), dt) for r, wd, dt in zip(rows, widths, dtypes)]
    return pl.pallas_call(
        functools.partial(_gdn_intra_body, sq=sq, heads=heads, hdim=hdim, cps=cps),
        grid=(sq.B, sq.nchunk // cps),
        in_specs=[pl.BlockSpec((None, tb, w), lambda b, i: (b, i, col_block)),
                  sq.prev_rows(w, col_block, tb=tb), sq.next_rows(w, col_block, tb=tb), small]
        + [const(c.shape) for c in consts],
        out_specs=outs,
        out_shape=shapes,
        compiler_params=_params(2),
        name="gdn_intra",
    )(proj, proj, proj, ab, *consts)


def _gdn_scan_body(f0_ref, ga0_ref, gb0_ref, f1_ref, ga1_ref, gb1_ref, of_ref, ob_ref, s_ref, *, heads, hdim, cps):
    @pl.when(pl.program_id(1) == 0)
    def _():
        s_ref[...] = jnp.zeros_like(s_ref)

    c = CHUNK
    fs, gas, gbs, outs = (f0_ref, f1_ref), (ga0_ref, ga1_ref), (gb0_ref, gb1_ref), (of_ref, ob_ref)
    chains = [(d, h) for d in range(2) for h in range(heads)]
    dot = functools.partial(jnp.dot, preferred_element_type=F32)
    st = {ch: s_ref[ch[0], ch[1]] for ch in chains}
    for step in range(cps):
        sub = (step, cps - 1 - step)
        of = [(c + SUBLANES) * sub[d] for d in range(2)]
        oa = [2 * c * sub[d] for d in range(2)]
        ob = [(c + hdim) * sub[d] for d in range(2)]
        m = {(d, h): dot(gas[d][h, oa[d]:oa[d] + 2 * c, :], st[d, h].astype(BF16)) for d, h in chains}
        vn = {(d, h): (fs[d][h, of[d]:of[d] + c, :] - m[d, h][:c]).astype(BF16) for d, h in chains}
        for d, h in chains:
            outs[d][sub[d] * c:(sub[d] + 1) * c, h * hdim:(h + 1) * hdim] = (
                m[d, h][c:] + dot(gbs[d][h, ob[d]:ob[d] + c, :], vn[d, h]))
            st[d, h] = (st[d, h] * fs[d][h, of[d] + c:of[d] + c + 1, :]
                        + dot(gbs[d][h, ob[d] + c:ob[d] + c + hdim, :], vn[d, h]))
    for d, h in chains:
        s_ref[d, h] = st[d, h]


def _gdn_scan(sq, f, ga, gb, heads):
    hdim = f.shape[-1]
    gdim = heads * hdim
    cps = _scan_chunks_per_step(sq)
    specs = []
    for arr in (f, ga, gb):
        specs.append(_scan_specs(sq, heads, arr.shape[3] // sq.nchunk, arr.shape[4]))
    one = jax.ShapeDtypeStruct((sq.B, sq.S, gdim), F32)
    return pl.pallas_call(
        functools.partial(_gdn_scan_body, heads=heads, hdim=hdim, cps=cps),
        grid=(sq.B, sq.nchunk // cps),
        in_specs=[s[0] for s in specs] + [s[1] for s in specs],
        out_specs=_scan_out_specs(sq, gdim),
        out_shape=[one, one],
        scratch_shapes=[pltpu.VMEM((2, heads, hdim, hdim), F32)],
        compiler_params=_params(2),
        name="gdn_scan",
    )(f, ga, gb, f, ga, gb)


def _rope(x, cos, sin_signed):
    n = x.shape[-1]
    lane = lax.broadcasted_iota(jnp.int32, x.shape, 1)
    partner = jnp.where((lane & 1) == 0, pltpu.roll(x, n - 1, axis=1), pltpu.roll(x, 1, axis=1))
    return x * cos + partner * sin_signed


def _attn_prep_body(q_ref, kv_ref, cs_ref, qn_ref, kn_ref, bd_ref, qo_ref, ko_ref, vo_ref,
                    *, q_heads, kv_heads, hdim):
    kvd = kv_heads * hdim
    cos = cs_ref[:, :kvd]
    sin = cs_ref[:, kvd:]
    reps = q_heads // kv_heads
    cos_q = jnp.concatenate([cos] * reps, axis=1)
    sin_q = jnp.concatenate([sin] * reps, axis=1)
    inv_n = 1.0 / hdim
    q = q_ref[...]
    q = q * lax.rsqrt(_segsum(q * q, bd_ref) * inv_n + RMS_EPS) * qn_ref[...]
    q = _rope(q, cos_q, sin_q) * (hdim ** -0.5)
    kv = kv_ref[...]
    k = kv[:, :kvd]
    kbd = bd_ref[:kvd, :kvd]
    hi, lo = _split2(k * k)
    ms = (jnp.dot(hi, kbd, preferred_element_type=F32) + jnp.dot(lo, kbd, preferred_element_type=F32)) * inv_n
    k = k * lax.rsqrt(ms + RMS_EPS) * kn_ref[...]
    k = _rope(k, cos, sin)
    v = kv[:, kvd:]
    for h in range(q_heads):
        qo_ref[h] = q[:, h * hdim:(h + 1) * hdim].astype(BF16)
    k_t = jnp.transpose(k)
    for h in range(kv_heads):
        ko_ref[h] = k_t[h * hdim:(h + 1) * hdim].astype(BF16)
        vh = v[:, h * hdim:(h + 1) * hdim]
        vo_ref[h] = jnp.concatenate([vh, jnp.ones_like(vh)], axis=1).astype(BF16)


def _attn_prep(sq, proj, cs_tab, lw, q_heads, kv_heads, hdim, q_col_block, kv_col_block):
    qd, kvd = q_heads * hdim, kv_heads * hdim
    consts = [lw["q_norm"], lw["k_norm"], lw["bd_r"]]
    k_spec = pl.BlockSpec((None, kv_heads, hdim, sq.TB), lambda b, j: (b, 0, 0, j))
    return pl.pallas_call(
        functools.partial(_attn_prep_body, q_heads=q_heads, kv_heads=kv_heads, hdim=hdim),
        grid=sq.grid,
        in_specs=[sq.rows(qd, q_col_block), sq.rows(2 * kvd, kv_col_block),
                  pl.BlockSpec((sq.TB, 2 * kvd), lambda b, j: (j, 0))]
        + [sq.const(c.shape) for c in consts],
        out_specs=[sq.heads(q_heads, hdim), k_spec, sq.heads(kv_heads, 2 * hdim)],
        out_shape=[jax.ShapeDtypeStruct((sq.B, q_heads, sq.S, hdim), BF16),
                   jax.ShapeDtypeStruct((sq.B, kv_heads, hdim, sq.S), BF16),
                   jax.ShapeDtypeStruct((sq.B, kv_heads, sq.S, 2 * hdim), BF16)],
        compiler_params=_params(2),
        name="attn_prep",
    )(proj, proj, cs_tab, *consts)


def _attn_body(q_ref, kt_ref, v_ref, o_ref, s_ref, p_ref, *, group, n_ctx_qblocks, n_ctx_keys):
    tq, hdim = q_ref.shape[1], q_ref.shape[2]
    n_keys = kt_ref.shape[1]

    def attend(nk):
        tiles = range(0, nk, KEY_BLOCK)
        for g in range(group):
            q = q_ref[g]
            rows = slice(g * tq, (g + 1) * tq)
            mx = None
            for t in tiles:
                s = jnp.dot(q, kt_ref[:, t:t + KEY_BLOCK], preferred_element_type=F32)
                s_ref[:, t:t + KEY_BLOCK] = s
                for c0 in range(0, KEY_BLOCK, LANES):
                    part = s[:, c0:c0 + LANES]
                    mx = part if mx is None else jnp.maximum(mx, part)
            m = jnp.broadcast_to(jnp.max(mx, axis=1, keepdims=True), (tq, LANES))
            for t in tiles:
                for c0 in range(t, t + KEY_BLOCK, LANES):
                    p_ref[rows, c0:c0 + LANES] = jnp.exp(s_ref[:, c0:c0 + LANES] - m).astype(BF16)
        acc = jnp.dot(p_ref[:, 0:nk], v_ref[0:nk, :], preferred_element_type=F32)
        out = acc[:, :hdim] / acc[:, hdim:]
        for g in range(group):
            o_ref[:, g * hdim:(g + 1) * hdim] = out[g * tq:(g + 1) * tq, :]

    is_ctx = pl.program_id(2) < n_ctx_qblocks

    @pl.when(is_ctx)
    def _():
        attend(n_ctx_keys)

    @pl.when(jnp.logical_not(is_ctx))
    def _():
        attend(n_keys)


def _attention(sq, q, kt, v):
    b, q_heads, s, hdim = q.shape
    kv_heads = kt.shape[1]
    group = q_heads // kv_heads
    tq = _largest_divisor(sq.ctx, (128, 64))
    assert sq.ctx % KEY_BLOCK == 0 and s % KEY_BLOCK == 0
    return pl.pallas_call(
        functools.partial(_attn_body, group=group, n_ctx_qblocks=sq.ctx // tq, n_ctx_keys=sq.ctx),
        grid=(b, kv_heads, s // tq),
        in_specs=[pl.BlockSpec((None, group, tq, hdim), lambda bi, g, i: (bi, g, i, 0)),
                  pl.BlockSpec((None, None, hdim, s), lambda bi, g, i: (bi, g, 0, 0)),
                  pl.BlockSpec((None, None, s, 2 * hdim), lambda bi, g, i: (bi, g, 0, 0))],
        out_specs=pl.BlockSpec((None, tq, group * hdim), lambda bi, g, i: (bi, i, g)),
        out_shape=jax.ShapeDtypeStruct((b, s, q_heads * hdim), F32),
        scratch_shapes=[pltpu.VMEM((tq, s), F32), pltpu.VMEM((group * tq, s), BF16)],
        compiler_params=_params(3),
        name="attention",
    )(q, kt, v)


def _merge_body(x_ref, rf_ref, rb_ref, bonus_ref, rg_ref, gf_ref, gb_ref, z_ref, yc_ref, gate_ref, mod_ref,
                lnw_ref, lnb_ref, bdr_ref, gn_ref, bdg_ref, wa_ref, wb_ref, wc_ref, wo_ref, o_ref,
                *, gate_row, r_hdim, g_hdim):
    d = x_ref.shape[-1]
    o = rf_ref[...] + rb_ref[...]
    mean = _segsum(o, bdr_ref) * (1.0 / r_hdim)
    cen = o - mean
    var = _segsum(cen * cen, bdr_ref) * (1.0 / r_hdim)
    ya = (cen * lax.rsqrt(var + RWKV_LNX_EPS) * lnw_ref[...] + lnb_ref[...] + bonus_ref[...]) * rg_ref[...]
    o = gf_ref[...] + gb_ref[...]
    ms = _segsum(o * o, bdg_ref) * (1.0 / g_hdim)
    z = z_ref[...]
    yb = o * lax.rsqrt(ms + RMS_EPS) * gn_ref[...] * (z * _sigmoid(z))
    m = gate_ref[:, :d] * _dot(ya, wa_ref[...])
    m = m + gate_ref[:, d:2 * d] * _dot(yb, wb_ref[...])
    m = m + gate_ref[:, 2 * d:] * _dot(yc_ref[...], wc_ref[...])
    y = _dot(m, wo_ref[...])
    o_ref[...] = x_ref[...] + mod_ref[gate_row:gate_row + 1, :] * y


def _merge(sq, xs, o_rf, o_rb, bonus, r_gate, o_gf, o_gb, proj, z_col_block, yc, gates, modtab, lw,
           r_heads, g_heads):
    d = xs.shape[-1]
    rdim, gdim = o_rf.shape[-1], o_gf.shape[-1]
    consts = [lw["lnx_w"], lw["lnx_b"], lw["bd_r"], lw["gdn_norm"], lw["bd_g"],
              lw["w_up_a"], lw["w_up_b"], lw["w_up_c"], lw["w_out"]]
    return pl.pallas_call(
        functools.partial(_merge_body, gate_row=2, r_hdim=rdim // r_heads, g_hdim=gdim // g_heads),
        grid=sq.grid,
        in_specs=[sq.rows(d)] + [sq.rows(rdim)] * 4 + [sq.rows(gdim)] * 2
        + [sq.rows(gdim, z_col_block), sq.rows(yc.shape[-1]), sq.rows(3 * d), sq.mod(d)]
        + [sq.const(c.shape) for c in consts],
        out_specs=sq.rows(d),
        out_shape=jax.ShapeDtypeStruct(xs.shape, F32),
        compiler_params=_params(2),
        name="merge",
    )(xs, o_rf, o_rb, bonus, r_gate, o_gf, o_gb, proj, yc, gates, modtab, *consts)


def _ffn_body(x_ref, g_ref, mod_ref, w1_ref, w3_ref, w2_ref, o_ref, *, h_chunk, ctx_len):
    rows = x_ref.shape[0]
    row = pl.program_id(1) * rows + lax.broadcasted_iota(jnp.int32, (rows, 1), 0)
    is_ctx = row < ctx_len
    pick = lambda i: jnp.where(is_ctx, mod_ref[0, i:i + 1, :], mod_ref[1, i:i + 1, :])
    x = x_ref[...]
    y = x * lax.rsqrt(jnp.mean(x * x, axis=-1, keepdims=True) + RMS_EPS) * g_ref[...]
    h = (y * (1.0 + pick(4)) + pick(3)).astype(BF16)
    hidden = w1_ref.shape[1]
    acc = jnp.zeros(x.shape, F32)
    for c0 in range(0, hidden, h_chunk):
        a = jnp.dot(h, w1_ref[:, c0:c0 + h_chunk], preferred_element_type=F32)
        b = jnp.dot(h, w3_ref[:, c0:c0 + h_chunk], preferred_element_type=F32)
        t = (a * _sigmoid(a) * b).astype(BF16)
        acc = acc + jnp.dot(t, w2_ref[c0:c0 + h_chunk, :], preferred_element_type=F32)
    o_ref[...] = x + pick(5) * acc


def _ffn(sq, xs, g, modtab, lw):
    d = xs.shape[-1]
    consts = [lw["ffn_w1"], lw["ffn_w3"], lw["ffn_w2"]]
    h_chunk = _largest_divisor(lw["ffn_w1"].shape[1], (512, 256, 128))
    tb = _largest_divisor(sq.S, FFN_ROWS)
    rows = pl.BlockSpec((None, tb, d), lambda b, j: (b, j, 0))
    const = lambda shape: pl.BlockSpec(shape, lambda b, j: (0,) * len(shape))
    return pl.pallas_call(
        functools.partial(_ffn_body, h_chunk=h_chunk, ctx_len=sq.ctx),
        grid=(sq.B, sq.S // tb),
        in_specs=[rows, const((1, d)), pl.BlockSpec((None, 2, 6, d), lambda b, j: (b, 0, 0, 0))]
        + [const(c.shape) for c in consts],
        out_specs=rows,
        out_shape=jax.ShapeDtypeStruct(xs.shape, F32),
        compiler_params=_params(2),
        name="ffn",
    )(xs, g.reshape(1, d), modtab, *consts)


def _block_diag_ones(n, seg):
    idx = jnp.arange(n) // seg
    return (idx[:, None] == idx[None, :]).astype(BF16)


def _block_diag2(m):
    z = jnp.zeros_like(m[0])
    return jnp.concatenate([jnp.concatenate([m[0], z], axis=1), jnp.concatenate([z, m[1]], axis=1)], axis=0)


def _pad_to(x, axis, size):
    pad = [(0, 0)] * x.ndim
    pad[axis] = (0, size - x.shape[axis])
    return jnp.pad(x, pad)


def _rope_table(ctx_len, seq_len, hdim, kv_heads):
    rows = seq_len // GRID_W
    row = jnp.repeat(jnp.arange(rows), GRID_W).astype(F32)
    col = jnp.tile(jnp.arange(GRID_W), rows).astype(F32)
    half = hdim // 2
    inv = ROPE_THETA ** (-jnp.arange(0, half, 2, dtype=F32) / half)
    ang = jnp.concatenate([row[:, None] * inv, col[:, None] * inv], axis=-1)
    cos = jnp.repeat(jnp.cos(ang), 2, axis=1)
    sin = jnp.repeat(jnp.sin(ang), 2, axis=1) * jnp.tile(jnp.array([-1.0, 1.0], F32), half)
    cos = jnp.concatenate([jnp.ones((ctx_len, hdim), F32), cos], axis=0)
    sin = jnp.concatenate([jnp.zeros((ctx_len, hdim), F32), sin], axis=0)
    return jnp.concatenate([jnp.tile(cos, (1, kv_heads)), jnp.tile(sin, (1, kv_heads))], axis=1)


def kernel(x, c, ctx, c_ctx, ada_w, ada_b, norm1, norm2, w_in, rwkv_mu_x, rwkv_mu_rkv, rwkv_w0, rwkv_w1, rwkv_w2, rwkv_a0, rwkv_a1, rwkv_a2, rwkv_g1, rwkv_g2, rwkv_k_k, rwkv_k_a, rwkv_r_k, rwkv_lnx_w, rwkv_lnx_b, gdn_conv, gdn_w_alpha, gdn_dt_bias, gdn_A_log, gdn_w_beta, gdn_norm, attn_q_norm, attn_k_norm, w_up_a, w_up_b, w_up_c, w_gate, b_gate, w_out, ffn_w1, ffn_w3, ffn_w2, final_norm):
    batch, seq_len, d = x.shape
    ctx_len = ctx.shape[1]
    depth = ada_w.shape[0]
    sq = _Seq(batch, ctx_len, seq_len)

    r_heads, r_hdim = rwkv_r_k.shape[1], rwkv_r_k.shape[2]
    rdim = r_heads * r_hdim
    g_heads, g_hdim = gdn_w_alpha.shape[-1], gdn_norm.shape[-1]
    gdim = g_heads * g_hdim
    a_hdim = attn_q_norm.shape[-1]
    qd = w_up_c.shape[1]
    q_heads = qd // a_hdim
    kvd = (w_in.shape[-1] - 3 * rdim - 4 * gdim - qd) // 2
    kv_heads = kvd // a_hdim
    assert rdim == gdim == qd and r_hdim == a_hdim, "lane-segment constants are shared between mixers"
    assert (3 * rdim) % (3 * gdim) == 0 and (3 * rdim + 3 * gdim) % gdim == 0
    gdn_col = (3 * rdim) // (3 * gdim)
    z_col = (3 * rdim + 3 * gdim) // gdim
    q_col = (3 * rdim + 4 * gdim) // qd
    assert (3 * rdim + 4 * gdim + qd) % (2 * kvd) == 0
    kv_col = (3 * rdim + 4 * gdim + qd) // (2 * kvd)

    bd_r = _block_diag_ones(rdim, r_hdim)
    bd_g = _block_diag_ones(gdim, g_hdim)
    cs_tab = _rope_table(ctx_len, seq_len, a_hdim, kv_heads)
    n_beta = 2 * g_heads
    assert 2 * n_beta <= LANES

    cond = jnp.concatenate([c, c_ctx[None, :]], axis=0)
    cond = _pad_to(cond, 0, -(-(batch + 1) // SUBLANES) * SUBLANES)

    xs = jnp.concatenate([ctx, x], axis=1)
    m_rows = batch * sq.S
    for l in range(depth):
        g1w = _pad_to(rwkv_g1[l], 1, 2 * LANES)
        lw = {
            "mu": rwkv_mu_x[l], "mu_rkv": rwkv_mu_rkv[l],
            "w1": jnp.concatenate([rwkv_w1[l, 0], rwkv_w1[l, 1]], axis=1).astype(BF16),
            "w2": _block_diag2(rwkv_w2[l]).astype(BF16),
            "w0": rwkv_w0[l].reshape(1, 2 * rdim),
            "a1": jnp.concatenate([rwkv_a1[l, 0], rwkv_a1[l, 1]], axis=1).astype(BF16),
            "a2": _block_diag2(rwkv_a2[l]).astype(BF16),
            "a0": rwkv_a0[l].reshape(1, 2 * rdim),
            "g1": g1w.astype(BF16),
            "g2": _pad_to(rwkv_g2[l], 0, 2 * LANES).astype(BF16),
            "wab": _pad_to(jnp.concatenate([gdn_w_beta[l, 0], gdn_w_beta[l, 1],
                                            gdn_w_alpha[l, 0], gdn_w_alpha[l, 1]], axis=1), 1, LANES).astype(BF16),
            "abb": _pad_to(jnp.concatenate([jnp.zeros((n_beta,), F32), gdn_dt_bias[l].reshape(-1)]), 0, LANES).reshape(1, LANES),
            "alog": _pad_to(jnp.concatenate([jnp.zeros((n_beta,), F32), gdn_A_log[l].reshape(-1)]), 0, LANES).reshape(1, LANES),
            "n_beta": n_beta,
            "k_k": rwkv_k_k[l].reshape(1, rdim), "k_a": rwkv_k_a[l].reshape(1, rdim),
            "r_k": rwkv_r_k[l].reshape(1, rdim),
            "lnx_w": rwkv_lnx_w[l].reshape(1, rdim), "lnx_b": rwkv_lnx_b[l].reshape(1, rdim),
            "bd_r": bd_r, "bd_g": bd_g,
            "conv": jnp.transpose(gdn_conv[l]),
            "gdn_norm": jnp.tile(gdn_norm[l], g_heads).reshape(1, gdim),
            "q_norm": jnp.tile(attn_q_norm[l], q_heads).reshape(1, qd),
            "k_norm": jnp.tile(attn_k_norm[l], kv_heads).reshape(1, kvd),
            "w_up_a": w_up_a[l].astype(BF16), "w_up_b": w_up_b[l].astype(BF16),
            "w_up_c": w_up_c[l].astype(BF16), "w_out": w_out[l].astype(BF16),
            "ffn_w1": ffn_w1[l].astype(BF16), "ffn_w3": ffn_w3[l].astype(BF16), "ffn_w2": ffn_w2[l].astype(BF16),
        }
        mod = _matmul(cond, ada_w[l].astype(BF16), bias=ada_b[l], pre_act="silu")
        mod_x = mod[:batch].reshape(batch, 6, d)
        mod_c = jnp.broadcast_to(mod[batch].reshape(1, 6, d), (batch, 6, d))
        modtab = jnp.stack([mod_c, mod_x], axis=1)

        h = _normmod(sq, xs, norm1[l], modtab, shift_row=0, scale_row=1)
        h2 = h.reshape(m_rows, d)
        proj = _matmul(h2, w_in[l].astype(BF16)).reshape(batch, sq.S, -1)
        gates = _matmul(h2, w_gate[l].astype(BF16), bias=b_gate[l], act="sigmoid",
                        out_dtype=BF16).reshape(batch, sq.S, -1)
        lw_dec, a_iclr, g_out, ab = _lora(sq, h, lw)

        rf, rg, bonus = _rwkv_intra(sq, proj, a_iclr, lw_dec, lw, r_heads)
        o_rf, o_rb = _rwkv_scan(sq, rf, rg, r_heads)

        gf, gga, ggb = _gdn_intra(sq, proj, ab, lw, g_heads, gdim, gdn_col)
        o_gf, o_gb = _gdn_scan(sq, gf, gga, ggb, g_heads)

        aq, ak, av = _attn_prep(sq, proj, cs_tab, lw, q_heads, kv_heads, a_hdim, q_col, kv_col)
        yc = _attention(sq, aq, ak, av)

        xs = _merge(sq, xs, o_rf, o_rb, bonus, g_out, o_gf, o_gb, proj, z_col, yc, gates, modtab, lw,
                    r_heads, g_heads)
        xs = _ffn(sq, xs, norm2[l], modtab, lw)
    return _final_norm(sq, xs, final_norm)
```

```python
import functools

import jax
import jax.numpy as jnp
from jax import lax
from jax.experimental import pallas as pl
from jax.experimental.pallas import tpu as pltpu

F32 = jnp.float32
BF16 = jnp.bfloat16

RMS_EPS = 1e-6
RWKV_LNX_EPS = 64e-5
ROPE_THETA = 10000.0
GRID_W = 64

SUBLANES = 8
LANES = 128
CHUNK = 64
MAX_TOKEN_BLOCK = 256
MAX_MM_ROWS = 512
FFN_ROWS = (544, 512, 256, 128)
KEY_BLOCK = 256
RWKV_PASSES = 1
GDN_PASSES = 1
INV_PASSES = 1
RWKV_INTRA_CHUNKS = 2
GDN_INTRA_CHUNKS = 4
SCAN_CHUNKS = 4
VMEM_LIMIT = 56 * 1024 * 1024


def _sigmoid(x):
    return 1.0 / (1.0 + jnp.exp(-x))


def _softplus(x):
    return jnp.maximum(x, 0.0) + jnp.log(1.0 + jnp.exp(-jnp.abs(x)))


def _split2(x):
    hi = x.astype(BF16)
    lo = (x - hi.astype(F32)).astype(BF16)
    return hi, lo


def _mm(a, b, dims, passes):
    d = functools.partial(lax.dot_general, dimension_numbers=(dims, ((), ())), preferred_element_type=F32)
    if passes == 1:
        return d(a.astype(BF16), b.astype(BF16))
    ah, al = _split2(a)
    bh, bl = _split2(b)
    return d(ah, bh) + (d(ah, bl) + d(al, bh))


def _dot(a, b, passes=1):
    return _mm(a, b, ((1,), (0,)), passes)


def _split3(x):
    hi = x.astype(BF16)
    r1 = x - hi.astype(F32)
    mid = r1.astype(BF16)
    lo = (r1 - mid.astype(F32)).astype(BF16)
    return hi, mid, lo


def _dot_exact_lhs(m01, x):
    d = functools.partial(jnp.dot, preferred_element_type=F32)
    hi, mid, lo = _split3(x)
    return d(m01, hi) + (d(m01, mid) + d(m01, lo))


def _segsum(x, bd_ref):
    d = functools.partial(jnp.dot, preferred_element_type=F32)
    hi, lo = _split2(x)
    bd = bd_ref[...]
    return d(hi, bd) + d(lo, bd)


def _row_iota(shape):
    return lax.broadcasted_iota(jnp.int32, shape, 0)


def _shift_rows(x, prev_ref, next_ref, k, first, last):
    n = x.shape[0]
    y = pltpu.roll(x, (-k) % n, axis=0)
    row = _row_iota((SUBLANES, x.shape[1]))
    if k < 0:
        tile = y[:SUBLANES]
        for i in range(-k):
            edge = prev_ref[SUBLANES + k + i:SUBLANES + k + i + 1, :]
            edge = jnp.where(first, 0.0, edge)
            tile = jnp.where(row == i, edge, tile)
        return jnp.concatenate([tile, y[SUBLANES:]], axis=0)
    tile = y[n - SUBLANES:]
    for i in range(k):
        edge = next_ref[i:i + 1, :]
        edge = jnp.where(last, 0.0, edge)
        tile = jnp.where(row == SUBLANES - k + i, edge, tile)
    return jnp.concatenate([y[:n - SUBLANES], tile], axis=0)


def _largest_divisor(n, candidates):
    for c in candidates:
        if n % c == 0:
            return c
    raise ValueError(f"no block size among {candidates} divides {n}")


class _Seq:
    def __init__(self, batch, ctx_len, seq_len):
        self.B = batch
        self.ctx = ctx_len
        self.S = ctx_len + seq_len
        self.TB = _largest_divisor(ctx_len, (MAX_TOKEN_BLOCK, 128, 64))
        assert seq_len % self.TB == 0 and self.TB % CHUNK == 0
        self.ncb = ctx_len // self.TB
        self.nblk = self.S // self.TB
        self.ncc = ctx_len // CHUNK
        self.nchunk = self.S // CHUNK
        self.grid = (batch, self.nblk)

    def rows(self, width, col_block=0):
        return pl.BlockSpec((None, self.TB, width), lambda b, j: (b, j, col_block))

    def rows2(self, width):
        return pl.BlockSpec((2, None, self.TB, width), lambda b, j: (0, b, j, 0))

    def heads(self, n_heads, width):
        return pl.BlockSpec((None, n_heads, self.TB, width), lambda b, j: (b, 0, j, 0))

    def prev_rows(self, width, col_block=0, tb=None):
        per = (tb or self.TB) // SUBLANES
        return pl.BlockSpec((None, SUBLANES, width),
                            lambda b, j: (b, jnp.maximum(j * per - 1, 0), col_block))

    def next_rows(self, width, col_block=0, tb=None):
        per = (tb or self.TB) // SUBLANES
        top = self.S // SUBLANES - 1
        return pl.BlockSpec((None, SUBLANES, width),
                            lambda b, j: (b, jnp.minimum((j + 1) * per, top), col_block))

    def const(self, shape):
        zeros = (0,) * len(shape)
        return pl.BlockSpec(shape, lambda b, j: zeros)

    def mod(self, d_model):
        ncb = self.ncb
        return pl.BlockSpec((None, None, 6, d_model),
                            lambda b, j: (b, (j >= ncb).astype(jnp.int32), 0, 0))

    def edges(self, tb=None):
        tb = tb or self.TB
        j = pl.program_id(1)
        ncb, nblk = self.ctx // tb, self.S // tb
        first = (j == 0) | (j == ncb)
        last = (j == ncb - 1) | (j == nblk - 1)
        return first, last


def _params(n_axes):
    return pltpu.CompilerParams(dimension_semantics=("arbitrary",) * n_axes,
                                vmem_limit_bytes=VMEM_LIMIT)


def _mm_body(*refs, act, pre_act, n_chunk, has_bias):
    if has_bias:
        x_ref, w_ref, b_ref, o_ref = refs
    else:
        x_ref, w_ref, o_ref = refs
        b_ref = None
    x = x_ref[...]
    if pre_act == "silu":
        x = x * _sigmoid(x)
    xb = x.astype(BF16)
    n = o_ref.shape[-1]
    for n0 in range(0, n, n_chunk):
        y = jnp.dot(xb, w_ref[:, n0:n0 + n_chunk], preferred_element_type=F32)
        if b_ref is not None:
            y = y + b_ref[:, n0:n0 + n_chunk]
        if act == "sigmoid":
            y = _sigmoid(y)
        o_ref[:, n0:n0 + n_chunk] = y.astype(o_ref.dtype)


def _matmul(x, w, bias=None, act=None, pre_act=None, out_dtype=F32):
    m, k = x.shape
    n = w.shape[1]
    tm = m if m <= MAX_MM_ROWS else _largest_divisor(m, (MAX_MM_ROWS, 256, 128, 64, 32, 16, 8))
    n_chunk = _largest_divisor(n, (512, 256, 128))
    in_specs = [pl.BlockSpec((tm, k), lambda i: (i, 0)),
                pl.BlockSpec((k, n), lambda i: (0, 0))]
    args = [x, w]
    if bias is not None:
        in_specs.append(pl.BlockSpec((1, n), lambda i: (0, 0)))
        args.append(bias.reshape(1, n))
    return pl.pallas_call(
        functools.partial(_mm_body, act=act, pre_act=pre_act, n_chunk=n_chunk,
                          has_bias=bias is not None),
        grid=(m // tm,),
        in_specs=in_specs,
        out_specs=pl.BlockSpec((tm, n), lambda i: (i, 0)),
        out_shape=jax.ShapeDtypeStruct((m, n), out_dtype),
        compiler_params=_params(1),
        name="matmul",
    )(*args)


def _pre_body(x_ref, g_ref, mod_ref, win_ref, wg_ref, bg_ref, h_ref, p_ref, gate_ref, *, n_chunk, ctx_len):
    rows = x_ref.shape[0]
    row = pl.program_id(1) * rows + lax.broadcasted_iota(jnp.int32, (rows, 1), 0)
    is_ctx = row < ctx_len
    pick = lambda i: jnp.where(is_ctx, mod_ref[0, i:i + 1, :], mod_ref[1, i:i + 1, :])
    x = x_ref[...]
    y = x * lax.rsqrt(jnp.mean(x * x, axis=-1, keepdims=True) + RMS_EPS) * g_ref[...]
    h = y * (1.0 + pick(1)) + pick(0)
    h_ref[...] = h
    hb = h.astype(BF16)
    for n0 in range(0, p_ref.shape[-1], n_chunk):
        p_ref[:, n0:n0 + n_chunk] = jnp.dot(hb, win_ref[:, n0:n0 + n_chunk], preferred_element_type=F32)
    for n0 in range(0, gate_ref.shape[-1], n_chunk):
        z = jnp.dot(hb, wg_ref[:, n0:n0 + n_chunk], preferred_element_type=F32) + bg_ref[:, n0:n0 + n_chunk]
        gate_ref[:, n0:n0 + n_chunk] = _sigmoid(z).astype(gate_ref.dtype)


def _pre(sq, xs, g, modtab, w_in, w_gate, b_gate):
    d = xs.shape[-1]
    n_in, n_gate = w_in.shape[1], w_gate.shape[1]
    n_chunk = _largest_divisor(n_in, (256, 128))
    assert n_gate % n_chunk == 0
    tb = _largest_divisor(sq.S, FFN_ROWS)
    rows = lambda width: pl.BlockSpec((None, tb, width), lambda b, j: (b, j, 0))
    const = lambda shape: pl.BlockSpec(shape, lambda b, j: (0,) * len(shape))
    resident = lambda shape: pl.BlockSpec(shape, lambda b, j: (0,) * len(shape), pipeline_mode=pl.Buffered(1))
    return pl.pallas_call(
        functools.partial(_pre_body, n_chunk=n_chunk, ctx_len=sq.ctx),
        grid=(sq.B, sq.S // tb),
        in_specs=[rows(d), const((1, d)), pl.BlockSpec((None, 2, 6, d), lambda b, j: (b, 0, 0, 0)),
                  resident(w_in.shape), resident(w_gate.shape), const((1, n_gate))],
        out_specs=[rows(d), rows(n_in), rows(n_gate)],
        out_shape=[jax.ShapeDtypeStruct(xs.shape, F32),
                   jax.ShapeDtypeStruct(xs.shape[:2] + (n_in,), F32),
                   jax.ShapeDtypeStruct(xs.shape[:2] + (n_gate,), BF16)],
        compiler_params=_params(2),
        name="pre",
    )(xs, g.reshape(1, d), modtab, w_in, w_gate, b_gate.reshape(1, n_gate))


def _final_norm_body(x_ref, g_ref, o_ref):
    x = x_ref[...]
    y = x * lax.rsqrt(jnp.mean(x * x, axis=-1, keepdims=True) + RMS_EPS)
    o_ref[...] = y * g_ref[...]


def _final_norm(sq, xs, g):
    d = xs.shape[-1]
    ncb = sq.ncb
    return pl.pallas_call(
        _final_norm_body,
        grid=(sq.B, sq.nblk - ncb),
        in_specs=[pl.BlockSpec((None, sq.TB, d), lambda b, j: (b, j + ncb, 0)),
                  pl.BlockSpec((1, d), lambda b, j: (0, 0))],
        out_specs=pl.BlockSpec((None, sq.TB, d), lambda b, j: (b, j, 0)),
        out_shape=jax.ShapeDtypeStruct((sq.B, sq.S - sq.ctx, d), F32),
        compiler_params=_params(2),
        name="final_norm",
    )(xs, g.reshape(1, d))


def _lora_body(h_ref, hp_ref, hn_ref, mu_ref, w1_ref, w2_ref, w0_ref, a1_ref, a2_ref, a0_ref,
               g1_ref, g2_ref, wab_ref, abb_ref, alog_ref,
               lw_ref, a_ref, g_ref, ab_ref, *, sq, rdim, n_beta):
    first, last = sq.edges()
    h = h_ref[...]
    nb = 0.5 * (_shift_rows(h, hp_ref, hn_ref, -1, first, last)
                + _shift_rows(h, hp_ref, hn_ref, 1, first, last))
    dlt = nb - h
    xw = h + dlt * mu_ref[0:1, :]
    xa = h + dlt * mu_ref[1:2, :]
    xg = h + dlt * mu_ref[2:3, :]
    wl = w0_ref[...] + _dot(jnp.tanh(_dot(xw, w1_ref[...])), w2_ref[...])
    w_log = -_softplus(-wl) - 0.5
    lw = -jnp.exp(w_log)
    lw_ref[0] = lw[:, :rdim]
    lw_ref[1] = lw[:, rdim:]
    a = _sigmoid(a0_ref[...] + _dot(_dot(xa, a1_ref[...]), a2_ref[...]))
    a_ref[0] = a[:, :rdim]
    a_ref[1] = a[:, rdim:]
    g_ref[...] = _dot(_sigmoid(_dot(xg, g1_ref[...])), g2_ref[...])
    z = _dot(h, wab_ref[...])
    col = lax.broadcasted_iota(jnp.int32, z.shape, 1)
    gl = -jnp.exp(alog_ref[...]) * _softplus(z + abb_ref[...])
    ab_ref[...] = jnp.where(col < n_beta, _sigmoid(z), gl)


def _lora(sq, h, lw):
    d = h.shape[-1]
    rdim = lw["w0"].shape[-1] // 2
    consts = [lw["mu"], lw["w1"], lw["w2"], lw["w0"], lw["a1"], lw["a2"], lw["a0"],
              lw["g1"], lw["g2"], lw["wab"], lw["abb"], lw["alog"]]
    bsd = (sq.B, sq.S)
    return pl.pallas_call(
        functools.partial(_lora_body, sq=sq, rdim=rdim, n_beta=lw["n_beta"]),
        grid=sq.grid,
        in_specs=[sq.rows(d), sq.prev_rows(d), sq.next_rows(d)] + [sq.const(c.shape) for c in consts],
        out_specs=[sq.rows2(rdim), sq.rows2(rdim), sq.rows(rdim), sq.rows(LANES)],
        out_shape=[jax.ShapeDtypeStruct((2,) + bsd + (rdim,), F32),
                   jax.ShapeDtypeStruct((2,) + bsd + (rdim,), F32),
                   jax.ShapeDtypeStruct(bsd + (rdim,), F32),
                   jax.ShapeDtypeStruct(bsd + (LANES,), F32)],
        compiler_params=_params(2),
        name="lora",
    )(h, h, h, *consts)


def _causal_masks(d, c, reps=1):
    row = lax.broadcasted_iota(jnp.int32, (c, reps * c), 0)
    col = lax.broadcasted_iota(jnp.int32, (c, reps * c), 1) & (c - 1)
    delta = jnp.where(d == 0, row - col, col - row)
    eye = jnp.where(row == col, 1.0, 0.0).astype(F32)
    return delta >= 0, delta > 0, eye


def _stacked_mask(d, c, reps=1):
    row = lax.broadcasted_iota(jnp.int32, (2 * c, reps * c), 0)
    col = lax.broadcasted_iota(jnp.int32, (2 * c, reps * c), 1) & (c - 1)
    rr = row & (c - 1)
    delta = jnp.where(d == 0, rr - col, col - rr)
    return (delta > 0) | ((row >= c) & (delta == 0))


def _block_rows(x, w):
    left = lax.broadcasted_iota(jnp.int32, x.shape, 1) < w
    zero = jnp.zeros_like(x)
    return jnp.concatenate([jnp.where(left, x, zero), jnp.where(left, zero, x)], axis=0)


def _fold_rows(y):
    n = y.shape[0] // 2
    return y[:n] + y[n:]


def _tri_inverse_pairs(ls):
    n = ls[0].shape[0]
    row = lax.broadcasted_iota(jnp.int32, (n, 2 * n), 0)
    col = lax.broadcasted_iota(jnp.int32, (n, 2 * n), 1) & (n - 1)
    eye = jnp.where(row == col, 1.0, 0.0).astype(F32)
    same = (row >> 1) == (col >> 1)
    ts = [eye - jnp.where(same, l, 0.0) for l in ls]
    for k in range(2, n.bit_length()):
        off = ((row >> k) == (col >> k)) & ((row >> (k - 1)) != (col >> (k - 1)))
        tl = [_dot(t, _block_rows(jnp.where(off, l, 0.0), n), INV_PASSES) for t, l in zip(ts, ls)]
        ts = [t - _dot(x, _block_rows(t, n), INV_PASSES) for x, t in zip(tl, ts)]
    return ts


def _tri_inverse_many(ls, eye):
    n = ls[0].shape[0]
    row = lax.broadcasted_iota(jnp.int32, (n, n), 0)
    col = lax.broadcasted_iota(jnp.int32, (n, n), 1)
    same = (row >> 1) == (col >> 1)
    ts = [eye - jnp.where(same, l, 0.0) for l in ls]
    for k in range(2, n.bit_length()):
        off = ((row >> k) == (col >> k)) & ((row >> (k - 1)) != (col >> (k - 1)))
        tl = [_dot(t, jnp.where(off, l, 0.0), INV_PASSES) for t, l in zip(ts, ls)]
        ts = [t - _dot(x, t, INV_PASSES) for x, t in zip(tl, ts)]
    return ts


RWKV_TILES = 7

def _rwkv_intra_body(p_ref, pp_ref, pn_ref, a_ref, lw_ref, mu_ref, kk_ref, ka_ref, rk_ref, bd_ref,
                     f_ref, g_ref, bonus_ref, *, sq, heads, hdim, cps):
    c = CHUNK
    pw = 2 * hdim
    rdim = heads * hdim
    first, last = sq.edges(cps * c)
    proj = p_ref[...]
    nb = 0.5 * (_shift_rows(proj, pp_ref, pn_ref, -1, first, last)
                + _shift_rows(proj, pp_ref, pn_ref, 1, first, last))
    dlt = nb - proj
    r_all = proj[:, :rdim] + dlt[:, :rdim] * mu_ref[0:1, :]
    k_all = proj[:, rdim:2 * rdim] + dlt[:, rdim:2 * rdim] * mu_ref[1:2, :]
    v_all = proj[:, 2 * rdim:] + dlt[:, 2 * rdim:] * mu_ref[2:3, :]
    kn = k_all * kk_ref[...]
    kk_all = kn * lax.rsqrt(_segsum(kn * kn, bd_ref) + RMS_EPS)
    kd_all = [k_all * (1.0 + (a_ref[d] - 1.0) * ka_ref[...]) for d in range(2)]
    rk = r_all * rk_ref[...]
    bonus_ref[...] = (_segsum(rk * kd_all[0], bd_ref) + _segsum(rk * kd_all[1], bd_ref)) * v_all
    keys, a_p, r_p, v_p, bt_p, kt_p, ec_p, mask2 = [], {}, {}, {}, {}, {}, {}, {}
    for d in range(2):
        incl, _, _ = _causal_masks(d, c)
        m01 = jnp.where(incl, 1.0, 0.0).astype(BF16)
        mask2[d] = _stacked_mask(d, c, reps=2)
        for j in range(cps):
            rows = slice(j * c, (j + 1) * c)
            lw = lw_ref[d, rows, :]
            cum = _dot_exact_lhs(m01, lw)
            e_neg = jnp.exp(-cum)
            e_row = jnp.broadcast_to(jnp.exp(jnp.sum(lw, axis=0, keepdims=True)), lw.shape)
            a_t = -kk_all[rows] * jnp.exp(cum - lw)
            r_t = r_all[rows] * jnp.exp(cum)
            b_t = kk_all[rows] * a_ref[d, rows, :] * e_neg
            k_t = kd_all[d][rows] * e_neg
            v = v_all[rows]
            for p in range(heads // 2):
                key = (d, j, p)
                keys.append(key)
                lanes = slice(p * pw, (p + 1) * pw)
                a_p[key], r_p[key], v_p[key] = a_t[:, lanes], r_t[:, lanes], v[:, lanes]
                bt_p[key] = jnp.transpose(_block_rows(b_t[:, lanes], hdim))
                kt_p[key] = jnp.transpose(_block_rows(k_t[:, lanes], hdim))
                ec_p[key] = jnp.transpose(_block_rows(e_row[:, lanes], hdim))
    ar = {k: jnp.concatenate([a_p[k], r_p[k]], axis=0) for k in keys}
    mb = {k: jnp.where(mask2[k[0]], _dot(ar[k], bt_p[k], RWKV_PASSES), 0.0) for k in keys}
    mk = {k: jnp.where(mask2[k[0]], _dot(ar[k], kt_p[k], RWKV_PASSES), 0.0) for k in keys}
    v_bd = {k: _block_rows(v_p[k], hdim) for k in keys}
    x = {k: _dot(mk[k], v_bd[k], RWKV_PASSES) for k in keys}
    h0 = {k: _dot(_fold_rows(ec_p[k] * kt_p[k]), v_bd[k], RWKV_PASSES) for k in keys}
    t = dict(zip(keys, _tri_inverse_pairs([-mb[k][:c] for k in keys])))
    wt = {k: _dot(t[k], _block_rows(a_p[k], hdim), RWKV_PASSES) for k in keys}
    ut = {k: _dot(t[k], _block_rows(x[k][:c], hdim), RWKV_PASSES) for k in keys}
    for k in keys:
        d, j, p = k
        lanes = slice(p * pw, (p + 1) * pw)
        f_ref[d, j * c:(j + 1) * c, lanes] = _fold_rows(ec_p[k])
        tiles = (wt[k], r_p[k], mb[k][c:], _fold_rows(ec_p[k] * bt_p[k]), ut[k], x[k][c:], h0[k])
        for i, tile in enumerate(tiles):
            o = (j * RWKV_TILES + i) * c
            g_ref[d, o:o + c, lanes] = tile.astype(BF16)


def _rwkv_intra(sq, proj, a, lw_dec, lw, heads):
    rdim = a.shape[-1]
    hdim = rdim // heads
    assert hdim == CHUNK and heads % 2 == 0 and 2 * hdim == LANES, "pair tiles are [CHUNK, 128 lanes]"
    cps = _largest_divisor(sq.ncc, (RWKV_INTRA_CHUNKS, 1))
    rows = cps * CHUNK
    consts = [lw["mu_rkv"], lw["k_k"], lw["k_a"], lw["r_k"], lw["bd_r"]]
    const = lambda shape: pl.BlockSpec(shape, lambda b, i: (0,) * len(shape))
    per_dir = pl.BlockSpec((2, None, rows, rdim), lambda b, i: (0, b, i, 0))
    out_g = pl.BlockSpec((2, None, RWKV_TILES * rows, rdim), lambda b, i: (0, b, i, 0))
    return pl.pallas_call(
        functools.partial(_rwkv_intra_body, sq=sq, heads=heads, hdim=hdim, cps=cps),
        grid=(sq.B, sq.nchunk // cps),
        in_specs=[pl.BlockSpec((None, rows, 3 * rdim), lambda b, i: (b, i, 0)),
                  sq.prev_rows(3 * rdim, tb=rows), sq.next_rows(3 * rdim, tb=rows), per_dir, per_dir]
        + [const(c.shape) for c in consts],
        out_specs=[per_dir, out_g, pl.BlockSpec((None, rows, rdim), lambda b, i: (b, i, 0))],
        out_shape=[jax.ShapeDtypeStruct((2, sq.B, sq.S, rdim), F32),
                   jax.ShapeDtypeStruct((2, sq.B, RWKV_TILES * sq.S, rdim), BF16),
                   jax.ShapeDtypeStruct((sq.B, sq.S, rdim), F32)],
        compiler_params=_params(2),
        name="rwkv_intra",
    )(proj, proj, proj, a, lw_dec, *consts)


def _rwkv_scan_body(f0_ref, g0_ref, f1_ref, g1_ref, of_ref, ob_ref, h_ref, *, heads, hdim, cps):
    @pl.when(pl.program_id(1) == 0)
    def _():
        h_ref[...] = jnp.zeros_like(h_ref)

    c = CHUNK
    pw = 2 * hdim
    fs, gs, outs = (f0_ref, f1_ref), (g0_ref, g1_ref), (of_ref, ob_ref)
    chains = [(d, p) for d in range(2) for p in range(heads // 2)]
    lanes = [slice(p * pw, (p + 1) * pw) for p in range(heads // 2)]
    dot = functools.partial(jnp.dot, preferred_element_type=F32)
    st = {ch: h_ref[ch[0], ch[1]] for ch in chains}
    for step in range(cps):
        sub = (step, cps - 1 - step)

        def tile(d, p, i, n=1):
            o = (sub[d] * RWKV_TILES + i) * c
            return gs[d][o:o + n * c, lanes[p]]

        m2 = {(d, p): dot(tile(d, p, 0, 2), st[d, p].astype(BF16)) for d, p in chains}
        ub = {(d, p): _block_rows((tile(d, p, 4) + m2[d, p][:c]).astype(BF16), hdim) for d, p in chains}
        for d, p in chains:
            y = tile(d, p, 5) + m2[d, p][c:] + dot(tile(d, p, 2), ub[d, p])
            outs[d][sub[d] * c:(sub[d] + 1) * c, lanes[p]] = y
            decay = _block_rows(fs[d][sub[d] * c:(sub[d] + 1) * c, lanes[p]], hdim)
            st[d, p] = (decay * st[d, p] + _block_rows(tile(d, p, 6), hdim)
                        + dot(_block_rows(tile(d, p, 3), hdim), ub[d, p]))
    for d, p in chains:
        h_ref[d, p] = st[d, p]


def _backward_block(i, ncb, nblk):
    return jnp.where(i < ncb, ncb - 1 - i, nblk + ncb - 1 - i)


def _scan_chunks_per_step(sq):
    cps = _largest_divisor(sq.ncc, (SCAN_CHUNKS, 2, 1))
    assert sq.nchunk % cps == 0
    return cps


def _scan_specs(sq, heads, rows, width):
    cps = _scan_chunks_per_step(sq)
    ncb, nblk = sq.ncc // cps, sq.nchunk // cps
    fwd = pl.BlockSpec((None, None, heads, cps * rows, width), lambda b, i: (0, b, 0, i, 0))
    bwd = pl.BlockSpec((None, None, heads, cps * rows, width),
                       lambda b, i: (1, b, 0, _backward_block(i, ncb, nblk), 0))
    return fwd, bwd


def _scan_out_specs(sq, width):
    cps = _scan_chunks_per_step(sq)
    ncb, nblk = sq.ncc // cps, sq.nchunk // cps
    fwd = pl.BlockSpec((None, cps * CHUNK, width), lambda b, i: (b, i, 0))
    bwd = pl.BlockSpec((None, cps * CHUNK, width), lambda b, i: (b, _backward_block(i, ncb, nblk), 0))
    return [fwd, bwd]


def _rwkv_scan(sq, f, g, heads):
    rdim = f.shape[-1]
    hdim = rdim // heads
    cps = _scan_chunks_per_step(sq)
    ncb, nblk = sq.ncc // cps, sq.nchunk // cps
    fwd = lambda rows: pl.BlockSpec((None, None, rows, rdim), lambda b, i: (0, b, i, 0))
    bwd = lambda rows: pl.BlockSpec((None, None, rows, rdim),
                                    lambda b, i: (1, b, _backward_block(i, ncb, nblk), 0))
    rows_f, rows_g = cps * CHUNK, cps * RWKV_TILES * CHUNK
    one = jax.ShapeDtypeStruct((sq.B, sq.S, rdim), F32)
    return pl.pallas_call(
        functools.partial(_rwkv_scan_body, heads=heads, hdim=hdim, cps=cps),
        grid=(sq.B, sq.nchunk // cps),
        in_specs=[fwd(rows_f), fwd(rows_g), bwd(rows_f), bwd(rows_g)],
        out_specs=_scan_out_specs(sq, rdim),
        out_shape=[one, one],
        scratch_shapes=[pltpu.VMEM((2, heads // 2, 2 * hdim, 2 * hdim), F32)],
        compiler_params=_params(2),
        name="rwkv_scan",
    )(f, g, f, g)


def _gdn_intra_body(p_ref, pp_ref, pn_ref, ab_ref, cw_ref, bd_ref, f_ref, ga_ref, gb_ref, *, sq, heads, hdim, cps):
    c = CHUNK
    gdim = heads * hdim
    scale = hdim ** -0.5
    first, last = sq.edges(cps * c)
    proj = p_ref[...]
    width = cw_ref.shape[0]
    half = width // 2
    acc = proj * cw_ref[half:half + 1, :]
    for j in range(width):
        if j != half:
            acc = acc + _shift_rows(proj, pp_ref, pn_ref, j - half, first, last) * cw_ref[j:j + 1, :]
    y = acc * _sigmoid(acc)
    q_blk = y[:, :gdim]
    k_blk = y[:, gdim:2 * gdim]
    q_blk = q_blk * lax.rsqrt(_segsum(q_blk * q_blk, bd_ref) + RMS_EPS)
    k_blk = k_blk * lax.rsqrt(_segsum(k_blk * k_blk, bd_ref) + RMS_EPS)
    v_blk = y[:, 2 * gdim:]
    sl = [slice(h * hdim, (h + 1) * hdim) for h in range(heads)]
    keys = []
    beta, gc, g_last, gc_row, decay, kb, kq, kt_h, vb, strict_of, eye = {}, {}, {}, {}, {}, {}, {}, {}, {}, {}, None
    for j in range(cps):
        rows = slice(j * c, (j + 1) * c)
        ab = ab_ref[rows, :]
        tot_all = jnp.sum(ab, axis=0, keepdims=True)
        lane = lax.broadcasted_iota(jnp.int32, ab.shape, 1)
        k_all = k_blk[rows]
        k_tt = jnp.transpose(k_all)
        for d in range(2):
            incl, strict, eye = _causal_masks(d, c)
            strict_of[d] = strict
            gc_all = _dot_exact_lhs(jnp.where(incl, 1.0, 0.0).astype(BF16), ab)
            gc_all_t = jnp.transpose(gc_all)
            sub = lax.broadcasted_iota(jnp.int32, gc_all_t.shape, 0)
            for h in range(heads):
                key = (d, j, h)
                keys.append(key)
                pick_b = lane == d * heads + h
                pick_g = lane == 2 * heads + d * heads + h
                beta[key] = jnp.sum(jnp.where(pick_b, ab, 0.0), axis=1, keepdims=True)
                gc[key] = jnp.sum(jnp.where(pick_g, gc_all, 0.0), axis=1, keepdims=True)
                g_last[key] = jnp.sum(jnp.where(pick_g[:1], tot_all, 0.0), axis=1, keepdims=True)
                gc_row[key] = jnp.sum(jnp.where(sub == 2 * heads + d * heads + h, gc_all_t, 0.0),
                                      axis=0, keepdims=True)
                diff = gc[key] - gc_row[key]
                decay[key] = jnp.where(incl, jnp.exp(jnp.where(incl, diff, 0.0)), 0.0)
                kb[key] = k_all[:, sl[h]] * beta[key]
                kq[key] = jnp.concatenate([kb[key], q_blk[rows, sl[h]] * scale], axis=0)
                kt_h[key] = k_tt[sl[h]]
                vb[key] = v_blk[rows, sl[h]] * beta[key]
    m = {k: _dot(kq[k], kt_h[k], GDN_PASSES) for k in keys}
    lower = [jnp.where(strict_of[k[0]], m[k][:c] * decay[k], 0.0) for k in keys]
    t = dict(zip(keys, _tri_inverse_many(lower, eye)))
    e_gc = {k: jnp.exp(gc[k]) for k in keys}
    sol = {k: _dot(t[k], jnp.concatenate([vb[k], kb[k] * e_gc[k]], axis=1), GDN_PASSES) for k in keys}
    for k in keys:
        d, j, h = k
        of, oa, ob = j * (c + SUBLANES), j * 2 * c, j * (c + hdim)
        f_ref[d, h, of:of + c] = sol[k][:, :hdim]
        f_ref[d, h, of + c:of + c + SUBLANES] = jnp.broadcast_to(jnp.exp(g_last[k]), (SUBLANES, hdim))
        ga_ref[d, h, oa:oa + c] = sol[k][:, hdim:].astype(BF16)
        ga_ref[d, h, oa + c:oa + 2 * c] = (kq[k][c:] * e_gc[k]).astype(BF16)
        gb_ref[d, h, ob:ob + c] = (m[k][c:] * decay[k]).astype(BF16)
        gb_ref[d, h, ob + c:ob + c + hdim] = (kt_h[k] * jnp.exp(g_last[k] - gc_row[k])).astype(BF16)


def _gdn_intra(sq, proj, ab, lw, heads, gdim, col_block):
    hdim = gdim // heads
    cps = _largest_divisor(sq.ncc, (GDN_INTRA_CHUNKS, 2, 1))
    tb = cps * CHUNK
    w = 3 * gdim
    consts = [lw["conv"], lw["bd_g"]]
    const = lambda shape: pl.BlockSpec(shape, lambda b, i: (0,) * len(shape))
    small = pl.BlockSpec((None, tb, LANES), lambda b, i: (b, i, 0))
    rows = (CHUNK + SUBLANES, 2 * CHUNK, CHUNK + hdim)
    widths = (hdim, hdim, CHUNK)
    dtypes = (F32, BF16, BF16)
    outs = [pl.BlockSpec((2, None, heads, cps * r, wd), lambda b, i: (0, b, 0, i, 0)) for r, wd in zip(rows, widths)]
    shapes = [jax.ShapeDtypeStruct((2, sq.B, heads, sq.nchunk * r, wd), dt) for r, wd, dt in zip(rows, widths, dtypes)]
    return pl.pallas_call(
        functools.partial(_gdn_intra_body, sq=sq, heads=heads, hdim=hdim, cps=cps),
        grid=(sq.B, sq.nchunk // cps),
        in_specs=[pl.BlockSpec((None, tb, w), lambda b, i: (b, i, col_block)),
                  sq.prev_rows(w, col_block, tb=tb), sq.next_rows(w, col_block, tb=tb), small]
        + [const(c.shape) for c in consts],
        out_specs=outs,
        out_shape=shapes,
        compiler_params=_params(2),
        name="gdn_intra",
    )(proj, proj, proj, ab, *consts)


def _gdn_scan_body(f0_ref, ga0_ref, gb0_ref, f1_ref, ga1_ref, gb1_ref, of_ref, ob_ref, s_ref, *, heads, hdim, cps):
    @pl.when(pl.program_id(1) == 0)
    def _():
        s_ref[...] = jnp.zeros_like(s_ref)

    c = CHUNK
    fs, gas, gbs, outs = (f0_ref, f1_ref), (ga0_ref, ga1_ref), (gb0_ref, gb1_ref), (of_ref, ob_ref)
    chains = [(d, h) for d in range(2) for h in range(heads)]
    dot = functools.partial(jnp.dot, preferred_element_type=F32)
    st = {ch: s_ref[ch[0], ch[1]] for ch in chains}
    for step in range(cps):
        sub = (step, cps - 1 - step)
        of = [(c + SUBLANES) * sub[d] for d in range(2)]
        oa = [2 * c * sub[d] for d in range(2)]
        ob = [(c + hdim) * sub[d] for d in range(2)]
        m = {(d, h): dot(gas[d][h, oa[d]:oa[d] + 2 * c, :], st[d, h].astype(BF16)) for d, h in chains}
        vn = {(d, h): (fs[d][h, of[d]:of[d] + c, :] - m[d, h][:c]).astype(BF16) for d, h in chains}
        for d, h in chains:
            outs[d][sub[d] * c:(sub[d] + 1) * c, h * hdim:(h + 1) * hdim] = (
                m[d, h][c:] + dot(gbs[d][h, ob[d]:ob[d] + c, :], vn[d, h]))
            st[d, h] = (st[d, h] * fs[d][h, of[d] + c:of[d] + c + 1, :]
                        + dot(gbs[d][h, ob[d] + c:ob[d] + c + hdim, :], vn[d, h]))
    for d, h in chains:
        s_ref[d, h] = st[d, h]


def _gdn_scan(sq, f, ga, gb, heads):
    hdim = f.shape[-1]
    gdim = heads * hdim
    cps = _scan_chunks_per_step(sq)
    specs = []
    for arr in (f, ga, gb):
        specs.append(_scan_specs(sq, heads, arr.shape[3] // sq.nchunk, arr.shape[4]))
    one = jax.ShapeDtypeStruct((sq.B, sq.S, gdim), F32)
    return pl.pallas_call(
        functools.partial(_gdn_scan_body, heads=heads, hdim=hdim, cps=cps),
        grid=(sq.B, sq.nchunk // cps),
        in_specs=[s[0] for s in specs] + [s[1] for s in specs],
        out_specs=_scan_out_specs(sq, gdim),
        out_shape=[one, one],
        scratch_shapes=[pltpu.VMEM((2, heads, hdim, hdim), F32)],
        compiler_params=_params(2),
        name="gdn_scan",
    )(f, ga, gb, f, ga, gb)


def _rope(x, cos, sin_signed):
    n = x.shape[-1]
    lane = lax.broadcasted_iota(jnp.int32, x.shape, 1)
    partner = jnp.where((lane & 1) == 0, pltpu.roll(x, n - 1, axis=1), pltpu.roll(x, 1, axis=1))
    return x * cos + partner * sin_signed


def _attn_prep_body(q_ref, kv_ref, cs_ref, qn_ref, kn_ref, bd_ref, qo_ref, ko_ref, vo_ref,
                    *, q_heads, kv_heads, hdim):
    kvd = kv_heads * hdim
    cos = cs_ref[:, :kvd]
    sin = cs_ref[:, kvd:]
    reps = q_heads // kv_heads
    cos_q = jnp.concatenate([cos] * reps, axis=1)
    sin_q = jnp.concatenate([sin] * reps, axis=1)
    inv_n = 1.0 / hdim
    q = q_ref[...]
    q = q * lax.rsqrt(_segsum(q * q, bd_ref) * inv_n + RMS_EPS) * qn_ref[...]
    q = _rope(q, cos_q, sin_q) * (hdim ** -0.5)
    kv = kv_ref[...]
    k = kv[:, :kvd]
    kbd = bd_ref[:kvd, :kvd]
    hi, lo = _split2(k * k)
    ms = (jnp.dot(hi, kbd, preferred_element_type=F32) + jnp.dot(lo, kbd, preferred_element_type=F32)) * inv_n
    k = k * lax.rsqrt(ms + RMS_EPS) * kn_ref[...]
    k = _rope(k, cos, sin)
    v = kv[:, kvd:]
    for h in range(q_heads):
        qo_ref[h] = q[:, h * hdim:(h + 1) * hdim].astype(BF16)
    k_t = jnp.transpose(k)
    for h in range(kv_heads):
        ko_ref[h] = k_t[h * hdim:(h + 1) * hdim].astype(BF16)
        vh = v[:, h * hdim:(h + 1) * hdim]
        vo_ref[h] = jnp.concatenate([vh, jnp.ones_like(vh)], axis=1).astype(BF16)


def _attn_prep(sq, proj, cs_tab, lw, q_heads, kv_heads, hdim, q_col_block, kv_col_block):
    qd, kvd = q_heads * hdim, kv_heads * hdim
    consts = [lw["q_norm"], lw["k_norm"], lw["bd_r"]]
    k_spec = pl.BlockSpec((None, kv_heads, hdim, sq.TB), lambda b, j: (b, 0, 0, j))
    return pl.pallas_call(
        functools.partial(_attn_prep_body, q_heads=q_heads, kv_heads=kv_heads, hdim=hdim),
        grid=sq.grid,
        in_specs=[sq.rows(qd, q_col_block), sq.rows(2 * kvd, kv_col_block),
                  pl.BlockSpec((sq.TB, 2 * kvd), lambda b, j: (j, 0))]
        + [sq.const(c.shape) for c in consts],
        out_specs=[sq.heads(q_heads, hdim), k_spec, sq.heads(kv_heads, 2 * hdim)],
        out_shape=[jax.ShapeDtypeStruct((sq.B, q_heads, sq.S, hdim), BF16),
                   jax.ShapeDtypeStruct((sq.B, kv_heads, hdim, sq.S), BF16),
                   jax.ShapeDtypeStruct((sq.B, kv_heads, sq.S, 2 * hdim), BF16)],
        compiler_params=_params(2),
        name="attn_prep",
    )(proj, proj, cs_tab, *consts)


def _attn_body(q_ref, kt_ref, v_ref, o_ref, s_ref, p_ref, *, group, n_ctx_qblocks, n_ctx_keys):
    tq, hdim = q_ref.shape[1], q_ref.shape[2]
    n_keys = kt_ref.shape[1]

    def attend(nk):
        tiles = range(0, nk, KEY_BLOCK)
        for g in range(group):
            q = q_ref[g]
            rows = slice(g * tq, (g + 1) * tq)
            mx = None
            for t in tiles:
                s = jnp.dot(q, kt_ref[:, t:t + KEY_BLOCK], preferred_element_type=F32)
                s_ref[:, t:t + KEY_BLOCK] = s
                for c0 in range(0, KEY_BLOCK, LANES):
                    part = s[:, c0:c0 + LANES]
                    mx = part if mx is None else jnp.maximum(mx, part)
            m = jnp.broadcast_to(jnp.max(mx, axis=1, keepdims=True), (tq, LANES))
            for t in tiles:
                for c0 in range(t, t + KEY_BLOCK, LANES):
                    p_ref[rows, c0:c0 + LANES] = jnp.exp(s_ref[:, c0:c0 + LANES] - m).astype(BF16)
        acc = jnp.dot(p_ref[:, 0:nk], v_ref[0:nk, :], preferred_element_type=F32)
        out = acc[:, :hdim] / acc[:, hdim:]
        for g in range(group):
            o_ref[:, g * hdim:(g + 1) * hdim] = out[g * tq:(g + 1) * tq, :]

    is_ctx = pl.program_id(2) < n_ctx_qblocks

    @pl.when(is_ctx)
    def _():
        attend(n_ctx_keys)

    @pl.when(jnp.logical_not(is_ctx))
    def _():
        attend(n_keys)


def _attention(sq, q, kt, v):
    b, q_heads, s, hdim = q.shape
    kv_heads = kt.shape[1]
    group = q_heads // kv_heads
    tq = _largest_divisor(sq.ctx, (128, 64))
    assert sq.ctx % KEY_BLOCK == 0 and s % KEY_BLOCK == 0
    return pl.pallas_call(
        functools.partial(_attn_body, group=group, n_ctx_qblocks=sq.ctx // tq, n_ctx_keys=sq.ctx),
        grid=(b, kv_heads, s // tq),
        in_specs=[pl.BlockSpec((None, group, tq, hdim), lambda bi, g, i: (bi, g, i, 0)),
                  pl.BlockSpec((None, None, hdim, s), lambda bi, g, i: (bi, g, 0, 0)),
                  pl.BlockSpec((None, None, s, 2 * hdim), lambda bi, g, i: (bi, g, 0, 0))],
        out_specs=pl.BlockSpec((None, tq, group * hdim), lambda bi, g, i: (bi, i, g)),
        out_shape=jax.ShapeDtypeStruct((b, s, q_heads * hdim), F32),
        scratch_shapes=[pltpu.VMEM((tq, s), F32), pltpu.VMEM((group * tq, s), BF16)],
        compiler_params=_params(3),
        name="attention",
    )(q, kt, v)


def _merge_body(x_ref, rf_ref, rb_ref, bonus_ref, rg_ref, gf_ref, gb_ref, z_ref, yc_ref, gate_ref, mod_ref,
                lnw_ref, lnb_ref, bdr_ref, gn_ref, bdg_ref, wa_ref, wb_ref, wc_ref, wo_ref, o_ref,
                *, gate_row, r_hdim, g_hdim):
    d = x_ref.shape[-1]
    o = rf_ref[...] + rb_ref[...]
    mean = _segsum(o, bdr_ref) * (1.0 / r_hdim)
    cen = o - mean
    var = _segsum(cen * cen, bdr_ref) * (1.0 / r_hdim)
    ya = (cen * lax.rsqrt(var + RWKV_LNX_EPS) * lnw_ref[...] + lnb_ref[...] + bonus_ref[...]) * rg_ref[...]
    o = gf_ref[...] + gb_ref[...]
    ms = _segsum(o * o, bdg_ref) * (1.0 / g_hdim)
    z = z_ref[...]
    yb = o * lax.rsqrt(ms + RMS_EPS) * gn_ref[...] * (z * _sigmoid(z))
    m = gate_ref[:, :d] * _dot(ya, wa_ref[...])
    m = m + gate_ref[:, d:2 * d] * _dot(yb, wb_ref[...])
    m = m + gate_ref[:, 2 * d:] * _dot(yc_ref[...], wc_ref[...])
    y = _dot(m, wo_ref[...])
    o_ref[...] = x_ref[...] + mod_ref[gate_row:gate_row + 1, :] * y


def _merge(sq, xs, o_rf, o_rb, bonus, r_gate, o_gf, o_gb, proj, z_col_block, yc, gates, modtab, lw,
           r_heads, g_heads):
    d = xs.shape[-1]
    rdim, gdim = o_rf.shape[-1], o_gf.shape[-1]
    consts = [lw["lnx_w"], lw["lnx_b"], lw["bd_r"], lw["gdn_norm"], lw["bd_g"],
              lw["w_up_a"], lw["w_up_b"], lw["w_up_c"], lw["w_out"]]
    return pl.pallas_call(
        functools.partial(_merge_body, gate_row=2, r_hdim=rdim // r_heads, g_hdim=gdim // g_heads),
        grid=sq.grid,
        in_specs=[sq.rows(d)] + [sq.rows(rdim)] * 4 + [sq.rows(gdim)] * 2
        + [sq.rows(gdim, z_col_block), sq.rows(yc.shape[-1]), sq.rows(3 * d), sq.mod(d)]
        + [sq.const(c.shape) for c in consts],
        out_specs=sq.rows(d),
        out_shape=jax.ShapeDtypeStruct(xs.shape, F32),
        compiler_params=_params(2),
        name="merge",
    )(xs, o_rf, o_rb, bonus, r_gate, o_gf, o_gb, proj, yc, gates, modtab, *consts)


def _ffn_body(x_ref, g_ref, mod_ref, w1_ref, w3_ref, w2_ref, o_ref, *, h_chunk, ctx_len):
    rows = x_ref.shape[0]
    row = pl.program_id(1) * rows + lax.broadcasted_iota(jnp.int32, (rows, 1), 0)
    is_ctx = row < ctx_len
    pick = lambda i: jnp.where(is_ctx, mod_ref[0, i:i + 1, :], mod_ref[1, i:i + 1, :])
    x = x_ref[...]
    y = x * lax.rsqrt(jnp.mean(x * x, axis=-1, keepdims=True) + RMS_EPS) * g_ref[...]
    h = (y * (1.0 + pick(4)) + pick(3)).astype(BF16)
    hidden = w1_ref.shape[1]
    acc = jnp.zeros(x.shape, F32)
    for c0 in range(0, hidden, h_chunk):
        a = jnp.dot(h, w1_ref[:, c0:c0 + h_chunk], preferred_element_type=F32)
        b = jnp.dot(h, w3_ref[:, c0:c0 + h_chunk], preferred_element_type=F32)
        t = (a * _sigmoid(a) * b).astype(BF16)
        acc = acc + jnp.dot(t, w2_ref[c0:c0 + h_chunk, :], preferred_element_type=F32)
    o_ref[...] = x + pick(5) * acc


def _ffn(sq, xs, g, modtab, lw):
    d = xs.shape[-1]
    consts = [lw["ffn_w1"], lw["ffn_w3"], lw["ffn_w2"]]
    h_chunk = _largest_divisor(lw["ffn_w1"].shape[1], (512, 256, 128))
    tb = _largest_divisor(sq.S, FFN_ROWS)
    rows = pl.BlockSpec((None, tb, d), lambda b, j: (b, j, 0))
    const = lambda shape: pl.BlockSpec(shape, lambda b, j: (0,) * len(shape))
    return pl.pallas_call(
        functools.partial(_ffn_body, h_chunk=h_chunk, ctx_len=sq.ctx),
        grid=(sq.B, sq.S // tb),
        in_specs=[rows, const((1, d)), pl.BlockSpec((None, 2, 6, d), lambda b, j: (b, 0, 0, 0))]
        + [const(c.shape) for c in consts],
        out_specs=rows,
        out_shape=jax.ShapeDtypeStruct(xs.shape, F32),
        compiler_params=_params(2),
        name="ffn",
    )(xs, g.reshape(1, d), modtab, *consts)


def _block_diag_ones(n, seg):
    idx = jnp.arange(n) // seg
    return (idx[:, None] == idx[None, :]).astype(BF16)


def _block_diag2(m):
    z = jnp.zeros_like(m[0])
    return jnp.concatenate([jnp.concatenate([m[0], z], axis=1), jnp.concatenate([z, m[1]], axis=1)], axis=0)


def _pad_to(x, axis, size):
    pad = [(0, 0)] * x.ndim
    pad[axis] = (0, size - x.shape[axis])
    return jnp.pad(x, pad)


def _rope_table(ctx_len, seq_len, hdim, kv_heads):
    rows = seq_len // GRID_W
    row = jnp.repeat(jnp.arange(rows), GRID_W).astype(F32)
    col = jnp.tile(jnp.arange(GRID_W), rows).astype(F32)
    half = hdim // 2
    inv = ROPE_THETA ** (-jnp.arange(0, half, 2, dtype=F32) / half)
    ang = jnp.concatenate([row[:, None] * inv, col[:, None] * inv], axis=-1)
    cos = jnp.repeat(jnp.cos(ang), 2, axis=1)
    sin = jnp.repeat(jnp.sin(ang), 2, axis=1) * jnp.tile(jnp.array([-1.0, 1.0], F32), half)
    cos = jnp.concatenate([jnp.ones((ctx_len, hdim), F32), cos], axis=0)
    sin = jnp.concatenate([jnp.zeros((ctx_len, hdim), F32), sin], axis=0)
    return jnp.concatenate([jnp.tile(cos, (1, kv_heads)), jnp.tile(sin, (1, kv_heads))], axis=1)


def kernel(x, c, ctx, c_ctx, ada_w, ada_b, norm1, norm2, w_in, rwkv_mu_x, rwkv_mu_rkv, rwkv_w0, rwkv_w1, rwkv_w2, rwkv_a0, rwkv_a1, rwkv_a2, rwkv_g1, rwkv_g2, rwkv_k_k, rwkv_k_a, rwkv_r_k, rwkv_lnx_w, rwkv_lnx_b, gdn_conv, gdn_w_alpha, gdn_dt_bias, gdn_A_log, gdn_w_beta, gdn_norm, attn_q_norm, attn_k_norm, w_up_a, w_up_b, w_up_c, w_gate, b_gate, w_out, ffn_w1, ffn_w3, ffn_w2, final_norm):
    batch, seq_len, d = x.shape
    ctx_len = ctx.shape[1]
    depth = ada_w.shape[0]
    sq = _Seq(batch, ctx_len, seq_len)

    r_heads, r_hdim = rwkv_r_k.shape[1], rwkv_r_k.shape[2]
    rdim = r_heads * r_hdim
    g_heads, g_hdim = gdn_w_alpha.shape[-1], gdn_norm.shape[-1]
    gdim = g_heads * g_hdim
    a_hdim = attn_q_norm.shape[-1]
    qd = w_up_c.shape[1]
    q_heads = qd // a_hdim
    kvd = (w_in.shape[-1] - 3 * rdim - 4 * gdim - qd) // 2
    kv_heads = kvd // a_hdim
    assert rdim == gdim == qd and r_hdim == a_hdim, "lane-segment constants are shared between mixers"
    assert (3 * rdim) % (3 * gdim) == 0 and (3 * rdim + 3 * gdim) % gdim == 0
    gdn_col = (3 * rdim) // (3 * gdim)
    z_col = (3 * rdim + 3 * gdim) // gdim
    q_col = (3 * rdim + 4 * gdim) // qd
    assert (3 * rdim + 4 * gdim + qd) % (2 * kvd) == 0
    kv_col = (3 * rdim + 4 * gdim + qd) // (2 * kvd)

    bd_r = _block_diag_ones(rdim, r_hdim)
    bd_g = _block_diag_ones(gdim, g_hdim)
    cs_tab = _rope_table(ctx_len, seq_len, a_hdim, kv_heads)
    n_beta = 2 * g_heads
    assert 2 * n_beta <= LANES

    cond = jnp.concatenate([c, c_ctx[None, :]], axis=0)
    cond = _pad_to(cond, 0, -(-(batch + 1) // SUBLANES) * SUBLANES)

    xs = jnp.concatenate([ctx, x], axis=1)
    for l in range(depth):
        g1w = _pad_to(rwkv_g1[l], 1, 2 * LANES)
        lw = {
            "mu": rwkv_mu_x[l], "mu_rkv": rwkv_mu_rkv[l],
            "w1": jnp.concatenate([rwkv_w1[l, 0], rwkv_w1[l, 1]], axis=1).astype(BF16),
            "w2": _block_diag2(rwkv_w2[l]).astype(BF16),
            "w0": rwkv_w0[l].reshape(1, 2 * rdim),
            "a1": jnp.concatenate([rwkv_a1[l, 0], rwkv_a1[l, 1]], axis=1).astype(BF16),
            "a2": _block_diag2(rwkv_a2[l]).astype(BF16),
            "a0": rwkv_a0[l].reshape(1, 2 * rdim),
            "g1": g1w.astype(BF16),
            "g2": _pad_to(rwkv_g2[l], 0, 2 * LANES).astype(BF16),
            "wab": _pad_to(jnp.concatenate([gdn_w_beta[l, 0], gdn_w_beta[l, 1],
                                            gdn_w_alpha[l, 0], gdn_w_alpha[l, 1]], axis=1), 1, LANES).astype(BF16),
            "abb": _pad_to(jnp.concatenate([jnp.zeros((n_beta,), F32), gdn_dt_bias[l].reshape(-1)]), 0, LANES).reshape(1, LANES),
            "alog": _pad_to(jnp.concatenate([jnp.zeros((n_beta,), F32), gdn_A_log[l].reshape(-1)]), 0, LANES).reshape(1, LANES),
            "n_beta": n_beta,
            "k_k": rwkv_k_k[l].reshape(1, rdim), "k_a": rwkv_k_a[l].reshape(1, rdim),
            "r_k": rwkv_r_k[l].reshape(1, rdim),
            "lnx_w": rwkv_lnx_w[l].reshape(1, rdim), "lnx_b": rwkv_lnx_b[l].reshape(1, rdim),
            "bd_r": bd_r, "bd_g": bd_g,
            "conv": jnp.transpose(gdn_conv[l]),
            "gdn_norm": jnp.tile(gdn_norm[l], g_heads).reshape(1, gdim),
            "q_norm": jnp.tile(attn_q_norm[l], q_heads).reshape(1, qd),
            "k_norm": jnp.tile(attn_k_norm[l], kv_heads).reshape(1, kvd),
            "w_up_a": w_up_a[l].astype(BF16), "w_up_b": w_up_b[l].astype(BF16),
            "w_up_c": w_up_c[l].astype(BF16), "w_out": w_out[l].astype(BF16),
            "ffn_w1": ffn_w1[l].astype(BF16), "ffn_w3": ffn_w3[l].astype(BF16), "ffn_w2": ffn_w2[l].astype(BF16),
        }
        mod = _matmul(cond, ada_w[l].astype(BF16), bias=ada_b[l], pre_act="silu")
        mod_x = mod[:batch].reshape(batch, 6, d)
        mod_c = jnp.broadcast_to(mod[batch].reshape(1, 6, d), (batch, 6, d))
        modtab = jnp.stack([mod_c, mod_x], axis=1)

        h, proj, gates = _pre(sq, xs, norm1[l], modtab, w_in[l].astype(BF16), w_gate[l].astype(BF16), b_gate[l])
        lw_dec, a_iclr, g_out, ab = _lora(sq, h, lw)

        rf, rg, bonus = _rwkv_intra(sq, proj, a_iclr, lw_dec, lw, r_heads)
        o_rf, o_rb = _rwkv_scan(sq, rf, rg, r_heads)

        gf, gga, ggb = _gdn_intra(sq, proj, ab, lw, g_heads, gdim, gdn_col)
        o_gf, o_gb = _gdn_scan(sq, gf, gga, ggb, g_heads)

        aq, ak, av = _attn_prep(sq, proj, cs_tab, lw, q_heads, kv_heads, a_hdim, q_col, kv_col)
        yc = _attention(sq, aq, ak, av)

        xs = _merge(sq, xs, o_rf, o_rb, bonus, g_out, o_gf, o_gb, proj, z_col, yc, gates, modtab, lw,
                    r_heads, g_heads)
        xs = _ffn(sq, xs, norm2[l], modtab, lw)
    return _final_norm(sq, xs, final_norm)
```

```python
import functools

import jax
import jax.numpy as jnp
from jax import lax
from jax.experimental import pallas as pl
from jax.experimental.pallas import tpu as pltpu

F32 = jnp.float32
BF16 = jnp.bfloat16

RMS_EPS = 1e-6
RWKV_LNX_EPS = 64e-5
ROPE_THETA = 10000.0
GRID_W = 64

SUBLANES = 8
LANES = 128
CHUNK = 64
MAX_TOKEN_BLOCK = 256
MAX_MM_ROWS = 512
FFN_ROWS = (544, 512, 256, 128)
KEY_BLOCK = 256
RWKV_PASSES = 1
GDN_PASSES = 1
INV_PASSES = 1
RWKV_INTRA_CHUNKS = 2
GDN_INTRA_CHUNKS = 4
SCAN_CHUNKS = 4
VMEM_LIMIT = 56 * 1024 * 1024


def _sigmoid(x):
    return 1.0 / (1.0 + jnp.exp(-x))


def _softplus(x):
    return jnp.maximum(x, 0.0) + jnp.log(1.0 + jnp.exp(-jnp.abs(x)))


def _split2(x):
    hi = x.astype(BF16)
    lo = (x - hi.astype(F32)).astype(BF16)
    return hi, lo


def _mm(a, b, dims, passes):
    d = functools.partial(lax.dot_general, dimension_numbers=(dims, ((), ())), preferred_element_type=F32)
    if passes == 1:
        return d(a.astype(BF16), b.astype(BF16))
    ah, al = _split2(a)
    bh, bl = _split2(b)
    return d(ah, bh) + (d(ah, bl) + d(al, bh))


def _dot(a, b, passes=1):
    return _mm(a, b, ((1,), (0,)), passes)


def _split3(x):
    hi = x.astype(BF16)
    r1 = x - hi.astype(F32)
    mid = r1.astype(BF16)
    lo = (r1 - mid.astype(F32)).astype(BF16)
    return hi, mid, lo


def _dot_exact_lhs(m01, x):
    d = functools.partial(jnp.dot, preferred_element_type=F32)
    hi, mid, lo = _split3(x)
    return d(m01, hi) + (d(m01, mid) + d(m01, lo))


def _segsum(x, bd_ref):
    d = functools.partial(jnp.dot, preferred_element_type=F32)
    hi, lo = _split2(x)
    bd = bd_ref[...]
    return d(hi, bd) + d(lo, bd)


def _row_iota(shape):
    return lax.broadcasted_iota(jnp.int32, shape, 0)


def _shift_rows(x, prev_ref, next_ref, k, first, last):
    n = x.shape[0]
    y = pltpu.roll(x, (-k) % n, axis=0)
    row = _row_iota((SUBLANES, x.shape[1]))
    if k < 0:
        tile = y[:SUBLANES]
        for i in range(-k):
            edge = prev_ref[SUBLANES + k + i:SUBLANES + k + i + 1, :]
            edge = jnp.where(first, 0.0, edge)
            tile = jnp.where(row == i, edge, tile)
        return jnp.concatenate([tile, y[SUBLANES:]], axis=0)
    tile = y[n - SUBLANES:]
    for i in range(k):
        edge = next_ref[i:i + 1, :]
        edge = jnp.where(last, 0.0, edge)
        tile = jnp.where(row == SUBLANES - k + i, edge, tile)
    return jnp.concatenate([y[:n - SUBLANES], tile], axis=0)


def _largest_divisor(n, candidates):
    for c in candidates:
        if n % c == 0:
            return c
    raise ValueError(f"no block size among {candidates} divides {n}")


class _Seq:
    def __init__(self, batch, ctx_len, seq_len):
        self.B = batch
        self.ctx = ctx_len
        self.S = ctx_len + seq_len
        self.TB = _largest_divisor(ctx_len, (MAX_TOKEN_BLOCK, 128, 64))
        assert seq_len % self.TB == 0 and self.TB % CHUNK == 0
        self.ncb = ctx_len // self.TB
        self.nblk = self.S // self.TB
        self.ncc = ctx_len // CHUNK
        self.nchunk = self.S // CHUNK
        self.grid = (batch, self.nblk)

    def rows(self, width, col_block=0):
        return pl.BlockSpec((None, self.TB, width), lambda b, j: (b, j, col_block))

    def rows2(self, width):
        return pl.BlockSpec((2, None, self.TB, width), lambda b, j: (0, b, j, 0))

    def heads(self, n_heads, width):
        return pl.BlockSpec((None, n_heads, self.TB, width), lambda b, j: (b, 0, j, 0))

    def prev_rows(self, width, col_block=0, tb=None):
        per = (tb or self.TB) // SUBLANES
        return pl.BlockSpec((None, SUBLANES, width),
                            lambda b, j: (b, jnp.maximum(j * per - 1, 0), col_block))

    def next_rows(self, width, col_block=0, tb=None):
        per = (tb or self.TB) // SUBLANES
        top = self.S // SUBLANES - 1
        return pl.BlockSpec((None, SUBLANES, width),
                            lambda b, j: (b, jnp.minimum((j + 1) * per, top), col_block))

    def const(self, shape):
        zeros = (0,) * len(shape)
        return pl.BlockSpec(shape, lambda b, j: zeros)

    def mod(self, d_model):
        ncb = self.ncb
        return pl.BlockSpec((None, None, 6, d_model),
                            lambda b, j: (b, (j >= ncb).astype(jnp.int32), 0, 0))

    def edges(self, tb=None):
        tb = tb or self.TB
        j = pl.program_id(1)
        ncb, nblk = self.ctx // tb, self.S // tb
        first = (j == 0) | (j == ncb)
        last = (j == ncb - 1) | (j == nblk - 1)
        return first, last


def _params(n_axes):
    return pltpu.CompilerParams(dimension_semantics=("arbitrary",) * n_axes,
                                vmem_limit_bytes=VMEM_LIMIT)


def _mm_body(*refs, act, pre_act, n_chunk, has_bias):
    if has_bias:
        x_ref, w_ref, b_ref, o_ref = refs
    else:
        x_ref, w_ref, o_ref = refs
        b_ref = None
    x = x_ref[...]
    if pre_act == "silu":
        x = x * _sigmoid(x)
    xb = x.astype(BF16)
    n = o_ref.shape[-1]
    for n0 in range(0, n, n_chunk):
        y = jnp.dot(xb, w_ref[:, n0:n0 + n_chunk], preferred_element_type=F32)
        if b_ref is not None:
            y = y + b_ref[:, n0:n0 + n_chunk]
        if act == "sigmoid":
            y = _sigmoid(y)
        o_ref[:, n0:n0 + n_chunk] = y.astype(o_ref.dtype)


def _matmul(x, w, bias=None, act=None, pre_act=None, out_dtype=F32):
    m, k = x.shape
    n = w.shape[1]
    tm = m if m <= MAX_MM_ROWS else _largest_divisor(m, (MAX_MM_ROWS, 256, 128, 64, 32, 16, 8))
    n_chunk = _largest_divisor(n, (512, 256, 128))
    in_specs = [pl.BlockSpec((tm, k), lambda i: (i, 0)),
                pl.BlockSpec((k, n), lambda i: (0, 0))]
    args = [x, w]
    if bias is not None:
        in_specs.append(pl.BlockSpec((1, n), lambda i: (0, 0)))
        args.append(bias.reshape(1, n))
    return pl.pallas_call(
        functools.partial(_mm_body, act=act, pre_act=pre_act, n_chunk=n_chunk,
                          has_bias=bias is not None),
        grid=(m // tm,),
        in_specs=in_specs,
        out_specs=pl.BlockSpec((tm, n), lambda i: (i, 0)),
        out_shape=jax.ShapeDtypeStruct((m, n), out_dtype),
        compiler_params=_params(1),
        name="matmul",
    )(*args)


def _pre_body(x_ref, g_ref, mod_ref, win_ref, wg_ref, bg_ref, h_ref, p_ref, gate_ref, *, n_chunk, ctx_len):
    rows = x_ref.shape[0]
    row = pl.program_id(1) * rows + lax.broadcasted_iota(jnp.int32, (rows, 1), 0)
    is_ctx = row < ctx_len
    pick = lambda i: jnp.where(is_ctx, mod_ref[0, i:i + 1, :], mod_ref[1, i:i + 1, :])
    x = x_ref[...]
    y = x * lax.rsqrt(jnp.mean(x * x, axis=-1, keepdims=True) + RMS_EPS) * g_ref[...]
    h = y * (1.0 + pick(1)) + pick(0)
    h_ref[...] = h
    hb = h.astype(BF16)
    for n0 in range(0, p_ref.shape[-1], n_chunk):
        p_ref[:, n0:n0 + n_chunk] = jnp.dot(hb, win_ref[:, n0:n0 + n_chunk], preferred_element_type=F32)
    for n0 in range(0, gate_ref.shape[-1], n_chunk):
        z = jnp.dot(hb, wg_ref[:, n0:n0 + n_chunk], preferred_element_type=F32) + bg_ref[:, n0:n0 + n_chunk]
        gate_ref[:, n0:n0 + n_chunk] = _sigmoid(z).astype(gate_ref.dtype)


def _pre(sq, xs, g, modtab, w_in, w_gate, b_gate):
    d = xs.shape[-1]
    n_in, n_gate = w_in.shape[1], w_gate.shape[1]
    n_chunk = _largest_divisor(n_in, (256, 128))
    assert n_gate % n_chunk == 0
    tb = _largest_divisor(sq.S, FFN_ROWS)
    rows = lambda width: pl.BlockSpec((None, tb, width), lambda b, j: (b, j, 0))
    const = lambda shape: pl.BlockSpec(shape, lambda b, j: (0,) * len(shape))
    resident = lambda shape: pl.BlockSpec(shape, lambda b, j: (0,) * len(shape), pipeline_mode=pl.Buffered(1))
    return pl.pallas_call(
        functools.partial(_pre_body, n_chunk=n_chunk, ctx_len=sq.ctx),
        grid=(sq.B, sq.S // tb),
        in_specs=[rows(d), const((1, d)), pl.BlockSpec((None, 2, 6, d), lambda b, j: (b, 0, 0, 0)),
                  resident(w_in.shape), resident(w_gate.shape), const((1, n_gate))],
        out_specs=[rows(d), rows(n_in), rows(n_gate)],
        out_shape=[jax.ShapeDtypeStruct(xs.shape, F32),
                   jax.ShapeDtypeStruct(xs.shape[:2] + (n_in,), F32),
                   jax.ShapeDtypeStruct(xs.shape[:2] + (n_gate,), BF16)],
        compiler_params=_params(2),
        name="pre",
    )(xs, g.reshape(1, d), modtab, w_in, w_gate, b_gate.reshape(1, n_gate))


def _final_norm_body(x_ref, g_ref, o_ref):
    x = x_ref[...]
    y = x * lax.rsqrt(jnp.mean(x * x, axis=-1, keepdims=True) + RMS_EPS)
    o_ref[...] = y * g_ref[...]


def _final_norm(sq, xs, g):
    d = xs.shape[-1]
    ncb = sq.ncb
    return pl.pallas_call(
        _final_norm_body,
        grid=(sq.B, sq.nblk - ncb),
        in_specs=[pl.BlockSpec((None, sq.TB, d), lambda b, j: (b, j + ncb, 0)),
                  pl.BlockSpec((1, d), lambda b, j: (0, 0))],
        out_specs=pl.BlockSpec((None, sq.TB, d), lambda b, j: (b, j, 0)),
        out_shape=jax.ShapeDtypeStruct((sq.B, sq.S - sq.ctx, d), F32),
        compiler_params=_params(2),
        name="final_norm",
    )(xs, g.reshape(1, d))


def _lora_body(h_ref, hp_ref, hn_ref, wh_ref, wd_ref, w2_ref, w0_ref, a2_ref, a0_ref,
               g2_ref, abb_ref, alog_ref, lw_ref, a_ref, g_ref, ab_ref, *, sq, rdim, n_beta, splits):
    first, last = sq.edges()
    h = h_ref[...]
    nb = 0.5 * (_shift_rows(h, hp_ref, hn_ref, -1, first, last)
                + _shift_rows(h, hp_ref, hn_ref, 1, first, last))
    zh = _dot(h, wh_ref[...])
    zd = _dot(nb - h, wd_ref[...])
    s1, s2, s3 = splits
    wl = w0_ref[...] + _dot(jnp.tanh(zh[:, :s1] + zd[:, :s1]), w2_ref[...])
    lw = -(jnp.exp(-0.5) * _sigmoid(wl))
    lw_ref[0] = lw[:, :rdim]
    lw_ref[1] = lw[:, rdim:]
    a = _sigmoid(a0_ref[...] + _dot(zh[:, s1:s2] + zd[:, s1:s2], a2_ref[...]))
    a_ref[0] = a[:, :rdim]
    a_ref[1] = a[:, rdim:]
    g_ref[...] = _dot(_sigmoid(zh[:, s2:s3] + zd[:, s2:s3]), g2_ref[...])
    z = zh[:, s3:]
    col = lax.broadcasted_iota(jnp.int32, z.shape, 1)
    gl = -jnp.exp(alog_ref[...]) * _softplus(z + abb_ref[...])
    ab_ref[...] = jnp.where(col < n_beta, _sigmoid(z), gl)


def _lora(sq, h, lw):
    d = h.shape[-1]
    rdim = lw["w0"].shape[-1] // 2
    consts = [lw["wh"], lw["wd"], lw["w2"], lw["w0"], lw["a2"], lw["a0"],
              lw["g2"], lw["abb"], lw["alog"]]
    bsd = (sq.B, sq.S)
    return pl.pallas_call(
        functools.partial(_lora_body, sq=sq, rdim=rdim, n_beta=lw["n_beta"], splits=lw["splits"]),
        grid=sq.grid,
        in_specs=[sq.rows(d), sq.prev_rows(d), sq.next_rows(d)] + [sq.const(c.shape) for c in consts],
        out_specs=[sq.rows2(rdim), sq.rows2(rdim), sq.rows(rdim), sq.rows(LANES)],
        out_shape=[jax.ShapeDtypeStruct((2,) + bsd + (rdim,), F32),
                   jax.ShapeDtypeStruct((2,) + bsd + (rdim,), F32),
                   jax.ShapeDtypeStruct(bsd + (rdim,), F32),
                   jax.ShapeDtypeStruct(bsd + (LANES,), F32)],
        compiler_params=_params(2),
        name="lora",
    )(h, h, h, *consts)


def _causal_masks(d, c, reps=1):
    row = lax.broadcasted_iota(jnp.int32, (c, reps * c), 0)
    col = lax.broadcasted_iota(jnp.int32, (c, reps * c), 1) & (c - 1)
    delta = jnp.where(d == 0, row - col, col - row)
    eye = jnp.where(row == col, 1.0, 0.0).astype(F32)
    return delta >= 0, delta > 0, eye


def _stacked_mask(d, c, reps=1):
    row = lax.broadcasted_iota(jnp.int32, (2 * c, reps * c), 0)
    col = lax.broadcasted_iota(jnp.int32, (2 * c, reps * c), 1) & (c - 1)
    rr = row & (c - 1)
    delta = jnp.where(d == 0, rr - col, col - rr)
    return (delta > 0) | ((row >= c) & (delta == 0))


def _block_rows(x, w):
    left = lax.broadcasted_iota(jnp.int32, x.shape, 1) < w
    zero = jnp.zeros_like(x)
    return jnp.concatenate([jnp.where(left, x, zero), jnp.where(left, zero, x)], axis=0)


def _fold_rows(y):
    n = y.shape[0] // 2
    return y[:n] + y[n:]


def _tri_inverse_pairs(ls):
    n = ls[0].shape[0]
    row = lax.broadcasted_iota(jnp.int32, (n, 2 * n), 0)
    col = lax.broadcasted_iota(jnp.int32, (n, 2 * n), 1) & (n - 1)
    eye = jnp.where(row == col, 1.0, 0.0).astype(F32)
    same = (row >> 1) == (col >> 1)
    ts = [eye - jnp.where(same, l, 0.0) for l in ls]
    for k in range(2, n.bit_length()):
        off = ((row >> k) == (col >> k)) & ((row >> (k - 1)) != (col >> (k - 1)))
        tl = [_dot(t, _block_rows(jnp.where(off, l, 0.0), n), INV_PASSES) for t, l in zip(ts, ls)]
        ts = [t - _dot(x, _block_rows(t, n), INV_PASSES) for x, t in zip(tl, ts)]
    return ts


def _tri_inverse_many(ls, eye):
    n = ls[0].shape[0]
    row = lax.broadcasted_iota(jnp.int32, (n, n), 0)
    col = lax.broadcasted_iota(jnp.int32, (n, n), 1)
    same = (row >> 1) == (col >> 1)
    ts = [eye - jnp.where(same, l, 0.0) for l in ls]
    for k in range(2, n.bit_length()):
        off = ((row >> k) == (col >> k)) & ((row >> (k - 1)) != (col >> (k - 1)))
        tl = [_dot(t, jnp.where(off, l, 0.0), INV_PASSES) for t, l in zip(ts, ls)]
        ts = [t - _dot(x, t, INV_PASSES) for x, t in zip(tl, ts)]
    return ts


RWKV_TILES = 7

def _rwkv_intra_body(p_ref, pp_ref, pn_ref, a_ref, lw_ref, mu_ref, kk_ref, ka_ref, rk_ref, bd_ref,
                     f_ref, g_ref, bonus_ref, *, sq, heads, hdim, cps):
    c = CHUNK
    pw = 2 * hdim
    rdim = heads * hdim
    first, last = sq.edges(cps * c)
    proj = p_ref[...]
    nb = 0.5 * (_shift_rows(proj, pp_ref, pn_ref, -1, first, last)
                + _shift_rows(proj, pp_ref, pn_ref, 1, first, last))
    dlt = nb - proj
    r_all = proj[:, :rdim] + dlt[:, :rdim] * mu_ref[0:1, :]
    k_all = proj[:, rdim:2 * rdim] + dlt[:, rdim:2 * rdim] * mu_ref[1:2, :]
    v_all = proj[:, 2 * rdim:] + dlt[:, 2 * rdim:] * mu_ref[2:3, :]
    kn = k_all * kk_ref[...]
    kk_all = kn * lax.rsqrt(_segsum(kn * kn, bd_ref) + RMS_EPS)
    kd_all = [k_all * (1.0 + (a_ref[d] - 1.0) * ka_ref[...]) for d in range(2)]
    rk = r_all * rk_ref[...]
    bonus_ref[...] = (_segsum(rk * kd_all[0], bd_ref) + _segsum(rk * kd_all[1], bd_ref)) * v_all
    keys, a_p, r_p, v_p, bt_p, kt_p, ec_p, mask2 = [], {}, {}, {}, {}, {}, {}, {}
    for d in range(2):
        incl, _, _ = _causal_masks(d, c)
        m01 = jnp.where(incl, 1.0, 0.0).astype(BF16)
        mask2[d] = _stacked_mask(d, c, reps=2)
        for j in range(cps):
            rows = slice(j * c, (j + 1) * c)
            lw = lw_ref[d, rows, :]
            cum = _dot_exact_lhs(m01, lw)
            e_neg = jnp.exp(-cum)
            e_row = jnp.broadcast_to(jnp.exp(jnp.sum(lw, axis=0, keepdims=True)), lw.shape)
            a_t = -kk_all[rows] * jnp.exp(cum - lw)
            r_t = r_all[rows] * jnp.exp(cum)
            b_t = kk_all[rows] * a_ref[d, rows, :] * e_neg
            k_t = kd_all[d][rows] * e_neg
            v = v_all[rows]
            for p in range(heads // 2):
                key = (d, j, p)
                keys.append(key)
                lanes = slice(p * pw, (p + 1) * pw)
                a_p[key], r_p[key], v_p[key] = a_t[:, lanes], r_t[:, lanes], v[:, lanes]
                bt_p[key] = jnp.transpose(_block_rows(b_t[:, lanes], hdim))
                kt_p[key] = jnp.transpose(_block_rows(k_t[:, lanes], hdim))
                ec_p[key] = jnp.transpose(_block_rows(e_row[:, lanes], hdim))
    ar = {k: jnp.concatenate([a_p[k], r_p[k]], axis=0) for k in keys}
    mb = {k: jnp.where(mask2[k[0]], _dot(ar[k], bt_p[k], RWKV_PASSES), 0.0) for k in keys}
    mk = {k: jnp.where(mask2[k[0]], _dot(ar[k], kt_p[k], RWKV_PASSES), 0.0) for k in keys}
    v_bd = {k: _block_rows(v_p[k], hdim) for k in keys}
    x = {k: _dot(mk[k], v_bd[k], RWKV_PASSES) for k in keys}
    h0 = {k: _dot(_fold_rows(ec_p[k] * kt_p[k]), v_bd[k], RWKV_PASSES) for k in keys}
    t = dict(zip(keys, _tri_inverse_pairs([-mb[k][:c] for k in keys])))
    wt = {k: _dot(t[k], _block_rows(a_p[k], hdim), RWKV_PASSES) for k in keys}
    ut = {k: _dot(t[k], _block_rows(x[k][:c], hdim), RWKV_PASSES) for k in keys}
    for k in keys:
        d, j, p = k
        lanes = slice(p * pw, (p + 1) * pw)
        f_ref[d, j * c:(j + 1) * c, lanes] = _fold_rows(ec_p[k])
        tiles = (wt[k], r_p[k], mb[k][c:], _fold_rows(ec_p[k] * bt_p[k]), ut[k], x[k][c:], h0[k])
        for i, tile in enumerate(tiles):
            o = (j * RWKV_TILES + i) * c
            g_ref[d, o:o + c, lanes] = tile.astype(BF16)


def _rwkv_intra(sq, proj, a, lw_dec, lw, heads):
    rdim = a.shape[-1]
    hdim = rdim // heads
    assert hdim == CHUNK and heads % 2 == 0 and 2 * hdim == LANES, "pair tiles are [CHUNK, 128 lanes]"
    cps = _largest_divisor(sq.ncc, (RWKV_INTRA_CHUNKS, 1))
    rows = cps * CHUNK
    consts = [lw["mu_rkv"], lw["k_k"], lw["k_a"], lw["r_k"], lw["bd_r"]]
    const = lambda shape: pl.BlockSpec(shape, lambda b, i: (0,) * len(shape))
    per_dir = pl.BlockSpec((2, None, rows, rdim), lambda b, i: (0, b, i, 0))
    out_g = pl.BlockSpec((2, None, RWKV_TILES * rows, rdim), lambda b, i: (0, b, i, 0))
    return pl.pallas_call(
        functools.partial(_rwkv_intra_body, sq=sq, heads=heads, hdim=hdim, cps=cps),
        grid=(sq.B, sq.nchunk // cps),
        in_specs=[pl.BlockSpec((None, rows, 3 * rdim), lambda b, i: (b, i, 0)),
                  sq.prev_rows(3 * rdim, tb=rows), sq.next_rows(3 * rdim, tb=rows), per_dir, per_dir]
        + [const(c.shape) for c in consts],
        out_specs=[per_dir, out_g, pl.BlockSpec((None, rows, rdim), lambda b, i: (b, i, 0))],
        out_shape=[jax.ShapeDtypeStruct((2, sq.B, sq.S, rdim), F32),
                   jax.ShapeDtypeStruct((2, sq.B, RWKV_TILES * sq.S, rdim), BF16),
                   jax.ShapeDtypeStruct((sq.B, sq.S, rdim), F32)],
        compiler_params=_params(2),
        name="rwkv_intra",
    )(proj, proj, proj, a, lw_dec, *consts)


def _backward_block(i, ncb, nblk):
    return jnp.where(i < ncb, ncb - 1 - i, nblk + ncb - 1 - i)


def _scan_chunks_per_step(sq):
    cps = _largest_divisor(sq.ncc, (SCAN_CHUNKS, 2, 1))
    assert sq.nchunk % cps == 0
    return cps


def _scan_specs(sq, heads, rows, width):
    cps = _scan_chunks_per_step(sq)
    ncb, nblk = sq.ncc // cps, sq.nchunk // cps
    fwd = pl.BlockSpec((None, None, heads, cps * rows, width), lambda b, i: (0, b, 0, i, 0))
    bwd = pl.BlockSpec((None, None, heads, cps * rows, width),
                       lambda b, i: (1, b, 0, _backward_block(i, ncb, nblk), 0))
    return fwd, bwd


def _scan_out_specs(sq, width):
    cps = _scan_chunks_per_step(sq)
    ncb, nblk = sq.ncc // cps, sq.nchunk // cps
    fwd = pl.BlockSpec((None, cps * CHUNK, width), lambda b, i: (b, i, 0))
    bwd = pl.BlockSpec((None, cps * CHUNK, width), lambda b, i: (b, _backward_block(i, ncb, nblk), 0))
    return [fwd, bwd]


def _gdn_intra_body(p_ref, pp_ref, pn_ref, ab_ref, cw_ref, bd_ref, f_ref, ga_ref, gb_ref, *, sq, heads, hdim, cps):
    c = CHUNK
    gdim = heads * hdim
    scale = hdim ** -0.5
    first, last = sq.edges(cps * c)
    proj = p_ref[...]
    width = cw_ref.shape[0]
    half = width // 2
    acc = proj * cw_ref[half:half + 1, :]
    for j in range(width):
        if j != half:
            acc = acc + _shift_rows(proj, pp_ref, pn_ref, j - half, first, last) * cw_ref[j:j + 1, :]
    y = acc * _sigmoid(acc)
    q_blk = y[:, :gdim]
    k_blk = y[:, gdim:2 * gdim]
    q_blk = q_blk * lax.rsqrt(_segsum(q_blk * q_blk, bd_ref) + RMS_EPS)
    k_blk = k_blk * lax.rsqrt(_segsum(k_blk * k_blk, bd_ref) + RMS_EPS)
    v_blk = y[:, 2 * gdim:]
    sl = [slice(h * hdim, (h + 1) * hdim) for h in range(heads)]
    keys = []
    beta, gc, g_last, gc_row, decay, kb, kq, kt_h, vb, strict_of, eye = {}, {}, {}, {}, {}, {}, {}, {}, {}, {}, None
    for j in range(cps):
        rows = slice(j * c, (j + 1) * c)
        ab = ab_ref[rows, :]
        tot_all = jnp.sum(ab, axis=0, keepdims=True)
        lane = lax.broadcasted_iota(jnp.int32, ab.shape, 1)
        k_all = k_blk[rows]
        k_tt = jnp.transpose(k_all)
        for d in range(2):
            incl, strict, eye = _causal_masks(d, c)
            strict_of[d] = strict
            gc_all = _dot_exact_lhs(jnp.where(incl, 1.0, 0.0).astype(BF16), ab)
            gc_all_t = jnp.transpose(gc_all)
            sub = lax.broadcasted_iota(jnp.int32, gc_all_t.shape, 0)
            for h in range(heads):
                key = (d, j, h)
                keys.append(key)
                pick_b = lane == d * heads + h
                pick_g = lane == 2 * heads + d * heads + h
                beta[key] = jnp.sum(jnp.where(pick_b, ab, 0.0), axis=1, keepdims=True)
                gc[key] = jnp.sum(jnp.where(pick_g, gc_all, 0.0), axis=1, keepdims=True)
                g_last[key] = jnp.sum(jnp.where(pick_g[:1], tot_all, 0.0), axis=1, keepdims=True)
                gc_row[key] = jnp.sum(jnp.where(sub == 2 * heads + d * heads + h, gc_all_t, 0.0),
                                      axis=0, keepdims=True)
                diff = gc[key] - gc_row[key]
                decay[key] = jnp.where(incl, jnp.exp(jnp.where(incl, diff, 0.0)), 0.0)
                kb[key] = k_all[:, sl[h]] * beta[key]
                kq[key] = jnp.concatenate([kb[key], q_blk[rows, sl[h]] * scale], axis=0)
                kt_h[key] = k_tt[sl[h]]
                vb[key] = v_blk[rows, sl[h]] * beta[key]
    m = {k: _dot(kq[k], kt_h[k], GDN_PASSES) for k in keys}
    lower = [jnp.where(strict_of[k[0]], m[k][:c] * decay[k], 0.0) for k in keys]
    t = dict(zip(keys, _tri_inverse_many(lower, eye)))
    e_gc = {k: jnp.exp(gc[k]) for k in keys}
    sol = {k: _dot(t[k], jnp.concatenate([vb[k], kb[k] * e_gc[k]], axis=1), GDN_PASSES) for k in keys}
    for k in keys:
        d, j, h = k
        of, oa, ob = j * (c + SUBLANES), j * 2 * c, j * (c + hdim)
        f_ref[d, h, of:of + c] = sol[k][:, :hdim]
        f_ref[d, h, of + c:of + c + SUBLANES] = jnp.broadcast_to(jnp.exp(g_last[k]), (SUBLANES, hdim))
        ga_ref[d, h, oa:oa + c] = sol[k][:, hdim:].astype(BF16)
        ga_ref[d, h, oa + c:oa + 2 * c] = (kq[k][c:] * e_gc[k]).astype(BF16)
        gb_ref[d, h, ob:ob + c] = (m[k][c:] * decay[k]).astype(BF16)
        gb_ref[d, h, ob + c:ob + c + hdim] = (kt_h[k] * jnp.exp(g_last[k] - gc_row[k])).astype(BF16)


def _gdn_intra(sq, proj, ab, lw, heads, gdim, col_block):
    hdim = gdim // heads
    cps = _largest_divisor(sq.ncc, (GDN_INTRA_CHUNKS, 2, 1))
    tb = cps * CHUNK
    w = 3 * gdim
    consts = [lw["conv"], lw["bd_g"]]
    const = lambda shape: pl.BlockSpec(shape, lambda b, i: (0,) * len(shape))
    small = pl.BlockSpec((None, tb, LANES), lambda b, i: (b, i, 0))
    rows = (CHUNK + SUBLANES, 2 * CHUNK, CHUNK + hdim)
    widths = (hdim, hdim, CHUNK)
    dtypes = (F32, BF16, BF16)
    outs = [pl.BlockSpec((2, None, heads, cps * r, wd), lambda b, i: (0, b, 0, i, 0)) for r, wd in zip(rows, widths)]
    shapes = [jax.ShapeDtypeStruct((2, sq.B, heads, sq.nchunk * r, wd), dt) for r, wd, dt in zip(rows, widths, dtypes)]
    return pl.pallas_call(
        functools.partial(_gdn_intra_body, sq=sq, heads=heads, hdim=hdim, cps=cps),
        grid=(sq.B, sq.nchunk // cps),
        in_specs=[pl.BlockSpec((None, tb, w), lambda b, i: (b, i, col_block)),
                  sq.prev_rows(w, col_block, tb=tb), sq.next_rows(w, col_block, tb=tb), small]
        + [const(c.shape) for c in consts],
        out_specs=outs,
        out_shape=shapes,
        compiler_params=_params(2),
        name="gdn_intra",
    )(proj, proj, proj, ab, *consts)


def _scan_body(rf0_ref, rg0_ref, rf1_ref, rg1_ref, f0_ref, ga0_ref, gb0_ref, f1_ref, ga1_ref, gb1_ref,
               rof_ref, rob_ref, gof_ref, gob_ref, h_ref, s_ref, *, r_heads, r_hdim, g_heads, g_hdim, cps):
    @pl.when(pl.program_id(1) == 0)
    def _():
        h_ref[...] = jnp.zeros_like(h_ref)
        s_ref[...] = jnp.zeros_like(s_ref)

    c = CHUNK
    dot = functools.partial(jnp.dot, preferred_element_type=F32)
    pw = 2 * r_hdim
    rfs, rgs, routs = (rf0_ref, rf1_ref), (rg0_ref, rg1_ref), (rof_ref, rob_ref)
    rch = [(d, p) for d in range(2) for p in range(r_heads // 2)]
    lanes = [slice(p * pw, (p + 1) * pw) for p in range(r_heads // 2)]
    hst = {ch: h_ref[ch[0], ch[1]] for ch in rch}
    fs, gas, gbs, gouts = (f0_ref, f1_ref), (ga0_ref, ga1_ref), (gb0_ref, gb1_ref), (gof_ref, gob_ref)
    gch = [(d, h) for d in range(2) for h in range(g_heads)]
    sst = {ch: s_ref[ch[0], ch[1]] for ch in gch}
    for step in range(cps):
        sub = (step, cps - 1 - step)

        def tile(d, p, i, n=1):
            o = (sub[d] * RWKV_TILES + i) * c
            return rgs[d][o:o + n * c, lanes[p]]

        of = [(c + SUBLANES) * sub[d] for d in range(2)]
        oa = [2 * c * sub[d] for d in range(2)]
        ob = [(c + g_hdim) * sub[d] for d in range(2)]
        m2 = {(d, p): dot(tile(d, p, 0, 2), hst[d, p].astype(BF16)) for d, p in rch}
        m = {(d, h): dot(gas[d][h, oa[d]:oa[d] + 2 * c, :], sst[d, h].astype(BF16)) for d, h in gch}
        ub = {(d, p): _block_rows((tile(d, p, 4) + m2[d, p][:c]).astype(BF16), r_hdim) for d, p in rch}
        vn = {(d, h): (fs[d][h, of[d]:of[d] + c, :] - m[d, h][:c]).astype(BF16) for d, h in gch}
        for d, p in rch:
            y = tile(d, p, 5) + m2[d, p][c:] + dot(tile(d, p, 2), ub[d, p])
            routs[d][sub[d] * c:(sub[d] + 1) * c, lanes[p]] = y
            decay = _block_rows(rfs[d][sub[d] * c:(sub[d] + 1) * c, lanes[p]], r_hdim)
            hst[d, p] = (decay * hst[d, p] + _block_rows(tile(d, p, 6), r_hdim)
                         + dot(_block_rows(tile(d, p, 3), r_hdim), ub[d, p]))
        for d, h in gch:
            gouts[d][sub[d] * c:(sub[d] + 1) * c, h * g_hdim:(h + 1) * g_hdim] = (
                m[d, h][c:] + dot(gbs[d][h, ob[d]:ob[d] + c, :], vn[d, h]))
            sst[d, h] = (sst[d, h] * fs[d][h, of[d] + c:of[d] + c + 1, :]
                         + dot(gbs[d][h, ob[d] + c:ob[d] + c + g_hdim, :], vn[d, h]))
    for d, p in rch:
        h_ref[d, p] = hst[d, p]
    for d, h in gch:
        s_ref[d, h] = sst[d, h]


def _scan(sq, rf, rg, gf, gga, ggb, r_heads, g_heads):
    rdim = rf.shape[-1]
    r_hdim = rdim // r_heads
    g_hdim = gf.shape[-1]
    gdim = g_heads * g_hdim
    cps = _scan_chunks_per_step(sq)
    ncb, nblk = sq.ncc // cps, sq.nchunk // cps
    fwd = lambda rows: pl.BlockSpec((None, None, rows, rdim), lambda b, i: (0, b, i, 0))
    bwd = lambda rows: pl.BlockSpec((None, None, rows, rdim),
                                    lambda b, i: (1, b, _backward_block(i, ncb, nblk), 0))
    rows_f, rows_g = cps * CHUNK, cps * RWKV_TILES * CHUNK
    gspecs = [_scan_specs(sq, g_heads, arr.shape[3] // sq.nchunk, arr.shape[4]) for arr in (gf, gga, ggb)]
    r_one = jax.ShapeDtypeStruct((sq.B, sq.S, rdim), F32)
    g_one = jax.ShapeDtypeStruct((sq.B, sq.S, gdim), F32)
    return pl.pallas_call(
        functools.partial(_scan_body, r_heads=r_heads, r_hdim=r_hdim, g_heads=g_heads, g_hdim=g_hdim, cps=cps),
        grid=(sq.B, sq.nchunk // cps),
        in_specs=[fwd(rows_f), fwd(rows_g), bwd(rows_f), bwd(rows_g)]
        + [s[0] for s in gspecs] + [s[1] for s in gspecs],
        out_specs=_scan_out_specs(sq, rdim) + _scan_out_specs(sq, gdim),
        out_shape=[r_one, r_one, g_one, g_one],
        scratch_shapes=[pltpu.VMEM((2, r_heads // 2, 2 * r_hdim, 2 * r_hdim), F32),
                        pltpu.VMEM((2, g_heads, g_hdim, g_hdim), F32)],
        compiler_params=_params(2),
        name="scan",
    )(rf, rg, rf, rg, gf, gga, ggb, gf, gga, ggb)


def _rope(x, cos, sin_signed):
    n = x.shape[-1]
    lane = lax.broadcasted_iota(jnp.int32, x.shape, 1)
    partner = jnp.where((lane & 1) == 0, pltpu.roll(x, n - 1, axis=1), pltpu.roll(x, 1, axis=1))
    return x * cos + partner * sin_signed


def _attn_prep_body(q_ref, kv_ref, cs_ref, qn_ref, kn_ref, bd_ref, qo_ref, ko_ref, vo_ref,
                    *, q_heads, kv_heads, hdim):
    kvd = kv_heads * hdim
    cos = cs_ref[:, :kvd]
    sin = cs_ref[:, kvd:]
    reps = q_heads // kv_heads
    cos_q = jnp.concatenate([cos] * reps, axis=1)
    sin_q = jnp.concatenate([sin] * reps, axis=1)
    inv_n = 1.0 / hdim
    q = q_ref[...]
    q = q * lax.rsqrt(_segsum(q * q, bd_ref) * inv_n + RMS_EPS) * qn_ref[...]
    q = _rope(q, cos_q, sin_q) * (hdim ** -0.5)
    kv = kv_ref[...]
    k = kv[:, :kvd]
    kbd = bd_ref[:kvd, :kvd]
    hi, lo = _split2(k * k)
    ms = (jnp.dot(hi, kbd, preferred_element_type=F32) + jnp.dot(lo, kbd, preferred_element_type=F32)) * inv_n
    k = k * lax.rsqrt(ms + RMS_EPS) * kn_ref[...]
    k = _rope(k, cos, sin)
    v = kv[:, kvd:]
    for h in range(q_heads):
        qo_ref[h] = q[:, h * hdim:(h + 1) * hdim].astype(BF16)
    k_t = jnp.transpose(k)
    for h in range(kv_heads):
        ko_ref[h] = k_t[h * hdim:(h + 1) * hdim].astype(BF16)
        vh = v[:, h * hdim:(h + 1) * hdim]
        vo_ref[h] = jnp.concatenate([vh, jnp.ones_like(vh)], axis=1).astype(BF16)


def _attn_prep(sq, proj, cs_tab, lw, q_heads, kv_heads, hdim, q_col_block, kv_col_block):
    qd, kvd = q_heads * hdim, kv_heads * hdim
    consts = [lw["q_norm"], lw["k_norm"], lw["bd_r"]]
    k_spec = pl.BlockSpec((None, kv_heads, hdim, sq.TB), lambda b, j: (b, 0, 0, j))
    return pl.pallas_call(
        functools.partial(_attn_prep_body, q_heads=q_heads, kv_heads=kv_heads, hdim=hdim),
        grid=sq.grid,
        in_specs=[sq.rows(qd, q_col_block), sq.rows(2 * kvd, kv_col_block),
                  pl.BlockSpec((sq.TB, 2 * kvd), lambda b, j: (j, 0))]
        + [sq.const(c.shape) for c in consts],
        out_specs=[sq.heads(q_heads, hdim), k_spec, sq.heads(kv_heads, 2 * hdim)],
        out_shape=[jax.ShapeDtypeStruct((sq.B, q_heads, sq.S, hdim), BF16),
                   jax.ShapeDtypeStruct((sq.B, kv_heads, hdim, sq.S), BF16),
                   jax.ShapeDtypeStruct((sq.B, kv_heads, sq.S, 2 * hdim), BF16)],
        compiler_params=_params(2),
        name="attn_prep",
    )(proj, proj, cs_tab, *consts)


def _attn_body(q_ref, kt_ref, v_ref, o_ref, s_ref, p_ref, *, group, n_ctx_qblocks, n_ctx_keys):
    tq, hdim = q_ref.shape[1], q_ref.shape[2]
    n_keys = kt_ref.shape[1]

    def attend(nk):
        tiles = range(0, nk, KEY_BLOCK)
        for g in range(group):
            q = q_ref[g]
            rows = slice(g * tq, (g + 1) * tq)
            mx = None
            for t in tiles:
                s = jnp.dot(q, kt_ref[:, t:t + KEY_BLOCK], preferred_element_type=F32)
                s_ref[:, t:t + KEY_BLOCK] = s
                for c0 in range(0, KEY_BLOCK, LANES):
                    part = s[:, c0:c0 + LANES]
                    mx = part if mx is None else jnp.maximum(mx, part)
            m = jnp.broadcast_to(jnp.max(mx, axis=1, keepdims=True), (tq, LANES))
            for t in tiles:
                for c0 in range(t, t + KEY_BLOCK, LANES):
                    p_ref[rows, c0:c0 + LANES] = jnp.exp(s_ref[:, c0:c0 + LANES] - m).astype(BF16)
        acc = jnp.dot(p_ref[:, 0:nk], v_ref[0:nk, :], preferred_element_type=F32)
        out = acc[:, :hdim] / acc[:, hdim:]
        for g in range(group):
            o_ref[:, g * hdim:(g + 1) * hdim] = out[g * tq:(g + 1) * tq, :]

    is_ctx = pl.program_id(2) < n_ctx_qblocks

    @pl.when(is_ctx)
    def _():
        attend(n_ctx_keys)

    @pl.when(jnp.logical_not(is_ctx))
    def _():
        attend(n_keys)


def _attention(sq, q, kt, v):
    b, q_heads, s, hdim = q.shape
    kv_heads = kt.shape[1]
    group = q_heads // kv_heads
    tq = _largest_divisor(sq.ctx, (128, 64))
    assert sq.ctx % KEY_BLOCK == 0 and s % KEY_BLOCK == 0
    return pl.pallas_call(
        functools.partial(_attn_body, group=group, n_ctx_qblocks=sq.ctx // tq, n_ctx_keys=sq.ctx),
        grid=(b, kv_heads, s // tq),
        in_specs=[pl.BlockSpec((None, group, tq, hdim), lambda bi, g, i: (bi, g, i, 0)),
                  pl.BlockSpec((None, None, hdim, s), lambda bi, g, i: (bi, g, 0, 0)),
                  pl.BlockSpec((None, None, s, 2 * hdim), lambda bi, g, i: (bi, g, 0, 0))],
        out_specs=pl.BlockSpec((None, tq, group * hdim), lambda bi, g, i: (bi, i, g)),
        out_shape=jax.ShapeDtypeStruct((b, s, q_heads * hdim), F32),
        scratch_shapes=[pltpu.VMEM((tq, s), F32), pltpu.VMEM((group * tq, s), BF16)],
        compiler_params=_params(3),
        name="attention",
    )(q, kt, v)


def _merge_body(x_ref, rf_ref, rb_ref, bonus_ref, rg_ref, gf_ref, gb_ref, z_ref, yc_ref, gate_ref, mod_ref,
                lnw_ref, lnb_ref, bdr_ref, gn_ref, bdg_ref, wa_ref, wb_ref, wc_ref, wo_ref, o_ref,
                *, gate_row, r_hdim, g_hdim):
    d = x_ref.shape[-1]
    o = rf_ref[...] + rb_ref[...]
    mean = _segsum(o, bdr_ref) * (1.0 / r_hdim)
    cen = o - mean
    var = _segsum(cen * cen, bdr_ref) * (1.0 / r_hdim)
    ya = (cen * lax.rsqrt(var + RWKV_LNX_EPS) * lnw_ref[...] + lnb_ref[...] + bonus_ref[...]) * rg_ref[...]
    o = gf_ref[...] + gb_ref[...]
    ms = _segsum(o * o, bdg_ref) * (1.0 / g_hdim)
    z = z_ref[...]
    yb = o * lax.rsqrt(ms + RMS_EPS) * gn_ref[...] * (z * _sigmoid(z))
    m = gate_ref[:, :d] * _dot(ya, wa_ref[...])
    m = m + gate_ref[:, d:2 * d] * _dot(yb, wb_ref[...])
    m = m + gate_ref[:, 2 * d:] * _dot(yc_ref[...], wc_ref[...])
    y = _dot(m, wo_ref[...])
    o_ref[...] = x_ref[...] + mod_ref[gate_row:gate_row + 1, :] * y


def _merge(sq, xs, o_rf, o_rb, bonus, r_gate, o_gf, o_gb, proj, z_col_block, yc, gates, modtab, lw,
           r_heads, g_heads):
    d = xs.shape[-1]
    rdim, gdim = o_rf.shape[-1], o_gf.shape[-1]
    consts = [lw["lnx_w"], lw["lnx_b"], lw["bd_r"], lw["gdn_norm"], lw["bd_g"],
              lw["w_up_a"], lw["w_up_b"], lw["w_up_c"], lw["w_out"]]
    return pl.pallas_call(
        functools.partial(_merge_body, gate_row=2, r_hdim=rdim // r_heads, g_hdim=gdim // g_heads),
        grid=sq.grid,
        in_specs=[sq.rows(d)] + [sq.rows(rdim)] * 4 + [sq.rows(gdim)] * 2
        + [sq.rows(gdim, z_col_block), sq.rows(yc.shape[-1]), sq.rows(3 * d), sq.mod(d)]
        + [sq.const(c.shape) for c in consts],
        out_specs=sq.rows(d),
        out_shape=jax.ShapeDtypeStruct(xs.shape, F32),
        compiler_params=_params(2),
        name="merge",
    )(xs, o_rf, o_rb, bonus, r_gate, o_gf, o_gb, proj, yc, gates, modtab, *consts)


def _ffn_body(x_ref, g_ref, mod_ref, w1_ref, w3_ref, w2_ref, o_ref, *, h_chunk, ctx_len):
    rows = x_ref.shape[0]
    row = pl.program_id(1) * rows + lax.broadcasted_iota(jnp.int32, (rows, 1), 0)
    is_ctx = row < ctx_len
    pick = lambda i: jnp.where(is_ctx, mod_ref[0, i:i + 1, :], mod_ref[1, i:i + 1, :])
    x = x_ref[...]
    y = x * lax.rsqrt(jnp.mean(x * x, axis=-1, keepdims=True) + RMS_EPS) * g_ref[...]
    h = (y * (1.0 + pick(4)) + pick(3)).astype(BF16)
    hidden = w1_ref.shape[1]
    acc = jnp.zeros(x.shape, F32)
    for c0 in range(0, hidden, h_chunk):
        a = jnp.dot(h, w1_ref[:, c0:c0 + h_chunk], preferred_element_type=F32)
        b = jnp.dot(h, w3_ref[:, c0:c0 + h_chunk], preferred_element_type=F32)
        t = (a * _sigmoid(a) * b).astype(BF16)
        acc = acc + jnp.dot(t, w2_ref[c0:c0 + h_chunk, :], preferred_element_type=F32)
    o_ref[...] = x + pick(5) * acc


def _ffn(sq, xs, g, modtab, lw):
    d = xs.shape[-1]
    consts = [lw["ffn_w1"], lw["ffn_w3"], lw["ffn_w2"]]
    h_chunk = _largest_divisor(lw["ffn_w1"].shape[1], (512, 256, 128))
    tb = _largest_divisor(sq.S, FFN_ROWS)
    rows = pl.BlockSpec((None, tb, d), lambda b, j: (b, j, 0))
    const = lambda shape: pl.BlockSpec(shape, lambda b, j: (0,) * len(shape))
    return pl.pallas_call(
        functools.partial(_ffn_body, h_chunk=h_chunk, ctx_len=sq.ctx),
        grid=(sq.B, sq.S // tb),
        in_specs=[rows, const((1, d)), pl.BlockSpec((None, 2, 6, d), lambda b, j: (b, 0, 0, 0))]
        + [const(c.shape) for c in consts],
        out_specs=rows,
        out_shape=jax.ShapeDtypeStruct(xs.shape, F32),
        compiler_params=_params(2),
        name="ffn",
    )(xs, g.reshape(1, d), modtab, *consts)


def _block_diag_ones(n, seg):
    idx = jnp.arange(n) // seg
    return (idx[:, None] == idx[None, :]).astype(BF16)


def _block_diag2(m):
    z = jnp.zeros_like(m[0])
    return jnp.concatenate([jnp.concatenate([m[0], z], axis=1), jnp.concatenate([z, m[1]], axis=1)], axis=0)


def _pad_to(x, axis, size):
    pad = [(0, 0)] * x.ndim
    pad[axis] = (0, size - x.shape[axis])
    return jnp.pad(x, pad)


def _rope_table(ctx_len, seq_len, hdim, kv_heads):
    rows = seq_len // GRID_W
    row = jnp.repeat(jnp.arange(rows), GRID_W).astype(F32)
    col = jnp.tile(jnp.arange(GRID_W), rows).astype(F32)
    half = hdim // 2
    inv = ROPE_THETA ** (-jnp.arange(0, half, 2, dtype=F32) / half)
    ang = jnp.concatenate([row[:, None] * inv, col[:, None] * inv], axis=-1)
    cos = jnp.repeat(jnp.cos(ang), 2, axis=1)
    sin = jnp.repeat(jnp.sin(ang), 2, axis=1) * jnp.tile(jnp.array([-1.0, 1.0], F32), half)
    cos = jnp.concatenate([jnp.ones((ctx_len, hdim), F32), cos], axis=0)
    sin = jnp.concatenate([jnp.zeros((ctx_len, hdim), F32), sin], axis=0)
    return jnp.concatenate([jnp.tile(cos, (1, kv_heads)), jnp.tile(sin, (1, kv_heads))], axis=1)


def kernel(x, c, ctx, c_ctx, ada_w, ada_b, norm1, norm2, w_in, rwkv_mu_x, rwkv_mu_rkv, rwkv_w0, rwkv_w1, rwkv_w2, rwkv_a0, rwkv_a1, rwkv_a2, rwkv_g1, rwkv_g2, rwkv_k_k, rwkv_k_a, rwkv_r_k, rwkv_lnx_w, rwkv_lnx_b, gdn_conv, gdn_w_alpha, gdn_dt_bias, gdn_A_log, gdn_w_beta, gdn_norm, attn_q_norm, attn_k_norm, w_up_a, w_up_b, w_up_c, w_gate, b_gate, w_out, ffn_w1, ffn_w3, ffn_w2, final_norm):
    batch, seq_len, d = x.shape
    ctx_len = ctx.shape[1]
    depth = ada_w.shape[0]
    sq = _Seq(batch, ctx_len, seq_len)

    r_heads, r_hdim = rwkv_r_k.shape[1], rwkv_r_k.shape[2]
    rdim = r_heads * r_hdim
    g_heads, g_hdim = gdn_w_alpha.shape[-1], gdn_norm.shape[-1]
    gdim = g_heads * g_hdim
    a_hdim = attn_q_norm.shape[-1]
    qd = w_up_c.shape[1]
    q_heads = qd // a_hdim
    kvd = (w_in.shape[-1] - 3 * rdim - 4 * gdim - qd) // 2
    kv_heads = kvd // a_hdim
    assert rdim == gdim == qd and r_hdim == a_hdim, "lane-segment constants are shared between mixers"
    assert (3 * rdim) % (3 * gdim) == 0 and (3 * rdim + 3 * gdim) % gdim == 0
    gdn_col = (3 * rdim) // (3 * gdim)
    z_col = (3 * rdim + 3 * gdim) // gdim
    q_col = (3 * rdim + 4 * gdim) // qd
    assert (3 * rdim + 4 * gdim + qd) % (2 * kvd) == 0
    kv_col = (3 * rdim + 4 * gdim + qd) // (2 * kvd)

    bd_r = _block_diag_ones(rdim, r_hdim)
    bd_g = _block_diag_ones(gdim, g_hdim)
    cs_tab = _rope_table(ctx_len, seq_len, a_hdim, kv_heads)
    n_beta = 2 * g_heads
    assert 2 * n_beta <= LANES

    cond = jnp.concatenate([c, c_ctx[None, :]], axis=0)
    cond = _pad_to(cond, 0, -(-(batch + 1) // SUBLANES) * SUBLANES)

    xs = jnp.concatenate([ctx, x], axis=1)
    for l in range(depth):
        w1c = jnp.concatenate([rwkv_w1[l, 0], rwkv_w1[l, 1]], axis=1)
        a1c = jnp.concatenate([rwkv_a1[l, 0], rwkv_a1[l, 1]], axis=1)
        g1w = _pad_to(rwkv_g1[l], 1, 2 * LANES)
        wab = _pad_to(jnp.concatenate([gdn_w_beta[l, 0], gdn_w_beta[l, 1],
                                       gdn_w_alpha[l, 0], gdn_w_alpha[l, 1]], axis=1), 1, LANES)
        mu = rwkv_mu_x[l]
        splits = (w1c.shape[1], w1c.shape[1] + a1c.shape[1], w1c.shape[1] + a1c.shape[1] + g1w.shape[1])
        lw = {
            "mu_rkv": rwkv_mu_rkv[l],
            "wh": jnp.concatenate([w1c, a1c, g1w, wab], axis=1).astype(BF16),
            "wd": jnp.concatenate([mu[0][:, None] * w1c, mu[1][:, None] * a1c, mu[2][:, None] * g1w],
                                  axis=1).astype(BF16),
            "splits": splits,
            "w2": _block_diag2(rwkv_w2[l]).astype(BF16),
            "w0": rwkv_w0[l].reshape(1, 2 * rdim),
            "a2": _block_diag2(rwkv_a2[l]).astype(BF16),
            "a0": rwkv_a0[l].reshape(1, 2 * rdim),
            "g2": _pad_to(rwkv_g2[l], 0, 2 * LANES).astype(BF16),
            "abb": _pad_to(jnp.concatenate([jnp.zeros((n_beta,), F32), gdn_dt_bias[l].reshape(-1)]), 0, LANES).reshape(1, LANES),
            "alog": _pad_to(jnp.concatenate([jnp.zeros((n_beta,), F32), gdn_A_log[l].reshape(-1)]), 0, LANES).reshape(1, LANES),
            "n_beta": n_beta,
            "k_k": rwkv_k_k[l].reshape(1, rdim), "k_a": rwkv_k_a[l].reshape(1, rdim),
            "r_k": rwkv_r_k[l].reshape(1, rdim),
            "lnx_w": rwkv_lnx_w[l].reshape(1, rdim), "lnx_b": rwkv_lnx_b[l].reshape(1, rdim),
            "bd_r": bd_r, "bd_g": bd_g,
            "conv": jnp.transpose(gdn_conv[l]),
            "gdn_norm": jnp.tile(gdn_norm[l], g_heads).reshape(1, gdim),
            "q_norm": jnp.tile(attn_q_norm[l], q_heads).reshape(1, qd),
            "k_norm": jnp.tile(attn_k_norm[l], kv_heads).reshape(1, kvd),
            "w_up_a": w_up_a[l].astype(BF16), "w_up_b": w_up_b[l].astype(BF16),
            "w_up_c": w_up_c[l].astype(BF16), "w_out": w_out[l].astype(BF16),
            "ffn_w1": ffn_w1[l].astype(BF16), "ffn_w3": ffn_w3[l].astype(BF16), "ffn_w2": ffn_w2[l].astype(BF16),
        }
        mod = _matmul(cond, ada_w[l].astype(BF16), bias=ada_b[l], pre_act="silu")
        mod_x = mod[:batch].reshape(batch, 6, d)
        mod_c = jnp.broadcast_to(mod[batch].reshape(1, 6, d), (batch, 6, d))
        modtab = jnp.stack([mod_c, mod_x], axis=1)

        h, proj, gates = _pre(sq, xs, norm1[l], modtab, w_in[l].astype(BF16), w_gate[l].astype(BF16), b_gate[l])
        lw_dec, a_iclr, g_out, ab = _lora(sq, h, lw)

        rf, rg, bonus = _rwkv_intra(sq, proj, a_iclr, lw_dec, lw, r_heads)
        gf, gga, ggb = _gdn_intra(sq, proj, ab, lw, g_heads, gdim, gdn_col)
        o_rf, o_rb, o_gf, o_gb = _scan(sq, rf, rg, gf, gga, ggb, r_heads, g_heads)

        aq, ak, av = _attn_prep(sq, proj, cs_tab, lw, q_heads, kv_heads, a_hdim, q_col, kv_col)
        yc = _attention(sq, aq, ak, av)

        xs = _merge(sq, xs, o_rf, o_rb, bonus, g_out, o_gf, o_gb, proj, z_col, yc, gates, modtab, lw,
                    r_heads, g_heads)
        xs = _ffn(sq, xs, norm2[l], modtab, lw)
    return _final_norm(sq, xs, final_norm)
```

```python
import functools

import jax
import jax.numpy as jnp
from jax import lax
from jax.experimental import pallas as pl
from jax.experimental.pallas import tpu as pltpu

F32 = jnp.float32
BF16 = jnp.bfloat16

RMS_EPS = 1e-6
RWKV_LNX_EPS = 64e-5
ROPE_THETA = 10000.0
GRID_W = 64

SUBLANES = 8
LANES = 128
CHUNK = 64
MAX_TOKEN_BLOCK = 256
MAX_MM_ROWS = 512
FFN_ROWS = (544, 512, 256, 128)
KEY_BLOCK = 256
RWKV_PASSES = 1
GDN_PASSES = 1
INV_PASSES = 1
RWKV_INTRA_CHUNKS = 2
GDN_INTRA_CHUNKS = 4
SCAN_CHUNKS = 4
VMEM_LIMIT = 56 * 1024 * 1024


def _sigmoid(x):
    return 1.0 / (1.0 + jnp.exp(-x))


def _softplus(x):
    return jnp.maximum(x, 0.0) + jnp.log(1.0 + jnp.exp(-jnp.abs(x)))


def _split2(x):
    hi = x.astype(BF16)
    lo = (x - hi.astype(F32)).astype(BF16)
    return hi, lo


def _mm(a, b, dims, passes):
    d = functools.partial(lax.dot_general, dimension_numbers=(dims, ((), ())), preferred_element_type=F32)
    if passes == 1:
        return d(a.astype(BF16), b.astype(BF16))
    ah, al = _split2(a)
    bh, bl = _split2(b)
    return d(ah, bh) + (d(ah, bl) + d(al, bh))


def _dot(a, b, passes=1):
    return _mm(a, b, ((1,), (0,)), passes)


def _split3(x):
    hi = x.astype(BF16)
    r1 = x - hi.astype(F32)
    mid = r1.astype(BF16)
    lo = (r1 - mid.astype(F32)).astype(BF16)
    return hi, mid, lo


def _dot_exact_lhs(m01, x):
    d = functools.partial(jnp.dot, preferred_element_type=F32)
    hi, mid, lo = _split3(x)
    return d(m01, hi) + (d(m01, mid) + d(m01, lo))


def _segsum(x, bd_ref):
    d = functools.partial(jnp.dot, preferred_element_type=F32)
    hi, lo = _split2(x)
    bd = bd_ref[...]
    return d(hi, bd) + d(lo, bd)


def _row_iota(shape):
    return lax.broadcasted_iota(jnp.int32, shape, 0)


def _shift_rows(x, prev_ref, next_ref, k, first, last):
    n = x.shape[0]
    y = pltpu.roll(x, (-k) % n, axis=0)
    row = _row_iota((SUBLANES, x.shape[1]))
    if k < 0:
        tile = y[:SUBLANES]
        for i in range(-k):
            edge = prev_ref[SUBLANES + k + i:SUBLANES + k + i + 1, :]
            edge = jnp.where(first, 0.0, edge)
            tile = jnp.where(row == i, edge, tile)
        return jnp.concatenate([tile, y[SUBLANES:]], axis=0)
    tile = y[n - SUBLANES:]
    for i in range(k):
        edge = next_ref[i:i + 1, :]
        edge = jnp.where(last, 0.0, edge)
        tile = jnp.where(row == SUBLANES - k + i, edge, tile)
    return jnp.concatenate([y[:n - SUBLANES], tile], axis=0)


def _largest_divisor(n, candidates):
    for c in candidates:
        if n % c == 0:
            return c
    raise ValueError(f"no block size among {candidates} divides {n}")


class _Seq:
    def __init__(self, batch, ctx_len, seq_len):
        self.B = batch
        self.ctx = ctx_len
        self.S = ctx_len + seq_len
        self.TB = _largest_divisor(ctx_len, (MAX_TOKEN_BLOCK, 128, 64))
        assert seq_len % self.TB == 0 and self.TB % CHUNK == 0
        self.ncb = ctx_len // self.TB
        self.nblk = self.S // self.TB
        self.ncc = ctx_len // CHUNK
        self.nchunk = self.S // CHUNK
        self.grid = (batch, self.nblk)

    def rows(self, width, col_block=0):
        return pl.BlockSpec((None, self.TB, width), lambda b, j: (b, j, col_block))

    def rows2(self, width):
        return pl.BlockSpec((2, None, self.TB, width), lambda b, j: (0, b, j, 0))

    def heads(self, n_heads, width):
        return pl.BlockSpec((None, n_heads, self.TB, width), lambda b, j: (b, 0, j, 0))

    def prev_rows(self, width, col_block=0, tb=None):
        per = (tb or self.TB) // SUBLANES
        return pl.BlockSpec((None, SUBLANES, width),
                            lambda b, j: (b, jnp.maximum(j * per - 1, 0), col_block))

    def next_rows(self, width, col_block=0, tb=None):
        per = (tb or self.TB) // SUBLANES
        top = self.S // SUBLANES - 1
        return pl.BlockSpec((None, SUBLANES, width),
                            lambda b, j: (b, jnp.minimum((j + 1) * per, top), col_block))

    def const(self, shape):
        zeros = (0,) * len(shape)
        return pl.BlockSpec(shape, lambda b, j: zeros)

    def mod(self, d_model):
        ncb = self.ncb
        return pl.BlockSpec((None, None, 6, d_model),
                            lambda b, j: (b, (j >= ncb).astype(jnp.int32), 0, 0))

    def edges(self, tb=None):
        tb = tb or self.TB
        j = pl.program_id(1)
        ncb, nblk = self.ctx // tb, self.S // tb
        first = (j == 0) | (j == ncb)
        last = (j == ncb - 1) | (j == nblk - 1)
        return first, last


def _params(n_axes):
    return pltpu.CompilerParams(dimension_semantics=("arbitrary",) * n_axes,
                                vmem_limit_bytes=VMEM_LIMIT)


def _mm_body(*refs, act, pre_act, n_chunk, has_bias):
    if has_bias:
        x_ref, w_ref, b_ref, o_ref = refs
    else:
        x_ref, w_ref, o_ref = refs
        b_ref = None
    x = x_ref[...]
    if pre_act == "silu":
        x = x * _sigmoid(x)
    xb = x.astype(BF16)
    n = o_ref.shape[-1]
    for n0 in range(0, n, n_chunk):
        y = jnp.dot(xb, w_ref[:, n0:n0 + n_chunk], preferred_element_type=F32)
        if b_ref is not None:
            y = y + b_ref[:, n0:n0 + n_chunk]
        if act == "sigmoid":
            y = _sigmoid(y)
        o_ref[:, n0:n0 + n_chunk] = y.astype(o_ref.dtype)


def _matmul(x, w, bias=None, act=None, pre_act=None, out_dtype=F32):
    m, k = x.shape
    n = w.shape[1]
    tm = m if m <= MAX_MM_ROWS else _largest_divisor(m, (MAX_MM_ROWS, 256, 128, 64, 32, 16, 8))
    n_chunk = _largest_divisor(n, (512, 256, 128))
    in_specs = [pl.BlockSpec((tm, k), lambda i: (i, 0)),
                pl.BlockSpec((k, n), lambda i: (0, 0))]
    args = [x, w]
    if bias is not None:
        in_specs.append(pl.BlockSpec((1, n), lambda i: (0, 0)))
        args.append(bias.reshape(1, n))
    return pl.pallas_call(
        functools.partial(_mm_body, act=act, pre_act=pre_act, n_chunk=n_chunk,
                          has_bias=bias is not None),
        grid=(m // tm,),
        in_specs=in_specs,
        out_specs=pl.BlockSpec((tm, n), lambda i: (i, 0)),
        out_shape=jax.ShapeDtypeStruct((m, n), out_dtype),
        compiler_params=_params(1),
        name="matmul",
    )(*args)


def _pre_body(x_ref, g_ref, mod_ref, win_ref, wg_ref, bg_ref, h_ref, p_ref, gate_ref, *, n_chunk, ctx_len):
    rows = x_ref.shape[0]
    row = pl.program_id(1) * rows + lax.broadcasted_iota(jnp.int32, (rows, 1), 0)
    is_ctx = row < ctx_len
    pick = lambda i: jnp.where(is_ctx, mod_ref[0, i:i + 1, :], mod_ref[1, i:i + 1, :])
    x = x_ref[...]
    y = x * lax.rsqrt(jnp.mean(x * x, axis=-1, keepdims=True) + RMS_EPS) * g_ref[...]
    h = y * (1.0 + pick(1)) + pick(0)
    h_ref[...] = h
    hb = h.astype(BF16)
    for n0 in range(0, p_ref.shape[-1], n_chunk):
        p_ref[:, n0:n0 + n_chunk] = jnp.dot(hb, win_ref[:, n0:n0 + n_chunk], preferred_element_type=F32)
    for n0 in range(0, gate_ref.shape[-1], n_chunk):
        z = jnp.dot(hb, wg_ref[:, n0:n0 + n_chunk], preferred_element_type=F32) + bg_ref[:, n0:n0 + n_chunk]
        gate_ref[:, n0:n0 + n_chunk] = _sigmoid(z).astype(gate_ref.dtype)


def _pre(sq, xs, g, modtab, w_in, w_gate, b_gate):
    d = xs.shape[-1]
    n_in, n_gate = w_in.shape[1], w_gate.shape[1]
    n_chunk = _largest_divisor(n_in, (256, 128))
    assert n_gate % n_chunk == 0
    tb = _largest_divisor(sq.S, FFN_ROWS)
    rows = lambda width: pl.BlockSpec((None, tb, width), lambda b, j: (b, j, 0))
    const = lambda shape: pl.BlockSpec(shape, lambda b, j: (0,) * len(shape))
    resident = lambda shape: pl.BlockSpec(shape, lambda b, j: (0,) * len(shape), pipeline_mode=pl.Buffered(1))
    return pl.pallas_call(
        functools.partial(_pre_body, n_chunk=n_chunk, ctx_len=sq.ctx),
        grid=(sq.B, sq.S // tb),
        in_specs=[rows(d), const((1, d)), pl.BlockSpec((None, 2, 6, d), lambda b, j: (b, 0, 0, 0)),
                  resident(w_in.shape), resident(w_gate.shape), const((1, n_gate))],
        out_specs=[rows(d), rows(n_in), rows(n_gate)],
        out_shape=[jax.ShapeDtypeStruct(xs.shape, F32),
                   jax.ShapeDtypeStruct(xs.shape[:2] + (n_in,), F32),
                   jax.ShapeDtypeStruct(xs.shape[:2] + (n_gate,), BF16)],
        compiler_params=_params(2),
        name="pre",
    )(xs, g.reshape(1, d), modtab, w_in, w_gate, b_gate.reshape(1, n_gate))


def _final_norm_body(x_ref, g_ref, o_ref):
    x = x_ref[...]
    y = x * lax.rsqrt(jnp.mean(x * x, axis=-1, keepdims=True) + RMS_EPS)
    o_ref[...] = y * g_ref[...]


def _final_norm(sq, xs, g):
    d = xs.shape[-1]
    ncb = sq.ncb
    return pl.pallas_call(
        _final_norm_body,
        grid=(sq.B, sq.nblk - ncb),
        in_specs=[pl.BlockSpec((None, sq.TB, d), lambda b, j: (b, j + ncb, 0)),
                  pl.BlockSpec((1, d), lambda b, j: (0, 0))],
        out_specs=pl.BlockSpec((None, sq.TB, d), lambda b, j: (b, j, 0)),
        out_shape=jax.ShapeDtypeStruct((sq.B, sq.S - sq.ctx, d), F32),
        compiler_params=_params(2),
        name="final_norm",
    )(xs, g.reshape(1, d))


def _lora_body(h_ref, hp_ref, hn_ref, wh_ref, wd_ref, w2_ref, w0_ref, a2_ref, a0_ref,
               g2_ref, abb_ref, alog_ref, lw_ref, a_ref, g_ref, ab_ref, *, sq, rdim, n_beta, splits):
    first, last = sq.edges()
    h = h_ref[...]
    nb = 0.5 * (_shift_rows(h, hp_ref, hn_ref, -1, first, last)
                + _shift_rows(h, hp_ref, hn_ref, 1, first, last))
    zh = _dot(h, wh_ref[...])
    zd = _dot(nb - h, wd_ref[...])
    s1, s2, s3 = splits
    wl = w0_ref[...] + _dot(jnp.tanh(zh[:, :s1] + zd[:, :s1]), w2_ref[...])
    lw = -(jnp.exp(-0.5) * _sigmoid(wl))
    lw_ref[0] = lw[:, :rdim]
    lw_ref[1] = lw[:, rdim:]
    a = _sigmoid(a0_ref[...] + _dot(zh[:, s1:s2] + zd[:, s1:s2], a2_ref[...]))
    a_ref[0] = a[:, :rdim]
    a_ref[1] = a[:, rdim:]
    g_ref[...] = _dot(_sigmoid(zh[:, s2:s3] + zd[:, s2:s3]), g2_ref[...])
    z = zh[:, s3:]
    col = lax.broadcasted_iota(jnp.int32, z.shape, 1)
    gl = -jnp.exp(alog_ref[...]) * _softplus(z + abb_ref[...])
    ab_ref[...] = jnp.where(col < n_beta, _sigmoid(z), gl)


def _lora(sq, h, lw):
    d = h.shape[-1]
    rdim = lw["w0"].shape[-1] // 2
    consts = [lw["wh"], lw["wd"], lw["w2"], lw["w0"], lw["a2"], lw["a0"],
              lw["g2"], lw["abb"], lw["alog"]]
    bsd = (sq.B, sq.S)
    return pl.pallas_call(
        functools.partial(_lora_body, sq=sq, rdim=rdim, n_beta=lw["n_beta"], splits=lw["splits"]),
        grid=sq.grid,
        in_specs=[sq.rows(d), sq.prev_rows(d), sq.next_rows(d)] + [sq.const(c.shape) for c in consts],
        out_specs=[sq.rows2(rdim), sq.rows2(rdim), sq.rows(rdim), sq.rows(LANES)],
        out_shape=[jax.ShapeDtypeStruct((2,) + bsd + (rdim,), F32),
                   jax.ShapeDtypeStruct((2,) + bsd + (rdim,), F32),
                   jax.ShapeDtypeStruct(bsd + (rdim,), F32),
                   jax.ShapeDtypeStruct(bsd + (LANES,), F32)],
        compiler_params=_params(2),
        name="lora",
    )(h, h, h, *consts)


def _causal_masks(d, c, reps=1):
    row = lax.broadcasted_iota(jnp.int32, (c, reps * c), 0)
    col = lax.broadcasted_iota(jnp.int32, (c, reps * c), 1) & (c - 1)
    delta = jnp.where(d == 0, row - col, col - row)
    eye = jnp.where(row == col, 1.0, 0.0).astype(F32)
    return delta >= 0, delta > 0, eye


def _stacked_mask(d, c, reps=1):
    row = lax.broadcasted_iota(jnp.int32, (2 * c, reps * c), 0)
    col = lax.broadcasted_iota(jnp.int32, (2 * c, reps * c), 1) & (c - 1)
    rr = row & (c - 1)
    delta = jnp.where(d == 0, rr - col, col - rr)
    return (delta > 0) | ((row >= c) & (delta == 0))


def _block_rows(x, w):
    left = lax.broadcasted_iota(jnp.int32, x.shape, 1) < w
    zero = jnp.zeros_like(x)
    return jnp.concatenate([jnp.where(left, x, zero), jnp.where(left, zero, x)], axis=0)


def _fold_rows(y):
    n = y.shape[0] // 2
    return y[:n] + y[n:]


def _tri_inverse_pairs(ls):
    n = ls[0].shape[0]
    row = lax.broadcasted_iota(jnp.int32, (n, 2 * n), 0)
    col = lax.broadcasted_iota(jnp.int32, (n, 2 * n), 1) & (n - 1)
    eye = jnp.where(row == col, 1.0, 0.0).astype(F32)
    same = (row >> 1) == (col >> 1)
    ts = [eye - jnp.where(same, l, 0.0) for l in ls]
    for k in range(2, n.bit_length()):
        off = ((row >> k) == (col >> k)) & ((row >> (k - 1)) != (col >> (k - 1)))
        tl = [_dot(t, _block_rows(jnp.where(off, l, 0.0), n), INV_PASSES) for t, l in zip(ts, ls)]
        ts = [t - _dot(x, _block_rows(t, n), INV_PASSES) for x, t in zip(tl, ts)]
    return ts


def _tri_inverse_many(ls, eye):
    n = ls[0].shape[0]
    row = lax.broadcasted_iota(jnp.int32, (n, n), 0)
    col = lax.broadcasted_iota(jnp.int32, (n, n), 1)
    same = (row >> 1) == (col >> 1)
    ts = [eye - jnp.where(same, l, 0.0) for l in ls]
    for k in range(2, n.bit_length()):
        off = ((row >> k) == (col >> k)) & ((row >> (k - 1)) != (col >> (k - 1)))
        tl = [_dot(t, jnp.where(off, l, 0.0), INV_PASSES) for t, l in zip(ts, ls)]
        ts = [t - _dot(x, t, INV_PASSES) for x, t in zip(tl, ts)]
    return ts


RWKV_TILES = 7

def _rwkv_intra_body(p_ref, pp_ref, pn_ref, a_ref, lw_ref, mu_ref, kk_ref, ka_ref, rk_ref, bd_ref,
                     f_ref, g_ref, bonus_ref, *, sq, heads, hdim, cps):
    c = CHUNK
    pw = 2 * hdim
    rdim = heads * hdim
    first, last = sq.edges(cps * c)
    proj = p_ref[...]
    nb = 0.5 * (_shift_rows(proj, pp_ref, pn_ref, -1, first, last)
                + _shift_rows(proj, pp_ref, pn_ref, 1, first, last))
    dlt = nb - proj
    r_all = proj[:, :rdim] + dlt[:, :rdim] * mu_ref[0:1, :]
    k_all = proj[:, rdim:2 * rdim] + dlt[:, rdim:2 * rdim] * mu_ref[1:2, :]
    v_all = proj[:, 2 * rdim:] + dlt[:, 2 * rdim:] * mu_ref[2:3, :]
    kn = k_all * kk_ref[...]
    kk_all = kn * lax.rsqrt(_segsum(kn * kn, bd_ref) + RMS_EPS)
    kd_all = [k_all * (1.0 + (a_ref[d] - 1.0) * ka_ref[...]) for d in range(2)]
    rk = r_all * rk_ref[...]
    bonus_ref[...] = (_segsum(rk * kd_all[0], bd_ref) + _segsum(rk * kd_all[1], bd_ref)) * v_all
    keys, a_p, r_p, v_p, bt_p, kt_p, ec_p, mask2 = [], {}, {}, {}, {}, {}, {}, {}
    for d in range(2):
        incl, _, _ = _causal_masks(d, c)
        m01 = jnp.where(incl, 1.0, 0.0).astype(BF16)
        mask2[d] = _stacked_mask(d, c, reps=2)
        for j in range(cps):
            rows = slice(j * c, (j + 1) * c)
            lw = lw_ref[d, rows, :]
            cum = _dot_exact_lhs(m01, lw)
            e_neg = jnp.exp(-cum)
            e_row = jnp.broadcast_to(jnp.exp(jnp.sum(lw, axis=0, keepdims=True)), lw.shape)
            a_t = -kk_all[rows] * jnp.exp(cum - lw)
            r_t = r_all[rows] * jnp.exp(cum)
            b_t = kk_all[rows] * a_ref[d, rows, :] * e_neg
            k_t = kd_all[d][rows] * e_neg
            v = v_all[rows]
            for p in range(heads // 2):
                key = (d, j, p)
                keys.append(key)
                lanes = slice(p * pw, (p + 1) * pw)
                a_p[key], r_p[key], v_p[key] = a_t[:, lanes], r_t[:, lanes], v[:, lanes]
                bt_p[key] = jnp.transpose(_block_rows(b_t[:, lanes], hdim))
                kt_p[key] = jnp.transpose(_block_rows(k_t[:, lanes], hdim))
                ec_p[key] = jnp.transpose(_block_rows(e_row[:, lanes], hdim))
    ar = {k: jnp.concatenate([a_p[k], r_p[k]], axis=0) for k in keys}
    mb = {k: jnp.where(mask2[k[0]], _dot(ar[k], bt_p[k], RWKV_PASSES), 0.0) for k in keys}
    mk = {k: jnp.where(mask2[k[0]], _dot(ar[k], kt_p[k], RWKV_PASSES), 0.0) for k in keys}
    v_bd = {k: _block_rows(v_p[k], hdim) for k in keys}
    x = {k: _dot(mk[k], v_bd[k], RWKV_PASSES) for k in keys}
    h0 = {k: _dot(_fold_rows(ec_p[k] * kt_p[k]), v_bd[k], RWKV_PASSES) for k in keys}
    t = dict(zip(keys, _tri_inverse_pairs([-mb[k][:c] for k in keys])))
    wt = {k: _dot(t[k], _block_rows(a_p[k], hdim), RWKV_PASSES) for k in keys}
    ut = {k: _dot(t[k], _block_rows(x[k][:c], hdim), RWKV_PASSES) for k in keys}
    for k in keys:
        d, j, p = k
        lanes = slice(p * pw, (p + 1) * pw)
        f_ref[d, j * c:(j + 1) * c, lanes] = _fold_rows(ec_p[k])
        tiles = (wt[k], r_p[k], mb[k][c:], _fold_rows(ec_p[k] * bt_p[k]), ut[k], x[k][c:], h0[k])
        for i, tile in enumerate(tiles):
            o = (j * RWKV_TILES + i) * c
            g_ref[d, o:o + c, lanes] = tile.astype(BF16)


def _rwkv_intra(sq, proj, a, lw_dec, lw, heads):
    rdim = a.shape[-1]
    hdim = rdim // heads
    assert hdim == CHUNK and heads % 2 == 0 and 2 * hdim == LANES, "pair tiles are [CHUNK, 128 lanes]"
    cps = _largest_divisor(sq.ncc, (RWKV_INTRA_CHUNKS, 1))
    rows = cps * CHUNK
    consts = [lw["mu_rkv"], lw["k_k"], lw["k_a"], lw["r_k"], lw["bd_r"]]
    const = lambda shape: pl.BlockSpec(shape, lambda b, i: (0,) * len(shape))
    per_dir = pl.BlockSpec((2, None, rows, rdim), lambda b, i: (0, b, i, 0))
    out_g = pl.BlockSpec((2, None, RWKV_TILES * rows, rdim), lambda b, i: (0, b, i, 0))
    return pl.pallas_call(
        functools.partial(_rwkv_intra_body, sq=sq, heads=heads, hdim=hdim, cps=cps),
        grid=(sq.B, sq.nchunk // cps),
        in_specs=[pl.BlockSpec((None, rows, 3 * rdim), lambda b, i: (b, i, 0)),
                  sq.prev_rows(3 * rdim, tb=rows), sq.next_rows(3 * rdim, tb=rows), per_dir, per_dir]
        + [const(c.shape) for c in consts],
        out_specs=[per_dir, out_g, pl.BlockSpec((None, rows, rdim), lambda b, i: (b, i, 0))],
        out_shape=[jax.ShapeDtypeStruct((2, sq.B, sq.S, rdim), F32),
                   jax.ShapeDtypeStruct((2, sq.B, RWKV_TILES * sq.S, rdim), BF16),
                   jax.ShapeDtypeStruct((sq.B, sq.S, rdim), F32)],
        compiler_params=_params(2),
        name="rwkv_intra",
    )(proj, proj, proj, a, lw_dec, *consts)


def _backward_block(i, ncb, nblk):
    return jnp.where(i < ncb, ncb - 1 - i, nblk + ncb - 1 - i)


def _scan_chunks_per_step(sq):
    cps = _largest_divisor(sq.ncc, (SCAN_CHUNKS, 2, 1))
    assert sq.nchunk % cps == 0
    return cps


def _scan_specs(sq, heads, rows, width):
    cps = _scan_chunks_per_step(sq)
    ncb, nblk = sq.ncc // cps, sq.nchunk // cps
    fwd = pl.BlockSpec((None, None, heads, cps * rows, width), lambda b, i: (0, b, 0, i, 0))
    bwd = pl.BlockSpec((None, None, heads, cps * rows, width),
                       lambda b, i: (1, b, 0, _backward_block(i, ncb, nblk), 0))
    return fwd, bwd


def _scan_out_specs(sq, width):
    cps = _scan_chunks_per_step(sq)
    ncb, nblk = sq.ncc // cps, sq.nchunk // cps
    fwd = pl.BlockSpec((None, cps * CHUNK, width), lambda b, i: (b, i, 0))
    bwd = pl.BlockSpec((None, cps * CHUNK, width), lambda b, i: (b, _backward_block(i, ncb, nblk), 0))
    return [fwd, bwd]


def _gdn_intra_body(p_ref, pp_ref, pn_ref, ab_ref, cw_ref, bd_ref, f_ref, ga_ref, gb_ref, *, sq, heads, hdim, cps):
    c = CHUNK
    gdim = heads * hdim
    scale = hdim ** -0.5
    first, last = sq.edges(cps * c)
    proj = p_ref[...]
    width = cw_ref.shape[0]
    half = width // 2
    acc = proj * cw_ref[half:half + 1, :]
    for j in range(width):
        if j != half:
            acc = acc + _shift_rows(proj, pp_ref, pn_ref, j - half, first, last) * cw_ref[j:j + 1, :]
    y = acc * _sigmoid(acc)
    q_blk = y[:, :gdim]
    k_blk = y[:, gdim:2 * gdim]
    q_blk = q_blk * lax.rsqrt(_segsum(q_blk * q_blk, bd_ref) + RMS_EPS)
    k_blk = k_blk * lax.rsqrt(_segsum(k_blk * k_blk, bd_ref) + RMS_EPS)
    v_blk = y[:, 2 * gdim:]
    sl = [slice(h * hdim, (h + 1) * hdim) for h in range(heads)]
    keys = []
    beta, gc, g_last, gc_row, decay, kb, kq, kt_h, vb, strict_of, eye = {}, {}, {}, {}, {}, {}, {}, {}, {}, {}, None
    for j in range(cps):
        rows = slice(j * c, (j + 1) * c)
        ab = ab_ref[rows, :]
        tot_all = jnp.sum(ab, axis=0, keepdims=True)
        lane = lax.broadcasted_iota(jnp.int32, ab.shape, 1)
        k_all = k_blk[rows]
        k_tt = jnp.transpose(k_all)
        for d in range(2):
            incl, strict, eye = _causal_masks(d, c)
            strict_of[d] = strict
            gc_all = _dot_exact_lhs(jnp.where(incl, 1.0, 0.0).astype(BF16), ab)
            gc_all_t = jnp.transpose(gc_all)
            sub = lax.broadcasted_iota(jnp.int32, gc_all_t.shape, 0)
            for h in range(heads):
                key = (d, j, h)
                keys.append(key)
                pick_b = lane == d * heads + h
                pick_g = lane == 2 * heads + d * heads + h
                beta[key] = jnp.sum(jnp.where(pick_b, ab, 0.0), axis=1, keepdims=True)
                gc[key] = jnp.sum(jnp.where(pick_g, gc_all, 0.0), axis=1, keepdims=True)
                g_last[key] = jnp.sum(jnp.where(pick_g[:1], tot_all, 0.0), axis=1, keepdims=True)
                gc_row[key] = jnp.sum(jnp.where(sub == 2 * heads + d * heads + h, gc_all_t, 0.0),
                                      axis=0, keepdims=True)
                diff = gc[key] - gc_row[key]
                decay[key] = jnp.where(incl, jnp.exp(jnp.where(incl, diff, 0.0)), 0.0)
                kb[key] = k_all[:, sl[h]] * beta[key]
                kq[key] = jnp.concatenate([kb[key], q_blk[rows, sl[h]] * scale], axis=0)
                kt_h[key] = k_tt[sl[h]]
                vb[key] = v_blk[rows, sl[h]] * beta[key]
    m = {k: _dot(kq[k], kt_h[k], GDN_PASSES) for k in keys}
    lower = [jnp.where(strict_of[k[0]], m[k][:c] * decay[k], 0.0) for k in keys]
    t = dict(zip(keys, _tri_inverse_many(lower, eye)))
    e_gc = {k: jnp.exp(gc[k]) for k in keys}
    sol = {k: _dot(t[k], jnp.concatenate([vb[k], kb[k] * e_gc[k]], axis=1), GDN_PASSES) for k in keys}
    for k in keys:
        d, j, h = k
        of, oa, ob = j * (c + SUBLANES), j * 2 * c, j * (c + hdim)
        f_ref[d, h, of:of + c] = sol[k][:, :hdim]
        f_ref[d, h, of + c:of + c + SUBLANES] = jnp.broadcast_to(jnp.exp(g_last[k]), (SUBLANES, hdim))
        ga_ref[d, h, oa:oa + c] = sol[k][:, hdim:].astype(BF16)
        ga_ref[d, h, oa + c:oa + 2 * c] = (kq[k][c:] * e_gc[k]).astype(BF16)
        gb_ref[d, h, ob:ob + c] = (m[k][c:] * decay[k]).astype(BF16)
        gb_ref[d, h, ob + c:ob + c + hdim] = (kt_h[k] * jnp.exp(g_last[k] - gc_row[k])).astype(BF16)


def _gdn_intra(sq, proj, ab, lw, heads, gdim, col_block):
    hdim = gdim // heads
    cps = _largest_divisor(sq.ncc, (GDN_INTRA_CHUNKS, 2, 1))
    tb = cps * CHUNK
    w = 3 * gdim
    consts = [lw["conv"], lw["bd_g"]]
    const = lambda shape: pl.BlockSpec(shape, lambda b, i: (0,) * len(shape))
    small = pl.BlockSpec((None, tb, LANES), lambda b, i: (b, i, 0))
    rows = (CHUNK + SUBLANES, 2 * CHUNK, CHUNK + hdim)
    widths = (hdim, hdim, CHUNK)
    dtypes = (F32, BF16, BF16)
    outs = [pl.BlockSpec((2, None, heads, cps * r, wd), lambda b, i: (0, b, 0, i, 0)) for r, wd in zip(rows, widths)]
    shapes = [jax.ShapeDtypeStruct((2, sq.B, heads, sq.nchunk * r, wd), dt) for r, wd, dt in zip(rows, widths, dtypes)]
    return pl.pallas_call(
        functools.partial(_gdn_intra_body, sq=sq, heads=heads, hdim=hdim, cps=cps),
        grid=(sq.B, sq.nchunk // cps),
        in_specs=[pl.BlockSpec((None, tb, w), lambda b, i: (b, i, col_block)),
                  sq.prev_rows(w, col_block, tb=tb), sq.next_rows(w, col_block, tb=tb), small]
        + [const(c.shape) for c in consts],
        out_specs=outs,
        out_shape=shapes,
        compiler_params=_params(2),
        name="gdn_intra",
    )(proj, proj, proj, ab, *consts)


def _scan_body(rf0_ref, rg0_ref, rf1_ref, rg1_ref, f0_ref, ga0_ref, gb0_ref, f1_ref, ga1_ref, gb1_ref,
               rof_ref, rob_ref, gof_ref, gob_ref, h_ref, s_ref, *, r_heads, r_hdim, g_heads, g_hdim, cps):
    @pl.when(pl.program_id(1) == 0)
    def _():
        h_ref[...] = jnp.zeros_like(h_ref)
        s_ref[...] = jnp.zeros_like(s_ref)

    c = CHUNK
    dot = functools.partial(jnp.dot, preferred_element_type=F32)
    pw = 2 * r_hdim
    rfs, rgs, routs = (rf0_ref, rf1_ref), (rg0_ref, rg1_ref), (rof_ref, rob_ref)
    rch = [(d, p) for d in range(2) for p in range(r_heads // 2)]
    lanes = [slice(p * pw, (p + 1) * pw) for p in range(r_heads // 2)]
    hst = {ch: h_ref[ch[0], ch[1]] for ch in rch}
    fs, gas, gbs, gouts = (f0_ref, f1_ref), (ga0_ref, ga1_ref), (gb0_ref, gb1_ref), (gof_ref, gob_ref)
    gch = [(d, h) for d in range(2) for h in range(g_heads)]
    sst = {ch: s_ref[ch[0], ch[1]] for ch in gch}
    for step in range(cps):
        sub = (step, cps - 1 - step)

        def tile(d, p, i, n=1):
            o = (sub[d] * RWKV_TILES + i) * c
            return rgs[d][o:o + n * c, lanes[p]]

        of = [(c + SUBLANES) * sub[d] for d in range(2)]
        oa = [2 * c * sub[d] for d in range(2)]
        ob = [(c + g_hdim) * sub[d] for d in range(2)]
        m2 = {(d, p): dot(tile(d, p, 0, 2), hst[d, p].astype(BF16)) for d, p in rch}
        m = {(d, h): dot(gas[d][h, oa[d]:oa[d] + 2 * c, :], sst[d, h].astype(BF16)) for d, h in gch}
        ub = {(d, p): _block_rows((tile(d, p, 4) + m2[d, p][:c]).astype(BF16), r_hdim) for d, p in rch}
        vn = {(d, h): (fs[d][h, of[d]:of[d] + c, :] - m[d, h][:c]).astype(BF16) for d, h in gch}
        for d, p in rch:
            y = tile(d, p, 5) + m2[d, p][c:] + dot(tile(d, p, 2), ub[d, p])
            routs[d][sub[d] * c:(sub[d] + 1) * c, lanes[p]] = y
            decay = _block_rows(rfs[d][sub[d] * c:(sub[d] + 1) * c, lanes[p]], r_hdim)
            hst[d, p] = (decay * hst[d, p] + _block_rows(tile(d, p, 6), r_hdim)
                         + dot(_block_rows(tile(d, p, 3), r_hdim), ub[d, p]))
        for d, h in gch:
            gouts[d][sub[d] * c:(sub[d] + 1) * c, h * g_hdim:(h + 1) * g_hdim] = (
                m[d, h][c:] + dot(gbs[d][h, ob[d]:ob[d] + c, :], vn[d, h]))
            sst[d, h] = (sst[d, h] * fs[d][h, of[d] + c:of[d] + c + 1, :]
                         + dot(gbs[d][h, ob[d] + c:ob[d] + c + g_hdim, :], vn[d, h]))
    for d, p in rch:
        h_ref[d, p] = hst[d, p]
    for d, h in gch:
        s_ref[d, h] = sst[d, h]


def _scan(sq, rf, rg, gf, gga, ggb, r_heads, g_heads):
    rdim = rf.shape[-1]
    r_hdim = rdim // r_heads
    g_hdim = gf.shape[-1]
    gdim = g_heads * g_hdim
    cps = _scan_chunks_per_step(sq)
    ncb, nblk = sq.ncc // cps, sq.nchunk // cps
    fwd = lambda rows: pl.BlockSpec((None, None, rows, rdim), lambda b, i: (0, b, i, 0))
    bwd = lambda rows: pl.BlockSpec((None, None, rows, rdim),
                                    lambda b, i: (1, b, _backward_block(i, ncb, nblk), 0))
    rows_f, rows_g = cps * CHUNK, cps * RWKV_TILES * CHUNK
    gspecs = [_scan_specs(sq, g_heads, arr.shape[3] // sq.nchunk, arr.shape[4]) for arr in (gf, gga, ggb)]
    r_one = jax.ShapeDtypeStruct((sq.B, sq.S, rdim), F32)
    g_one = jax.ShapeDtypeStruct((sq.B, sq.S, gdim), F32)
    return pl.pallas_call(
        functools.partial(_scan_body, r_heads=r_heads, r_hdim=r_hdim, g_heads=g_heads, g_hdim=g_hdim, cps=cps),
        grid=(sq.B, sq.nchunk // cps),
        in_specs=[fwd(rows_f), fwd(rows_g), bwd(rows_f), bwd(rows_g)]
        + [s[0] for s in gspecs] + [s[1] for s in gspecs],
        out_specs=_scan_out_specs(sq, rdim) + _scan_out_specs(sq, gdim),
        out_shape=[r_one, r_one, g_one, g_one],
        scratch_shapes=[pltpu.VMEM((2, r_heads // 2, 2 * r_hdim, 2 * r_hdim), F32),
                        pltpu.VMEM((2, g_heads, g_hdim, g_hdim), F32)],
        compiler_params=_params(2),
        name="scan",
    )(rf, rg, rf, rg, gf, gga, ggb, gf, gga, ggb)


def _rope(x, cos, sin_signed):
    n = x.shape[-1]
    lane = lax.broadcasted_iota(jnp.int32, x.shape, 1)
    partner = jnp.where((lane & 1) == 0, pltpu.roll(x, n - 1, axis=1), pltpu.roll(x, 1, axis=1))
    return x * cos + partner * sin_signed


def _attn_prep_body(q_ref, kv_ref, cs_ref, qn_ref, kn_ref, bd_ref, qo_ref, ko_ref, vo_ref,
                    *, q_heads, kv_heads, hdim):
    kvd = kv_heads * hdim
    cos = cs_ref[:, :kvd]
    sin = cs_ref[:, kvd:]
    reps = q_heads // kv_heads
    cos_q = jnp.concatenate([cos] * reps, axis=1)
    sin_q = jnp.concatenate([sin] * reps, axis=1)
    inv_n = 1.0 / hdim
    q = q_ref[...]
    q = q * lax.rsqrt(_segsum(q * q, bd_ref) * inv_n + RMS_EPS) * qn_ref[...]
    q = _rope(q, cos_q, sin_q) * (hdim ** -0.5)
    kv = kv_ref[...]
    k = kv[:, :kvd]
    kbd = bd_ref[:kvd, :kvd]
    hi, lo = _split2(k * k)
    ms = (jnp.dot(hi, kbd, preferred_element_type=F32) + jnp.dot(lo, kbd, preferred_element_type=F32)) * inv_n
    k = k * lax.rsqrt(ms + RMS_EPS) * kn_ref[...]
    k = _rope(k, cos, sin)
    v = kv[:, kvd:]
    for h in range(q_heads):
        qo_ref[h] = q[:, h * hdim:(h + 1) * hdim].astype(BF16)
    k_t = jnp.transpose(k)
    for h in range(kv_heads):
        ko_ref[h] = k_t[h * hdim:(h + 1) * hdim].astype(BF16)
        vh = v[:, h * hdim:(h + 1) * hdim]
        vo_ref[h] = jnp.concatenate([vh, jnp.ones_like(vh)], axis=1).astype(BF16)


def _attn_prep(sq, proj, cs_tab, lw, q_heads, kv_heads, hdim, q_col_block, kv_col_block):
    qd, kvd = q_heads * hdim, kv_heads * hdim
    consts = [lw["q_norm"], lw["k_norm"], lw["bd_r"]]
    k_spec = pl.BlockSpec((None, kv_heads, hdim, sq.TB), lambda b, j: (b, 0, 0, j))
    return pl.pallas_call(
        functools.partial(_attn_prep_body, q_heads=q_heads, kv_heads=kv_heads, hdim=hdim),
        grid=sq.grid,
        in_specs=[sq.rows(qd, q_col_block), sq.rows(2 * kvd, kv_col_block),
                  pl.BlockSpec((sq.TB, 2 * kvd), lambda b, j: (j, 0))]
        + [sq.const(c.shape) for c in consts],
        out_specs=[sq.heads(q_heads, hdim), k_spec, sq.heads(kv_heads, 2 * hdim)],
        out_shape=[jax.ShapeDtypeStruct((sq.B, q_heads, sq.S, hdim), BF16),
                   jax.ShapeDtypeStruct((sq.B, kv_heads, hdim, sq.S), BF16),
                   jax.ShapeDtypeStruct((sq.B, kv_heads, sq.S, 2 * hdim), BF16)],
        compiler_params=_params(2),
        name="attn_prep",
    )(proj, proj, cs_tab, *consts)


def _attn_body(q_ref, kt_ref, v_ref, o_ref, s_ref, p_ref, *, group, n_ctx_qblocks, n_ctx_keys):
    tq, hdim = q_ref.shape[1], q_ref.shape[2]
    n_keys = kt_ref.shape[1]

    def attend(nk):
        tiles = range(0, nk, KEY_BLOCK)
        for g in range(group):
            q = q_ref[g]
            rows = slice(g * tq, (g + 1) * tq)
            mx = None
            for t in tiles:
                s = jnp.dot(q, kt_ref[:, t:t + KEY_BLOCK], preferred_element_type=F32)
                s_ref[:, t:t + KEY_BLOCK] = s
                for c0 in range(0, KEY_BLOCK, LANES):
                    part = s[:, c0:c0 + LANES]
                    mx = part if mx is None else jnp.maximum(mx, part)
            m = jnp.broadcast_to(jnp.max(mx, axis=1, keepdims=True), (tq, LANES))
            for t in tiles:
                for c0 in range(t, t + KEY_BLOCK, LANES):
                    p_ref[rows, c0:c0 + LANES] = jnp.exp(s_ref[:, c0:c0 + LANES] - m).astype(BF16)
        acc = jnp.dot(p_ref[:, 0:nk], v_ref[0:nk, :], preferred_element_type=F32)
        out = acc[:, :hdim] / acc[:, hdim:]
        for g in range(group):
            o_ref[:, g * hdim:(g + 1) * hdim] = out[g * tq:(g + 1) * tq, :]

    is_ctx = pl.program_id(2) < n_ctx_qblocks

    @pl.when(is_ctx)
    def _():
        attend(n_ctx_keys)

    @pl.when(jnp.logical_not(is_ctx))
    def _():
        attend(n_keys)


def _attention(sq, q, kt, v):
    b, q_heads, s, hdim = q.shape
    kv_heads = kt.shape[1]
    group = q_heads // kv_heads
    tq = _largest_divisor(sq.ctx, (128, 64))
    assert sq.ctx % KEY_BLOCK == 0 and s % KEY_BLOCK == 0
    return pl.pallas_call(
        functools.partial(_attn_body, group=group, n_ctx_qblocks=sq.ctx // tq, n_ctx_keys=sq.ctx),
        grid=(b, kv_heads, s // tq),
        in_specs=[pl.BlockSpec((None, group, tq, hdim), lambda bi, g, i: (bi, g, i, 0)),
                  pl.BlockSpec((None, None, hdim, s), lambda bi, g, i: (bi, g, 0, 0)),
                  pl.BlockSpec((None, None, s, 2 * hdim), lambda bi, g, i: (bi, g, 0, 0))],
        out_specs=pl.BlockSpec((None, tq, group * hdim), lambda bi, g, i: (bi, i, g)),
        out_shape=jax.ShapeDtypeStruct((b, s, q_heads * hdim), F32),
        scratch_shapes=[pltpu.VMEM((tq, s), F32), pltpu.VMEM((group * tq, s), BF16)],
        compiler_params=_params(3),
        name="attention",
    )(q, kt, v)


def _merge_body(x_ref, rf_ref, rb_ref, bonus_ref, rg_ref, gf_ref, gb_ref, z_ref, yc_ref, gate_ref, mod_ref,
                lnw_ref, lnb_ref, bdr_ref, gn_ref, bdg_ref, wa_ref, wb_ref, wc_ref, wo_ref, o_ref,
                *, gate_row, r_hdim, g_hdim, ctx_len):
    d = x_ref.shape[-1]
    o = rf_ref[...] + rb_ref[...]
    mean = _segsum(o, bdr_ref) * (1.0 / r_hdim)
    cen = o - mean
    var = _segsum(cen * cen, bdr_ref) * (1.0 / r_hdim)
    ya = (cen * lax.rsqrt(var + RWKV_LNX_EPS) * lnw_ref[...] + lnb_ref[...] + bonus_ref[...]) * rg_ref[...]
    o = gf_ref[...] + gb_ref[...]
    ms = _segsum(o * o, bdg_ref) * (1.0 / g_hdim)
    z = z_ref[...]
    yb = o * lax.rsqrt(ms + RMS_EPS) * gn_ref[...] * (z * _sigmoid(z))
    m = gate_ref[:, :d] * _dot(ya, wa_ref[...])
    m = m + gate_ref[:, d:2 * d] * _dot(yb, wb_ref[...])
    m = m + gate_ref[:, 2 * d:] * _dot(yc_ref[...], wc_ref[...])
    y = _dot(m, wo_ref[...])
    rows = x_ref.shape[0]
    row = pl.program_id(1) * rows + lax.broadcasted_iota(jnp.int32, (rows, 1), 0)
    gate = jnp.where(row < ctx_len, mod_ref[0, gate_row:gate_row + 1, :], mod_ref[1, gate_row:gate_row + 1, :])
    o_ref[...] = x_ref[...] + gate * y


def _merge(sq, xs, o_rf, o_rb, bonus, r_gate, o_gf, o_gb, proj, z_col_block, yc, gates, modtab, lw,
           r_heads, g_heads):
    d = xs.shape[-1]
    rdim, gdim = o_rf.shape[-1], o_gf.shape[-1]
    consts = [lw["lnx_w"], lw["lnx_b"], lw["bd_r"], lw["gdn_norm"], lw["bd_g"],
              lw["w_up_a"], lw["w_up_b"], lw["w_up_c"], lw["w_out"]]
    tb = _largest_divisor(sq.S, FFN_ROWS)
    rows = lambda width, col_block=0: pl.BlockSpec((None, tb, width), lambda b, j: (b, j, col_block))
    const = lambda shape: pl.BlockSpec(shape, lambda b, j: (0,) * len(shape))
    return pl.pallas_call(
        functools.partial(_merge_body, gate_row=2, r_hdim=rdim // r_heads, g_hdim=gdim // g_heads,
                          ctx_len=sq.ctx),
        grid=(sq.B, sq.S // tb),
        in_specs=[rows(d)] + [rows(rdim)] * 4 + [rows(gdim)] * 2
        + [rows(gdim, z_col_block), rows(yc.shape[-1]), rows(3 * d),
           pl.BlockSpec((None, 2, 6, d), lambda b, j: (b, 0, 0, 0))]
        + [const(c.shape) for c in consts],
        out_specs=rows(d),
        out_shape=jax.ShapeDtypeStruct(xs.shape, F32),
        compiler_params=_params(2),
        name="merge",
    )(xs, o_rf, o_rb, bonus, r_gate, o_gf, o_gb, proj, yc, gates, modtab, *consts)


def _ffn_body(x_ref, g_ref, mod_ref, w1_ref, w3_ref, w2_ref, o_ref, *, h_chunk, ctx_len):
    rows = x_ref.shape[0]
    row = pl.program_id(1) * rows + lax.broadcasted_iota(jnp.int32, (rows, 1), 0)
    is_ctx = row < ctx_len
    pick = lambda i: jnp.where(is_ctx, mod_ref[0, i:i + 1, :], mod_ref[1, i:i + 1, :])
    x = x_ref[...]
    y = x * lax.rsqrt(jnp.mean(x * x, axis=-1, keepdims=True) + RMS_EPS) * g_ref[...]
    h = (y * (1.0 + pick(4)) + pick(3)).astype(BF16)
    hidden = w1_ref.shape[1]
    acc = jnp.zeros(x.shape, F32)
    for c0 in range(0, hidden, h_chunk):
        a = jnp.dot(h, w1_ref[:, c0:c0 + h_chunk], preferred_element_type=F32)
        b = jnp.dot(h, w3_ref[:, c0:c0 + h_chunk], preferred_element_type=F32)
        t = (a * _sigmoid(a) * b).astype(BF16)
        acc = acc + jnp.dot(t, w2_ref[c0:c0 + h_chunk, :], preferred_element_type=F32)
    o_ref[...] = x + pick(5) * acc


def _ffn(sq, xs, g, modtab, lw):
    d = xs.shape[-1]
    consts = [lw["ffn_w1"], lw["ffn_w3"], lw["ffn_w2"]]
    h_chunk = _largest_divisor(lw["ffn_w1"].shape[1], (512, 256, 128))
    tb = _largest_divisor(sq.S, FFN_ROWS)
    rows = pl.BlockSpec((None, tb, d), lambda b, j: (b, j, 0))
    const = lambda shape: pl.BlockSpec(shape, lambda b, j: (0,) * len(shape))
    return pl.pallas_call(
        functools.partial(_ffn_body, h_chunk=h_chunk, ctx_len=sq.ctx),
        grid=(sq.B, sq.S // tb),
        in_specs=[rows, const((1, d)), pl.BlockSpec((None, 2, 6, d), lambda b, j: (b, 0, 0, 0))]
        + [const(c.shape) for c in consts],
        out_specs=rows,
        out_shape=jax.ShapeDtypeStruct(xs.shape, F32),
        compiler_params=_params(2),
        name="ffn",
    )(xs, g.reshape(1, d), modtab, *consts)


def _block_diag_ones(n, seg):
    idx = jnp.arange(n) // seg
    return (idx[:, None] == idx[None, :]).astype(BF16)


def _block_diag2(m):
    z = jnp.zeros_like(m[0])
    return jnp.concatenate([jnp.concatenate([m[0], z], axis=1), jnp.concatenate([z, m[1]], axis=1)], axis=0)


def _pad_to(x, axis, size):
    pad = [(0, 0)] * x.ndim
    pad[axis] = (0, size - x.shape[axis])
    return jnp.pad(x, pad)


def _rope_table(ctx_len, seq_len, hdim, kv_heads):
    rows = seq_len // GRID_W
    row = jnp.repeat(jnp.arange(rows), GRID_W).astype(F32)
    col = jnp.tile(jnp.arange(GRID_W), rows).astype(F32)
    half = hdim // 2
    inv = ROPE_THETA ** (-jnp.arange(0, half, 2, dtype=F32) / half)
    ang = jnp.concatenate([row[:, None] * inv, col[:, None] * inv], axis=-1)
    cos = jnp.repeat(jnp.cos(ang), 2, axis=1)
    sin = jnp.repeat(jnp.sin(ang), 2, axis=1) * jnp.tile(jnp.array([-1.0, 1.0], F32), half)
    cos = jnp.concatenate([jnp.ones((ctx_len, hdim), F32), cos], axis=0)
    sin = jnp.concatenate([jnp.zeros((ctx_len, hdim), F32), sin], axis=0)
    return jnp.concatenate([jnp.tile(cos, (1, kv_heads)), jnp.tile(sin, (1, kv_heads))], axis=1)


def kernel(x, c, ctx, c_ctx, ada_w, ada_b, norm1, norm2, w_in, rwkv_mu_x, rwkv_mu_rkv, rwkv_w0, rwkv_w1, rwkv_w2, rwkv_a0, rwkv_a1, rwkv_a2, rwkv_g1, rwkv_g2, rwkv_k_k, rwkv_k_a, rwkv_r_k, rwkv_lnx_w, rwkv_lnx_b, gdn_conv, gdn_w_alpha, gdn_dt_bias, gdn_A_log, gdn_w_beta, gdn_norm, attn_q_norm, attn_k_norm, w_up_a, w_up_b, w_up_c, w_gate, b_gate, w_out, ffn_w1, ffn_w3, ffn_w2, final_norm):
    batch, seq_len, d = x.shape
    ctx_len = ctx.shape[1]
    depth = ada_w.shape[0]
    sq = _Seq(batch, ctx_len, seq_len)

    r_heads, r_hdim = rwkv_r_k.shape[1], rwkv_r_k.shape[2]
    rdim = r_heads * r_hdim
    g_heads, g_hdim = gdn_w_alpha.shape[-1], gdn_norm.shape[-1]
    gdim = g_heads * g_hdim
    a_hdim = attn_q_norm.shape[-1]
    qd = w_up_c.shape[1]
    q_heads = qd // a_hdim
    kvd = (w_in.shape[-1] - 3 * rdim - 4 * gdim - qd) // 2
    kv_heads = kvd // a_hdim
    assert rdim == gdim == qd and r_hdim == a_hdim, "lane-segment constants are shared between mixers"
    assert (3 * rdim) % (3 * gdim) == 0 and (3 * rdim + 3 * gdim) % gdim == 0
    gdn_col = (3 * rdim) // (3 * gdim)
    z_col = (3 * rdim + 3 * gdim) // gdim
    q_col = (3 * rdim + 4 * gdim) // qd
    assert (3 * rdim + 4 * gdim + qd) % (2 * kvd) == 0
    kv_col = (3 * rdim + 4 * gdim + qd) // (2 * kvd)

    bd_r = _block_diag_ones(rdim, r_hdim)
    bd_g = _block_diag_ones(gdim, g_hdim)
    cs_tab = _rope_table(ctx_len, seq_len, a_hdim, kv_heads)
    n_beta = 2 * g_heads
    assert 2 * n_beta <= LANES

    cond = jnp.concatenate([c, c_ctx[None, :]], axis=0)
    cond = _pad_to(cond, 0, -(-(batch + 1) // SUBLANES) * SUBLANES)

    xs = jnp.concatenate([ctx, x], axis=1)
    for l in range(depth):
        w1c = jnp.concatenate([rwkv_w1[l, 0], rwkv_w1[l, 1]], axis=1)
        a1c = jnp.concatenate([rwkv_a1[l, 0], rwkv_a1[l, 1]], axis=1)
        g1w = _pad_to(rwkv_g1[l], 1, 2 * LANES)
        wab = _pad_to(jnp.concatenate([gdn_w_beta[l, 0], gdn_w_beta[l, 1],
                                       gdn_w_alpha[l, 0], gdn_w_alpha[l, 1]], axis=1), 1, LANES)
        mu = rwkv_mu_x[l]
        splits = (w1c.shape[1], w1c.shape[1] + a1c.shape[1], w1c.shape[1] + a1c.shape[1] + g1w.shape[1])
        lw = {
            "mu_rkv": rwkv_mu_rkv[l],
            "wh": jnp.concatenate([w1c, a1c, g1w, wab], axis=1).astype(BF16),
            "wd": jnp.concatenate([mu[0][:, None] * w1c, mu[1][:, None] * a1c, mu[2][:, None] * g1w],
                                  axis=1).astype(BF16),
            "splits": splits,
            "w2": _block_diag2(rwkv_w2[l]).astype(BF16),
            "w0": rwkv_w0[l].reshape(1, 2 * rdim),
            "a2": _block_diag2(rwkv_a2[l]).astype(BF16),
            "a0": rwkv_a0[l].reshape(1, 2 * rdim),
            "g2": _pad_to(rwkv_g2[l], 0, 2 * LANES).astype(BF16),
            "abb": _pad_to(jnp.concatenate([jnp.zeros((n_beta,), F32), gdn_dt_bias[l].reshape(-1)]), 0, LANES).reshape(1, LANES),
            "alog": _pad_to(jnp.concatenate([jnp.zeros((n_beta,), F32), gdn_A_log[l].reshape(-1)]), 0, LANES).reshape(1, LANES),
            "n_beta": n_beta,
            "k_k": rwkv_k_k[l].reshape(1, rdim), "k_a": rwkv_k_a[l].reshape(1, rdim),
            "r_k": rwkv_r_k[l].reshape(1, rdim),
            "lnx_w": rwkv_lnx_w[l].reshape(1, rdim), "lnx_b": rwkv_lnx_b[l].reshape(1, rdim),
            "bd_r": bd_r, "bd_g": bd_g,
            "conv": jnp.transpose(gdn_conv[l]),
            "gdn_norm": jnp.tile(gdn_norm[l], g_heads).reshape(1, gdim),
            "q_norm": jnp.tile(attn_q_norm[l], q_heads).reshape(1, qd),
            "k_norm": jnp.tile(attn_k_norm[l], kv_heads).reshape(1, kvd),
            "w_up_a": w_up_a[l].astype(BF16), "w_up_b": w_up_b[l].astype(BF16),
            "w_up_c": w_up_c[l].astype(BF16), "w_out": w_out[l].astype(BF16),
            "ffn_w1": ffn_w1[l].astype(BF16), "ffn_w3": ffn_w3[l].astype(BF16), "ffn_w2": ffn_w2[l].astype(BF16),
        }
        mod = _matmul(cond, ada_w[l].astype(BF16), bias=ada_b[l], pre_act="silu")
        mod_x = mod[:batch].reshape(batch, 6, d)
        mod_c = jnp.broadcast_to(mod[batch].reshape(1, 6, d), (batch, 6, d))
        modtab = jnp.stack([mod_c, mod_x], axis=1)

        h, proj, gates = _pre(sq, xs, norm1[l], modtab, w_in[l].astype(BF16), w_gate[l].astype(BF16), b_gate[l])
        lw_dec, a_iclr, g_out, ab = _lora(sq, h, lw)

        rf, rg, bonus = _rwkv_intra(sq, proj, a_iclr, lw_dec, lw, r_heads)
        gf, gga, ggb = _gdn_intra(sq, proj, ab, lw, g_heads, gdim, gdn_col)
        o_rf, o_rb, o_gf, o_gb = _scan(sq, rf, rg, gf, gga, ggb, r_heads, g_heads)

        aq, ak, av = _attn_prep(sq, proj, cs_tab, lw, q_heads, kv_heads, a_hdim, q_col, kv_col)
        yc = _attention(sq, aq, ak, av)

        xs = _merge(sq, xs, o_rf, o_rb, bonus, g_out, o_gf, o_gb, proj, z_col, yc, gates, modtab, lw,
                    r_heads, g_heads)
        xs = _ffn(sq, xs, norm2[l], modtab, lw)
    return _final_norm(sq, xs, final_norm)
```

```python
import functools

import jax
import jax.numpy as jnp
from jax import lax
from jax.experimental import pallas as pl
from jax.experimental.pallas import tpu as pltpu

F32 = jnp.float32
BF16 = jnp.bfloat16

RMS_EPS = 1e-6
RWKV_LNX_EPS = 64e-5
ROPE_THETA = 10000.0
GRID_W = 64

SUBLANES = 8
LANES = 128
CHUNK = 64
MAX_TOKEN_BLOCK = 256
MAX_MM_ROWS = 512
FFN_ROWS = (544, 512, 256, 128)
KEY_BLOCK = 256
RWKV_PASSES = 1
GDN_PASSES = 1
INV_PASSES = 1
RWKV_INTRA_CHUNKS = 2
GDN_INTRA_CHUNKS = 4
SCAN_CHUNKS = 4
VMEM_LIMIT = 56 * 1024 * 1024


def _sigmoid(x):
    return 1.0 / (1.0 + jnp.exp(-x))


def _softplus(x):
    return jnp.maximum(x, 0.0) + jnp.log(1.0 + jnp.exp(-jnp.abs(x)))


def _split2(x):
    hi = x.astype(BF16)
    lo = (x - hi.astype(F32)).astype(BF16)
    return hi, lo


def _mm(a, b, dims, passes):
    d = functools.partial(lax.dot_general, dimension_numbers=(dims, ((), ())), preferred_element_type=F32)
    if passes == 1:
        return d(a.astype(BF16), b.astype(BF16))
    ah, al = _split2(a)
    bh, bl = _split2(b)
    return d(ah, bh) + (d(ah, bl) + d(al, bh))


def _dot(a, b, passes=1):
    return _mm(a, b, ((1,), (0,)), passes)


def _split3(x):
    hi = x.astype(BF16)
    r1 = x - hi.astype(F32)
    mid = r1.astype(BF16)
    lo = (r1 - mid.astype(F32)).astype(BF16)
    return hi, mid, lo


def _dot_exact_lhs(m01, x):
    d = functools.partial(jnp.dot, preferred_element_type=F32)
    hi, mid, lo = _split3(x)
    return d(m01, hi) + (d(m01, mid) + d(m01, lo))


def _segsum(x, bd_ref):
    d = functools.partial(jnp.dot, preferred_element_type=F32)
    hi, lo = _split2(x)
    bd = bd_ref[...]
    return d(hi, bd) + d(lo, bd)


def _row_iota(shape):
    return lax.broadcasted_iota(jnp.int32, shape, 0)


def _shift_rows(x, prev_ref, next_ref, k, first, last):
    n = x.shape[0]
    y = pltpu.roll(x, (-k) % n, axis=0)
    row = _row_iota((SUBLANES, x.shape[1]))
    if k < 0:
        tile = y[:SUBLANES]
        for i in range(-k):
            edge = prev_ref[SUBLANES + k + i:SUBLANES + k + i + 1, :]
            edge = jnp.where(first, 0.0, edge)
            tile = jnp.where(row == i, edge, tile)
        return jnp.concatenate([tile, y[SUBLANES:]], axis=0)
    tile = y[n - SUBLANES:]
    for i in range(k):
        edge = next_ref[i:i + 1, :]
        edge = jnp.where(last, 0.0, edge)
        tile = jnp.where(row == SUBLANES - k + i, edge, tile)
    return jnp.concatenate([y[:n - SUBLANES], tile], axis=0)


def _largest_divisor(n, candidates):
    for c in candidates:
        if n % c == 0:
            return c
    raise ValueError(f"no block size among {candidates} divides {n}")


class _Seq:
    def __init__(self, batch, ctx_len, seq_len):
        self.B = batch
        self.ctx = ctx_len
        self.S = ctx_len + seq_len
        self.TB = _largest_divisor(ctx_len, (MAX_TOKEN_BLOCK, 128, 64))
        assert seq_len % self.TB == 0 and self.TB % CHUNK == 0
        self.ncb = ctx_len // self.TB
        self.nblk = self.S // self.TB
        self.ncc = ctx_len // CHUNK
        self.nchunk = self.S // CHUNK
        self.grid = (batch, self.nblk)

    def rows(self, width, col_block=0):
        return pl.BlockSpec((None, self.TB, width), lambda b, j: (b, j, col_block))

    def rows2(self, width):
        return pl.BlockSpec((2, None, self.TB, width), lambda b, j: (0, b, j, 0))

    def heads(self, n_heads, width):
        return pl.BlockSpec((None, n_heads, self.TB, width), lambda b, j: (b, 0, j, 0))

    def prev_rows(self, width, col_block=0, tb=None):
        per = (tb or self.TB) // SUBLANES
        return pl.BlockSpec((None, SUBLANES, width),
                            lambda b, j: (b, jnp.maximum(j * per - 1, 0), col_block))

    def next_rows(self, width, col_block=0, tb=None):
        per = (tb or self.TB) // SUBLANES
        top = self.S // SUBLANES - 1
        return pl.BlockSpec((None, SUBLANES, width),
                            lambda b, j: (b, jnp.minimum((j + 1) * per, top), col_block))

    def const(self, shape):
        zeros = (0,) * len(shape)
        return pl.BlockSpec(shape, lambda b, j: zeros)

    def edges(self, tb=None):
        tb = tb or self.TB
        j = pl.program_id(1)
        ncb, nblk = self.ctx // tb, self.S // tb
        first = (j == 0) | (j == ncb)
        last = (j == ncb - 1) | (j == nblk - 1)
        return first, last


def _params(n_axes):
    return pltpu.CompilerParams(dimension_semantics=("arbitrary",) * n_axes,
                                vmem_limit_bytes=VMEM_LIMIT)


def _mm_body(*refs, act, pre_act, n_chunk, has_bias):
    if has_bias:
        x_ref, w_ref, b_ref, o_ref = refs
    else:
        x_ref, w_ref, o_ref = refs
        b_ref = None
    x = x_ref[...]
    if pre_act == "silu":
        x = x * _sigmoid(x)
    xb = x.astype(BF16)
    n = o_ref.shape[-1]
    for n0 in range(0, n, n_chunk):
        y = jnp.dot(xb, w_ref[:, n0:n0 + n_chunk], preferred_element_type=F32)
        if b_ref is not None:
            y = y + b_ref[:, n0:n0 + n_chunk]
        if act == "sigmoid":
            y = _sigmoid(y)
        o_ref[:, n0:n0 + n_chunk] = y.astype(o_ref.dtype)


def _matmul(x, w, bias=None, act=None, pre_act=None, out_dtype=F32):
    m, k = x.shape
    n = w.shape[1]
    tm = m if m <= MAX_MM_ROWS else _largest_divisor(m, (MAX_MM_ROWS, 256, 128, 64, 32, 16, 8))
    n_chunk = _largest_divisor(n, (512, 256, 128))
    in_specs = [pl.BlockSpec((tm, k), lambda i: (i, 0)),
                pl.BlockSpec((k, n), lambda i: (0, 0))]
    args = [x, w]
    if bias is not None:
        in_specs.append(pl.BlockSpec((1, n), lambda i: (0, 0)))
        args.append(bias.reshape(1, n))
    return pl.pallas_call(
        functools.partial(_mm_body, act=act, pre_act=pre_act, n_chunk=n_chunk,
                          has_bias=bias is not None),
        grid=(m // tm,),
        in_specs=in_specs,
        out_specs=pl.BlockSpec((tm, n), lambda i: (i, 0)),
        out_shape=jax.ShapeDtypeStruct((m, n), out_dtype),
        compiler_params=_params(1),
        name="matmul",
    )(*args)


def _pre_body(x_ref, g_ref, mod_ref, win_ref, wg_ref, bg_ref, h_ref, p_ref, gate_ref, *, n_chunk, ctx_len):
    rows = x_ref.shape[0]
    row = pl.program_id(1) * rows + lax.broadcasted_iota(jnp.int32, (rows, 1), 0)
    is_ctx = row < ctx_len
    pick = lambda i: jnp.where(is_ctx, mod_ref[0, i:i + 1, :], mod_ref[1, i:i + 1, :])
    x = x_ref[...]
    y = x * lax.rsqrt(jnp.mean(x * x, axis=-1, keepdims=True) + RMS_EPS) * g_ref[...]
    h = y * (1.0 + pick(1)) + pick(0)
    h_ref[...] = h
    hb = h.astype(BF16)
    for n0 in range(0, p_ref.shape[-1], n_chunk):
        p_ref[:, n0:n0 + n_chunk] = jnp.dot(hb, win_ref[:, n0:n0 + n_chunk], preferred_element_type=F32)
    for n0 in range(0, gate_ref.shape[-1], n_chunk):
        z = jnp.dot(hb, wg_ref[:, n0:n0 + n_chunk], preferred_element_type=F32) + bg_ref[:, n0:n0 + n_chunk]
        gate_ref[:, n0:n0 + n_chunk] = _sigmoid(z).astype(gate_ref.dtype)


def _pre(sq, xs, g, modtab, w_in, w_gate, b_gate):
    d = xs.shape[-1]
    n_in, n_gate = w_in.shape[1], w_gate.shape[1]
    n_chunk = _largest_divisor(n_in, (256, 128))
    assert n_gate % n_chunk == 0
    tb = _largest_divisor(sq.S, FFN_ROWS)
    rows = lambda width: pl.BlockSpec((None, tb, width), lambda b, j: (b, j, 0))
    const = lambda shape: pl.BlockSpec(shape, lambda b, j: (0,) * len(shape))
    resident = lambda shape: pl.BlockSpec(shape, lambda b, j: (0,) * len(shape), pipeline_mode=pl.Buffered(1))
    return pl.pallas_call(
        functools.partial(_pre_body, n_chunk=n_chunk, ctx_len=sq.ctx),
        grid=(sq.B, sq.S // tb),
        in_specs=[rows(d), const((1, d)), pl.BlockSpec((None, 2, 6, d), lambda b, j: (b, 0, 0, 0)),
                  resident(w_in.shape), resident(w_gate.shape), const((1, n_gate))],
        out_specs=[rows(d), rows(n_in), rows(n_gate)],
        out_shape=[jax.ShapeDtypeStruct(xs.shape, F32),
                   jax.ShapeDtypeStruct(xs.shape[:2] + (n_in,), F32),
                   jax.ShapeDtypeStruct(xs.shape[:2] + (n_gate,), BF16)],
        compiler_params=_params(2),
        name="pre",
    )(xs, g.reshape(1, d), modtab, w_in, w_gate, b_gate.reshape(1, n_gate))


def _final_norm_body(x_ref, g_ref, o_ref):
    x = x_ref[...]
    y = x * lax.rsqrt(jnp.mean(x * x, axis=-1, keepdims=True) + RMS_EPS)
    o_ref[...] = y * g_ref[...]


def _final_norm(sq, xs, g):
    d = xs.shape[-1]
    ncb = sq.ncb
    return pl.pallas_call(
        _final_norm_body,
        grid=(sq.B, sq.nblk - ncb),
        in_specs=[pl.BlockSpec((None, sq.TB, d), lambda b, j: (b, j + ncb, 0)),
                  pl.BlockSpec((1, d), lambda b, j: (0, 0))],
        out_specs=pl.BlockSpec((None, sq.TB, d), lambda b, j: (b, j, 0)),
        out_shape=jax.ShapeDtypeStruct((sq.B, sq.S - sq.ctx, d), F32),
        compiler_params=_params(2),
        name="final_norm",
    )(xs, g.reshape(1, d))


def _lora_body(h_ref, hp_ref, hn_ref, wh_ref, wd_ref, w2_ref, w0_ref, a2_ref, a0_ref,
               g2_ref, abb_ref, alog_ref, lw_ref, a_ref, g_ref, ab_ref, *, sq, rdim, n_beta, splits):
    first, last = sq.edges()
    h = h_ref[...]
    nb = 0.5 * (_shift_rows(h, hp_ref, hn_ref, -1, first, last)
                + _shift_rows(h, hp_ref, hn_ref, 1, first, last))
    zh = _dot(h, wh_ref[...])
    zd = _dot(nb - h, wd_ref[...])
    s1, s2, s3 = splits
    wl = w0_ref[...] + _dot(jnp.tanh(zh[:, :s1] + zd[:, :s1]), w2_ref[...])
    lw = -(jnp.exp(-0.5) * _sigmoid(wl))
    lw_ref[0] = lw[:, :rdim]
    lw_ref[1] = lw[:, rdim:]
    a = _sigmoid(a0_ref[...] + _dot(zh[:, s1:s2] + zd[:, s1:s2], a2_ref[...]))
    a_ref[0] = a[:, :rdim]
    a_ref[1] = a[:, rdim:]
    g_ref[...] = _dot(_sigmoid(zh[:, s2:s3] + zd[:, s2:s3]), g2_ref[...])
    z = zh[:, s3:]
    col = lax.broadcasted_iota(jnp.int32, z.shape, 1)
    gl = -jnp.exp(alog_ref[...]) * _softplus(z + abb_ref[...])
    ab_ref[...] = jnp.where(col < n_beta, _sigmoid(z), gl)


def _lora(sq, h, lw):
    d = h.shape[-1]
    rdim = lw["w0"].shape[-1] // 2
    consts = [lw["wh"], lw["wd"], lw["w2"], lw["w0"], lw["a2"], lw["a0"],
              lw["g2"], lw["abb"], lw["alog"]]
    bsd = (sq.B, sq.S)
    return pl.pallas_call(
        functools.partial(_lora_body, sq=sq, rdim=rdim, n_beta=lw["n_beta"], splits=lw["splits"]),
        grid=sq.grid,
        in_specs=[sq.rows(d), sq.prev_rows(d), sq.next_rows(d)] + [sq.const(c.shape) for c in consts],
        out_specs=[sq.rows2(rdim), sq.rows2(rdim), sq.rows(rdim), sq.rows(LANES)],
        out_shape=[jax.ShapeDtypeStruct((2,) + bsd + (rdim,), F32),
                   jax.ShapeDtypeStruct((2,) + bsd + (rdim,), F32),
                   jax.ShapeDtypeStruct(bsd + (rdim,), F32),
                   jax.ShapeDtypeStruct(bsd + (LANES,), F32)],
        compiler_params=_params(2),
        name="lora",
    )(h, h, h, *consts)


def _causal_masks(d, c):
    row = lax.broadcasted_iota(jnp.int32, (c, c), 0)
    col = lax.broadcasted_iota(jnp.int32, (c, c), 1)
    delta = jnp.where(d == 0, row - col, col - row)
    eye = jnp.where(row == col, 1.0, 0.0).astype(F32)
    return delta >= 0, delta > 0, eye


def _stacked_mask(d, c, reps=1):
    row = lax.broadcasted_iota(jnp.int32, (2 * c, reps * c), 0)
    col = lax.broadcasted_iota(jnp.int32, (2 * c, reps * c), 1) & (c - 1)
    rr = row & (c - 1)
    delta = jnp.where(d == 0, rr - col, col - rr)
    return (delta > 0) | ((row >= c) & (delta == 0))


def _block_rows(x, w):
    left = lax.broadcasted_iota(jnp.int32, x.shape, 1) < w
    zero = jnp.zeros_like(x)
    return jnp.concatenate([jnp.where(left, x, zero), jnp.where(left, zero, x)], axis=0)


def _fold_rows(y):
    n = y.shape[0] // 2
    return y[:n] + y[n:]


def _tri_inverse_pairs(ls):
    n = ls[0].shape[0]
    row = lax.broadcasted_iota(jnp.int32, (n, 2 * n), 0)
    col = lax.broadcasted_iota(jnp.int32, (n, 2 * n), 1) & (n - 1)
    eye = jnp.where(row == col, 1.0, 0.0).astype(F32)
    same = (row >> 1) == (col >> 1)
    ts = [eye - jnp.where(same, l, 0.0) for l in ls]
    for k in range(2, n.bit_length()):
        off = ((row >> k) == (col >> k)) & ((row >> (k - 1)) != (col >> (k - 1)))
        tl = [_dot(t, _block_rows(jnp.where(off, l, 0.0), n), INV_PASSES) for t, l in zip(ts, ls)]
        ts = [t - _dot(x, _block_rows(t, n), INV_PASSES) for x, t in zip(tl, ts)]
    return ts


def _tri_inverse_many(ls, eye):
    n = ls[0].shape[0]
    row = lax.broadcasted_iota(jnp.int32, (n, n), 0)
    col = lax.broadcasted_iota(jnp.int32, (n, n), 1)
    same = (row >> 1) == (col >> 1)
    ts = [eye - jnp.where(same, l, 0.0) for l in ls]
    for k in range(2, n.bit_length()):
        off = ((row >> k) == (col >> k)) & ((row >> (k - 1)) != (col >> (k - 1)))
        tl = [_dot(t, jnp.where(off, l, 0.0), INV_PASSES) for t, l in zip(ts, ls)]
        ts = [t - _dot(x, t, INV_PASSES) for x, t in zip(tl, ts)]
    return ts


RWKV_TILES = 7

def _rwkv_intra_body(p_ref, pp_ref, pn_ref, a_ref, lw_ref, mu_ref, kk_ref, ka_ref, rk_ref, bd_ref,
                     f_ref, g_ref, bonus_ref, *, sq, heads, hdim, cps):
    c = CHUNK
    pw = 2 * hdim
    rdim = heads * hdim
    first, last = sq.edges(cps * c)
    proj = p_ref[...]
    nb = 0.5 * (_shift_rows(proj, pp_ref, pn_ref, -1, first, last)
                + _shift_rows(proj, pp_ref, pn_ref, 1, first, last))
    dlt = nb - proj
    r_all = proj[:, :rdim] + dlt[:, :rdim] * mu_ref[0:1, :]
    k_all = proj[:, rdim:2 * rdim] + dlt[:, rdim:2 * rdim] * mu_ref[1:2, :]
    v_all = proj[:, 2 * rdim:] + dlt[:, 2 * rdim:] * mu_ref[2:3, :]
    kn = k_all * kk_ref[...]
    kk_all = kn * lax.rsqrt(_segsum(kn * kn, bd_ref) + RMS_EPS)
    kd_all = [k_all * (1.0 + (a_ref[d] - 1.0) * ka_ref[...]) for d in range(2)]
    rk = r_all * rk_ref[...]
    bonus_ref[...] = (_segsum(rk * kd_all[0], bd_ref) + _segsum(rk * kd_all[1], bd_ref)) * v_all
    keys, a_p, r_p, v_p, bt_p, kt_p, ec_p, mask2 = [], {}, {}, {}, {}, {}, {}, {}
    for d in range(2):
        incl, _, _ = _causal_masks(d, c)
        m01 = jnp.where(incl, 1.0, 0.0).astype(BF16)
        mask2[d] = _stacked_mask(d, c, reps=2)
        for j in range(cps):
            rows = slice(j * c, (j + 1) * c)
            lw = lw_ref[d, rows, :]
            cum = _dot_exact_lhs(m01, lw)
            e_neg = jnp.exp(-cum)
            e_row = jnp.broadcast_to(jnp.exp(jnp.sum(lw, axis=0, keepdims=True)), lw.shape)
            a_t = -kk_all[rows] * jnp.exp(cum - lw)
            r_t = r_all[rows] * jnp.exp(cum)
            b_t = kk_all[rows] * a_ref[d, rows, :] * e_neg
            k_t = kd_all[d][rows] * e_neg
            v = v_all[rows]
            for p in range(heads // 2):
                key = (d, j, p)
                keys.append(key)
                lanes = slice(p * pw, (p + 1) * pw)
                a_p[key], r_p[key], v_p[key] = a_t[:, lanes], r_t[:, lanes], v[:, lanes]
                bt_p[key] = jnp.transpose(_block_rows(b_t[:, lanes], hdim))
                kt_p[key] = jnp.transpose(_block_rows(k_t[:, lanes], hdim))
                ec_p[key] = jnp.transpose(_block_rows(e_row[:, lanes], hdim))
    ar = {k: jnp.concatenate([a_p[k], r_p[k]], axis=0) for k in keys}
    mb = {k: jnp.where(mask2[k[0]], _dot(ar[k], bt_p[k], RWKV_PASSES), 0.0) for k in keys}
    mk = {k: jnp.where(mask2[k[0]], _dot(ar[k], kt_p[k], RWKV_PASSES), 0.0) for k in keys}
    v_bd = {k: _block_rows(v_p[k], hdim) for k in keys}
    x = {k: _dot(mk[k], v_bd[k], RWKV_PASSES) for k in keys}
    h0 = {k: _dot(_fold_rows(ec_p[k] * kt_p[k]), v_bd[k], RWKV_PASSES) for k in keys}
    t = dict(zip(keys, _tri_inverse_pairs([-mb[k][:c] for k in keys])))
    wt = {k: _dot(t[k], _block_rows(a_p[k], hdim), RWKV_PASSES) for k in keys}
    ut = {k: _dot(t[k], _block_rows(x[k][:c], hdim), RWKV_PASSES) for k in keys}
    for k in keys:
        d, j, p = k
        lanes = slice(p * pw, (p + 1) * pw)
        f_ref[d, j * c:(j + 1) * c, lanes] = _fold_rows(ec_p[k])
        tiles = (wt[k], r_p[k], mb[k][c:], _fold_rows(ec_p[k] * bt_p[k]), ut[k], x[k][c:], h0[k])
        for i, tile in enumerate(tiles):
            o = (j * RWKV_TILES + i) * c
            g_ref[d, o:o + c, lanes] = tile.astype(BF16)


def _rwkv_intra(sq, proj, a, lw_dec, lw, heads):
    rdim = a.shape[-1]
    hdim = rdim // heads
    assert hdim == CHUNK and heads % 2 == 0 and 2 * hdim == LANES, "pair tiles are [CHUNK, 128 lanes]"
    cps = _largest_divisor(sq.ncc, (RWKV_INTRA_CHUNKS, 1))
    rows = cps * CHUNK
    consts = [lw["mu_rkv"], lw["k_k"], lw["k_a"], lw["r_k"], lw["bd_r"]]
    const = lambda shape: pl.BlockSpec(shape, lambda b, i: (0,) * len(shape))
    per_dir = pl.BlockSpec((2, None, rows, rdim), lambda b, i: (0, b, i, 0))
    out_g = pl.BlockSpec((2, None, RWKV_TILES * rows, rdim), lambda b, i: (0, b, i, 0))
    return pl.pallas_call(
        functools.partial(_rwkv_intra_body, sq=sq, heads=heads, hdim=hdim, cps=cps),
        grid=(sq.B, sq.nchunk // cps),
        in_specs=[pl.BlockSpec((None, rows, 3 * rdim), lambda b, i: (b, i, 0)),
                  sq.prev_rows(3 * rdim, tb=rows), sq.next_rows(3 * rdim, tb=rows), per_dir, per_dir]
        + [const(c.shape) for c in consts],
        out_specs=[per_dir, out_g, pl.BlockSpec((None, rows, rdim), lambda b, i: (b, i, 0))],
        out_shape=[jax.ShapeDtypeStruct((2, sq.B, sq.S, rdim), F32),
                   jax.ShapeDtypeStruct((2, sq.B, RWKV_TILES * sq.S, rdim), BF16),
                   jax.ShapeDtypeStruct((sq.B, sq.S, rdim), F32)],
        compiler_params=_params(2),
        name="rwkv_intra",
    )(proj, proj, proj, a, lw_dec, *consts)


def _backward_block(i, ncb, nblk):
    return jnp.where(i < ncb, ncb - 1 - i, nblk + ncb - 1 - i)


def _scan_chunks_per_step(sq):
    cps = _largest_divisor(sq.ncc, (SCAN_CHUNKS, 2, 1))
    assert sq.nchunk % cps == 0
    return cps


def _scan_specs(sq, heads, rows, width):
    cps = _scan_chunks_per_step(sq)
    ncb, nblk = sq.ncc // cps, sq.nchunk // cps
    fwd = pl.BlockSpec((None, None, heads, cps * rows, width), lambda b, i: (0, b, 0, i, 0))
    bwd = pl.BlockSpec((None, None, heads, cps * rows, width),
                       lambda b, i: (1, b, 0, _backward_block(i, ncb, nblk), 0))
    return fwd, bwd


def _scan_out_specs(sq, width):
    cps = _scan_chunks_per_step(sq)
    ncb, nblk = sq.ncc // cps, sq.nchunk // cps
    fwd = pl.BlockSpec((None, cps * CHUNK, width), lambda b, i: (b, i, 0))
    bwd = pl.BlockSpec((None, cps * CHUNK, width), lambda b, i: (b, _backward_block(i, ncb, nblk), 0))
    return [fwd, bwd]


def _gdn_intra_body(p_ref, pp_ref, pn_ref, ab_ref, cw_ref, bd_ref, f_ref, ga_ref, gb_ref, *, sq, heads, hdim, cps):
    c = CHUNK
    gdim = heads * hdim
    scale = hdim ** -0.5
    first, last = sq.edges(cps * c)
    proj = p_ref[...]
    width = cw_ref.shape[0]
    half = width // 2
    acc = proj * cw_ref[half:half + 1, :]
    for j in range(width):
        if j != half:
            acc = acc + _shift_rows(proj, pp_ref, pn_ref, j - half, first, last) * cw_ref[j:j + 1, :]
    y = acc * _sigmoid(acc)
    q_blk = y[:, :gdim]
    k_blk = y[:, gdim:2 * gdim]
    q_blk = q_blk * lax.rsqrt(_segsum(q_blk * q_blk, bd_ref) + RMS_EPS)
    k_blk = k_blk * lax.rsqrt(_segsum(k_blk * k_blk, bd_ref) + RMS_EPS)
    v_blk = y[:, 2 * gdim:]
    sl = [slice(h * hdim, (h + 1) * hdim) for h in range(heads)]
    keys = []
    beta, gc, g_last, gc_row, decay, kb, kq, kt_h, vb, strict_of, eye = {}, {}, {}, {}, {}, {}, {}, {}, {}, {}, None
    for j in range(cps):
        rows = slice(j * c, (j + 1) * c)
        ab = ab_ref[rows, :]
        tot_all = jnp.sum(ab, axis=0, keepdims=True)
        lane = lax.broadcasted_iota(jnp.int32, ab.shape, 1)
        k_all = k_blk[rows]
        k_tt = jnp.transpose(k_all)
        for d in range(2):
            incl, strict, eye = _causal_masks(d, c)
            strict_of[d] = strict
            gc_all = _dot_exact_lhs(jnp.where(incl, 1.0, 0.0).astype(BF16), ab)
            gc_all_t = jnp.transpose(gc_all)
            sub = lax.broadcasted_iota(jnp.int32, gc_all_t.shape, 0)
            for h in range(heads):
                key = (d, j, h)
                keys.append(key)
                pick_b = lane == d * heads + h
                pick_g = lane == 2 * heads + d * heads + h
                beta[key] = jnp.sum(jnp.where(pick_b, ab, 0.0), axis=1, keepdims=True)
                gc[key] = jnp.sum(jnp.where(pick_g, gc_all, 0.0), axis=1, keepdims=True)
                g_last[key] = jnp.sum(jnp.where(pick_g[:1], tot_all, 0.0), axis=1, keepdims=True)
                gc_row[key] = jnp.sum(jnp.where(sub == 2 * heads + d * heads + h, gc_all_t, 0.0),
                                      axis=0, keepdims=True)
                diff = gc[key] - gc_row[key]
                decay[key] = jnp.where(incl, jnp.exp(jnp.where(incl, diff, 0.0)), 0.0)
                kb[key] = k_all[:, sl[h]] * beta[key]
                kq[key] = jnp.concatenate([kb[key], q_blk[rows, sl[h]] * scale], axis=0)
                kt_h[key] = k_tt[sl[h]]
                vb[key] = v_blk[rows, sl[h]] * beta[key]
    m = {k: _dot(kq[k], kt_h[k], GDN_PASSES) for k in keys}
    lower = [jnp.where(strict_of[k[0]], m[k][:c] * decay[k], 0.0) for k in keys]
    t = dict(zip(keys, _tri_inverse_many(lower, eye)))
    e_gc = {k: jnp.exp(gc[k]) for k in keys}
    sol = {k: _dot(t[k], jnp.concatenate([vb[k], kb[k] * e_gc[k]], axis=1), GDN_PASSES) for k in keys}
    for k in keys:
        d, j, h = k
        of, oa, ob = j * (c + SUBLANES), j * 2 * c, j * (c + hdim)
        f_ref[d, h, of:of + c] = sol[k][:, :hdim]
        f_ref[d, h, of + c:of + c + SUBLANES] = jnp.broadcast_to(jnp.exp(g_last[k]), (SUBLANES, hdim))
        ga_ref[d, h, oa:oa + c] = sol[k][:, hdim:].astype(BF16)
        ga_ref[d, h, oa + c:oa + 2 * c] = (kq[k][c:] * e_gc[k]).astype(BF16)
        gb_ref[d, h, ob:ob + c] = (m[k][c:] * decay[k]).astype(BF16)
        gb_ref[d, h, ob + c:ob + c + hdim] = (kt_h[k] * jnp.exp(g_last[k] - gc_row[k])).astype(BF16)


def _gdn_intra(sq, proj, ab, lw, heads, gdim, col_block):
    hdim = gdim // heads
    cps = _largest_divisor(sq.ncc, (GDN_INTRA_CHUNKS, 2, 1))
    tb = cps * CHUNK
    w = 3 * gdim
    consts = [lw["conv"], lw["bd_g"]]
    const = lambda shape: pl.BlockSpec(shape, lambda b, i: (0,) * len(shape))
    small = pl.BlockSpec((None, tb, LANES), lambda b, i: (b, i, 0))
    rows = (CHUNK + SUBLANES, 2 * CHUNK, CHUNK + hdim)
    widths = (hdim, hdim, CHUNK)
    dtypes = (F32, BF16, BF16)
    outs = [pl.BlockSpec((2, None, heads, cps * r, wd), lambda b, i: (0, b, 0, i, 0)) for r, wd in zip(rows, widths)]
    shapes = [jax.ShapeDtypeStruct((2, sq.B, heads, sq.nchunk * r, wd), dt) for r, wd, dt in zip(rows, widths, dtypes)]
    return pl.pallas_call(
        functools.partial(_gdn_intra_body, sq=sq, heads=heads, hdim=hdim, cps=cps),
        grid=(sq.B, sq.nchunk // cps),
        in_specs=[pl.BlockSpec((None, tb, w), lambda b, i: (b, i, col_block)),
                  sq.prev_rows(w, col_block, tb=tb), sq.next_rows(w, col_block, tb=tb), small]
        + [const(c.shape) for c in consts],
        out_specs=outs,
        out_shape=shapes,
        compiler_params=_params(2),
        name="gdn_intra",
    )(proj, proj, proj, ab, *consts)


def _scan_body(rf0_ref, rg0_ref, rf1_ref, rg1_ref, f0_ref, ga0_ref, gb0_ref, f1_ref, ga1_ref, gb1_ref,
               rof_ref, rob_ref, gof_ref, gob_ref, h_ref, s_ref, *, r_heads, r_hdim, g_heads, g_hdim, cps):
    @pl.when(pl.program_id(1) == 0)
    def _():
        h_ref[...] = jnp.zeros_like(h_ref)
        s_ref[...] = jnp.zeros_like(s_ref)

    c = CHUNK
    dot = functools.partial(jnp.dot, preferred_element_type=F32)
    pw = 2 * r_hdim
    rfs, rgs, routs = (rf0_ref, rf1_ref), (rg0_ref, rg1_ref), (rof_ref, rob_ref)
    rch = [(d, p) for d in range(2) for p in range(r_heads // 2)]
    lanes = [slice(p * pw, (p + 1) * pw) for p in range(r_heads // 2)]
    hst = {ch: h_ref[ch[0], ch[1]] for ch in rch}
    fs, gas, gbs, gouts = (f0_ref, f1_ref), (ga0_ref, ga1_ref), (gb0_ref, gb1_ref), (gof_ref, gob_ref)
    gch = [(d, h) for d in range(2) for h in range(g_heads)]
    sst = {ch: s_ref[ch[0], ch[1]] for ch in gch}
    for step in range(cps):
        sub = (step, cps - 1 - step)

        def tile(d, p, i, n=1):
            o = (sub[d] * RWKV_TILES + i) * c
            return rgs[d][o:o + n * c, lanes[p]]

        of = [(c + SUBLANES) * sub[d] for d in range(2)]
        oa = [2 * c * sub[d] for d in range(2)]
        ob = [(c + g_hdim) * sub[d] for d in range(2)]
        m2 = {(d, p): dot(tile(d, p, 0, 2), hst[d, p].astype(BF16)) for d, p in rch}
        m = {(d, h): dot(gas[d][h, oa[d]:oa[d] + 2 * c, :], sst[d, h].astype(BF16)) for d, h in gch}
        ub = {(d, p): _block_rows((tile(d, p, 4) + m2[d, p][:c]).astype(BF16), r_hdim) for d, p in rch}
        vn = {(d, h): (fs[d][h, of[d]:of[d] + c, :] - m[d, h][:c]).astype(BF16) for d, h in gch}
        for d, p in rch:
            y = tile(d, p, 5) + m2[d, p][c:] + dot(tile(d, p, 2), ub[d, p])
            routs[d][sub[d] * c:(sub[d] + 1) * c, lanes[p]] = y
            decay = _block_rows(rfs[d][sub[d] * c:(sub[d] + 1) * c, lanes[p]], r_hdim)
            hst[d, p] = (decay * hst[d, p] + _block_rows(tile(d, p, 6), r_hdim)
                         + dot(_block_rows(tile(d, p, 3), r_hdim), ub[d, p]))
        for d, h in gch:
            gouts[d][sub[d] * c:(sub[d] + 1) * c, h * g_hdim:(h + 1) * g_hdim] = (
                m[d, h][c:] + dot(gbs[d][h, ob[d]:ob[d] + c, :], vn[d, h]))
            sst[d, h] = (sst[d, h] * fs[d][h, of[d] + c:of[d] + c + 1, :]
                         + dot(gbs[d][h, ob[d] + c:ob[d] + c + g_hdim, :], vn[d, h]))
    for d, p in rch:
        h_ref[d, p] = hst[d, p]
    for d, h in gch:
        s_ref[d, h] = sst[d, h]


def _scan(sq, rf, rg, gf, gga, ggb, r_heads, g_heads):
    rdim = rf.shape[-1]
    r_hdim = rdim // r_heads
    g_hdim = gf.shape[-1]
    gdim = g_heads * g_hdim
    cps = _scan_chunks_per_step(sq)
    ncb, nblk = sq.ncc // cps, sq.nchunk // cps
    fwd = lambda rows: pl.BlockSpec((None, None, rows, rdim), lambda b, i: (0, b, i, 0))
    bwd = lambda rows: pl.BlockSpec((None, None, rows, rdim),
                                    lambda b, i: (1, b, _backward_block(i, ncb, nblk), 0))
    rows_f, rows_g = cps * CHUNK, cps * RWKV_TILES * CHUNK
    gspecs = [_scan_specs(sq, g_heads, arr.shape[3] // sq.nchunk, arr.shape[4]) for arr in (gf, gga, ggb)]
    r_one = jax.ShapeDtypeStruct((sq.B, sq.S, rdim), F32)
    g_one = jax.ShapeDtypeStruct((sq.B, sq.S, gdim), F32)
    return pl.pallas_call(
        functools.partial(_scan_body, r_heads=r_heads, r_hdim=r_hdim, g_heads=g_heads, g_hdim=g_hdim, cps=cps),
        grid=(sq.B, sq.nchunk // cps),
        in_specs=[fwd(rows_f), fwd(rows_g), bwd(rows_f), bwd(rows_g)]
        + [s[0] for s in gspecs] + [s[1] for s in gspecs],
        out_specs=_scan_out_specs(sq, rdim) + _scan_out_specs(sq, gdim),
        out_shape=[r_one, r_one, g_one, g_one],
        scratch_shapes=[pltpu.VMEM((2, r_heads // 2, 2 * r_hdim, 2 * r_hdim), F32),
                        pltpu.VMEM((2, g_heads, g_hdim, g_hdim), F32)],
        compiler_params=_params(2),
        name="scan",
    )(rf, rg, rf, rg, gf, gga, ggb, gf, gga, ggb)


def _rope(x, cos, sin_signed):
    n = x.shape[-1]
    lane = lax.broadcasted_iota(jnp.int32, x.shape, 1)
    partner = jnp.where((lane & 1) == 0, pltpu.roll(x, n - 1, axis=1), pltpu.roll(x, 1, axis=1))
    return x * cos + partner * sin_signed


def _attn_prep_body(q_ref, kv_ref, cs_ref, qn_ref, kn_ref, bd_ref, qo_ref, ko_ref, vo_ref,
                    *, q_heads, kv_heads, hdim):
    kvd = kv_heads * hdim
    cos = cs_ref[:, :kvd]
    sin = cs_ref[:, kvd:]
    reps = q_heads // kv_heads
    cos_q = jnp.concatenate([cos] * reps, axis=1)
    sin_q = jnp.concatenate([sin] * reps, axis=1)
    inv_n = 1.0 / hdim
    q = q_ref[...]
    q = q * lax.rsqrt(_segsum(q * q, bd_ref) * inv_n + RMS_EPS) * qn_ref[...]
    q = _rope(q, cos_q, sin_q) * (hdim ** -0.5)
    kv = kv_ref[...]
    k = kv[:, :kvd]
    kbd = bd_ref[:kvd, :kvd]
    hi, lo = _split2(k * k)
    ms = (jnp.dot(hi, kbd, preferred_element_type=F32) + jnp.dot(lo, kbd, preferred_element_type=F32)) * inv_n
    k = k * lax.rsqrt(ms + RMS_EPS) * kn_ref[...]
    k = _rope(k, cos, sin)
    v = kv[:, kvd:]
    for h in range(q_heads):
        qo_ref[h] = q[:, h * hdim:(h + 1) * hdim].astype(BF16)
    k_t = jnp.transpose(k)
    for h in range(kv_heads):
        ko_ref[h] = k_t[h * hdim:(h + 1) * hdim].astype(BF16)
        vh = v[:, h * hdim:(h + 1) * hdim]
        vo_ref[h] = jnp.concatenate([vh, jnp.ones_like(vh)], axis=1).astype(BF16)


def _attn_prep(sq, proj, cs_tab, lw, q_heads, kv_heads, hdim, q_col_block, kv_col_block):
    qd, kvd = q_heads * hdim, kv_heads * hdim
    consts = [lw["q_norm"], lw["k_norm"], lw["bd_r"]]
    k_spec = pl.BlockSpec((None, kv_heads, hdim, sq.TB), lambda b, j: (b, 0, 0, j))
    return pl.pallas_call(
        functools.partial(_attn_prep_body, q_heads=q_heads, kv_heads=kv_heads, hdim=hdim),
        grid=sq.grid,
        in_specs=[sq.rows(qd, q_col_block), sq.rows(2 * kvd, kv_col_block),
                  pl.BlockSpec((sq.TB, 2 * kvd), lambda b, j: (j, 0))]
        + [sq.const(c.shape) for c in consts],
        out_specs=[sq.heads(q_heads, hdim), k_spec, sq.heads(kv_heads, 2 * hdim)],
        out_shape=[jax.ShapeDtypeStruct((sq.B, q_heads, sq.S, hdim), BF16),
                   jax.ShapeDtypeStruct((sq.B, kv_heads, hdim, sq.S), BF16),
                   jax.ShapeDtypeStruct((sq.B, kv_heads, sq.S, 2 * hdim), BF16)],
        compiler_params=_params(2),
        name="attn_prep",
    )(proj, proj, cs_tab, *consts)


def _attn_body(q_ref, kt_ref, v_ref, o_ref, s_ref, p_ref, *, group, n_ctx_qblocks, n_ctx_keys):
    tq, hdim = q_ref.shape[1], q_ref.shape[2]
    n_keys = kt_ref.shape[1]

    def attend(nk):
        tiles = range(0, nk, KEY_BLOCK)
        for g in range(group):
            q = q_ref[g]
            rows = slice(g * tq, (g + 1) * tq)
            mx = None
            for t in tiles:
                s = jnp.dot(q, kt_ref[:, t:t + KEY_BLOCK], preferred_element_type=F32)
                s_ref[:, t:t + KEY_BLOCK] = s
                for c0 in range(0, KEY_BLOCK, LANES):
                    part = s[:, c0:c0 + LANES]
                    mx = part if mx is None else jnp.maximum(mx, part)
            m = jnp.broadcast_to(jnp.max(mx, axis=1, keepdims=True), (tq, LANES))
            for t in tiles:
                for c0 in range(t, t + KEY_BLOCK, LANES):
                    p_ref[rows, c0:c0 + LANES] = jnp.exp(s_ref[:, c0:c0 + LANES] - m).astype(BF16)
        acc = jnp.dot(p_ref[:, 0:nk], v_ref[0:nk, :], preferred_element_type=F32)
        out = acc[:, :hdim] / acc[:, hdim:]
        for g in range(group):
            o_ref[:, g * hdim:(g + 1) * hdim] = out[g * tq:(g + 1) * tq, :]

    is_ctx = pl.program_id(2) < n_ctx_qblocks

    @pl.when(is_ctx)
    def _():
        attend(n_ctx_keys)

    @pl.when(jnp.logical_not(is_ctx))
    def _():
        attend(n_keys)


def _attention(sq, q, kt, v):
    b, q_heads, s, hdim = q.shape
    kv_heads = kt.shape[1]
    group = q_heads // kv_heads
    tq = _largest_divisor(sq.ctx, (128, 64))
    assert sq.ctx % KEY_BLOCK == 0 and s % KEY_BLOCK == 0
    return pl.pallas_call(
        functools.partial(_attn_body, group=group, n_ctx_qblocks=sq.ctx // tq, n_ctx_keys=sq.ctx),
        grid=(b, kv_heads, s // tq),
        in_specs=[pl.BlockSpec((None, group, tq, hdim), lambda bi, g, i: (bi, g, i, 0)),
                  pl.BlockSpec((None, None, hdim, s), lambda bi, g, i: (bi, g, 0, 0)),
                  pl.BlockSpec((None, None, s, 2 * hdim), lambda bi, g, i: (bi, g, 0, 0))],
        out_specs=pl.BlockSpec((None, tq, group * hdim), lambda bi, g, i: (bi, i, g)),
        out_shape=jax.ShapeDtypeStruct((b, s, q_heads * hdim), F32),
        scratch_shapes=[pltpu.VMEM((tq, s), F32), pltpu.VMEM((group * tq, s), BF16)],
        compiler_params=_params(3),
        name="attention",
    )(q, kt, v)


def _merge_body(x_ref, rf_ref, rb_ref, bonus_ref, rg_ref, gf_ref, gb_ref, z_ref, yc_ref, gate_ref, mod_ref,
                lnw_ref, lnb_ref, bdr_ref, gn_ref, bdg_ref, wa_ref, wb_ref, wc_ref, wo_ref, o_ref,
                *, gate_row, r_hdim, g_hdim, ctx_len):
    d = x_ref.shape[-1]
    o = rf_ref[...] + rb_ref[...]
    mean = _segsum(o, bdr_ref) * (1.0 / r_hdim)
    cen = o - mean
    var = _segsum(cen * cen, bdr_ref) * (1.0 / r_hdim)
    ya = (cen * lax.rsqrt(var + RWKV_LNX_EPS) * lnw_ref[...] + lnb_ref[...] + bonus_ref[...]) * rg_ref[...]
    o = gf_ref[...] + gb_ref[...]
    ms = _segsum(o * o, bdg_ref) * (1.0 / g_hdim)
    z = z_ref[...]
    yb = o * lax.rsqrt(ms + RMS_EPS) * gn_ref[...] * (z * _sigmoid(z))
    m = gate_ref[:, :d] * _dot(ya, wa_ref[...])
    m = m + gate_ref[:, d:2 * d] * _dot(yb, wb_ref[...])
    m = m + gate_ref[:, 2 * d:] * _dot(yc_ref[...], wc_ref[...])
    y = _dot(m, wo_ref[...])
    rows = x_ref.shape[0]
    row = pl.program_id(1) * rows + lax.broadcasted_iota(jnp.int32, (rows, 1), 0)
    gate = jnp.where(row < ctx_len, mod_ref[0, gate_row:gate_row + 1, :], mod_ref[1, gate_row:gate_row + 1, :])
    o_ref[...] = x_ref[...] + gate * y


def _merge(sq, xs, o_rf, o_rb, bonus, r_gate, o_gf, o_gb, proj, z_col_block, yc, gates, modtab, lw,
           r_heads, g_heads):
    d = xs.shape[-1]
    rdim, gdim = o_rf.shape[-1], o_gf.shape[-1]
    consts = [lw["lnx_w"], lw["lnx_b"], lw["bd_r"], lw["gdn_norm"], lw["bd_g"],
              lw["w_up_a"], lw["w_up_b"], lw["w_up_c"], lw["w_out"]]
    tb = _largest_divisor(sq.S, FFN_ROWS)
    rows = lambda width, col_block=0: pl.BlockSpec((None, tb, width), lambda b, j: (b, j, col_block))
    const = lambda shape: pl.BlockSpec(shape, lambda b, j: (0,) * len(shape))
    return pl.pallas_call(
        functools.partial(_merge_body, gate_row=2, r_hdim=rdim // r_heads, g_hdim=gdim // g_heads,
                          ctx_len=sq.ctx),
        grid=(sq.B, sq.S // tb),
        in_specs=[rows(d)] + [rows(rdim)] * 4 + [rows(gdim)] * 2
        + [rows(gdim, z_col_block), rows(yc.shape[-1]), rows(3 * d),
           pl.BlockSpec((None, 2, 6, d), lambda b, j: (b, 0, 0, 0))]
        + [const(c.shape) for c in consts],
        out_specs=rows(d),
        out_shape=jax.ShapeDtypeStruct(xs.shape, F32),
        compiler_params=_params(2),
        name="merge",
    )(xs, o_rf, o_rb, bonus, r_gate, o_gf, o_gb, proj, yc, gates, modtab, *consts)


def _ffn_body(x_ref, g_ref, mod_ref, w1_ref, w3_ref, w2_ref, o_ref, *, h_chunk, ctx_len):
    rows = x_ref.shape[0]
    row = pl.program_id(1) * rows + lax.broadcasted_iota(jnp.int32, (rows, 1), 0)
    is_ctx = row < ctx_len
    pick = lambda i: jnp.where(is_ctx, mod_ref[0, i:i + 1, :], mod_ref[1, i:i + 1, :])
    x = x_ref[...]
    y = x * lax.rsqrt(jnp.mean(x * x, axis=-1, keepdims=True) + RMS_EPS) * g_ref[...]
    h = (y * (1.0 + pick(4)) + pick(3)).astype(BF16)
    hidden = w1_ref.shape[1]
    acc = jnp.zeros(x.shape, F32)
    for c0 in range(0, hidden, h_chunk):
        a = jnp.dot(h, w1_ref[:, c0:c0 + h_chunk], preferred_element_type=F32)
        b = jnp.dot(h, w3_ref[:, c0:c0 + h_chunk], preferred_element_type=F32)
        t = (a * _sigmoid(a) * b).astype(BF16)
        acc = acc + jnp.dot(t, w2_ref[c0:c0 + h_chunk, :], preferred_element_type=F32)
    o_ref[...] = x + pick(5) * acc


def _ffn(sq, xs, g, modtab, lw):
    d = xs.shape[-1]
    consts = [lw["ffn_w1"], lw["ffn_w3"], lw["ffn_w2"]]
    h_chunk = _largest_divisor(lw["ffn_w1"].shape[1], (512, 256, 128))
    tb = _largest_divisor(sq.S, FFN_ROWS)
    rows = pl.BlockSpec((None, tb, d), lambda b, j: (b, j, 0))
    const = lambda shape: pl.BlockSpec(shape, lambda b, j: (0,) * len(shape))
    return pl.pallas_call(
        functools.partial(_ffn_body, h_chunk=h_chunk, ctx_len=sq.ctx),
        grid=(sq.B, sq.S // tb),
        in_specs=[rows, const((1, d)), pl.BlockSpec((None, 2, 6, d), lambda b, j: (b, 0, 0, 0))]
        + [const(c.shape) for c in consts],
        out_specs=rows,
        out_shape=jax.ShapeDtypeStruct(xs.shape, F32),
        compiler_params=_params(2),
        name="ffn",
    )(xs, g.reshape(1, d), modtab, *consts)


def _block_diag_ones(n, seg):
    idx = jnp.arange(n) // seg
    return (idx[:, None] == idx[None, :]).astype(BF16)


def _block_diag2(m):
    z = jnp.zeros_like(m[0])
    return jnp.concatenate([jnp.concatenate([m[0], z], axis=1), jnp.concatenate([z, m[1]], axis=1)], axis=0)


def _pad_to(x, axis, size):
    pad = [(0, 0)] * x.ndim
    pad[axis] = (0, size - x.shape[axis])
    return jnp.pad(x, pad)


def _rope_table(ctx_len, seq_len, hdim, kv_heads):
    rows = seq_len // GRID_W
    row = jnp.repeat(jnp.arange(rows), GRID_W).astype(F32)
    col = jnp.tile(jnp.arange(GRID_W), rows).astype(F32)
    half = hdim // 2
    inv = ROPE_THETA ** (-jnp.arange(0, half, 2, dtype=F32) / half)
    ang = jnp.concatenate([row[:, None] * inv, col[:, None] * inv], axis=-1)
    cos = jnp.repeat(jnp.cos(ang), 2, axis=1)
    sin = jnp.repeat(jnp.sin(ang), 2, axis=1) * jnp.tile(jnp.array([-1.0, 1.0], F32), half)
    cos = jnp.concatenate([jnp.ones((ctx_len, hdim), F32), cos], axis=0)
    sin = jnp.concatenate([jnp.zeros((ctx_len, hdim), F32), sin], axis=0)
    return jnp.concatenate([jnp.tile(cos, (1, kv_heads)), jnp.tile(sin, (1, kv_heads))], axis=1)


def kernel(x, c, ctx, c_ctx, ada_w, ada_b, norm1, norm2, w_in, rwkv_mu_x, rwkv_mu_rkv, rwkv_w0, rwkv_w1, rwkv_w2, rwkv_a0, rwkv_a1, rwkv_a2, rwkv_g1, rwkv_g2, rwkv_k_k, rwkv_k_a, rwkv_r_k, rwkv_lnx_w, rwkv_lnx_b, gdn_conv, gdn_w_alpha, gdn_dt_bias, gdn_A_log, gdn_w_beta, gdn_norm, attn_q_norm, attn_k_norm, w_up_a, w_up_b, w_up_c, w_gate, b_gate, w_out, ffn_w1, ffn_w3, ffn_w2, final_norm):
    batch, seq_len, d = x.shape
    ctx_len = ctx.shape[1]
    depth = ada_w.shape[0]
    sq = _Seq(batch, ctx_len, seq_len)

    r_heads, r_hdim = rwkv_r_k.shape[1], rwkv_r_k.shape[2]
    rdim = r_heads * r_hdim
    g_heads, g_hdim = gdn_w_alpha.shape[-1], gdn_norm.shape[-1]
    gdim = g_heads * g_hdim
    a_hdim = attn_q_norm.shape[-1]
    qd = w_up_c.shape[1]
    q_heads = qd // a_hdim
    kvd = (w_in.shape[-1] - 3 * rdim - 4 * gdim - qd) // 2
    kv_heads = kvd // a_hdim
    assert rdim == gdim == qd and r_hdim == a_hdim, "lane-segment constants are shared between mixers"
    assert (3 * rdim) % (3 * gdim) == 0 and (3 * rdim + 3 * gdim) % gdim == 0
    gdn_col = (3 * rdim) // (3 * gdim)
    z_col = (3 * rdim + 3 * gdim) // gdim
    q_col = (3 * rdim + 4 * gdim) // qd
    assert (3 * rdim + 4 * gdim + qd) % (2 * kvd) == 0
    kv_col = (3 * rdim + 4 * gdim + qd) // (2 * kvd)

    bd_r = _block_diag_ones(rdim, r_hdim)
    bd_g = _block_diag_ones(gdim, g_hdim)
    cs_tab = _rope_table(ctx_len, seq_len, a_hdim, kv_heads)
    n_beta = 2 * g_heads
    assert 2 * n_beta <= LANES

    cond = jnp.concatenate([c, c_ctx[None, :]], axis=0)
    cond = _pad_to(cond, 0, -(-(batch + 1) // SUBLANES) * SUBLANES)

    xs = jnp.concatenate([ctx, x], axis=1)
    for l in range(depth):
        w1c = jnp.concatenate([rwkv_w1[l, 0], rwkv_w1[l, 1]], axis=1)
        a1c = jnp.concatenate([rwkv_a1[l, 0], rwkv_a1[l, 1]], axis=1)
        g1w = _pad_to(rwkv_g1[l], 1, 2 * LANES)
        wab = _pad_to(jnp.concatenate([gdn_w_beta[l, 0], gdn_w_beta[l, 1],
                                       gdn_w_alpha[l, 0], gdn_w_alpha[l, 1]], axis=1), 1, LANES)
        mu = rwkv_mu_x[l]
        splits = (w1c.shape[1], w1c.shape[1] + a1c.shape[1], w1c.shape[1] + a1c.shape[1] + g1w.shape[1])
        lw = {
            "mu_rkv": rwkv_mu_rkv[l],
            "wh": jnp.concatenate([w1c, a1c, g1w, wab], axis=1).astype(BF16),
            "wd": jnp.concatenate([mu[0][:, None] * w1c, mu[1][:, None] * a1c, mu[2][:, None] * g1w],
                                  axis=1).astype(BF16),
            "splits": splits,
            "w2": _block_diag2(rwkv_w2[l]).astype(BF16),
            "w0": rwkv_w0[l].reshape(1, 2 * rdim),
            "a2": _block_diag2(rwkv_a2[l]).astype(BF16),
            "a0": rwkv_a0[l].reshape(1, 2 * rdim),
            "g2": _pad_to(rwkv_g2[l], 0, 2 * LANES).astype(BF16),
            "abb": _pad_to(jnp.concatenate([jnp.zeros((n_beta,), F32), gdn_dt_bias[l].reshape(-1)]), 0, LANES).reshape(1, LANES),
            "alog": _pad_to(jnp.concatenate([jnp.zeros((n_beta,), F32), gdn_A_log[l].reshape(-1)]), 0, LANES).reshape(1, LANES),
            "n_beta": n_beta,
            "k_k": rwkv_k_k[l].reshape(1, rdim), "k_a": rwkv_k_a[l].reshape(1, rdim),
            "r_k": rwkv_r_k[l].reshape(1, rdim),
            "lnx_w": rwkv_lnx_w[l].reshape(1, rdim), "lnx_b": rwkv_lnx_b[l].reshape(1, rdim),
            "bd_r": bd_r, "bd_g": bd_g,
            "conv": jnp.transpose(gdn_conv[l]),
            "gdn_norm": jnp.tile(gdn_norm[l], g_heads).reshape(1, gdim),
            "q_norm": jnp.tile(attn_q_norm[l], q_heads).reshape(1, qd),
            "k_norm": jnp.tile(attn_k_norm[l], kv_heads).reshape(1, kvd),
            "w_up_a": w_up_a[l].astype(BF16), "w_up_b": w_up_b[l].astype(BF16),
            "w_up_c": w_up_c[l].astype(BF16), "w_out": w_out[l].astype(BF16),
            "ffn_w1": ffn_w1[l].astype(BF16), "ffn_w3": ffn_w3[l].astype(BF16), "ffn_w2": ffn_w2[l].astype(BF16),
        }
        mod = _matmul(cond, ada_w[l].astype(BF16), bias=ada_b[l], pre_act="silu")
        mod_x = mod[:batch].reshape(batch, 6, d)
        mod_c = jnp.broadcast_to(mod[batch].reshape(1, 6, d), (batch, 6, d))
        modtab = jnp.stack([mod_c, mod_x], axis=1)

        h, proj, gates = _pre(sq, xs, norm1[l], modtab, w_in[l].astype(BF16), w_gate[l].astype(BF16), b_gate[l])
        lw_dec, a_iclr, g_out, ab = _lora(sq, h, lw)

        rf, rg, bonus = _rwkv_intra(sq, proj, a_iclr, lw_dec, lw, r_heads)
        gf, gga, ggb = _gdn_intra(sq, proj, ab, lw, g_heads, gdim, gdn_col)
        o_rf, o_rb, o_gf, o_gb = _scan(sq, rf, rg, gf, gga, ggb, r_heads, g_heads)

        aq, ak, av = _attn_prep(sq, proj, cs_tab, lw, q_heads, kv_heads, a_hdim, q_col, kv_col)
        yc = _attention(sq, aq, ak, av)

        xs = _merge(sq, xs, o_rf, o_rb, bonus, g_out, o_gf, o_gb, proj, z_col, yc, gates, modtab, lw,
                    r_heads, g_heads)
        xs = _ffn(sq, xs, norm2[l], modtab, lw)
    return _final_norm(sq, xs, final_norm)
```

```python
import functools

import jax
import jax.numpy as jnp
from jax import lax
from jax.experimental import pallas as pl
from jax.experimental.pallas import tpu as pltpu

F32 = jnp.float32
BF16 = jnp.bfloat16

RMS_EPS = 1e-6
RWKV_LNX_EPS = 64e-5
ROPE_THETA = 10000.0
GRID_W = 64

SUBLANES = 8
LANES = 128
CHUNK = 64
MAX_TOKEN_BLOCK = 256
MAX_MM_ROWS = 512
FFN_ROWS = (544, 512, 256, 128)
KEY_BLOCK = 256
RWKV_PASSES = 1
GDN_PASSES = 1
INV_PASSES = 1
RWKV_INTRA_CHUNKS = 2
GDN_INTRA_CHUNKS = 4
SCAN_CHUNKS = 4
VMEM_LIMIT = 56 * 1024 * 1024


def _sigmoid(x):
    return 1.0 / (1.0 + jnp.exp(-x))


def _softplus(x):
    return jnp.maximum(x, 0.0) + jnp.log(1.0 + jnp.exp(-jnp.abs(x)))


def _split2(x):
    hi = x.astype(BF16)
    lo = (x - hi.astype(F32)).astype(BF16)
    return hi, lo


def _mm(a, b, dims, passes):
    d = functools.partial(lax.dot_general, dimension_numbers=(dims, ((), ())), preferred_element_type=F32)
    if passes == 1:
        return d(a.astype(BF16), b.astype(BF16))
    ah, al = _split2(a)
    bh, bl = _split2(b)
    return d(ah, bh) + (d(ah, bl) + d(al, bh))


def _dot(a, b, passes=1):
    return _mm(a, b, ((1,), (0,)), passes)


def _split3(x):
    hi = x.astype(BF16)
    r1 = x - hi.astype(F32)
    mid = r1.astype(BF16)
    lo = (r1 - mid.astype(F32)).astype(BF16)
    return hi, mid, lo


def _dot_exact_lhs(m01, x):
    d = functools.partial(jnp.dot, preferred_element_type=F32)
    hi, mid, lo = _split3(x)
    return d(m01, hi) + (d(m01, mid) + d(m01, lo))


def _segsum(x, bd_ref):
    d = functools.partial(jnp.dot, preferred_element_type=F32)
    hi, lo = _split2(x)
    bd = bd_ref[...]
    return d(hi, bd) + d(lo, bd)


def _row_iota(shape):
    return lax.broadcasted_iota(jnp.int32, shape, 0)


def _shift_rows(x, prev_ref, next_ref, k, first, last):
    n = x.shape[0]
    y = pltpu.roll(x, (-k) % n, axis=0)
    row = _row_iota((SUBLANES, x.shape[1]))
    if k < 0:
        tile = y[:SUBLANES]
        for i in range(-k):
            edge = prev_ref[SUBLANES + k + i:SUBLANES + k + i + 1, :]
            edge = jnp.where(first, 0.0, edge)
            tile = jnp.where(row == i, edge, tile)
        return jnp.concatenate([tile, y[SUBLANES:]], axis=0)
    tile = y[n - SUBLANES:]
    for i in range(k):
        edge = next_ref[i:i + 1, :]
        edge = jnp.where(last, 0.0, edge)
        tile = jnp.where(row == SUBLANES - k + i, edge, tile)
    return jnp.concatenate([y[:n - SUBLANES], tile], axis=0)


def _largest_divisor(n, candidates):
    for c in candidates:
        if n % c == 0:
            return c
    raise ValueError(f"no block size among {candidates} divides {n}")


class _Seq:
    def __init__(self, batch, ctx_len, seq_len):
        self.B = batch
        self.ctx = ctx_len
        self.S = ctx_len + seq_len
        self.TB = _largest_divisor(ctx_len, (MAX_TOKEN_BLOCK, 128, 64))
        assert seq_len % self.TB == 0 and self.TB % CHUNK == 0
        self.ncb = ctx_len // self.TB
        self.nblk = self.S // self.TB
        self.ncc = ctx_len // CHUNK
        self.nchunk = self.S // CHUNK
        self.grid = (batch, self.nblk)

    def rows(self, width, col_block=0):
        return pl.BlockSpec((None, self.TB, width), lambda b, j: (b, j, col_block))

    def rows2(self, width):
        return pl.BlockSpec((2, None, self.TB, width), lambda b, j: (0, b, j, 0))

    def heads(self, n_heads, width):
        return pl.BlockSpec((None, n_heads, self.TB, width), lambda b, j: (b, 0, j, 0))

    def prev_rows(self, width, col_block=0, tb=None):
        per = (tb or self.TB) // SUBLANES
        return pl.BlockSpec((None, SUBLANES, width),
                            lambda b, j: (b, jnp.maximum(j * per - 1, 0), col_block))

    def next_rows(self, width, col_block=0, tb=None):
        per = (tb or self.TB) // SUBLANES
        top = self.S // SUBLANES - 1
        return pl.BlockSpec((None, SUBLANES, width),
                            lambda b, j: (b, jnp.minimum((j + 1) * per, top), col_block))

    def const(self, shape):
        zeros = (0,) * len(shape)
        return pl.BlockSpec(shape, lambda b, j: zeros)

    def edges(self, tb=None):
        tb = tb or self.TB
        j = pl.program_id(1)
        ncb, nblk = self.ctx // tb, self.S // tb
        first = (j == 0) | (j == ncb)
        last = (j == ncb - 1) | (j == nblk - 1)
        return first, last


def _params(n_axes):
    return pltpu.CompilerParams(dimension_semantics=("arbitrary",) * n_axes,
                                vmem_limit_bytes=VMEM_LIMIT)


def _mm_body(*refs, act, pre_act, n_chunk, has_bias):
    if has_bias:
        x_ref, w_ref, b_ref, o_ref = refs
    else:
        x_ref, w_ref, o_ref = refs
        b_ref = None
    x = x_ref[...]
    if pre_act == "silu":
        x = x * _sigmoid(x)
    xb = x.astype(BF16)
    n = o_ref.shape[-1]
    for n0 in range(0, n, n_chunk):
        y = jnp.dot(xb, w_ref[:, n0:n0 + n_chunk], preferred_element_type=F32)
        if b_ref is not None:
            y = y + b_ref[:, n0:n0 + n_chunk]
        if act == "sigmoid":
            y = _sigmoid(y)
        o_ref[:, n0:n0 + n_chunk] = y.astype(o_ref.dtype)


def _matmul(x, w, bias=None, act=None, pre_act=None, out_dtype=F32):
    m, k = x.shape
    n = w.shape[1]
    tm = m if m <= MAX_MM_ROWS else _largest_divisor(m, (MAX_MM_ROWS, 256, 128, 64, 32, 16, 8))
    n_chunk = _largest_divisor(n, (512, 256, 128))
    in_specs = [pl.BlockSpec((tm, k), lambda i: (i, 0)),
                pl.BlockSpec((k, n), lambda i: (0, 0))]
    args = [x, w]
    if bias is not None:
        in_specs.append(pl.BlockSpec((1, n), lambda i: (0, 0)))
        args.append(bias.reshape(1, n))
    return pl.pallas_call(
        functools.partial(_mm_body, act=act, pre_act=pre_act, n_chunk=n_chunk,
                          has_bias=bias is not None),
        grid=(m // tm,),
        in_specs=in_specs,
        out_specs=pl.BlockSpec((tm, n), lambda i: (i, 0)),
        out_shape=jax.ShapeDtypeStruct((m, n), out_dtype),
        compiler_params=_params(1),
        name="matmul",
    )(*args)


def _pre_body(x_ref, g_ref, mod_ref, win_ref, wg_ref, bg_ref, h_ref, p_ref, gate_ref, *, n_chunk, ctx_len):
    rows = x_ref.shape[0]
    row = pl.program_id(1) * rows + lax.broadcasted_iota(jnp.int32, (rows, 1), 0)
    is_ctx = row < ctx_len
    pick = lambda i: jnp.where(is_ctx, mod_ref[0, i:i + 1, :], mod_ref[1, i:i + 1, :])
    x = x_ref[...]
    y = x * lax.rsqrt(jnp.mean(x * x, axis=-1, keepdims=True) + RMS_EPS) * g_ref[...]
    h = y * (1.0 + pick(1)) + pick(0)
    h_ref[...] = h
    hb = h.astype(BF16)
    for n0 in range(0, p_ref.shape[-1], n_chunk):
        p_ref[:, n0:n0 + n_chunk] = jnp.dot(hb, win_ref[:, n0:n0 + n_chunk], preferred_element_type=F32)
    for n0 in range(0, gate_ref.shape[-1], n_chunk):
        z = jnp.dot(hb, wg_ref[:, n0:n0 + n_chunk], preferred_element_type=F32) + bg_ref[:, n0:n0 + n_chunk]
        gate_ref[:, n0:n0 + n_chunk] = _sigmoid(z).astype(gate_ref.dtype)


def _pre(sq, xs, g, modtab, w_in, w_gate, b_gate):
    d = xs.shape[-1]
    n_in, n_gate = w_in.shape[1], w_gate.shape[1]
    n_chunk = _largest_divisor(n_in, (256, 128))
    assert n_gate % n_chunk == 0
    tb = _largest_divisor(sq.S, FFN_ROWS)
    rows = lambda width: pl.BlockSpec((None, tb, width), lambda b, j: (b, j, 0))
    const = lambda shape: pl.BlockSpec(shape, lambda b, j: (0,) * len(shape))
    resident = lambda shape: pl.BlockSpec(shape, lambda b, j: (0,) * len(shape), pipeline_mode=pl.Buffered(1))
    return pl.pallas_call(
        functools.partial(_pre_body, n_chunk=n_chunk, ctx_len=sq.ctx),
        grid=(sq.B, sq.S // tb),
        in_specs=[rows(d), const((1, d)), pl.BlockSpec((None, 2, 6, d), lambda b, j: (b, 0, 0, 0)),
                  resident(w_in.shape), resident(w_gate.shape), const((1, n_gate))],
        out_specs=[rows(d), rows(n_in), rows(n_gate)],
        out_shape=[jax.ShapeDtypeStruct(xs.shape, F32),
                   jax.ShapeDtypeStruct(xs.shape[:2] + (n_in,), F32),
                   jax.ShapeDtypeStruct(xs.shape[:2] + (n_gate,), BF16)],
        compiler_params=_params(2),
        name="pre",
    )(xs, g.reshape(1, d), modtab, w_in, w_gate, b_gate.reshape(1, n_gate))


def _final_norm_body(x_ref, g_ref, o_ref):
    x = x_ref[...]
    y = x * lax.rsqrt(jnp.mean(x * x, axis=-1, keepdims=True) + RMS_EPS)
    o_ref[...] = y * g_ref[...]


def _final_norm(sq, xs, g):
    d = xs.shape[-1]
    ncb = sq.ncb
    return pl.pallas_call(
        _final_norm_body,
        grid=(sq.B, sq.nblk - ncb),
        in_specs=[pl.BlockSpec((None, sq.TB, d), lambda b, j: (b, j + ncb, 0)),
                  pl.BlockSpec((1, d), lambda b, j: (0, 0))],
        out_specs=pl.BlockSpec((None, sq.TB, d), lambda b, j: (b, j, 0)),
        out_shape=jax.ShapeDtypeStruct((sq.B, sq.S - sq.ctx, d), F32),
        compiler_params=_params(2),
        name="final_norm",
    )(xs, g.reshape(1, d))


def _lora_body(h_ref, hp_ref, hn_ref, wh_ref, wd_ref, w2_ref, w0_ref, a2_ref, a0_ref,
               g2_ref, abb_ref, alog_ref, lw_ref, a_ref, g_ref, ab_ref, *, sq, rdim, n_beta, splits):
    first, last = sq.edges()
    h = h_ref[...]
    nb = 0.5 * (_shift_rows(h, hp_ref, hn_ref, -1, first, last)
                + _shift_rows(h, hp_ref, hn_ref, 1, first, last))
    zh = _dot(h, wh_ref[...])
    zd = _dot(nb - h, wd_ref[...])
    s1, s2, s3 = splits
    wl = w0_ref[...] + _dot(jnp.tanh(zh[:, :s1] + zd[:, :s1]), w2_ref[...])
    lw = -(jnp.exp(-0.5) * _sigmoid(wl))
    lw_ref[0] = lw[:, :rdim]
    lw_ref[1] = lw[:, rdim:]
    a = _sigmoid(a0_ref[...] + _dot(zh[:, s1:s2] + zd[:, s1:s2], a2_ref[...]))
    a_ref[0] = a[:, :rdim]
    a_ref[1] = a[:, rdim:]
    g_ref[...] = _dot(_sigmoid(zh[:, s2:s3] + zd[:, s2:s3]), g2_ref[...])
    z = zh[:, s3:]
    col = lax.broadcasted_iota(jnp.int32, z.shape, 1)
    gl = -jnp.exp(alog_ref[...]) * _softplus(z + abb_ref[...])
    ab_ref[...] = jnp.where(col < n_beta, _sigmoid(z), gl)


def _lora(sq, h, lw):
    d = h.shape[-1]
    rdim = lw["w0"].shape[-1] // 2
    consts = [lw["wh"], lw["wd"], lw["w2"], lw["w0"], lw["a2"], lw["a0"],
              lw["g2"], lw["abb"], lw["alog"]]
    bsd = (sq.B, sq.S)
    return pl.pallas_call(
        functools.partial(_lora_body, sq=sq, rdim=rdim, n_beta=lw["n_beta"], splits=lw["splits"]),
        grid=sq.grid,
        in_specs=[sq.rows(d), sq.prev_rows(d), sq.next_rows(d)] + [sq.const(c.shape) for c in consts],
        out_specs=[sq.rows2(rdim), sq.rows2(rdim), sq.rows(rdim), sq.rows(LANES)],
        out_shape=[jax.ShapeDtypeStruct((2,) + bsd + (rdim,), F32),
                   jax.ShapeDtypeStruct((2,) + bsd + (rdim,), F32),
                   jax.ShapeDtypeStruct(bsd + (rdim,), F32),
                   jax.ShapeDtypeStruct(bsd + (LANES,), F32)],
        compiler_params=_params(2),
        name="lora",
    )(h, h, h, *consts)


def _causal_masks(d, c):
    row = lax.broadcasted_iota(jnp.int32, (c, c), 0)
    col = lax.broadcasted_iota(jnp.int32, (c, c), 1)
    delta = jnp.where(d == 0, row - col, col - row)
    eye = jnp.where(row == col, 1.0, 0.0).astype(F32)
    return delta >= 0, delta > 0, eye


def _stacked_mask(d, c, reps=1):
    row = lax.broadcasted_iota(jnp.int32, (2 * c, reps * c), 0)
    col = lax.broadcasted_iota(jnp.int32, (2 * c, reps * c), 1) & (c - 1)
    rr = row & (c - 1)
    delta = jnp.where(d == 0, rr - col, col - rr)
    return (delta > 0) | ((row >= c) & (delta == 0))


def _block_rows(x, w):
    left = lax.broadcasted_iota(jnp.int32, x.shape, 1) < w
    zero = jnp.zeros_like(x)
    return jnp.concatenate([jnp.where(left, x, zero), jnp.where(left, zero, x)], axis=0)


def _fold_rows(y):
    n = y.shape[0] // 2
    return y[:n] + y[n:]


def _tri_inverse_pairs(ls):
    n = ls[0].shape[0]
    row = lax.broadcasted_iota(jnp.int32, (n, 2 * n), 0)
    col = lax.broadcasted_iota(jnp.int32, (n, 2 * n), 1) & (n - 1)
    eye = jnp.where(row == col, 1.0, 0.0).astype(F32)
    same = (row >> 1) == (col >> 1)
    ts = [eye - jnp.where(same, l, 0.0) for l in ls]
    for k in range(2, n.bit_length()):
        off = ((row >> k) == (col >> k)) & ((row >> (k - 1)) != (col >> (k - 1)))
        tl = [_dot(t, _block_rows(jnp.where(off, l, 0.0), n), INV_PASSES) for t, l in zip(ts, ls)]
        ts = [t - _dot(x, _block_rows(t, n), INV_PASSES) for x, t in zip(tl, ts)]
    return ts


def _tri_inverse_many(ls, eye):
    n = ls[0].shape[0]
    row = lax.broadcasted_iota(jnp.int32, (n, n), 0)
    col = lax.broadcasted_iota(jnp.int32, (n, n), 1)
    same = (row >> 1) == (col >> 1)
    ts = [eye - jnp.where(same, l, 0.0) for l in ls]
    for k in range(2, n.bit_length()):
        off = ((row >> k) == (col >> k)) & ((row >> (k - 1)) != (col >> (k - 1)))
        tl = [_dot(t, jnp.where(off, l, 0.0), INV_PASSES) for t, l in zip(ts, ls)]
        ts = [t - _dot(x, t, INV_PASSES) for x, t in zip(tl, ts)]
    return ts


RWKV_TILES = 6

def _rwkv_intra_body(p_ref, pp_ref, pn_ref, a_ref, lw_ref, mu_ref, kk_ref, ka_ref, rk_ref, bd_ref,
                     f_ref, g_ref, bonus_ref, y0_ref, *, sq, heads, hdim, cps):
    c = CHUNK
    pw = 2 * hdim
    rdim = heads * hdim
    first, last = sq.edges(cps * c)
    proj = p_ref[...]
    nb = 0.5 * (_shift_rows(proj, pp_ref, pn_ref, -1, first, last)
                + _shift_rows(proj, pp_ref, pn_ref, 1, first, last))
    dlt = nb - proj
    r_all = proj[:, :rdim] + dlt[:, :rdim] * mu_ref[0:1, :]
    k_all = proj[:, rdim:2 * rdim] + dlt[:, rdim:2 * rdim] * mu_ref[1:2, :]
    v_all = proj[:, 2 * rdim:] + dlt[:, 2 * rdim:] * mu_ref[2:3, :]
    kn = k_all * kk_ref[...]
    kk_all = kn * lax.rsqrt(_segsum(kn * kn, bd_ref) + RMS_EPS)
    kd_all = [k_all * (1.0 + (a_ref[d] - 1.0) * ka_ref[...]) for d in range(2)]
    rk = r_all * rk_ref[...]
    bonus_ref[...] = (_segsum(rk * kd_all[0], bd_ref) + _segsum(rk * kd_all[1], bd_ref)) * v_all
    keys, a_p, r_p, v_p, bt_p, kt_p, ec_p, mask2 = [], {}, {}, {}, {}, {}, {}, {}
    for d in range(2):
        incl, _, _ = _causal_masks(d, c)
        m01 = jnp.where(incl, 1.0, 0.0).astype(BF16)
        mask2[d] = _stacked_mask(d, c, reps=2)
        for j in range(cps):
            rows = slice(j * c, (j + 1) * c)
            lw = lw_ref[d, rows, :]
            cum = _dot_exact_lhs(m01, lw)
            e_neg = jnp.exp(-cum)
            e_row = jnp.broadcast_to(jnp.exp(jnp.sum(lw, axis=0, keepdims=True)), lw.shape)
            a_t = -kk_all[rows] * jnp.exp(cum - lw)
            r_t = r_all[rows] * jnp.exp(cum)
            b_t = kk_all[rows] * a_ref[d, rows, :] * e_neg
            k_t = kd_all[d][rows] * e_neg
            v = v_all[rows]
            for p in range(heads // 2):
                key = (d, j, p)
                keys.append(key)
                lanes = slice(p * pw, (p + 1) * pw)
                a_p[key], r_p[key], v_p[key] = a_t[:, lanes], r_t[:, lanes], v[:, lanes]
                bt_p[key] = jnp.transpose(_block_rows(b_t[:, lanes], hdim))
                kt_p[key] = jnp.transpose(_block_rows(k_t[:, lanes], hdim))
                ec_p[key] = jnp.transpose(_block_rows(e_row[:, lanes], hdim))
    ar = {k: jnp.concatenate([a_p[k], r_p[k]], axis=0) for k in keys}
    mb = {k: jnp.where(mask2[k[0]], _dot(ar[k], bt_p[k], RWKV_PASSES), 0.0) for k in keys}
    mk = {k: jnp.where(mask2[k[0]], _dot(ar[k], kt_p[k], RWKV_PASSES), 0.0) for k in keys}
    v_bd = {k: _block_rows(v_p[k], hdim) for k in keys}
    x = {k: _dot(mk[k], v_bd[k], RWKV_PASSES) for k in keys}
    h0 = {k: _dot(_fold_rows(ec_p[k] * kt_p[k]), v_bd[k], RWKV_PASSES) for k in keys}
    t = dict(zip(keys, _tri_inverse_pairs([-mb[k][:c] for k in keys])))
    wt = {k: _dot(t[k], _block_rows(a_p[k], hdim), RWKV_PASSES) for k in keys}
    ut = {k: _dot(t[k], _block_rows(x[k][:c], hdim), RWKV_PASSES) for k in keys}
    for k in keys:
        d, j, p = k
        lanes = slice(p * pw, (p + 1) * pw)
        f_ref[d, j * c:(j + 1) * c, lanes] = _fold_rows(ec_p[k])
        tiles = (wt[k], r_p[k], mb[k][c:], _fold_rows(ec_p[k] * bt_p[k]), ut[k], h0[k])
        for i, tile in enumerate(tiles):
            o = (j * RWKV_TILES + i) * c
            g_ref[d, o:o + c, lanes] = tile.astype(BF16)
        if d == 0:
            y0_ref[j * c:(j + 1) * c, lanes] = x[k][c:] + x[1, j, p][c:]


def _rwkv_intra(sq, proj, a, lw_dec, lw, heads):
    rdim = a.shape[-1]
    hdim = rdim // heads
    assert hdim == CHUNK and heads % 2 == 0 and 2 * hdim == LANES, "pair tiles are [CHUNK, 128 lanes]"
    cps = _largest_divisor(sq.ncc, (RWKV_INTRA_CHUNKS, 1))
    rows = cps * CHUNK
    consts = [lw["mu_rkv"], lw["k_k"], lw["k_a"], lw["r_k"], lw["bd_r"]]
    const = lambda shape: pl.BlockSpec(shape, lambda b, i: (0,) * len(shape))
    per_dir = pl.BlockSpec((2, None, rows, rdim), lambda b, i: (0, b, i, 0))
    out_g = pl.BlockSpec((2, None, RWKV_TILES * rows, rdim), lambda b, i: (0, b, i, 0))
    return pl.pallas_call(
        functools.partial(_rwkv_intra_body, sq=sq, heads=heads, hdim=hdim, cps=cps),
        grid=(sq.B, sq.nchunk // cps),
        in_specs=[pl.BlockSpec((None, rows, 3 * rdim), lambda b, i: (b, i, 0)),
                  sq.prev_rows(3 * rdim, tb=rows), sq.next_rows(3 * rdim, tb=rows), per_dir, per_dir]
        + [const(c.shape) for c in consts],
        out_specs=[per_dir, out_g] + [pl.BlockSpec((None, rows, rdim), lambda b, i: (b, i, 0))] * 2,
        out_shape=[jax.ShapeDtypeStruct((2, sq.B, sq.S, rdim), F32),
                   jax.ShapeDtypeStruct((2, sq.B, RWKV_TILES * sq.S, rdim), BF16)]
        + [jax.ShapeDtypeStruct((sq.B, sq.S, rdim), F32)] * 2,
        compiler_params=_params(2),
        name="rwkv_intra",
    )(proj, proj, proj, a, lw_dec, *consts)


def _backward_block(i, ncb, nblk):
    return jnp.where(i < ncb, ncb - 1 - i, nblk + ncb - 1 - i)


def _scan_chunks_per_step(sq):
    cps = _largest_divisor(sq.ncc, (SCAN_CHUNKS, 2, 1))
    assert sq.nchunk % cps == 0
    return cps


def _scan_specs(sq, heads, rows, width):
    cps = _scan_chunks_per_step(sq)
    ncb, nblk = sq.ncc // cps, sq.nchunk // cps
    fwd = pl.BlockSpec((None, None, heads, cps * rows, width), lambda b, i: (0, b, 0, i, 0))
    bwd = pl.BlockSpec((None, None, heads, cps * rows, width),
                       lambda b, i: (1, b, 0, _backward_block(i, ncb, nblk), 0))
    return fwd, bwd


def _scan_out_specs(sq, width):
    cps = _scan_chunks_per_step(sq)
    ncb, nblk = sq.ncc // cps, sq.nchunk // cps
    fwd = pl.BlockSpec((None, cps * CHUNK, width), lambda b, i: (b, i, 0))
    bwd = pl.BlockSpec((None, cps * CHUNK, width), lambda b, i: (b, _backward_block(i, ncb, nblk), 0))
    return [fwd, bwd]


def _gdn_intra_body(p_ref, pp_ref, pn_ref, ab_ref, cw_ref, bd_ref, f_ref, ga_ref, gb_ref, *, sq, heads, hdim, cps):
    c = CHUNK
    gdim = heads * hdim
    scale = hdim ** -0.5
    first, last = sq.edges(cps * c)
    proj = p_ref[...]
    width = cw_ref.shape[0]
    half = width // 2
    acc = proj * cw_ref[half:half + 1, :]
    for j in range(width):
        if j != half:
            acc = acc + _shift_rows(proj, pp_ref, pn_ref, j - half, first, last) * cw_ref[j:j + 1, :]
    y = acc * _sigmoid(acc)
    q_blk = y[:, :gdim]
    k_blk = y[:, gdim:2 * gdim]
    q_blk = q_blk * lax.rsqrt(_segsum(q_blk * q_blk, bd_ref) + RMS_EPS)
    k_blk = k_blk * lax.rsqrt(_segsum(k_blk * k_blk, bd_ref) + RMS_EPS)
    v_blk = y[:, 2 * gdim:]
    sl = [slice(h * hdim, (h + 1) * hdim) for h in range(heads)]
    keys = []
    beta, gc, g_last, gc_row, decay, kb, kq, kt_h, vb, strict_of, eye = {}, {}, {}, {}, {}, {}, {}, {}, {}, {}, None
    for j in range(cps):
        rows = slice(j * c, (j + 1) * c)
        ab = ab_ref[rows, :]
        tot_all = jnp.sum(ab, axis=0, keepdims=True)
        lane = lax.broadcasted_iota(jnp.int32, ab.shape, 1)
        k_all = k_blk[rows]
        k_tt = jnp.transpose(k_all)
        for d in range(2):
            incl, strict, eye = _causal_masks(d, c)
            strict_of[d] = strict
            gc_all = _dot_exact_lhs(jnp.where(incl, 1.0, 0.0).astype(BF16), ab)
            gc_all_t = jnp.transpose(gc_all)
            sub = lax.broadcasted_iota(jnp.int32, gc_all_t.shape, 0)
            for h in range(heads):
                key = (d, j, h)
                keys.append(key)
                pick_b = lane == d * heads + h
                pick_g = lane == 2 * heads + d * heads + h
                beta[key] = jnp.sum(jnp.where(pick_b, ab, 0.0), axis=1, keepdims=True)
                gc[key] = jnp.sum(jnp.where(pick_g, gc_all, 0.0), axis=1, keepdims=True)
                g_last[key] = jnp.sum(jnp.where(pick_g[:1], tot_all, 0.0), axis=1, keepdims=True)
                gc_row[key] = jnp.sum(jnp.where(sub == 2 * heads + d * heads + h, gc_all_t, 0.0),
                                      axis=0, keepdims=True)
                diff = gc[key] - gc_row[key]
                decay[key] = jnp.where(incl, jnp.exp(jnp.where(incl, diff, 0.0)), 0.0)
                kb[key] = k_all[:, sl[h]] * beta[key]
                kq[key] = jnp.concatenate([kb[key], q_blk[rows, sl[h]] * scale], axis=0)
                kt_h[key] = k_tt[sl[h]]
                vb[key] = v_blk[rows, sl[h]] * beta[key]
    m = {k: _dot(kq[k], kt_h[k], GDN_PASSES) for k in keys}
    lower = [jnp.where(strict_of[k[0]], m[k][:c] * decay[k], 0.0) for k in keys]
    t = dict(zip(keys, _tri_inverse_many(lower, eye)))
    e_gc = {k: jnp.exp(gc[k]) for k in keys}
    sol = {k: _dot(t[k], jnp.concatenate([vb[k], kb[k] * e_gc[k]], axis=1), GDN_PASSES) for k in keys}
    for k in keys:
        d, j, h = k
        of, oa, ob = j * (c + SUBLANES), j * 2 * c, j * (c + hdim)
        f_ref[d, h, of:of + c] = sol[k][:, :hdim]
        f_ref[d, h, of + c:of + c + SUBLANES] = jnp.broadcast_to(jnp.exp(g_last[k]), (SUBLANES, hdim))
        ga_ref[d, h, oa:oa + c] = sol[k][:, hdim:].astype(BF16)
        ga_ref[d, h, oa + c:oa + 2 * c] = (kq[k][c:] * e_gc[k]).astype(BF16)
        gb_ref[d, h, ob:ob + c] = (m[k][c:] * decay[k]).astype(BF16)
        gb_ref[d, h, ob + c:ob + c + hdim] = (kt_h[k] * jnp.exp(g_last[k] - gc_row[k])).astype(BF16)


def _gdn_intra(sq, proj, ab, lw, heads, gdim, col_block):
    hdim = gdim // heads
    cps = _largest_divisor(sq.ncc, (GDN_INTRA_CHUNKS, 2, 1))
    tb = cps * CHUNK
    w = 3 * gdim
    consts = [lw["conv"], lw["bd_g"]]
    const = lambda shape: pl.BlockSpec(shape, lambda b, i: (0,) * len(shape))
    small = pl.BlockSpec((None, tb, LANES), lambda b, i: (b, i, 0))
    rows = (CHUNK + SUBLANES, 2 * CHUNK, CHUNK + hdim)
    widths = (hdim, hdim, CHUNK)
    dtypes = (F32, BF16, BF16)
    outs = [pl.BlockSpec((2, None, heads, cps * r, wd), lambda b, i: (0, b, 0, i, 0)) for r, wd in zip(rows, widths)]
    shapes = [jax.ShapeDtypeStruct((2, sq.B, heads, sq.nchunk * r, wd), dt) for r, wd, dt in zip(rows, widths, dtypes)]
    return pl.pallas_call(
        functools.partial(_gdn_intra_body, sq=sq, heads=heads, hdim=hdim, cps=cps),
        grid=(sq.B, sq.nchunk // cps),
        in_specs=[pl.BlockSpec((None, tb, w), lambda b, i: (b, i, col_block)),
                  sq.prev_rows(w, col_block, tb=tb), sq.next_rows(w, col_block, tb=tb), small]
        + [const(c.shape) for c in consts],
        out_specs=outs,
        out_shape=shapes,
        compiler_params=_params(2),
        name="gdn_intra",
    )(proj, proj, proj, ab, *consts)


def _scan_body(rf0_ref, rg0_ref, rf1_ref, rg1_ref, f0_ref, ga0_ref, gb0_ref, f1_ref, ga1_ref, gb1_ref,
               rof_ref, rob_ref, gof_ref, gob_ref, h_ref, s_ref, *, r_heads, r_hdim, g_heads, g_hdim, cps):
    @pl.when(pl.program_id(1) == 0)
    def _():
        h_ref[...] = jnp.zeros_like(h_ref)
        s_ref[...] = jnp.zeros_like(s_ref)

    c = CHUNK
    dot = functools.partial(jnp.dot, preferred_element_type=F32)
    pw = 2 * r_hdim
    rfs, rgs, routs = (rf0_ref, rf1_ref), (rg0_ref, rg1_ref), (rof_ref, rob_ref)
    rch = [(d, p) for d in range(2) for p in range(r_heads // 2)]
    lanes = [slice(p * pw, (p + 1) * pw) for p in range(r_heads // 2)]
    hst = {ch: h_ref[ch[0], ch[1]] for ch in rch}
    fs, gas, gbs, gouts = (f0_ref, f1_ref), (ga0_ref, ga1_ref), (gb0_ref, gb1_ref), (gof_ref, gob_ref)
    gch = [(d, h) for d in range(2) for h in range(g_heads)]
    sst = {ch: s_ref[ch[0], ch[1]] for ch in gch}
    for step in range(cps):
        sub = (step, cps - 1 - step)

        def tile(d, p, i, n=1):
            o = (sub[d] * RWKV_TILES + i) * c
            return rgs[d][o:o + n * c, lanes[p]]

        of = [(c + SUBLANES) * sub[d] for d in range(2)]
        oa = [2 * c * sub[d] for d in range(2)]
        ob = [(c + g_hdim) * sub[d] for d in range(2)]
        m2 = {(d, p): dot(tile(d, p, 0, 2), hst[d, p].astype(BF16)) for d, p in rch}
        m = {(d, h): dot(gas[d][h, oa[d]:oa[d] + 2 * c, :], sst[d, h].astype(BF16)) for d, h in gch}
        ub = {(d, p): _block_rows((tile(d, p, 4) + m2[d, p][:c]).astype(BF16), r_hdim) for d, p in rch}
        vn = {(d, h): (fs[d][h, of[d]:of[d] + c, :] - m[d, h][:c]).astype(BF16) for d, h in gch}
        for d, p in rch:
            routs[d][sub[d] * c:(sub[d] + 1) * c, lanes[p]] = m2[d, p][c:] + dot(tile(d, p, 2), ub[d, p])
            decay = _block_rows(rfs[d][sub[d] * c:(sub[d] + 1) * c, lanes[p]], r_hdim)
            hst[d, p] = (decay * hst[d, p] + _block_rows(tile(d, p, 5), r_hdim)
                         + dot(_block_rows(tile(d, p, 3), r_hdim), ub[d, p]))
        for d, h in gch:
            gouts[d][sub[d] * c:(sub[d] + 1) * c, h * g_hdim:(h + 1) * g_hdim] = (
                m[d, h][c:] + dot(gbs[d][h, ob[d]:ob[d] + c, :], vn[d, h]))
            sst[d, h] = (sst[d, h] * fs[d][h, of[d] + c:of[d] + c + 1, :]
                         + dot(gbs[d][h, ob[d] + c:ob[d] + c + g_hdim, :], vn[d, h]))
    for d, p in rch:
        h_ref[d, p] = hst[d, p]
    for d, h in gch:
        s_ref[d, h] = sst[d, h]


def _scan(sq, rf, rg, gf, gga, ggb, r_heads, g_heads):
    rdim = rf.shape[-1]
    r_hdim = rdim // r_heads
    g_hdim = gf.shape[-1]
    gdim = g_heads * g_hdim
    cps = _scan_chunks_per_step(sq)
    ncb, nblk = sq.ncc // cps, sq.nchunk // cps
    fwd = lambda rows: pl.BlockSpec((None, None, rows, rdim), lambda b, i: (0, b, i, 0))
    bwd = lambda rows: pl.BlockSpec((None, None, rows, rdim),
                                    lambda b, i: (1, b, _backward_block(i, ncb, nblk), 0))
    rows_f, rows_g = cps * CHUNK, cps * RWKV_TILES * CHUNK
    gspecs = [_scan_specs(sq, g_heads, arr.shape[3] // sq.nchunk, arr.shape[4]) for arr in (gf, gga, ggb)]
    r_one = jax.ShapeDtypeStruct((sq.B, sq.S, rdim), F32)
    g_one = jax.ShapeDtypeStruct((sq.B, sq.S, gdim), F32)
    return pl.pallas_call(
        functools.partial(_scan_body, r_heads=r_heads, r_hdim=r_hdim, g_heads=g_heads, g_hdim=g_hdim, cps=cps),
        grid=(sq.B, sq.nchunk // cps),
        in_specs=[fwd(rows_f), fwd(rows_g), bwd(rows_f), bwd(rows_g)]
        + [s[0] for s in gspecs] + [s[1] for s in gspecs],
        out_specs=_scan_out_specs(sq, rdim) + _scan_out_specs(sq, gdim),
        out_shape=[r_one, r_one, g_one, g_one],
        scratch_shapes=[pltpu.VMEM((2, r_heads // 2, 2 * r_hdim, 2 * r_hdim), F32),
                        pltpu.VMEM((2, g_heads, g_hdim, g_hdim), F32)],
        compiler_params=_params(2),
        name="scan",
    )(rf, rg, rf, rg, gf, gga, ggb, gf, gga, ggb)


def _rope(x, cos, sin_signed):
    n = x.shape[-1]
    lane = lax.broadcasted_iota(jnp.int32, x.shape, 1)
    partner = jnp.where((lane & 1) == 0, pltpu.roll(x, n - 1, axis=1), pltpu.roll(x, 1, axis=1))
    return x * cos + partner * sin_signed


def _attn_prep_body(q_ref, kv_ref, cs_ref, qn_ref, kn_ref, bd_ref, qo_ref, ko_ref, vo_ref,
                    *, q_heads, kv_heads, hdim):
    kvd = kv_heads * hdim
    cos = cs_ref[:, :kvd]
    sin = cs_ref[:, kvd:]
    reps = q_heads // kv_heads
    cos_q = jnp.concatenate([cos] * reps, axis=1)
    sin_q = jnp.concatenate([sin] * reps, axis=1)
    inv_n = 1.0 / hdim
    q = q_ref[...]
    q = q * lax.rsqrt(_segsum(q * q, bd_ref) * inv_n + RMS_EPS) * qn_ref[...]
    q = _rope(q, cos_q, sin_q) * (hdim ** -0.5)
    kv = kv_ref[...]
    k = kv[:, :kvd]
    kbd = bd_ref[:kvd, :kvd]
    hi, lo = _split2(k * k)
    ms = (jnp.dot(hi, kbd, preferred_element_type=F32) + jnp.dot(lo, kbd, preferred_element_type=F32)) * inv_n
    k = k * lax.rsqrt(ms + RMS_EPS) * kn_ref[...]
    k = _rope(k, cos, sin)
    v = kv[:, kvd:]
    for h in range(q_heads):
        qo_ref[h] = q[:, h * hdim:(h + 1) * hdim].astype(BF16)
    k_t = jnp.transpose(k)
    for h in range(kv_heads):
        ko_ref[h] = k_t[h * hdim:(h + 1) * hdim].astype(BF16)
        vh = v[:, h * hdim:(h + 1) * hdim]
        vo_ref[h] = jnp.concatenate([vh, jnp.ones_like(vh)], axis=1).astype(BF16)


def _attn_prep(sq, proj, cs_tab, lw, q_heads, kv_heads, hdim, q_col_block, kv_col_block):
    qd, kvd = q_heads * hdim, kv_heads * hdim
    consts = [lw["q_norm"], lw["k_norm"], lw["bd_r"]]
    k_spec = pl.BlockSpec((None, kv_heads, hdim, sq.TB), lambda b, j: (b, 0, 0, j))
    return pl.pallas_call(
        functools.partial(_attn_prep_body, q_heads=q_heads, kv_heads=kv_heads, hdim=hdim),
        grid=sq.grid,
        in_specs=[sq.rows(qd, q_col_block), sq.rows(2 * kvd, kv_col_block),
                  pl.BlockSpec((sq.TB, 2 * kvd), lambda b, j: (j, 0))]
        + [sq.const(c.shape) for c in consts],
        out_specs=[sq.heads(q_heads, hdim), k_spec, sq.heads(kv_heads, 2 * hdim)],
        out_shape=[jax.ShapeDtypeStruct((sq.B, q_heads, sq.S, hdim), BF16),
                   jax.ShapeDtypeStruct((sq.B, kv_heads, hdim, sq.S), BF16),
                   jax.ShapeDtypeStruct((sq.B, kv_heads, sq.S, 2 * hdim), BF16)],
        compiler_params=_params(2),
        name="attn_prep",
    )(proj, proj, cs_tab, *consts)


def _attn_body(q_ref, kt_ref, v_ref, o_ref, s_ref, p_ref, *, group, n_ctx_qblocks, n_ctx_keys):
    tq, hdim = q_ref.shape[1], q_ref.shape[2]
    n_keys = kt_ref.shape[1]

    def attend(nk):
        tiles = range(0, nk, KEY_BLOCK)
        for g in range(group):
            q = q_ref[g]
            rows = slice(g * tq, (g + 1) * tq)
            mx = None
            for t in tiles:
                s = jnp.dot(q, kt_ref[:, t:t + KEY_BLOCK], preferred_element_type=F32)
                s_ref[:, t:t + KEY_BLOCK] = s
                for c0 in range(0, KEY_BLOCK, LANES):
                    part = s[:, c0:c0 + LANES]
                    mx = part if mx is None else jnp.maximum(mx, part)
            m = jnp.broadcast_to(jnp.max(mx, axis=1, keepdims=True), (tq, LANES))
            for t in tiles:
                for c0 in range(t, t + KEY_BLOCK, LANES):
                    p_ref[rows, c0:c0 + LANES] = jnp.exp(s_ref[:, c0:c0 + LANES] - m).astype(BF16)
        acc = jnp.dot(p_ref[:, 0:nk], v_ref[0:nk, :], preferred_element_type=F32)
        out = acc[:, :hdim] / acc[:, hdim:]
        for g in range(group):
            o_ref[:, g * hdim:(g + 1) * hdim] = out[g * tq:(g + 1) * tq, :]

    is_ctx = pl.program_id(2) < n_ctx_qblocks

    @pl.when(is_ctx)
    def _():
        attend(n_ctx_keys)

    @pl.when(jnp.logical_not(is_ctx))
    def _():
        attend(n_keys)


def _attention(sq, q, kt, v):
    b, q_heads, s, hdim = q.shape
    kv_heads = kt.shape[1]
    group = q_heads // kv_heads
    tq = _largest_divisor(sq.ctx, (128, 64))
    assert sq.ctx % KEY_BLOCK == 0 and s % KEY_BLOCK == 0
    return pl.pallas_call(
        functools.partial(_attn_body, group=group, n_ctx_qblocks=sq.ctx // tq, n_ctx_keys=sq.ctx),
        grid=(b, kv_heads, s // tq),
        in_specs=[pl.BlockSpec((None, group, tq, hdim), lambda bi, g, i: (bi, g, i, 0)),
                  pl.BlockSpec((None, None, hdim, s), lambda bi, g, i: (bi, g, 0, 0)),
                  pl.BlockSpec((None, None, s, 2 * hdim), lambda bi, g, i: (bi, g, 0, 0))],
        out_specs=pl.BlockSpec((None, tq, group * hdim), lambda bi, g, i: (bi, i, g)),
        out_shape=jax.ShapeDtypeStruct((b, s, q_heads * hdim), F32),
        scratch_shapes=[pltpu.VMEM((tq, s), F32), pltpu.VMEM((group * tq, s), BF16)],
        compiler_params=_params(3),
        name="attention",
    )(q, kt, v)


def _merge_body(x_ref, rf_ref, rb_ref, y0_ref, bonus_ref, rg_ref, gf_ref, gb_ref, z_ref, yc_ref, gate_ref, mod_ref,
                lnw_ref, lnb_ref, bdr_ref, gn_ref, bdg_ref, wa_ref, wb_ref, wc_ref, wo_ref, o_ref,
                *, gate_row, r_hdim, g_hdim, ctx_len):
    d = x_ref.shape[-1]
    o = rf_ref[...] + rb_ref[...] + y0_ref[...]
    mean = _segsum(o, bdr_ref) * (1.0 / r_hdim)
    cen = o - mean
    var = _segsum(cen * cen, bdr_ref) * (1.0 / r_hdim)
    ya = (cen * lax.rsqrt(var + RWKV_LNX_EPS) * lnw_ref[...] + lnb_ref[...] + bonus_ref[...]) * rg_ref[...]
    o = gf_ref[...] + gb_ref[...]
    ms = _segsum(o * o, bdg_ref) * (1.0 / g_hdim)
    z = z_ref[...]
    yb = o * lax.rsqrt(ms + RMS_EPS) * gn_ref[...] * (z * _sigmoid(z))
    m = gate_ref[:, :d] * _dot(ya, wa_ref[...])
    m = m + gate_ref[:, d:2 * d] * _dot(yb, wb_ref[...])
    m = m + gate_ref[:, 2 * d:] * _dot(yc_ref[...], wc_ref[...])
    y = _dot(m, wo_ref[...])
    rows = x_ref.shape[0]
    row = pl.program_id(1) * rows + lax.broadcasted_iota(jnp.int32, (rows, 1), 0)
    gate = jnp.where(row < ctx_len, mod_ref[0, gate_row:gate_row + 1, :], mod_ref[1, gate_row:gate_row + 1, :])
    o_ref[...] = x_ref[...] + gate * y


def _merge(sq, xs, o_rf, o_rb, y0, bonus, r_gate, o_gf, o_gb, proj, z_col_block, yc, gates, modtab, lw,
           r_heads, g_heads):
    d = xs.shape[-1]
    rdim, gdim = o_rf.shape[-1], o_gf.shape[-1]
    consts = [lw["lnx_w"], lw["lnx_b"], lw["bd_r"], lw["gdn_norm"], lw["bd_g"],
              lw["w_up_a"], lw["w_up_b"], lw["w_up_c"], lw["w_out"]]
    tb = _largest_divisor(sq.S, FFN_ROWS)
    rows = lambda width, col_block=0: pl.BlockSpec((None, tb, width), lambda b, j: (b, j, col_block))
    const = lambda shape: pl.BlockSpec(shape, lambda b, j: (0,) * len(shape))
    return pl.pallas_call(
        functools.partial(_merge_body, gate_row=2, r_hdim=rdim // r_heads, g_hdim=gdim // g_heads,
                          ctx_len=sq.ctx),
        grid=(sq.B, sq.S // tb),
        in_specs=[rows(d)] + [rows(rdim)] * 5 + [rows(gdim)] * 2
        + [rows(gdim, z_col_block), rows(yc.shape[-1]), rows(3 * d),
           pl.BlockSpec((None, 2, 6, d), lambda b, j: (b, 0, 0, 0))]
        + [const(c.shape) for c in consts],
        out_specs=rows(d),
        out_shape=jax.ShapeDtypeStruct(xs.shape, F32),
        compiler_params=_params(2),
        name="merge",
    )(xs, o_rf, o_rb, y0, bonus, r_gate, o_gf, o_gb, proj, yc, gates, modtab, *consts)


def _ffn_body(x_ref, g_ref, mod_ref, w1_ref, w3_ref, w2_ref, o_ref, *, h_chunk, ctx_len):
    rows = x_ref.shape[0]
    row = pl.program_id(1) * rows + lax.broadcasted_iota(jnp.int32, (rows, 1), 0)
    is_ctx = row < ctx_len
    pick = lambda i: jnp.where(is_ctx, mod_ref[0, i:i + 1, :], mod_ref[1, i:i + 1, :])
    x = x_ref[...]
    y = x * lax.rsqrt(jnp.mean(x * x, axis=-1, keepdims=True) + RMS_EPS) * g_ref[...]
    h = (y * (1.0 + pick(4)) + pick(3)).astype(BF16)
    hidden = w1_ref.shape[1]
    acc = jnp.zeros(x.shape, F32)
    for c0 in range(0, hidden, h_chunk):
        a = jnp.dot(h, w1_ref[:, c0:c0 + h_chunk], preferred_element_type=F32)
        b = jnp.dot(h, w3_ref[:, c0:c0 + h_chunk], preferred_element_type=F32)
        t = (a * _sigmoid(a) * b).astype(BF16)
        acc = acc + jnp.dot(t, w2_ref[c0:c0 + h_chunk, :], preferred_element_type=F32)
    o_ref[...] = x + pick(5) * acc


def _ffn(sq, xs, g, modtab, lw):
    d = xs.shape[-1]
    consts = [lw["ffn_w1"], lw["ffn_w3"], lw["ffn_w2"]]
    h_chunk = _largest_divisor(lw["ffn_w1"].shape[1], (512, 256, 128))
    tb = _largest_divisor(sq.S, FFN_ROWS)
    rows = pl.BlockSpec((None, tb, d), lambda b, j: (b, j, 0))
    const = lambda shape: pl.BlockSpec(shape, lambda b, j: (0,) * len(shape))
    return pl.pallas_call(
        functools.partial(_ffn_body, h_chunk=h_chunk, ctx_len=sq.ctx),
        grid=(sq.B, sq.S // tb),
        in_specs=[rows, const((1, d)), pl.BlockSpec((None, 2, 6, d), lambda b, j: (b, 0, 0, 0))]
        + [const(c.shape) for c in consts],
        out_specs=rows,
        out_shape=jax.ShapeDtypeStruct(xs.shape, F32),
        compiler_params=_params(2),
        name="ffn",
    )(xs, g.reshape(1, d), modtab, *consts)


def _block_diag_ones(n, seg):
    idx = jnp.arange(n) // seg
    return (idx[:, None] == idx[None, :]).astype(BF16)


def _block_diag2(m):
    z = jnp.zeros_like(m[0])
    return jnp.concatenate([jnp.concatenate([m[0], z], axis=1), jnp.concatenate([z, m[1]], axis=1)], axis=0)


def _pad_to(x, axis, size):
    pad = [(0, 0)] * x.ndim
    pad[axis] = (0, size - x.shape[axis])
    return jnp.pad(x, pad)


def _rope_table(ctx_len, seq_len, hdim, kv_heads):
    rows = seq_len // GRID_W
    row = jnp.repeat(jnp.arange(rows), GRID_W).astype(F32)
    col = jnp.tile(jnp.arange(GRID_W), rows).astype(F32)
    half = hdim // 2
    inv = ROPE_THETA ** (-jnp.arange(0, half, 2, dtype=F32) / half)
    ang = jnp.concatenate([row[:, None] * inv, col[:, None] * inv], axis=-1)
    cos = jnp.repeat(jnp.cos(ang), 2, axis=1)
    sin = jnp.repeat(jnp.sin(ang), 2, axis=1) * jnp.tile(jnp.array([-1.0, 1.0], F32), half)
    cos = jnp.concatenate([jnp.ones((ctx_len, hdim), F32), cos], axis=0)
    sin = jnp.concatenate([jnp.zeros((ctx_len, hdim), F32), sin], axis=0)
    return jnp.concatenate([jnp.tile(cos, (1, kv_heads)), jnp.tile(sin, (1, kv_heads))], axis=1)


def kernel(x, c, ctx, c_ctx, ada_w, ada_b, norm1, norm2, w_in, rwkv_mu_x, rwkv_mu_rkv, rwkv_w0, rwkv_w1, rwkv_w2, rwkv_a0, rwkv_a1, rwkv_a2, rwkv_g1, rwkv_g2, rwkv_k_k, rwkv_k_a, rwkv_r_k, rwkv_lnx_w, rwkv_lnx_b, gdn_conv, gdn_w_alpha, gdn_dt_bias, gdn_A_log, gdn_w_beta, gdn_norm, attn_q_norm, attn_k_norm, w_up_a, w_up_b, w_up_c, w_gate, b_gate, w_out, ffn_w1, ffn_w3, ffn_w2, final_norm):
    batch, seq_len, d = x.shape
    ctx_len = ctx.shape[1]
    depth = ada_w.shape[0]
    sq = _Seq(batch, ctx_len, seq_len)

    r_heads, r_hdim = rwkv_r_k.shape[1], rwkv_r_k.shape[2]
    rdim = r_heads * r_hdim
    g_heads, g_hdim = gdn_w_alpha.shape[-1], gdn_norm.shape[-1]
    gdim = g_heads * g_hdim
    a_hdim = attn_q_norm.shape[-1]
    qd = w_up_c.shape[1]
    q_heads = qd // a_hdim
    kvd = (w_in.shape[-1] - 3 * rdim - 4 * gdim - qd) // 2
    kv_heads = kvd // a_hdim
    assert rdim == gdim == qd and r_hdim == a_hdim, "lane-segment constants are shared between mixers"
    assert (3 * rdim) % (3 * gdim) == 0 and (3 * rdim + 3 * gdim) % gdim == 0
    gdn_col = (3 * rdim) // (3 * gdim)
    z_col = (3 * rdim + 3 * gdim) // gdim
    q_col = (3 * rdim + 4 * gdim) // qd
    assert (3 * rdim + 4 * gdim + qd) % (2 * kvd) == 0
    kv_col = (3 * rdim + 4 * gdim + qd) // (2 * kvd)

    bd_r = _block_diag_ones(rdim, r_hdim)
    bd_g = _block_diag_ones(gdim, g_hdim)
    cs_tab = _rope_table(ctx_len, seq_len, a_hdim, kv_heads)
    n_beta = 2 * g_heads
    assert 2 * n_beta <= LANES

    cond = jnp.concatenate([c, c_ctx[None, :]], axis=0)
    cond = _pad_to(cond, 0, -(-(batch + 1) // SUBLANES) * SUBLANES)

    xs = jnp.concatenate([ctx, x], axis=1)
    for l in range(depth):
        w1c = jnp.concatenate([rwkv_w1[l, 0], rwkv_w1[l, 1]], axis=1)
        a1c = jnp.concatenate([rwkv_a1[l, 0], rwkv_a1[l, 1]], axis=1)
        g1w = _pad_to(rwkv_g1[l], 1, 2 * LANES)
        wab = _pad_to(jnp.concatenate([gdn_w_beta[l, 0], gdn_w_beta[l, 1],
                                       gdn_w_alpha[l, 0], gdn_w_alpha[l, 1]], axis=1), 1, LANES)
        mu = rwkv_mu_x[l]
        splits = (w1c.shape[1], w1c.shape[1] + a1c.shape[1], w1c.shape[1] + a1c.shape[1] + g1w.shape[1])
        lw = {
            "mu_rkv": rwkv_mu_rkv[l],
            "wh": jnp.concatenate([w1c, a1c, g1w, wab], axis=1).astype(BF16),
            "wd": jnp.concatenate([mu[0][:, None] * w1c, mu[1][:, None] * a1c, mu[2][:, None] * g1w],
                                  axis=1).astype(BF16),
            "splits": splits,
            "w2": _block_diag2(rwkv_w2[l]).astype(BF16),
            "w0": rwkv_w0[l].reshape(1, 2 * rdim),
            "a2": _block_diag2(rwkv_a2[l]).astype(BF16),
            "a0": rwkv_a0[l].reshape(1, 2 * rdim),
            "g2": _pad_to(rwkv_g2[l], 0, 2 * LANES).astype(BF16),
            "abb": _pad_to(jnp.concatenate([jnp.zeros((n_beta,), F32), gdn_dt_bias[l].reshape(-1)]), 0, LANES).reshape(1, LANES),
            "alog": _pad_to(jnp.concatenate([jnp.zeros((n_beta,), F32), gdn_A_log[l].reshape(-1)]), 0, LANES).reshape(1, LANES),
            "n_beta": n_beta,
            "k_k": rwkv_k_k[l].reshape(1, rdim), "k_a": rwkv_k_a[l].reshape(1, rdim),
            "r_k": rwkv_r_k[l].reshape(1, rdim),
            "lnx_w": rwkv_lnx_w[l].reshape(1, rdim), "lnx_b": rwkv_lnx_b[l].reshape(1, rdim),
            "bd_r": bd_r, "bd_g": bd_g,
            "conv": jnp.transpose(gdn_conv[l]),
            "gdn_norm": jnp.tile(gdn_norm[l], g_heads).reshape(1, gdim),
            "q_norm": jnp.tile(attn_q_norm[l], q_heads).reshape(1, qd),
            "k_norm": jnp.tile(attn_k_norm[l], kv_heads).reshape(1, kvd),
            "w_up_a": w_up_a[l].astype(BF16), "w_up_b": w_up_b[l].astype(BF16),
            "w_up_c": w_up_c[l].astype(BF16), "w_out": w_out[l].astype(BF16),
            "ffn_w1": ffn_w1[l].astype(BF16), "ffn_w3": ffn_w3[l].astype(BF16), "ffn_w2": ffn_w2[l].astype(BF16),
        }
        mod = _matmul(cond, ada_w[l].astype(BF16), bias=ada_b[l], pre_act="silu")
        mod_x = mod[:batch].reshape(batch, 6, d)
        mod_c = jnp.broadcast_to(mod[batch].reshape(1, 6, d), (batch, 6, d))
        modtab = jnp.stack([mod_c, mod_x], axis=1)

        h, proj, gates = _pre(sq, xs, norm1[l], modtab, w_in[l].astype(BF16), w_gate[l].astype(BF16), b_gate[l])
        lw_dec, a_iclr, g_out, ab = _lora(sq, h, lw)

        rf, rg, bonus, y0 = _rwkv_intra(sq, proj, a_iclr, lw_dec, lw, r_heads)
        gf, gga, ggb = _gdn_intra(sq, proj, ab, lw, g_heads, gdim, gdn_col)
        o_rf, o_rb, o_gf, o_gb = _scan(sq, rf, rg, gf, gga, ggb, r_heads, g_heads)

        aq, ak, av = _attn_prep(sq, proj, cs_tab, lw, q_heads, kv_heads, a_hdim, q_col, kv_col)
        yc = _attention(sq, aq, ak, av)

        xs = _merge(sq, xs, o_rf, o_rb, y0, bonus, g_out, o_gf, o_gb, proj, z_col, yc, gates, modtab, lw,
                    r_heads, g_heads)
        xs = _ffn(sq, xs, norm2[l], modtab, lw)
    return _final_norm(sq, xs, final_norm)
```

```python
import functools

import jax
import jax.numpy as jnp
from jax import lax
from jax.experimental import pallas as pl
from jax.experimental.pallas import tpu as pltpu

F32 = jnp.float32
BF16 = jnp.bfloat16

RMS_EPS = 1e-6
RWKV_LNX_EPS = 64e-5
ROPE_THETA = 10000.0
GRID_W = 64

SUBLANES = 8
LANES = 128
CHUNK = 64
MAX_TOKEN_BLOCK = 256
MAX_MM_ROWS = 512
FFN_ROWS = (544, 512, 256, 128)
KEY_BLOCK = 256
RWKV_PASSES = 1
GDN_PASSES = 1
INV_PASSES = 1
RWKV_INTRA_CHUNKS = 2
GDN_INTRA_CHUNKS = 4
SCAN_CHUNKS = 4
VMEM_LIMIT = 56 * 1024 * 1024


def _sigmoid(x):
    return 1.0 / (1.0 + jnp.exp(-x))


def _softplus(x):
    return jnp.maximum(x, 0.0) + jnp.log(1.0 + jnp.exp(-jnp.abs(x)))


def _split2(x):
    hi = x.astype(BF16)
    lo = (x - hi.astype(F32)).astype(BF16)
    return hi, lo


def _mm(a, b, dims, passes):
    d = functools.partial(lax.dot_general, dimension_numbers=(dims, ((), ())), preferred_element_type=F32)
    if passes == 1:
        return d(a.astype(BF16), b.astype(BF16))
    ah, al = _split2(a)
    bh, bl = _split2(b)
    return d(ah, bh) + (d(ah, bl) + d(al, bh))


def _dot(a, b, passes=1):
    return _mm(a, b, ((1,), (0,)), passes)


def _split3(x):
    hi = x.astype(BF16)
    r1 = x - hi.astype(F32)
    mid = r1.astype(BF16)
    lo = (r1 - mid.astype(F32)).astype(BF16)
    return hi, mid, lo


def _dot_exact_lhs(m01, x):
    d = functools.partial(jnp.dot, preferred_element_type=F32)
    hi, mid, lo = _split3(x)
    return d(m01, hi) + (d(m01, mid) + d(m01, lo))


def _segsum(x, bd_ref):
    d = functools.partial(jnp.dot, preferred_element_type=F32)
    hi, lo = _split2(x)
    bd = bd_ref[...]
    return d(hi, bd) + d(lo, bd)


def _row_iota(shape):
    return lax.broadcasted_iota(jnp.int32, shape, 0)


def _shift_rows(x, prev_ref, next_ref, k, first, last):
    n = x.shape[0]
    y = pltpu.roll(x, (-k) % n, axis=0)
    row = _row_iota((SUBLANES, x.shape[1]))
    if k < 0:
        tile = y[:SUBLANES]
        for i in range(-k):
            edge = prev_ref[SUBLANES + k + i:SUBLANES + k + i + 1, :]
            edge = jnp.where(first, 0.0, edge)
            tile = jnp.where(row == i, edge, tile)
        return jnp.concatenate([tile, y[SUBLANES:]], axis=0)
    tile = y[n - SUBLANES:]
    for i in range(k):
        edge = next_ref[i:i + 1, :]
        edge = jnp.where(last, 0.0, edge)
        tile = jnp.where(row == SUBLANES - k + i, edge, tile)
    return jnp.concatenate([y[:n - SUBLANES], tile], axis=0)


def _largest_divisor(n, candidates):
    for c in candidates:
        if n % c == 0:
            return c
    raise ValueError(f"no block size among {candidates} divides {n}")


class _Seq:
    def __init__(self, batch, ctx_len, seq_len):
        self.B = batch
        self.ctx = ctx_len
        self.S = ctx_len + seq_len
        self.TB = _largest_divisor(ctx_len, (MAX_TOKEN_BLOCK, 128, 64))
        assert seq_len % self.TB == 0 and self.TB % CHUNK == 0
        self.ncb = ctx_len // self.TB
        self.nblk = self.S // self.TB
        self.ncc = ctx_len // CHUNK
        self.nchunk = self.S // CHUNK
        self.grid = (batch, self.nblk)

    def rows(self, width, col_block=0):
        return pl.BlockSpec((None, self.TB, width), lambda b, j: (b, j, col_block))

    def rows2(self, width):
        return pl.BlockSpec((2, None, self.TB, width), lambda b, j: (0, b, j, 0))

    def heads(self, n_heads, width):
        return pl.BlockSpec((None, n_heads, self.TB, width), lambda b, j: (b, 0, j, 0))

    def prev_rows(self, width, col_block=0, tb=None):
        per = (tb or self.TB) // SUBLANES
        return pl.BlockSpec((None, SUBLANES, width),
                            lambda b, j: (b, jnp.maximum(j * per - 1, 0), col_block))

    def next_rows(self, width, col_block=0, tb=None):
        per = (tb or self.TB) // SUBLANES
        top = self.S // SUBLANES - 1
        return pl.BlockSpec((None, SUBLANES, width),
                            lambda b, j: (b, jnp.minimum((j + 1) * per, top), col_block))

    def const(self, shape):
        zeros = (0,) * len(shape)
        return pl.BlockSpec(shape, lambda b, j: zeros)

    def edges(self, tb=None):
        tb = tb or self.TB
        j = pl.program_id(1)
        ncb, nblk = self.ctx // tb, self.S // tb
        first = (j == 0) | (j == ncb)
        last = (j == ncb - 1) | (j == nblk - 1)
        return first, last


def _params(n_axes):
    return pltpu.CompilerParams(dimension_semantics=("arbitrary",) * n_axes,
                                vmem_limit_bytes=VMEM_LIMIT)


def _mm_body(*refs, act, pre_act, n_chunk, has_bias):
    if has_bias:
        x_ref, w_ref, b_ref, o_ref = refs
    else:
        x_ref, w_ref, o_ref = refs
        b_ref = None
    x = x_ref[...]
    if pre_act == "silu":
        x = x * _sigmoid(x)
    xb = x.astype(BF16)
    n = o_ref.shape[-1]
    for n0 in range(0, n, n_chunk):
        y = jnp.dot(xb, w_ref[:, n0:n0 + n_chunk], preferred_element_type=F32)
        if b_ref is not None:
            y = y + b_ref[:, n0:n0 + n_chunk]
        if act == "sigmoid":
            y = _sigmoid(y)
        o_ref[:, n0:n0 + n_chunk] = y.astype(o_ref.dtype)


def _matmul(x, w, bias=None, act=None, pre_act=None, out_dtype=F32):
    m, k = x.shape
    n = w.shape[1]
    tm = m if m <= MAX_MM_ROWS else _largest_divisor(m, (MAX_MM_ROWS, 256, 128, 64, 32, 16, 8))
    n_chunk = _largest_divisor(n, (512, 256, 128))
    in_specs = [pl.BlockSpec((tm, k), lambda i: (i, 0)),
                pl.BlockSpec((k, n), lambda i: (0, 0))]
    args = [x, w]
    if bias is not None:
        in_specs.append(pl.BlockSpec((1, n), lambda i: (0, 0)))
        args.append(bias.reshape(1, n))
    return pl.pallas_call(
        functools.partial(_mm_body, act=act, pre_act=pre_act, n_chunk=n_chunk,
                          has_bias=bias is not None),
        grid=(m // tm,),
        in_specs=in_specs,
        out_specs=pl.BlockSpec((tm, n), lambda i: (i, 0)),
        out_shape=jax.ShapeDtypeStruct((m, n), out_dtype),
        compiler_params=_params(1),
        name="matmul",
    )(*args)


def _pre_body(x_ref, g_ref, mod_ref, win_ref, wg_ref, bg_ref, h_ref, p_ref, gate_ref, *, n_chunk, ctx_len):
    rows = x_ref.shape[0]
    row = pl.program_id(1) * rows + lax.broadcasted_iota(jnp.int32, (rows, 1), 0)
    is_ctx = row < ctx_len
    pick = lambda i: jnp.where(is_ctx, mod_ref[0, i:i + 1, :], mod_ref[1, i:i + 1, :])
    x = x_ref[...]
    y = x * lax.rsqrt(jnp.mean(x * x, axis=-1, keepdims=True) + RMS_EPS) * g_ref[...]
    h = y * (1.0 + pick(1)) + pick(0)
    h_ref[...] = h
    hb = h.astype(BF16)
    for n0 in range(0, p_ref.shape[-1], n_chunk):
        p_ref[:, n0:n0 + n_chunk] = jnp.dot(hb, win_ref[:, n0:n0 + n_chunk], preferred_element_type=F32)
    for n0 in range(0, gate_ref.shape[-1], n_chunk):
        z = jnp.dot(hb, wg_ref[:, n0:n0 + n_chunk], preferred_element_type=F32) + bg_ref[:, n0:n0 + n_chunk]
        gate_ref[:, n0:n0 + n_chunk] = _sigmoid(z).astype(gate_ref.dtype)


def _pre(sq, xs, g, modtab, w_in, w_gate, b_gate):
    d = xs.shape[-1]
    n_in, n_gate = w_in.shape[1], w_gate.shape[1]
    n_chunk = _largest_divisor(n_in, (256, 128))
    assert n_gate % n_chunk == 0
    tb = _largest_divisor(sq.S, FFN_ROWS)
    rows = lambda width: pl.BlockSpec((None, tb, width), lambda b, j: (b, j, 0))
    const = lambda shape: pl.BlockSpec(shape, lambda b, j: (0,) * len(shape))
    resident = lambda shape: pl.BlockSpec(shape, lambda b, j: (0,) * len(shape), pipeline_mode=pl.Buffered(1))
    return pl.pallas_call(
        functools.partial(_pre_body, n_chunk=n_chunk, ctx_len=sq.ctx),
        grid=(sq.B, sq.S // tb),
        in_specs=[rows(d), const((1, d)), pl.BlockSpec((None, 2, 6, d), lambda b, j: (b, 0, 0, 0)),
                  resident(w_in.shape), resident(w_gate.shape), const((1, n_gate))],
        out_specs=[rows(d), rows(n_in), rows(n_gate)],
        out_shape=[jax.ShapeDtypeStruct(xs.shape, F32),
                   jax.ShapeDtypeStruct(xs.shape[:2] + (n_in,), F32),
                   jax.ShapeDtypeStruct(xs.shape[:2] + (n_gate,), BF16)],
        compiler_params=_params(2),
        name="pre",
    )(xs, g.reshape(1, d), modtab, w_in, w_gate, b_gate.reshape(1, n_gate))


def _final_norm_body(x_ref, g_ref, o_ref):
    x = x_ref[...]
    y = x * lax.rsqrt(jnp.mean(x * x, axis=-1, keepdims=True) + RMS_EPS)
    o_ref[...] = y * g_ref[...]


def _final_norm(sq, xs, g):
    d = xs.shape[-1]
    ncb = sq.ncb
    return pl.pallas_call(
        _final_norm_body,
        grid=(sq.B, sq.nblk - ncb),
        in_specs=[pl.BlockSpec((None, sq.TB, d), lambda b, j: (b, j + ncb, 0)),
                  pl.BlockSpec((1, d), lambda b, j: (0, 0))],
        out_specs=pl.BlockSpec((None, sq.TB, d), lambda b, j: (b, j, 0)),
        out_shape=jax.ShapeDtypeStruct((sq.B, sq.S - sq.ctx, d), F32),
        compiler_params=_params(2),
        name="final_norm",
    )(xs, g.reshape(1, d))


def _lora_body(h_ref, hp_ref, hn_ref, wh_ref, wd_ref, w2_ref, w0_ref, a2_ref, a0_ref,
               g2_ref, abb_ref, alog_ref, lw_ref, a_ref, g_ref, ab_ref, *, sq, rdim, n_beta, splits):
    first, last = sq.edges()
    h = h_ref[...]
    nb = 0.5 * (_shift_rows(h, hp_ref, hn_ref, -1, first, last)
                + _shift_rows(h, hp_ref, hn_ref, 1, first, last))
    zh = _dot(h, wh_ref[...])
    zd = _dot(nb - h, wd_ref[...])
    s1, s2, s3 = splits
    wl = w0_ref[...] + _dot(jnp.tanh(zh[:, :s1] + zd[:, :s1]), w2_ref[...])
    lw = -(jnp.exp(-0.5) * _sigmoid(wl))
    lw_ref[0] = lw[:, :rdim]
    lw_ref[1] = lw[:, rdim:]
    a = _sigmoid(a0_ref[...] + _dot(zh[:, s1:s2] + zd[:, s1:s2], a2_ref[...]))
    a_ref[0] = a[:, :rdim]
    a_ref[1] = a[:, rdim:]
    g_ref[...] = _dot(_sigmoid(zh[:, s2:s3] + zd[:, s2:s3]), g2_ref[...])
    z = zh[:, s3:]
    col = lax.broadcasted_iota(jnp.int32, z.shape, 1)
    gl = -jnp.exp(alog_ref[...]) * _softplus(z + abb_ref[...])
    ab_ref[...] = jnp.where(col < n_beta, _sigmoid(z), gl)


def _lora(sq, h, lw):
    d = h.shape[-1]
    rdim = lw["w0"].shape[-1] // 2
    consts = [lw["wh"], lw["wd"], lw["w2"], lw["w0"], lw["a2"], lw["a0"],
              lw["g2"], lw["abb"], lw["alog"]]
    bsd = (sq.B, sq.S)
    return pl.pallas_call(
        functools.partial(_lora_body, sq=sq, rdim=rdim, n_beta=lw["n_beta"], splits=lw["splits"]),
        grid=sq.grid,
        in_specs=[sq.rows(d), sq.prev_rows(d), sq.next_rows(d)] + [sq.const(c.shape) for c in consts],
        out_specs=[sq.rows2(rdim), sq.rows2(rdim), sq.rows(rdim), sq.rows(LANES)],
        out_shape=[jax.ShapeDtypeStruct((2,) + bsd + (rdim,), F32),
                   jax.ShapeDtypeStruct((2,) + bsd + (rdim,), F32),
                   jax.ShapeDtypeStruct(bsd + (rdim,), F32),
                   jax.ShapeDtypeStruct(bsd + (LANES,), F32)],
        compiler_params=_params(2),
        name="lora",
    )(h, h, h, *consts)


def _causal_masks(d, c):
    row = lax.broadcasted_iota(jnp.int32, (c, c), 0)
    col = lax.broadcasted_iota(jnp.int32, (c, c), 1)
    delta = jnp.where(d == 0, row - col, col - row)
    eye = jnp.where(row == col, 1.0, 0.0).astype(F32)
    return delta >= 0, delta > 0, eye


def _stacked_mask(d, c, reps=1):
    row = lax.broadcasted_iota(jnp.int32, (2 * c, reps * c), 0)
    col = lax.broadcasted_iota(jnp.int32, (2 * c, reps * c), 1) & (c - 1)
    rr = row & (c - 1)
    delta = jnp.where(d == 0, rr - col, col - rr)
    return (delta > 0) | ((row >= c) & (delta == 0))


def _block_rows(x, w):
    left = lax.broadcasted_iota(jnp.int32, x.shape, 1) < w
    zero = jnp.zeros_like(x)
    return jnp.concatenate([jnp.where(left, x, zero), jnp.where(left, zero, x)], axis=0)


def _fold_rows(y):
    n = y.shape[0] // 2
    return y[:n] + y[n:]


def _tri_inverse_pairs(ls):
    n = ls[0].shape[0]
    row = lax.broadcasted_iota(jnp.int32, (n, 2 * n), 0)
    col = lax.broadcasted_iota(jnp.int32, (n, 2 * n), 1) & (n - 1)
    eye = jnp.where(row == col, 1.0, 0.0).astype(F32)
    same = (row >> 1) == (col >> 1)
    ts = [eye - jnp.where(same, l, 0.0) for l in ls]
    for k in range(2, n.bit_length()):
        off = ((row >> k) == (col >> k)) & ((row >> (k - 1)) != (col >> (k - 1)))
        tl = [_dot(t, _block_rows(jnp.where(off, l, 0.0), n), INV_PASSES) for t, l in zip(ts, ls)]
        ts = [t - _dot(x, _block_rows(t, n), INV_PASSES) for x, t in zip(tl, ts)]
    return ts


def _tri_inverse_many(ls, eye):
    n = ls[0].shape[0]
    row = lax.broadcasted_iota(jnp.int32, (n, n), 0)
    col = lax.broadcasted_iota(jnp.int32, (n, n), 1)
    same = (row >> 1) == (col >> 1)
    ts = [eye - jnp.where(same, l, 0.0) for l in ls]
    for k in range(2, n.bit_length()):
        off = ((row >> k) == (col >> k)) & ((row >> (k - 1)) != (col >> (k - 1)))
        tl = [_dot(t, jnp.where(off, l, 0.0), INV_PASSES) for t, l in zip(ts, ls)]
        ts = [t - _dot(x, t, INV_PASSES) for x, t in zip(tl, ts)]
    return ts


RWKV_TILES = 6

def _rwkv_intra_body(p_ref, pp_ref, pn_ref, a_ref, lw_ref, mu_ref, kk_ref, ka_ref, rk_ref, bd_ref,
                     f_ref, g_ref, bonus_ref, y0_ref, *, sq, heads, hdim, cps):
    c = CHUNK
    pw = 2 * hdim
    rdim = heads * hdim
    first, last = sq.edges(cps * c)
    proj = p_ref[...]
    nb = 0.5 * (_shift_rows(proj, pp_ref, pn_ref, -1, first, last)
                + _shift_rows(proj, pp_ref, pn_ref, 1, first, last))
    dlt = nb - proj
    r_all = proj[:, :rdim] + dlt[:, :rdim] * mu_ref[0:1, :]
    k_all = proj[:, rdim:2 * rdim] + dlt[:, rdim:2 * rdim] * mu_ref[1:2, :]
    v_all = proj[:, 2 * rdim:] + dlt[:, 2 * rdim:] * mu_ref[2:3, :]
    kn = k_all * kk_ref[...]
    kk_all = kn * lax.rsqrt(_segsum(kn * kn, bd_ref) + RMS_EPS)
    kd_all = [k_all * (1.0 + (a_ref[d] - 1.0) * ka_ref[...]) for d in range(2)]
    rk = r_all * rk_ref[...]
    bonus_ref[...] = (_segsum(rk * kd_all[0], bd_ref) + _segsum(rk * kd_all[1], bd_ref)) * v_all
    keys, a_p, r_p, v_p, bt_p, kt_p, ec_p, mask2 = [], {}, {}, {}, {}, {}, {}, {}
    for d in range(2):
        incl, _, _ = _causal_masks(d, c)
        m01 = jnp.where(incl, 1.0, 0.0).astype(BF16)
        mask2[d] = _stacked_mask(d, c, reps=2)
        for j in range(cps):
            rows = slice(j * c, (j + 1) * c)
            lw = lw_ref[d, rows, :]
            cum = _dot_exact_lhs(m01, lw)
            e_neg = jnp.exp(-cum)
            e_row = jnp.broadcast_to(jnp.exp(jnp.sum(lw, axis=0, keepdims=True)), lw.shape)
            a_t = -kk_all[rows] * jnp.exp(cum - lw)
            r_t = r_all[rows] * jnp.exp(cum)
            b_t = kk_all[rows] * a_ref[d, rows, :] * e_neg
            k_t = kd_all[d][rows] * e_neg
            v = v_all[rows]
            for p in range(heads // 2):
                key = (d, j, p)
                keys.append(key)
                lanes = slice(p * pw, (p + 1) * pw)
                a_p[key], r_p[key], v_p[key] = a_t[:, lanes], r_t[:, lanes], v[:, lanes]
                bt_p[key] = jnp.transpose(_block_rows(b_t[:, lanes], hdim))
                kt_p[key] = jnp.transpose(_block_rows(k_t[:, lanes], hdim))
                ec_p[key] = jnp.transpose(_block_rows(e_row[:, lanes], hdim))
    ar = {k: jnp.concatenate([a_p[k], r_p[k]], axis=0) for k in keys}
    mb = {k: jnp.where(mask2[k[0]], _dot(ar[k], bt_p[k], RWKV_PASSES), 0.0) for k in keys}
    mk = {k: jnp.where(mask2[k[0]], _dot(ar[k], kt_p[k], RWKV_PASSES), 0.0) for k in keys}
    v_bd = {k: _block_rows(v_p[k], hdim) for k in keys}
    x = {k: _dot(mk[k], v_bd[k], RWKV_PASSES) for k in keys}
    h0 = {k: _dot(_fold_rows(ec_p[k] * kt_p[k]), v_bd[k], RWKV_PASSES) for k in keys}
    t = dict(zip(keys, _tri_inverse_pairs([-mb[k][:c] for k in keys])))
    wt = {k: _dot(t[k], _block_rows(a_p[k], hdim), RWKV_PASSES) for k in keys}
    ut = {k: _dot(t[k], _block_rows(x[k][:c], hdim), RWKV_PASSES) for k in keys}
    for k in keys:
        d, j, p = k
        lanes = slice(p * pw, (p + 1) * pw)
        f_ref[d, j * c:(j + 1) * c, lanes] = _fold_rows(ec_p[k])
        tiles = (wt[k], r_p[k], mb[k][c:], _fold_rows(ec_p[k] * bt_p[k]), ut[k], h0[k])
        for i, tile in enumerate(tiles):
            o = (j * RWKV_TILES + i) * c
            g_ref[d, o:o + c, lanes] = tile.astype(BF16)
        if d == 0:
            y0_ref[j * c:(j + 1) * c, lanes] = x[k][c:] + x[1, j, p][c:]


def _rwkv_intra(sq, proj, a, lw_dec, lw, heads):
    rdim = a.shape[-1]
    hdim = rdim // heads
    assert hdim == CHUNK and heads % 2 == 0 and 2 * hdim == LANES, "pair tiles are [CHUNK, 128 lanes]"
    cps = _largest_divisor(sq.ncc, (RWKV_INTRA_CHUNKS, 1))
    rows = cps * CHUNK
    consts = [lw["mu_rkv"], lw["k_k"], lw["k_a"], lw["r_k"], lw["bd_r"]]
    const = lambda shape: pl.BlockSpec(shape, lambda b, i: (0,) * len(shape))
    per_dir = pl.BlockSpec((2, None, rows, rdim), lambda b, i: (0, b, i, 0))
    out_g = pl.BlockSpec((2, None, RWKV_TILES * rows, rdim), lambda b, i: (0, b, i, 0))
    return pl.pallas_call(
        functools.partial(_rwkv_intra_body, sq=sq, heads=heads, hdim=hdim, cps=cps),
        grid=(sq.B, sq.nchunk // cps),
        in_specs=[pl.BlockSpec((None, rows, 3 * rdim), lambda b, i: (b, i, 0)),
                  sq.prev_rows(3 * rdim, tb=rows), sq.next_rows(3 * rdim, tb=rows), per_dir, per_dir]
        + [const(c.shape) for c in consts],
        out_specs=[per_dir, out_g] + [pl.BlockSpec((None, rows, rdim), lambda b, i: (b, i, 0))] * 2,
        out_shape=[jax.ShapeDtypeStruct((2, sq.B, sq.S, rdim), F32),
                   jax.ShapeDtypeStruct((2, sq.B, RWKV_TILES * sq.S, rdim), BF16)]
        + [jax.ShapeDtypeStruct((sq.B, sq.S, rdim), F32)] * 2,
        compiler_params=_params(2),
        name="rwkv_intra",
    )(proj, proj, proj, a, lw_dec, *consts)


def _backward_block(i, ncb, nblk):
    return jnp.where(i < ncb, ncb - 1 - i, nblk + ncb - 1 - i)


def _scan_chunks_per_step(sq):
    cps = _largest_divisor(sq.ncc, (SCAN_CHUNKS, 2, 1))
    assert sq.nchunk % cps == 0
    return cps


def _scan_specs(sq, heads, rows, width):
    cps = _scan_chunks_per_step(sq)
    ncb, nblk = sq.ncc // cps, sq.nchunk // cps
    fwd = pl.BlockSpec((None, None, heads, cps * rows, width), lambda b, i: (0, b, 0, i, 0))
    bwd = pl.BlockSpec((None, None, heads, cps * rows, width),
                       lambda b, i: (1, b, 0, _backward_block(i, ncb, nblk), 0))
    return fwd, bwd


def _scan_out_specs(sq, width):
    cps = _scan_chunks_per_step(sq)
    ncb, nblk = sq.ncc // cps, sq.nchunk // cps
    fwd = pl.BlockSpec((None, cps * CHUNK, width), lambda b, i: (b, i, 0))
    bwd = pl.BlockSpec((None, cps * CHUNK, width), lambda b, i: (b, _backward_block(i, ncb, nblk), 0))
    return [fwd, bwd]


def _gdn_intra_body(p_ref, pp_ref, pn_ref, ab_ref, cw_ref, bd_ref, f_ref, ga_ref, gb_ref, *, sq, heads, hdim, cps):
    c = CHUNK
    gdim = heads * hdim
    scale = hdim ** -0.5
    first, last = sq.edges(cps * c)
    proj = p_ref[...]
    width = cw_ref.shape[0]
    half = width // 2
    acc = proj * cw_ref[half:half + 1, :]
    for j in range(width):
        if j != half:
            acc = acc + _shift_rows(proj, pp_ref, pn_ref, j - half, first, last) * cw_ref[j:j + 1, :]
    y = acc * _sigmoid(acc)
    q_blk = y[:, :gdim]
    k_blk = y[:, gdim:2 * gdim]
    q_blk = q_blk * lax.rsqrt(_segsum(q_blk * q_blk, bd_ref) + RMS_EPS)
    k_blk = k_blk * lax.rsqrt(_segsum(k_blk * k_blk, bd_ref) + RMS_EPS)
    v_blk = y[:, 2 * gdim:]
    sl = [slice(h * hdim, (h + 1) * hdim) for h in range(heads)]
    keys = []
    beta, gc, g_last, gc_row, decay, kb, kq, kt_h, vb, strict_of, eye = {}, {}, {}, {}, {}, {}, {}, {}, {}, {}, None
    for j in range(cps):
        rows = slice(j * c, (j + 1) * c)
        ab = ab_ref[rows, :]
        tot_all = jnp.sum(ab, axis=0, keepdims=True)
        lane = lax.broadcasted_iota(jnp.int32, ab.shape, 1)
        k_all = k_blk[rows]
        k_tt = jnp.transpose(k_all)
        for d in range(2):
            incl, strict, eye = _causal_masks(d, c)
            strict_of[d] = strict
            gc_all = _dot_exact_lhs(jnp.where(incl, 1.0, 0.0).astype(BF16), ab)
            gc_all_t = jnp.transpose(gc_all)
            sub = lax.broadcasted_iota(jnp.int32, gc_all_t.shape, 0)
            for h in range(heads):
                key = (d, j, h)
                keys.append(key)
                pick_b = lane == d * heads + h
                pick_g = lane == 2 * heads + d * heads + h
                beta[key] = jnp.sum(jnp.where(pick_b, ab, 0.0), axis=1, keepdims=True)
                gc[key] = jnp.sum(jnp.where(pick_g, gc_all, 0.0), axis=1, keepdims=True)
                g_last[key] = jnp.sum(jnp.where(pick_g[:1], tot_all, 0.0), axis=1, keepdims=True)
                gc_row[key] = jnp.sum(jnp.where(sub == 2 * heads + d * heads + h, gc_all_t, 0.0),
                                      axis=0, keepdims=True)
                diff = gc[key] - gc_row[key]
                decay[key] = jnp.where(incl, jnp.exp(jnp.where(incl, diff, 0.0)), 0.0)
                kb[key] = k_all[:, sl[h]] * beta[key]
                kq[key] = jnp.concatenate([kb[key], q_blk[rows, sl[h]] * scale], axis=0)
                kt_h[key] = k_tt[sl[h]]
                vb[key] = v_blk[rows, sl[h]] * beta[key]
    m = {k: _dot(kq[k], kt_h[k], GDN_PASSES) for k in keys}
    lower = [jnp.where(strict_of[k[0]], m[k][:c] * decay[k], 0.0) for k in keys]
    t = dict(zip(keys, _tri_inverse_many(lower, eye)))
    e_gc = {k: jnp.exp(gc[k]) for k in keys}
    sol = {k: _dot(t[k], jnp.concatenate([vb[k], kb[k] * e_gc[k]], axis=1), GDN_PASSES) for k in keys}
    for k in keys:
        d, j, h = k
        of, oa, ob = j * (c + SUBLANES), j * 2 * c, j * (c + hdim)
        f_ref[d, h, of:of + c] = sol[k][:, :hdim]
        f_ref[d, h, of + c:of + c + SUBLANES] = jnp.broadcast_to(jnp.exp(g_last[k]), (SUBLANES, hdim))
        ga_ref[d, h, oa:oa + c] = sol[k][:, hdim:].astype(BF16)
        ga_ref[d, h, oa + c:oa + 2 * c] = (kq[k][c:] * e_gc[k]).astype(BF16)
        gb_ref[d, h, ob:ob + c] = (m[k][c:] * decay[k]).astype(BF16)
        gb_ref[d, h, ob + c:ob + c + hdim] = (kt_h[k] * jnp.exp(g_last[k] - gc_row[k])).astype(BF16)


def _gdn_intra(sq, proj, ab, lw, heads, gdim, col_block):
    hdim = gdim // heads
    cps = _largest_divisor(sq.ncc, (GDN_INTRA_CHUNKS, 2, 1))
    tb = cps * CHUNK
    w = 3 * gdim
    consts = [lw["conv"], lw["bd_g"]]
    const = lambda shape: pl.BlockSpec(shape, lambda b, i: (0,) * len(shape))
    small = pl.BlockSpec((None, tb, LANES), lambda b, i: (b, i, 0))
    rows = (CHUNK + SUBLANES, 2 * CHUNK, CHUNK + hdim)
    widths = (hdim, hdim, CHUNK)
    dtypes = (F32, BF16, BF16)
    outs = [pl.BlockSpec((2, None, heads, cps * r, wd), lambda b, i: (0, b, 0, i, 0)) for r, wd in zip(rows, widths)]
    shapes = [jax.ShapeDtypeStruct((2, sq.B, heads, sq.nchunk * r, wd), dt) for r, wd, dt in zip(rows, widths, dtypes)]
    return pl.pallas_call(
        functools.partial(_gdn_intra_body, sq=sq, heads=heads, hdim=hdim, cps=cps),
        grid=(sq.B, sq.nchunk // cps),
        in_specs=[pl.BlockSpec((None, tb, w), lambda b, i: (b, i, col_block)),
                  sq.prev_rows(w, col_block, tb=tb), sq.next_rows(w, col_block, tb=tb), small]
        + [const(c.shape) for c in consts],
        out_specs=outs,
        out_shape=shapes,
        compiler_params=_params(2),
        name="gdn_intra",
    )(proj, proj, proj, ab, *consts)


def _scan_body(rf0_ref, rg0_ref, rf1_ref, rg1_ref, f0_ref, ga0_ref, gb0_ref, f1_ref, ga1_ref, gb1_ref,
               rof_ref, rob_ref, gof_ref, gob_ref, h_ref, s_ref, *, r_heads, r_hdim, g_heads, g_hdim, cps):
    @pl.when(pl.program_id(1) == 0)
    def _():
        h_ref[...] = jnp.zeros_like(h_ref)
        s_ref[...] = jnp.zeros_like(s_ref)

    c = CHUNK
    dot = functools.partial(jnp.dot, preferred_element_type=F32)
    pw = 2 * r_hdim
    rfs, rgs, routs = (rf0_ref, rf1_ref), (rg0_ref, rg1_ref), (rof_ref, rob_ref)
    rch = [(d, p) for d in range(2) for p in range(r_heads // 2)]
    lanes = [slice(p * pw, (p + 1) * pw) for p in range(r_heads // 2)]
    hst = {ch: h_ref[ch[0], ch[1]] for ch in rch}
    fs, gas, gbs, gouts = (f0_ref, f1_ref), (ga0_ref, ga1_ref), (gb0_ref, gb1_ref), (gof_ref, gob_ref)
    gch = [(d, h) for d in range(2) for h in range(g_heads)]
    sst = {ch: s_ref[ch[0], ch[1]] for ch in gch}
    for step in range(cps):
        sub = (step, cps - 1 - step)

        def tile(d, p, i, n=1):
            o = (sub[d] * RWKV_TILES + i) * c
            return rgs[d][o:o + n * c, lanes[p]]

        of = [(c + SUBLANES) * sub[d] for d in range(2)]
        oa = [2 * c * sub[d] for d in range(2)]
        ob = [(c + g_hdim) * sub[d] for d in range(2)]
        m2 = {(d, p): dot(tile(d, p, 0, 2), hst[d, p].astype(BF16)) for d, p in rch}
        m = {(d, h): dot(gas[d][h, oa[d]:oa[d] + 2 * c, :], sst[d, h].astype(BF16)) for d, h in gch}
        ub = {(d, p): _block_rows((tile(d, p, 4) + m2[d, p][:c]).astype(BF16), r_hdim) for d, p in rch}
        vn = {(d, h): (fs[d][h, of[d]:of[d] + c, :] - m[d, h][:c]).astype(BF16) for d, h in gch}
        for d, p in rch:
            routs[d][sub[d] * c:(sub[d] + 1) * c, lanes[p]] = m2[d, p][c:] + dot(tile(d, p, 2), ub[d, p])
            decay = _block_rows(rfs[d][sub[d] * c:(sub[d] + 1) * c, lanes[p]], r_hdim)
            hst[d, p] = (decay * hst[d, p] + _block_rows(tile(d, p, 5), r_hdim)
                         + dot(_block_rows(tile(d, p, 3), r_hdim), ub[d, p]))
        for d, h in gch:
            gouts[d][sub[d] * c:(sub[d] + 1) * c, h * g_hdim:(h + 1) * g_hdim] = (
                m[d, h][c:] + dot(gbs[d][h, ob[d]:ob[d] + c, :], vn[d, h]))
            sst[d, h] = (sst[d, h] * fs[d][h, of[d] + c:of[d] + c + 1, :]
                         + dot(gbs[d][h, ob[d] + c:ob[d] + c + g_hdim, :], vn[d, h]))
    for d, p in rch:
        h_ref[d, p] = hst[d, p]
    for d, h in gch:
        s_ref[d, h] = sst[d, h]


def _scan(sq, rf, rg, gf, gga, ggb, r_heads, g_heads):
    rdim = rf.shape[-1]
    r_hdim = rdim // r_heads
    g_hdim = gf.shape[-1]
    gdim = g_heads * g_hdim
    cps = _scan_chunks_per_step(sq)
    ncb, nblk = sq.ncc // cps, sq.nchunk // cps
    fwd = lambda rows: pl.BlockSpec((None, None, rows, rdim), lambda b, i: (0, b, i, 0))
    bwd = lambda rows: pl.BlockSpec((None, None, rows, rdim),
                                    lambda b, i: (1, b, _backward_block(i, ncb, nblk), 0))
    rows_f, rows_g = cps * CHUNK, cps * RWKV_TILES * CHUNK
    gspecs = [_scan_specs(sq, g_heads, arr.shape[3] // sq.nchunk, arr.shape[4]) for arr in (gf, gga, ggb)]
    r_one = jax.ShapeDtypeStruct((sq.B, sq.S, rdim), F32)
    g_one = jax.ShapeDtypeStruct((sq.B, sq.S, gdim), F32)
    return pl.pallas_call(
        functools.partial(_scan_body, r_heads=r_heads, r_hdim=r_hdim, g_heads=g_heads, g_hdim=g_hdim, cps=cps),
        grid=(sq.B, sq.nchunk // cps),
        in_specs=[fwd(rows_f), fwd(rows_g), bwd(rows_f), bwd(rows_g)]
        + [s[0] for s in gspecs] + [s[1] for s in gspecs],
        out_specs=_scan_out_specs(sq, rdim) + _scan_out_specs(sq, gdim),
        out_shape=[r_one, r_one, g_one, g_one],
        scratch_shapes=[pltpu.VMEM((2, r_heads // 2, 2 * r_hdim, 2 * r_hdim), F32),
                        pltpu.VMEM((2, g_heads, g_hdim, g_hdim), F32)],
        compiler_params=_params(2),
        name="scan",
    )(rf, rg, rf, rg, gf, gga, ggb, gf, gga, ggb)


def _rope(x, cos, sin_signed):
    n = x.shape[-1]
    lane = lax.broadcasted_iota(jnp.int32, x.shape, 1)
    partner = jnp.where((lane & 1) == 0, pltpu.roll(x, n - 1, axis=1), pltpu.roll(x, 1, axis=1))
    return x * cos + partner * sin_signed


def _attn_prep_body(q_ref, kv_ref, cs_ref, qn_ref, kn_ref, bd_ref, qo_ref, ko_ref, vo_ref,
                    *, q_heads, kv_heads, hdim):
    kvd = kv_heads * hdim
    cos = cs_ref[:, :kvd]
    sin = cs_ref[:, kvd:]
    reps = q_heads // kv_heads
    cos_q = jnp.concatenate([cos] * reps, axis=1)
    sin_q = jnp.concatenate([sin] * reps, axis=1)
    inv_n = 1.0 / hdim
    q = q_ref[...]
    q = q * lax.rsqrt(_segsum(q * q, bd_ref) * inv_n + RMS_EPS) * qn_ref[...]
    q = _rope(q, cos_q, sin_q) * (hdim ** -0.5)
    kv = kv_ref[...]
    k = kv[:, :kvd]
    kbd = bd_ref[:kvd, :kvd]
    hi, lo = _split2(k * k)
    ms = (jnp.dot(hi, kbd, preferred_element_type=F32) + jnp.dot(lo, kbd, preferred_element_type=F32)) * inv_n
    k = k * lax.rsqrt(ms + RMS_EPS) * kn_ref[...]
    k = _rope(k, cos, sin)
    v = kv[:, kvd:]
    for h in range(q_heads):
        qo_ref[h] = q[:, h * hdim:(h + 1) * hdim].astype(BF16)
    k_t = jnp.transpose(k)
    for h in range(kv_heads):
        ko_ref[h] = k_t[h * hdim:(h + 1) * hdim].astype(BF16)
        vh = v[:, h * hdim:(h + 1) * hdim]
        vo_ref[h] = jnp.concatenate([vh, jnp.ones_like(vh)], axis=1).astype(BF16)


def _attn_prep(sq, proj, cs_tab, lw, q_heads, kv_heads, hdim, q_col_block, kv_col_block):
    qd, kvd = q_heads * hdim, kv_heads * hdim
    consts = [lw["q_norm"], lw["k_norm"], lw["bd_r"]]
    k_spec = pl.BlockSpec((None, kv_heads, hdim, sq.TB), lambda b, j: (b, 0, 0, j))
    return pl.pallas_call(
        functools.partial(_attn_prep_body, q_heads=q_heads, kv_heads=kv_heads, hdim=hdim),
        grid=sq.grid,
        in_specs=[sq.rows(qd, q_col_block), sq.rows(2 * kvd, kv_col_block),
                  pl.BlockSpec((sq.TB, 2 * kvd), lambda b, j: (j, 0))]
        + [sq.const(c.shape) for c in consts],
        out_specs=[sq.heads(q_heads, hdim), k_spec, sq.heads(kv_heads, 2 * hdim)],
        out_shape=[jax.ShapeDtypeStruct((sq.B, q_heads, sq.S, hdim), BF16),
                   jax.ShapeDtypeStruct((sq.B, kv_heads, hdim, sq.S), BF16),
                   jax.ShapeDtypeStruct((sq.B, kv_heads, sq.S, 2 * hdim), BF16)],
        compiler_params=_params(2),
        name="attn_prep",
    )(proj, proj, cs_tab, *consts)


def _attn_body(q_ref, kt_ref, v_ref, o_ref, s_ref, p_ref, *, group, n_ctx_qblocks, n_ctx_keys):
    tq, hdim = q_ref.shape[1], q_ref.shape[2]
    n_keys = kt_ref.shape[1]

    def attend(nk):
        tiles = range(0, nk, KEY_BLOCK)
        for g in range(group):
            q = q_ref[g]
            rows = slice(g * tq, (g + 1) * tq)
            mx = None
            for t in tiles:
                s = jnp.dot(q, kt_ref[:, t:t + KEY_BLOCK], preferred_element_type=F32)
                s_ref[:, t:t + KEY_BLOCK] = s
                for c0 in range(0, KEY_BLOCK, LANES):
                    part = s[:, c0:c0 + LANES]
                    mx = part if mx is None else jnp.maximum(mx, part)
            m = jnp.broadcast_to(jnp.max(mx, axis=1, keepdims=True), (tq, LANES))
            for t in tiles:
                for c0 in range(t, t + KEY_BLOCK, LANES):
                    p_ref[rows, c0:c0 + LANES] = jnp.exp(s_ref[:, c0:c0 + LANES] - m).astype(BF16)
        acc = jnp.dot(p_ref[:, 0:nk], v_ref[0:nk, :], preferred_element_type=F32)
        out = acc[:, :hdim] / acc[:, hdim:]
        for g in range(group):
            o_ref[:, g * hdim:(g + 1) * hdim] = out[g * tq:(g + 1) * tq, :].astype(o_ref.dtype)

    is_ctx = pl.program_id(2) < n_ctx_qblocks

    @pl.when(is_ctx)
    def _():
        attend(n_ctx_keys)

    @pl.when(jnp.logical_not(is_ctx))
    def _():
        attend(n_keys)


def _attention(sq, q, kt, v):
    b, q_heads, s, hdim = q.shape
    kv_heads = kt.shape[1]
    group = q_heads // kv_heads
    tq = _largest_divisor(sq.ctx, (128, 64))
    assert sq.ctx % KEY_BLOCK == 0 and s % KEY_BLOCK == 0
    return pl.pallas_call(
        functools.partial(_attn_body, group=group, n_ctx_qblocks=sq.ctx // tq, n_ctx_keys=sq.ctx),
        grid=(b, kv_heads, s // tq),
        in_specs=[pl.BlockSpec((None, group, tq, hdim), lambda bi, g, i: (bi, g, i, 0)),
                  pl.BlockSpec((None, None, hdim, s), lambda bi, g, i: (bi, g, 0, 0)),
                  pl.BlockSpec((None, None, s, 2 * hdim), lambda bi, g, i: (bi, g, 0, 0))],
        out_specs=pl.BlockSpec((None, tq, group * hdim), lambda bi, g, i: (bi, i, g)),
        out_shape=jax.ShapeDtypeStruct((b, s, q_heads * hdim), BF16),
        scratch_shapes=[pltpu.VMEM((tq, s), F32), pltpu.VMEM((group * tq, s), BF16)],
        compiler_params=_params(3),
        name="attention",
    )(q, kt, v)


def _merge_body(x_ref, rf_ref, rb_ref, y0_ref, bonus_ref, rg_ref, gf_ref, gb_ref, z_ref, yc_ref, gate_ref, mod_ref,
                lnw_ref, lnb_ref, bdr_ref, gn_ref, bdg_ref, wa_ref, wb_ref, wc_ref, wo_ref, o_ref,
                *, gate_row, r_hdim, g_hdim, ctx_len):
    d = x_ref.shape[-1]
    o = rf_ref[...] + rb_ref[...] + y0_ref[...]
    mean = _segsum(o, bdr_ref) * (1.0 / r_hdim)
    cen = o - mean
    var = _segsum(cen * cen, bdr_ref) * (1.0 / r_hdim)
    ya = (cen * lax.rsqrt(var + RWKV_LNX_EPS) * lnw_ref[...] + lnb_ref[...] + bonus_ref[...]) * rg_ref[...]
    o = gf_ref[...] + gb_ref[...]
    ms = _segsum(o * o, bdg_ref) * (1.0 / g_hdim)
    z = z_ref[...]
    yb = o * lax.rsqrt(ms + RMS_EPS) * gn_ref[...] * (z * _sigmoid(z))
    m = gate_ref[:, :d] * _dot(ya, wa_ref[...])
    m = m + gate_ref[:, d:2 * d] * _dot(yb, wb_ref[...])
    m = m + gate_ref[:, 2 * d:] * _dot(yc_ref[...], wc_ref[...])
    y = _dot(m, wo_ref[...])
    rows = x_ref.shape[0]
    row = pl.program_id(1) * rows + lax.broadcasted_iota(jnp.int32, (rows, 1), 0)
    gate = jnp.where(row < ctx_len, mod_ref[0, gate_row:gate_row + 1, :], mod_ref[1, gate_row:gate_row + 1, :])
    o_ref[...] = x_ref[...] + gate * y


def _merge(sq, xs, o_rf, o_rb, y0, bonus, r_gate, o_gf, o_gb, proj, z_col_block, yc, gates, modtab, lw,
           r_heads, g_heads):
    d = xs.shape[-1]
    rdim, gdim = o_rf.shape[-1], o_gf.shape[-1]
    consts = [lw["lnx_w"], lw["lnx_b"], lw["bd_r"], lw["gdn_norm"], lw["bd_g"],
              lw["w_up_a"], lw["w_up_b"], lw["w_up_c"], lw["w_out"]]
    tb = _largest_divisor(sq.S, FFN_ROWS)
    rows = lambda width, col_block=0: pl.BlockSpec((None, tb, width), lambda b, j: (b, j, col_block))
    const = lambda shape: pl.BlockSpec(shape, lambda b, j: (0,) * len(shape))
    return pl.pallas_call(
        functools.partial(_merge_body, gate_row=2, r_hdim=rdim // r_heads, g_hdim=gdim // g_heads,
                          ctx_len=sq.ctx),
        grid=(sq.B, sq.S // tb),
        in_specs=[rows(d)] + [rows(rdim)] * 5 + [rows(gdim)] * 2
        + [rows(gdim, z_col_block), rows(yc.shape[-1]), rows(3 * d),
           pl.BlockSpec((None, 2, 6, d), lambda b, j: (b, 0, 0, 0))]
        + [const(c.shape) for c in consts],
        out_specs=rows(d),
        out_shape=jax.ShapeDtypeStruct(xs.shape, F32),
        compiler_params=_params(2),
        name="merge",
    )(xs, o_rf, o_rb, y0, bonus, r_gate, o_gf, o_gb, proj, yc, gates, modtab, *consts)


def _ffn_body(x_ref, g_ref, mod_ref, w1_ref, w3_ref, w2_ref, o_ref, *, h_chunk, ctx_len):
    rows = x_ref.shape[0]
    row = pl.program_id(1) * rows + lax.broadcasted_iota(jnp.int32, (rows, 1), 0)
    is_ctx = row < ctx_len
    pick = lambda i: jnp.where(is_ctx, mod_ref[0, i:i + 1, :], mod_ref[1, i:i + 1, :])
    x = x_ref[...]
    y = x * lax.rsqrt(jnp.mean(x * x, axis=-1, keepdims=True) + RMS_EPS) * g_ref[...]
    h = (y * (1.0 + pick(4)) + pick(3)).astype(BF16)
    hidden = w1_ref.shape[1]
    acc = jnp.zeros(x.shape, F32)
    for c0 in range(0, hidden, h_chunk):
        a = jnp.dot(h, w1_ref[:, c0:c0 + h_chunk], preferred_element_type=F32)
        b = jnp.dot(h, w3_ref[:, c0:c0 + h_chunk], preferred_element_type=F32)
        t = (a * _sigmoid(a) * b).astype(BF16)
        acc = acc + jnp.dot(t, w2_ref[c0:c0 + h_chunk, :], preferred_element_type=F32)
    o_ref[...] = x + pick(5) * acc


def _ffn(sq, xs, g, modtab, lw):
    d = xs.shape[-1]
    consts = [lw["ffn_w1"], lw["ffn_w3"], lw["ffn_w2"]]
    h_chunk = _largest_divisor(lw["ffn_w1"].shape[1], (512, 256, 128))
    tb = _largest_divisor(sq.S, FFN_ROWS)
    rows = pl.BlockSpec((None, tb, d), lambda b, j: (b, j, 0))
    const = lambda shape: pl.BlockSpec(shape, lambda b, j: (0,) * len(shape))
    return pl.pallas_call(
        functools.partial(_ffn_body, h_chunk=h_chunk, ctx_len=sq.ctx),
        grid=(sq.B, sq.S // tb),
        in_specs=[rows, const((1, d)), pl.BlockSpec((None, 2, 6, d), lambda b, j: (b, 0, 0, 0))]
        + [const(c.shape) for c in consts],
        out_specs=rows,
        out_shape=jax.ShapeDtypeStruct(xs.shape, F32),
        compiler_params=_params(2),
        name="ffn",
    )(xs, g.reshape(1, d), modtab, *consts)


def _block_diag_ones(n, seg):
    idx = jnp.arange(n) // seg
    return (idx[:, None] == idx[None, :]).astype(BF16)


def _block_diag2(m):
    z = jnp.zeros_like(m[0])
    return jnp.concatenate([jnp.concatenate([m[0], z], axis=1), jnp.concatenate([z, m[1]], axis=1)], axis=0)


def _pad_to(x, axis, size):
    pad = [(0, 0)] * x.ndim
    pad[axis] = (0, size - x.shape[axis])
    return jnp.pad(x, pad)


def _rope_table(ctx_len, seq_len, hdim, kv_heads):
    rows = seq_len // GRID_W
    row = jnp.repeat(jnp.arange(rows), GRID_W).astype(F32)
    col = jnp.tile(jnp.arange(GRID_W), rows).astype(F32)
    half = hdim // 2
    inv = ROPE_THETA ** (-jnp.arange(0, half, 2, dtype=F32) / half)
    ang = jnp.concatenate([row[:, None] * inv, col[:, None] * inv], axis=-1)
    cos = jnp.repeat(jnp.cos(ang), 2, axis=1)
    sin = jnp.repeat(jnp.sin(ang), 2, axis=1) * jnp.tile(jnp.array([-1.0, 1.0], F32), half)
    cos = jnp.concatenate([jnp.ones((ctx_len, hdim), F32), cos], axis=0)
    sin = jnp.concatenate([jnp.zeros((ctx_len, hdim), F32), sin], axis=0)
    return jnp.concatenate([jnp.tile(cos, (1, kv_heads)), jnp.tile(sin, (1, kv_heads))], axis=1)


def kernel(x, c, ctx, c_ctx, ada_w, ada_b, norm1, norm2, w_in, rwkv_mu_x, rwkv_mu_rkv, rwkv_w0, rwkv_w1, rwkv_w2, rwkv_a0, rwkv_a1, rwkv_a2, rwkv_g1, rwkv_g2, rwkv_k_k, rwkv_k_a, rwkv_r_k, rwkv_lnx_w, rwkv_lnx_b, gdn_conv, gdn_w_alpha, gdn_dt_bias, gdn_A_log, gdn_w_beta, gdn_norm, attn_q_norm, attn_k_norm, w_up_a, w_up_b, w_up_c, w_gate, b_gate, w_out, ffn_w1, ffn_w3, ffn_w2, final_norm):
    batch, seq_len, d = x.shape
    ctx_len = ctx.shape[1]
    depth = ada_w.shape[0]
    sq = _Seq(batch, ctx_len, seq_len)

    r_heads, r_hdim = rwkv_r_k.shape[1], rwkv_r_k.shape[2]
    rdim = r_heads * r_hdim
    g_heads, g_hdim = gdn_w_alpha.shape[-1], gdn_norm.shape[-1]
    gdim = g_heads * g_hdim
    a_hdim = attn_q_norm.shape[-1]
    qd = w_up_c.shape[1]
    q_heads = qd // a_hdim
    kvd = (w_in.shape[-1] - 3 * rdim - 4 * gdim - qd) // 2
    kv_heads = kvd // a_hdim
    assert rdim == gdim == qd and r_hdim == a_hdim, "lane-segment constants are shared between mixers"
    assert (3 * rdim) % (3 * gdim) == 0 and (3 * rdim + 3 * gdim) % gdim == 0
    gdn_col = (3 * rdim) // (3 * gdim)
    z_col = (3 * rdim + 3 * gdim) // gdim
    q_col = (3 * rdim + 4 * gdim) // qd
    assert (3 * rdim + 4 * gdim + qd) % (2 * kvd) == 0
    kv_col = (3 * rdim + 4 * gdim + qd) // (2 * kvd)

    bd_r = _block_diag_ones(rdim, r_hdim)
    bd_g = _block_diag_ones(gdim, g_hdim)
    cs_tab = _rope_table(ctx_len, seq_len, a_hdim, kv_heads)
    n_beta = 2 * g_heads
    assert 2 * n_beta <= LANES

    cond = jnp.concatenate([c, c_ctx[None, :]], axis=0)
    cond = _pad_to(cond, 0, -(-(batch + 1) // SUBLANES) * SUBLANES)

    xs = jnp.concatenate([ctx, x], axis=1)
    for l in range(depth):
        w1c = jnp.concatenate([rwkv_w1[l, 0], rwkv_w1[l, 1]], axis=1)
        a1c = jnp.concatenate([rwkv_a1[l, 0], rwkv_a1[l, 1]], axis=1)
        g1w = _pad_to(rwkv_g1[l], 1, 2 * LANES)
        wab = _pad_to(jnp.concatenate([gdn_w_beta[l, 0], gdn_w_beta[l, 1],
                                       gdn_w_alpha[l, 0], gdn_w_alpha[l, 1]], axis=1), 1, LANES)
        mu = rwkv_mu_x[l]
        splits = (w1c.shape[1], w1c.shape[1] + a1c.shape[1], w1c.shape[1] + a1c.shape[1] + g1w.shape[1])
        lw = {
            "mu_rkv": rwkv_mu_rkv[l],
            "wh": jnp.concatenate([w1c, a1c, g1w, wab], axis=1).astype(BF16),
            "wd": jnp.concatenate([mu[0][:, None] * w1c, mu[1][:, None] * a1c, mu[2][:, None] * g1w],
                                  axis=1).astype(BF16),
            "splits": splits,
            "w2": _block_diag2(rwkv_w2[l]).astype(BF16),
            "w0": rwkv_w0[l].reshape(1, 2 * rdim),
            "a2": _block_diag2(rwkv_a2[l]).astype(BF16),
            "a0": rwkv_a0[l].reshape(1, 2 * rdim),
            "g2": _pad_to(rwkv_g2[l], 0, 2 * LANES).astype(BF16),
            "abb": _pad_to(jnp.concatenate([jnp.zeros((n_beta,), F32), gdn_dt_bias[l].reshape(-1)]), 0, LANES).reshape(1, LANES),
            "alog": _pad_to(jnp.concatenate([jnp.zeros((n_beta,), F32), gdn_A_log[l].reshape(-1)]), 0, LANES).reshape(1, LANES),
            "n_beta": n_beta,
            "k_k": rwkv_k_k[l].reshape(1, rdim), "k_a": rwkv_k_a[l].reshape(1, rdim),
            "r_k": rwkv_r_k[l].reshape(1, rdim),
            "lnx_w": rwkv_lnx_w[l].reshape(1, rdim), "lnx_b": rwkv_lnx_b[l].reshape(1, rdim),
            "bd_r": bd_r, "bd_g": bd_g,
            "conv": jnp.transpose(gdn_conv[l]),
            "gdn_norm": jnp.tile(gdn_norm[l], g_heads).reshape(1, gdim),
            "q_norm": jnp.tile(attn_q_norm[l], q_heads).reshape(1, qd),
            "k_norm": jnp.tile(attn_k_norm[l], kv_heads).reshape(1, kvd),
            "w_up_a": w_up_a[l].astype(BF16), "w_up_b": w_up_b[l].astype(BF16),
            "w_up_c": w_up_c[l].astype(BF16), "w_out": w_out[l].astype(BF16),
            "ffn_w1": ffn_w1[l].astype(BF16), "ffn_w3": ffn_w3[l].astype(BF16), "ffn_w2": ffn_w2[l].astype(BF16),
        }
        mod = _matmul(cond, ada_w[l].astype(BF16), bias=ada_b[l], pre_act="silu")
        mod_x = mod[:batch].reshape(batch, 6, d)
        mod_c = jnp.broadcast_to(mod[batch].reshape(1, 6, d), (batch, 6, d))
        modtab = jnp.stack([mod_c, mod_x], axis=1)

        h, proj, gates = _pre(sq, xs, norm1[l], modtab, w_in[l].astype(BF16), w_gate[l].astype(BF16), b_gate[l])
        lw_dec, a_iclr, g_out, ab = _lora(sq, h, lw)

        rf, rg, bonus, y0 = _rwkv_intra(sq, proj, a_iclr, lw_dec, lw, r_heads)
        gf, gga, ggb = _gdn_intra(sq, proj, ab, lw, g_heads, gdim, gdn_col)
        o_rf, o_rb, o_gf, o_gb = _scan(sq, rf, rg, gf, gga, ggb, r_heads, g_heads)

        aq, ak, av = _attn_prep(sq, proj, cs_tab, lw, q_heads, kv_heads, a_hdim, q_col, kv_col)
        yc = _attention(sq, aq, ak, av)

        xs = _merge(sq, xs, o_rf, o_rb, y0, bonus, g_out, o_gf, o_gb, proj, z_col, yc, gates, modtab, lw,
                    r_heads, g_heads)
        xs = _ffn(sq, xs, norm2[l], modtab, lw)
    return _final_norm(sq, xs, final_norm)
```
